```python
import math
import jax
import jax.numpy as jnp
from jax import lax
import numpy as np

D_MODEL = 1024
BATCH = 8
SEQ = 4096
DEPTH = 2

HEAD_DIM = 64
Q_BLOCK = 128

A_HEADS = 4
A_PATTERNS = ((128, 1), (512, 4), (2048, 16))
A_GROUPS = 3
A_WIDTH = A_HEADS * HEAD_DIM

B_HEADS = 4
B_WIDTH = B_HEADS * HEAD_DIM
IDX_HEADS = 4
IDX_DIM = 64
TOPK_MAX = 256

C_HEADS = 4
C_WIDTH = C_HEADS * HEAD_DIM
C_LORA_W = 32
C_LORA_A = 32
C_LORA_G = 64
C_LORA_V = 16
C_IN = 3 * C_WIDTH + C_LORA_W + C_LORA_A + C_LORA_G
C_GN_EPS = 64e-5

D_HEADS = 4
D_INNER = D_HEADS * HEAD_DIM
D_GROUPS = 2
D_STATE = 64
D_CONV = 4
D_CHUNK = 128
D_XBC = D_INNER + 2 * D_GROUPS * D_STATE
D_NORM_EPS = 1e-5

N_BRANCH = 4
BRANCH_WIDTH = 256

N_EXPERT_GROUPS = 4
EXPERTS_PER_GROUP = 8
N_EXPERTS = N_EXPERT_GROUPS * EXPERTS_PER_GROUP
TOP_K_EXPERTS = 2
D_EXPERT = 512
MOE_BLOCK = 128

LN_EPS = 1e-5
DEEPNORM_ALPHA = (2 * DEPTH) ** 0.25
DEEPNORM_BETA = (8 * DEPTH) ** -0.25

IN_SEGMENTS = (
    ('a_qkv', 3 * A_GROUPS * A_WIDTH),
    ('b_qkv', 3 * B_WIDTH),
    ('b_idx_q', IDX_HEADS * IDX_DIM),
    ('b_idx_k', IDX_DIM),
    ('b_idx_w', IDX_HEADS),
    ('c_in', C_IN),
    ('d_z', D_INNER),
    ('d_xbc', D_XBC),
    ('d_dt', D_HEADS),
    ('gates', N_BRANCH * D_MODEL),
)
IN_WIDTH = sum(size for _, size in IN_SEGMENTS)

kernel_name = 'hybrid_gated_dilated_dsa_rwkv7_ssd_hiermoe'


def _split(x, sizes):
    out, off = [], 0
    for s in sizes:
        out.append(x[..., off:off + s])
        off += s
    return out


def layer_norm(x, g, b):
    xf = x.astype(jnp.float32)
    mu = jnp.mean(xf, axis=-1, keepdims=True)
    var = jnp.mean(jnp.square(xf - mu), axis=-1, keepdims=True)
    return ((xf - mu) * lax.rsqrt(var + LN_EPS) * g + b).astype(x.dtype)


def dilated_window_attention(qs, ks, vs):
    b, T = qs[0].shape[:2]
    scale = HEAD_DIM ** -0.5
    kps = [jnp.pad(k, ((0, 0), (w, 0), (0, 0), (0, 0))) for k, (w, _) in zip(ks, A_PATTERNS)]
    vps = [jnp.pad(v, ((0, 0), (w, 0), (0, 0), (0, 0))) for v, (w, _) in zip(vs, A_PATTERNS)]

    def block(i):
        start = i * Q_BLOCK
        outs, lses = [], []
        for q, kp, vp, (win, dil) in zip(qs, kps, vps, A_PATTERNS):
            nq, nk, span = Q_BLOCK // dil, win // dil, (win + Q_BLOCK) // dil
            qb = lax.dynamic_slice_in_dim(q, start, Q_BLOCK, axis=1).reshape(b, nq, dil, A_HEADS, HEAD_DIM)
            kb = lax.dynamic_slice_in_dim(kp, start, win + Q_BLOCK, axis=1).reshape(b, span, dil, A_HEADS, HEAD_DIM)
            vb = lax.dynamic_slice_in_dim(vp, start, win + Q_BLOCK, axis=1).reshape(b, span, dil, A_HEADS, HEAD_DIM)
            s = jnp.einsum('barhe,bcrhe->brhac', qb, kb).astype(jnp.float32) * scale
            a_idx = jnp.arange(nq)[:, None]
            c_idx = jnp.arange(span)[None, :]
            band = (c_idx >= a_idx) & (c_idx <= a_idx + nk)
            key_pos = start - win + jnp.arange(span)[None, :] * dil + jnp.arange(dil)[:, None]
            ok = band[None, :, :] & (key_pos >= 0)[:, None, :]
            s = jnp.where(ok[None, :, None], s, -jnp.inf)
            lse = jax.nn.logsumexp(s, axis=-1)
            p = jnp.exp(s - lse[..., None]).astype(vb.dtype)
            o = jnp.einsum('brhac,bcrhe->barhe', p, vb).reshape(b, Q_BLOCK, A_HEADS, HEAD_DIM)
            outs.append(o)
            lses.append(lse.transpose(0, 3, 1, 2).reshape(b, Q_BLOCK, A_HEADS))
        mix = jax.nn.softmax(jnp.stack(lses), axis=0)
        out = mix[0][..., None].astype(outs[0].dtype) * outs[0]
        for g in range(1, A_GROUPS):
            out = out + mix[g][..., None].astype(outs[g].dtype) * outs[g]
        return out

    out = lax.map(block, jnp.arange(T // Q_BLOCK))
    return out.transpose(1, 0, 2, 3, 4).reshape(b, T, A_WIDTH)


def indexer_sparse_attention(q, k, v, iq, ik, iw):
    b, T = q.shape[:2]
    top_k = min(TOPK_MAX, T // 4)
    scale = HEAD_DIM ** -0.5
    key_pos = jnp.arange(T)
    gather = jax.vmap(lambda arr, idx: arr[idx])

    def block(i):
        start = i * Q_BLOCK
        t = start + jnp.arange(Q_BLOCK)
        qi = lax.dynamic_slice_in_dim(iq, start, Q_BLOCK, axis=1)
        wi = lax.dynamic_slice_in_dim(iw, start, Q_BLOCK, axis=1).astype(jnp.float32)
        logits = jnp.einsum('bqhd,bsd->bqhs', qi, ik).astype(jnp.float32) * IDX_DIM ** -0.5
        score = jnp.einsum('bqhs,bqh->bqs', jax.nn.relu(logits), wi) * IDX_HEADS ** -0.5
        score = jnp.where((key_pos[None, :] <= t[:, None])[None], score, -jnp.inf)
        _, sel = lax.top_k(score, top_k)
        valid = sel <= t[None, :, None]
        kg = gather(k, sel)
        vg = gather(v, sel)
        qb = lax.dynamic_slice_in_dim(q, start, Q_BLOCK, axis=1)
        s = jnp.einsum('bqhe,bqjhe->bhqj', qb, kg).astype(jnp.float32) * scale
        s = jnp.where(valid[:, None], s, -jnp.inf)
        p = jax.nn.softmax(s, axis=-1).astype(vg.dtype)
        return jnp.einsum('bhqj,bqjhe->bqhe', p, vg)

    out = lax.map(block, jnp.arange(T // Q_BLOCK))
    return out.transpose(1, 0, 2, 3, 4).reshape(b, T, B_WIDTH)


def token_shift(x):
    return jnp.pad(x, ((0, 0), (1, 0), (0, 0)))[:, :-1]


def wkv7_scan(r, w, k, v, a, bb):
    b, T, H, N = r.shape

    def step(state, inp):
        rt, wt, kt, vt, at, bt = inp
        sa = jnp.einsum('bhij,bhj->bhi', state, at)
        state = state * wt[:, :, None, :] + sa[..., None] * bt[:, :, None, :] + vt[..., None] * kt[:, :, None, :]
        return state, jnp.einsum('bhij,bhj->bhi', state, rt)

    seq_major = tuple(jnp.moveaxis(t.astype(jnp.float32), 1, 0) for t in (r, w, k, v, a, bb))
    _, y = lax.scan(step, jnp.zeros((b, H, N, N), jnp.float32), seq_major)
    return jnp.moveaxis(y, 0, 1)


def rwkv7_time_mix(pc, mu, w0, w2, a0, a2, g2, k_k, k_a, r_k, gn_w, gn_b, v_first, v_lora):
    b, T, _ = pc.shape
    pc = pc + (token_shift(pc) - pc) * mu
    r, k, v, xw, xa, xg = _split(pc, (C_WIDTH, C_WIDTH, C_WIDTH, C_LORA_W, C_LORA_A, C_LORA_G))
    w = -jax.nn.softplus(-(w0 + jnp.tanh(xw) @ w2)) - 0.5
    a = jax.nn.sigmoid(a0 + xa @ a2)
    g = jax.nn.sigmoid(xg) @ g2
    if v_lora is not None:
        v0, v1, v2 = v_lora
        v = v + (v_first - v) * jax.nn.sigmoid(v0 + (v @ v1) @ v2)
    heads = lambda t: t.astype(jnp.float32).reshape(b, T, C_HEADS, HEAD_DIM)
    kk = heads(k * k_k)
    kk = kk / jnp.maximum(jnp.linalg.norm(kk, axis=-1, keepdims=True), 1e-12)
    k_mod = heads(k * (1.0 + (a - 1.0) * k_a))
    r_h, v_h, a_h = heads(r), heads(v), heads(a)
    decay = jnp.exp(-jnp.exp(heads(w)))
    y = wkv7_scan(r_h, decay, k_mod, v_h, -kk, kk * a_h)
    y_mu = jnp.mean(y, axis=-1, keepdims=True)
    y_var = jnp.mean(jnp.square(y - y_mu), axis=-1, keepdims=True)
    y = ((y - y_mu) * lax.rsqrt(y_var + C_GN_EPS)).reshape(b, T, C_WIDTH) * gn_w + gn_b
    bonus = jnp.sum(r_h * k_mod * r_k, axis=-1, keepdims=True) * v_h
    o = (y + bonus.reshape(b, T, C_WIDTH)) * g
    return o.astype(pc.dtype), v


def ssd_chunked(x, a, bm, cm):
    b, T, H, P = x.shape
    nc = T // D_CHUNK
    x = x.reshape(b, nc, D_CHUNK, H, P)
    bm = bm.reshape(b, nc, D_CHUNK, H, -1)
    cm = cm.reshape(b, nc, D_CHUNK, H, -1)
    a_cs = jnp.cumsum(a.reshape(b, nc, D_CHUNK, H).transpose(0, 3, 1, 2), axis=-1)
    seg = a_cs[..., :, None] - a_cs[..., None, :]
    causal = jnp.tril(jnp.ones((D_CHUNK, D_CHUNK), dtype=bool))
    decay_ls = jnp.exp(jnp.where(causal, seg, -jnp.inf))
    scores = jnp.einsum('bclhn,bcshn->bhcls', cm, bm) * decay_ls
    y_diag = jnp.einsum('bhcls,bcshp->bclhp', scores, x)
    decay_to_end = jnp.exp(a_cs[..., -1:] - a_cs)
    chunk_states = jnp.einsum('bclhn,bhcl,bclhp->bchpn', bm, decay_to_end, x)
    chunk_decay = jnp.exp(a_cs[..., -1])

    def step(state, inp):
        st, dc = inp
        return state * dc[..., None, None] + st, state

    _, states_in = lax.scan(step, jnp.zeros((b, H, P, bm.shape[-1]), jnp.float32),
                            (jnp.moveaxis(chunk_states, 1, 0), jnp.moveaxis(chunk_decay, 2, 0)))
    states_in = jnp.moveaxis(states_in, 0, 1)
    y_off = jnp.einsum('bclhn,bchpn,bhcl->bclhp', cm, states_in, jnp.exp(a_cs))
    return (y_diag + y_off).reshape(b, T, H, P)


def mamba2_mixer(z, xbc, dt_raw, conv_w, conv_b, dt_bias, a_log, d_skip, norm_w):
    b, T, _ = z.shape
    xbc = lax.conv_general_dilated(xbc, conv_w[:, None, :], window_strides=(1,),
                                   padding=[(D_CONV - 1, 0)],
                                   dimension_numbers=('NWC', 'WIO', 'NWC'),
                                   feature_group_count=D_XBC)
    xbc = jax.nn.silu(xbc + conv_b).astype(jnp.float32)
    heads_per_group = D_HEADS // D_GROUPS
    xs = xbc[..., :D_INNER].reshape(b, T, D_HEADS, HEAD_DIM)
    bm = xbc[..., D_INNER:D_INNER + D_GROUPS * D_STATE].reshape(b, T, D_GROUPS, D_STATE)
    cm = xbc[..., D_INNER + D_GROUPS * D_STATE:].reshape(b, T, D_GROUPS, D_STATE)
    bm = jnp.repeat(bm, heads_per_group, axis=2)
    cm = jnp.repeat(cm, heads_per_group, axis=2)
    dt = jax.nn.softplus(dt_raw.astype(jnp.float32) + dt_bias.astype(jnp.float32))
    a = -jnp.exp(a_log.astype(jnp.float32))
    y = ssd_chunked(xs * dt[..., None], dt * a, bm, cm) + d_skip.astype(jnp.float32)[:, None] * xs
    y = y.reshape(b, T, D_INNER) * jax.nn.silu(z.astype(jnp.float32))
    y = y.reshape(b, T, D_GROUPS, D_INNER // D_GROUPS)
    y = y * lax.rsqrt(jnp.mean(jnp.square(y), axis=-1, keepdims=True) + D_NORM_EPS)
    return (y.reshape(b, T, D_INNER) * norm_w).astype(z.dtype)


def grouped_expert_ffn(x2d, expert, gate, w_gate, w_up, w_down):
    m, dm = x2d.shape
    n_assign = m * TOP_K_EXPERTS
    flat_e = expert.reshape(-1)
    flat_tok = jnp.arange(n_assign) // TOP_K_EXPERTS
    flat_gate = gate.reshape(-1)
    order = jnp.argsort(flat_e)
    se, stok, sgate = flat_e[order], flat_tok[order], flat_gate[order]
    counts = jnp.bincount(flat_e, length=N_EXPERTS)
    padded = (counts + MOE_BLOCK - 1) // MOE_BLOCK * MOE_BLOCK
    pad_end = jnp.cumsum(padded)
    pad_start = pad_end - padded
    start = jnp.cumsum(counts) - counts
    dest = pad_start[se] + jnp.arange(n_assign) - start[se]
    cap = (n_assign + N_EXPERTS * (MOE_BLOCK - 1) + MOE_BLOCK - 1) // MOE_BLOCK * MOE_BLOCK
    n_blocks = cap // MOE_BLOCK
    buf_tok = jnp.full((cap,), m, dtype=jnp.int32).at[dest].set(stok)
    x_ext = jnp.concatenate([x2d, jnp.zeros((1, dm), x2d.dtype)], axis=0)
    xb = x_ext[buf_tok].reshape(n_blocks, MOE_BLOCK, dm)
    blk_expert = jnp.minimum(jnp.searchsorted(pad_end, jnp.arange(n_blocks) * MOE_BLOCK, side='right'),
                             N_EXPERTS - 1)

    def expert_block(args):
        xblk, e = args
        h = jax.nn.silu(xblk @ w_gate[e]) * (xblk @ w_up[e])
        return h @ w_down[e]

    yb = lax.map(expert_block, (xb, blk_expert)).reshape(cap, dm)
    y = yb[dest] * sgate[:, None]
    return jnp.zeros_like(x2d).at[stok].add(y)


def hierarchical_moe(x2d, wg, bg, we, be, w_gate, w_up, w_down):
    m = x2d.shape[0]
    xf = x2d.astype(jnp.float32)
    p_group = jax.nn.softmax(xf @ wg.astype(jnp.float32) + bg.astype(jnp.float32), axis=-1)
    pg_top, g_sel = lax.top_k(p_group, 1)
    logits_e = (xf @ we.astype(jnp.float32) + be.astype(jnp.float32)).reshape(m, N_EXPERT_GROUPS, EXPERTS_PER_GROUP)
    logits_e = logits_e[jnp.arange(m), g_sel[:, 0]]
    pe_top, e_sel = lax.top_k(jax.nn.softmax(logits_e, axis=-1), TOP_K_EXPERTS)
    gate = pg_top * pe_top / jnp.sum(pe_top, axis=-1, keepdims=True)
    expert = g_sel * EXPERTS_PER_GROUP + e_sel
    return grouped_expert_ffn(x2d, expert, gate.astype(x2d.dtype), w_gate, w_up, w_down)


def setup_inputs(seed: int = 0) -> dict:
    key = jax.random.key(seed)
    ks = iter(jax.random.split(key, 48))
    L = DEPTH

    def nrm(shape, scale):
        return jax.random.normal(next(ks), shape, jnp.float32) * scale

    def unif(shape, lo, hi):
        return jax.random.uniform(next(ks), shape, jnp.float32, minval=lo, maxval=hi)

    dt = jnp.exp(unif((L, D_HEADS), math.log(1e-3), math.log(1e-1)))
    return {
        'x': nrm((BATCH, SEQ, D_MODEL), 1.0),
        'w_in': nrm((L, D_MODEL, IN_WIDTH), D_MODEL ** -0.5),
        'c_mu': unif((L, C_IN), 0.0, 1.0),
        'c_w0': unif((L, C_WIDTH), -6.0, -1.0),
        'c_w2': nrm((L, C_LORA_W, C_WIDTH), 0.1),
        'c_a0': nrm((L, C_WIDTH), 0.1),
        'c_a2': nrm((L, C_LORA_A, C_WIDTH), C_LORA_A ** -0.5),
        'c_g2': nrm((L, C_LORA_G, C_WIDTH), C_LORA_G ** -0.5),
        'c_kk': 0.85 + nrm((L, C_WIDTH), 0.02),
        'c_ka': 1.0 + nrm((L, C_WIDTH), 0.02),
        'c_rk': nrm((L, C_HEADS, HEAD_DIM), 0.1),
        'c_gn_w': 1.0 + nrm((L, C_WIDTH), 0.02),
        'c_gn_b': nrm((L, C_WIDTH), 0.02),
        'c_v0': nrm((L - 1, C_WIDTH), 0.1),
        'c_v1': nrm((L - 1, C_WIDTH, C_LORA_V), C_WIDTH ** -0.5),
        'c_v2': nrm((L - 1, C_LORA_V, C_WIDTH), C_LORA_V ** -0.5),
        'd_conv_w': nrm((L, D_CONV, D_XBC), D_CONV ** -0.5),
        'd_conv_b': nrm((L, D_XBC), 0.02),
        'd_dt_bias': dt + jnp.log(-jnp.expm1(-dt)),
        'd_a_log': jnp.log(unif((L, D_HEADS), 1.0, 16.0)),
        'd_skip': 1.0 + nrm((L, D_HEADS), 0.02),
        'd_norm_w': 1.0 + nrm((L, D_INNER), 0.02),
        'w_branch': nrm((L, N_BRANCH, BRANCH_WIDTH, D_MODEL), BRANCH_WIDTH ** -0.5),
        'w_out': nrm((L, D_MODEL, D_MODEL), D_MODEL ** -0.5 * DEEPNORM_BETA),
        'ln1_g': 1.0 + nrm((L, D_MODEL), 0.02),
        'ln1_b': nrm((L, D_MODEL), 0.02),
        'r_group': nrm((L, D_MODEL, N_EXPERT_GROUPS), D_MODEL ** -0.5),
        'r_group_b': nrm((L, N_EXPERT_GROUPS), 0.01),
        'r_expert': nrm((L, D_MODEL, N_EXPERTS), D_MODEL ** -0.5),
        'r_expert_b': nrm((L, N_EXPERTS), 0.01),
        'e_gate': nrm((L, N_EXPERTS, D_MODEL, D_EXPERT), D_MODEL ** -0.5),
        'e_up': nrm((L, N_EXPERTS, D_MODEL, D_EXPERT), D_MODEL ** -0.5),
        'e_down': nrm((L, N_EXPERTS, D_EXPERT, D_MODEL), D_EXPERT ** -0.5 * DEEPNORM_BETA),
        'ln2_g': 1.0 + nrm((L, D_MODEL), 0.02),
        'ln2_b': nrm((L, D_MODEL), 0.02),
    }


def reference(x, w_in, c_mu, c_w0, c_w2, c_a0, c_a2, c_g2, c_kk, c_ka, c_rk, c_gn_w, c_gn_b,
              c_v0, c_v1, c_v2, d_conv_w, d_conv_b, d_dt_bias, d_a_log, d_skip, d_norm_w,
              w_branch, w_out, ln1_g, ln1_b, r_group, r_group_b, r_expert, r_expert_b,
              e_gate, e_up, e_down, ln2_g, ln2_b):
    b, T, _ = x.shape
    v_first = None
    for l in range(DEPTH):
        seg, off = {}, 0
        for name, size in IN_SEGMENTS:
            seg[name] = x @ w_in[l, :, off:off + size]
            off += size

        a_qkv = seg['a_qkv'].reshape(b, T, 3, A_GROUPS, A_HEADS, HEAD_DIM)
        o_a = dilated_window_attention([a_qkv[:, :, 0, g] for g in range(A_GROUPS)],
                                       [a_qkv[:, :, 1, g] for g in range(A_GROUPS)],
                                       [a_qkv[:, :, 2, g] for g in range(A_GROUPS)])

        b_qkv = seg['b_qkv'].reshape(b, T, 3, B_HEADS, HEAD_DIM)
        o_b = indexer_sparse_attention(b_qkv[:, :, 0], b_qkv[:, :, 1], b_qkv[:, :, 2],
                                       seg['b_idx_q'].reshape(b, T, IDX_HEADS, IDX_DIM),
                                       seg['b_idx_k'], seg['b_idx_w'])

        v_lora = None if l == 0 else (c_v0[l - 1], c_v1[l - 1], c_v2[l - 1])
        o_c, v_c = rwkv7_time_mix(seg['c_in'], c_mu[l], c_w0[l], c_w2[l], c_a0[l], c_a2[l], c_g2[l],
                                  c_kk[l], c_ka[l], c_rk[l], c_gn_w[l], c_gn_b[l], v_first, v_lora)
        if l == 0:
            v_first = v_c

        o_d = mamba2_mixer(seg['d_z'], seg['d_xbc'], seg['d_dt'], d_conv_w[l], d_conv_b[l],
                           d_dt_bias[l], d_a_log[l], d_skip[l], d_norm_w[l])

        gates = jax.nn.sigmoid(seg['gates'].reshape(b, T, N_BRANCH, D_MODEL))
        merged = jnp.zeros_like(x)
        for n, o_n in enumerate((o_a, o_b, o_c, o_d)):
            merged = merged + gates[:, :, n] * (o_n @ w_branch[l, n])
        x = layer_norm(DEEPNORM_ALPHA * x + merged @ w_out[l], ln1_g[l], ln1_b[l])

        y = hierarchical_moe(x.reshape(b * T, D_MODEL), r_group[l], r_group_b[l], r_expert[l],
                             r_expert_b[l], e_gate[l], e_up[l], e_down[l])
        x = layer_norm(DEEPNORM_ALPHA * x + y.reshape(b, T, D_MODEL), ln2_g[l], ln2_b[l])
    return x
```

```python
import functools
import math

import jax
import jax.numpy as jnp
import numpy as np
from jax import lax
from jax.experimental import pallas as pl
from jax.experimental.pallas import tpu as pltpu

F32 = jnp.float32
BF16 = jnp.bfloat16
I32 = jnp.int32
HIGHEST = lax.Precision.HIGHEST
NEG_INF = float("-inf")
INT_MIN = -(2 ** 31)

LANES = 128
SUBLANES = 8

D_MODEL = 1024
DEPTH = 2
HEAD_DIM = 64
Q_BLOCK = 128

A_HEADS = 4
A_PATTERNS = ((128, 1), (512, 4), (2048, 16))
A_GROUPS = len(A_PATTERNS)
A_WIDTH = A_HEADS * HEAD_DIM
A_MAXWIN = max(w for w, _ in A_PATTERNS)

B_HEADS = 4
B_WIDTH = B_HEADS * HEAD_DIM
IDX_HEADS = 4
IDX_DIM = 64
TOPK_MAX = 256

C_HEADS = 4
C_WIDTH = C_HEADS * HEAD_DIM
C_LORA_W = 32
C_LORA_A = 32
C_LORA_G = 64
C_LORA_V = 16
C_IN = 3 * C_WIDTH + C_LORA_W + C_LORA_A + C_LORA_G
C_GN_EPS = 64e-5
C_CHUNK = 64

D_HEADS = 4
D_INNER = D_HEADS * HEAD_DIM
D_GROUPS = 2
D_STATE = 64
D_CONV = 4
D_CHUNK = 128
D_XBC = D_INNER + 2 * D_GROUPS * D_STATE
D_NORM_EPS = 1e-5

N_BRANCH = 4
N_EXPERT_GROUPS = 4
EXPERTS_PER_GROUP = 8
N_EXPERTS = N_EXPERT_GROUPS * EXPERTS_PER_GROUP
TOP_K_EXPERTS = 2
D_EXPERT = 512
MOE_ROWS = 256

LN_EPS = 1e-5
DEEPNORM_ALPHA = (2 * DEPTH) ** 0.25

PB_AKV = 0
PB_AQ = PB_AKV + A_HEADS * A_GROUPS * 2 * HEAD_DIM
PB_BQ = PB_AQ + A_HEADS * 4 * HEAD_DIM
PB_BK = PB_BQ + B_WIDTH
PB_BV = PB_BK + B_WIDTH
PB_IQ = PB_BV + B_WIDTH
PB_IK = PB_IQ + IDX_HEADS * IDX_DIM
PB_WIDTH = PB_IK + 256
PF_GATES = 0
PF_CIN = PF_GATES + N_BRANCH * D_MODEL
PF_MISC = PF_CIN + C_IN
PF_XBC = PF_CIN + 1024
PF_Z = PF_XBC + D_XBC
PF_WIDTH = PF_Z + D_INNER
MISC_DT = 4


def _dot(a, b, precision=None):
    return jnp.dot(a, b, preferred_element_type=F32, precision=precision)


def _dot_nt(a, b, precision=None):
    return lax.dot_general(a, b, (((1,), (1,)), ((), ())), preferred_element_type=F32, precision=precision)


def _doth(a, b):
    return _dot(a, b, HIGHEST)


def _sigmoid(x):
    return 1.0 / (1.0 + jnp.exp(-x))


def _softplus(x):
    return jnp.maximum(x, 0.0) + jnp.log1p(jnp.exp(-jnp.abs(x)))


def _silu(x):
    return x * _sigmoid(x)


def _layer_norm(h, g, b):
    mu = jnp.mean(h, axis=-1, keepdims=True)
    d = h - mu
    var = jnp.mean(d * d, axis=-1, keepdims=True)
    return d * lax.rsqrt(var + LN_EPS) * g + b


def _params(*sem, vmem_mb=None):
    kw = {}
    if vmem_mb is not None:
        kw["vmem_limit_bytes"] = vmem_mb * 1024 * 1024
    return pltpu.CompilerParams(dimension_semantics=sem, **kw)


def _proj_kernel(x_ref, w_ref, o_ref):
    o_ref[...] = _dot(x_ref[...].astype(BF16), w_ref[...]).astype(o_ref.dtype)


def _proj(x2d, w, out_dtype, tm, tn):
    m, k = x2d.shape
    n = w.shape[1]
    return pl.pallas_call(
        _proj_kernel,
        grid=(m // tm, n // tn),
        in_specs=[pl.BlockSpec((tm, k), lambda i, j: (i, 0)),
                  pl.BlockSpec((k, tn), lambda i, j: (0, j))],
        out_specs=pl.BlockSpec((tm, tn), lambda i, j: (i, j)),
        out_shape=jax.ShapeDtypeStruct((m, n), out_dtype),
        compiler_params=_params("parallel", "arbitrary", vmem_mb=48),
        name="in_proj",
    )(x2d, w)


def _proj_weights(w):
    k = w.shape[0]
    off = 0
    seg = {}
    for name, size in (("a_qkv", 3 * A_GROUPS * A_WIDTH), ("b_qkv", 3 * B_WIDTH), ("b_idx_q", IDX_HEADS * IDX_DIM),
                       ("b_idx_k", IDX_DIM), ("b_idx_w", IDX_HEADS), ("c_in", C_IN), ("d_z", D_INNER),
                       ("d_xbc", D_XBC), ("d_dt", D_HEADS), ("gates", N_BRANCH * D_MODEL)):
        seg[name] = w[:, off:off + size]
        off += size
    a = seg["a_qkv"].reshape(k, 3, A_GROUPS, A_HEADS, HEAD_DIM)
    a_kv = jnp.transpose(a[:, 1:3], (0, 3, 2, 1, 4)).reshape(k, -1)
    a_q = jnp.transpose(a[:, 0], (0, 2, 1, 3))
    a_q = jnp.pad(a_q, ((0, 0), (0, 0), (0, 4 - A_GROUPS), (0, 0))).reshape(k, -1)
    zeros = lambda n: jnp.zeros((k, n), w.dtype)
    wb = jnp.concatenate([a_kv, a_q, seg["b_qkv"], seg["b_idx_q"], seg["b_idx_k"], zeros(256 - IDX_DIM)], axis=1)
    misc = jnp.concatenate([seg["b_idx_w"], seg["d_dt"], zeros(LANES - IDX_HEADS - D_HEADS)], axis=1)
    wf = jnp.concatenate([seg["gates"], seg["c_in"], misc, seg["d_xbc"], seg["d_z"]], axis=1)
    assert wb.shape[1] == PB_WIDTH and wf.shape[1] == PF_WIDTH
    return wb.astype(BF16), wf.astype(BF16)


def _mixa_bias():
    out = []
    for win, dil in A_PATTERNS:
        r = np.arange(Q_BLOCK)[:, None]
        c = np.arange(win + Q_BLOCK)[None, :]
        d = r + win - c
        ok = (d >= 0) & (d <= win) & (d % dil == 0)
        out.append(jnp.asarray(np.where(ok, 0.0, -np.inf).astype(np.float32)))
    return out


def _mixa_kernel(q_ref, kv_ref, b0_ref, b1_ref, b2_ref, o_ref):
    start = pl.program_id(2) * Q_BLOCK
    scale = HEAD_DIM ** -0.5
    bias_refs = (b0_ref, b1_ref, b2_ref)
    outs = []
    for hh in range(2):
        ms, ls, os_ = [], [], []
        for g, (win, _) in enumerate(A_PATTERNS):
            span = win + Q_BLOCK
            q = q_ref[0, :, hh * 256 + g * HEAD_DIM: hh * 256 + (g + 1) * HEAD_DIM]
            base = pl.multiple_of(start + (A_MAXWIN - win), Q_BLOCK)
            c0 = hh * (A_GROUPS * 2 * HEAD_DIM) + g * 2 * HEAD_DIM
            k = kv_ref[0, pl.ds(base, span), c0:c0 + HEAD_DIM]
            v = kv_ref[0, pl.ds(base, span), c0 + HEAD_DIM:c0 + 2 * HEAD_DIM]
            s = _dot_nt(q, k) * scale + bias_refs[g][...]
            col = lax.broadcasted_iota(I32, (Q_BLOCK, span), 1)
            s = jnp.where(col >= win - start, s, NEG_INF)
            m = jnp.max(s, axis=-1, keepdims=True)
            p = jnp.exp(s - m)
            ls.append(jnp.sum(p, axis=-1, keepdims=True))
            os_.append(_dot(p.astype(BF16), v))
            ms.append(m)
        m_all = jnp.maximum(jnp.maximum(ms[0], ms[1]), ms[2])
        num = jnp.zeros((Q_BLOCK, HEAD_DIM), F32)
        den = jnp.zeros((Q_BLOCK, 1), F32)
        for g in range(A_GROUPS):
            wg = jnp.exp(ms[g] - m_all)
            num = num + wg * os_[g]
            den = den + wg * ls[g]
        outs.append(num / den)
    o_ref[0] = jnp.concatenate(outs, axis=1).astype(o_ref.dtype)


def _mixer_a(pb, kv_pad, biases, batch, seq):
    nblk = seq // Q_BLOCK
    pair_kv = 2 * A_GROUPS * 2 * HEAD_DIM
    in_specs = [pl.BlockSpec((1, Q_BLOCK, 512), lambda b, hp, i: (b, i, PB_AQ // 512 + hp)),
                pl.BlockSpec((1, seq + A_MAXWIN, pair_kv), lambda b, hp, i: (b, 0, hp))]
    for bias in biases:
        in_specs.append(pl.BlockSpec(bias.shape, lambda b, hp, i: (0, 0)))
    return pl.pallas_call(
        _mixa_kernel,
        grid=(batch, A_HEADS // 2, nblk),
        in_specs=in_specs,
        out_specs=pl.BlockSpec((1, Q_BLOCK, 2 * HEAD_DIM), lambda b, hp, i: (b, i, hp)),
        out_shape=jax.ShapeDtypeStruct((batch, seq, A_WIDTH), BF16),
        compiler_params=_params("parallel", "parallel", "arbitrary", vmem_mb=48),
        name="mixer_a",
    )(pb, kv_pad, *biases)


def _mixb_kernel(qb_ref, kb_ref, vb_ref, iq_ref, ik_ref, misc_ref, tri_ref, o_ref, keys_ref, selb_ref, *, seq, top_k):
    start = pl.program_id(1) * Q_BLOCK
    qpos = start + lax.broadcasted_iota(I32, (Q_BLOCK, 1), 0)
    w = misc_ref[0][:, 0:IDX_HEADS] * (IDX_DIM ** -0.5 * IDX_HEADS ** -0.5)
    iq = iq_ref[0]
    ch = min(512, seq)
    for c in range(seq // ch):
        ik = ik_ref[0, c * ch:(c + 1) * ch, 0:IDX_DIM]
        acc = jnp.zeros((Q_BLOCK, ch), F32)
        for h in range(IDX_HEADS):
            lg = _dot_nt(iq[:, h * IDX_DIM:(h + 1) * IDX_DIM], ik)
            acc = acc + jnp.maximum(lg, 0.0) * w[:, h:h + 1]
        acc = acc + 0.0
        bits = lax.bitcast_convert_type(acc, I32)
        key = bits ^ ((bits >> 31) & 0x7FFFFFFF)
        kpos = c * ch + lax.broadcasted_iota(I32, (Q_BLOCK, ch), 1)
        keys_ref[:, c * ch:(c + 1) * ch] = jnp.where(kpos <= qpos, key, INT_MIN)

    nl = seq // LANES

    def count_ge(cand):
        cnt = jnp.zeros((Q_BLOCK, LANES), F32)
        for c in range(nl):
            cnt = cnt + jnp.where(keys_ref[:, c * LANES:(c + 1) * LANES] >= cand, 1.0, 0.0)
        return jnp.sum(cnt, axis=-1, keepdims=True)

    kf = float(top_k)
    ans = jnp.where(count_ge(jnp.zeros((Q_BLOCK, 1), I32)) >= kf, 0, INT_MIN).astype(I32)

    def body(it, ans):
        cand = ans + jnp.left_shift(jnp.int32(1), 30 - it)
        return jnp.where(count_ge(cand) >= kf, cand, ans)

    thr = lax.fori_loop(0, 31, body, ans)

    cnt = jnp.zeros((Q_BLOCK, LANES), F32)
    for c in range(nl):
        cnt = cnt + jnp.where(keys_ref[:, c * LANES:(c + 1) * LANES] > thr, 1.0, 0.0)
    need = kf - jnp.sum(cnt, axis=-1, keepdims=True)

    tw = tri_ref.shape[0]
    run = jnp.zeros((Q_BLOCK, 1), F32)
    for c in range(seq // tw):
        kc = keys_ref[:, c * tw:(c + 1) * tw]
        eq = kc == thr
        pre = _dot(jnp.where(eq, 1.0, 0.0).astype(BF16), tri_ref[...])
        take = jnp.where(eq, jnp.where(pre + run <= need, 1.0, 0.0), 0.0)
        sel = jnp.where(kc > thr, 1.0, take)
        sel = jnp.where(kc != INT_MIN, sel, 0.0)
        selb_ref[:, c * tw:(c + 1) * tw] = jnp.where(sel > 0.5, 0.0, NEG_INF)
        run = run + pre[:, tw - 1:tw]

    scale = HEAD_DIM ** -0.5
    outs = []
    for h in range(B_HEADS):
        sl = slice(h * HEAD_DIM, (h + 1) * HEAD_DIM)
        s = _dot_nt(qb_ref[0][:, sl], kb_ref[0][:, sl]) * scale + selb_ref[...]
        m = jnp.max(s, axis=-1, keepdims=True)
        p = jnp.exp(s - m)
        l = jnp.sum(p, axis=-1, keepdims=True)
        outs.append(_dot(p.astype(BF16), vb_ref[0][:, sl]) / l)
    o_ref[0] = jnp.concatenate(outs, axis=1).astype(o_ref.dtype)


def _mixer_b(pb, pf, tri, batch, seq):
    nblk = seq // Q_BLOCK
    top_k = min(TOPK_MAX, seq // 4)
    kern = functools.partial(_mixb_kernel, seq=seq, top_k=top_k)
    return pl.pallas_call(
        kern,
        grid=(batch, nblk),
        in_specs=[pl.BlockSpec((1, Q_BLOCK, B_WIDTH), lambda b, i: (b, i, PB_BQ // B_WIDTH)),
                  pl.BlockSpec((1, seq, B_WIDTH), lambda b, i: (b, 0, PB_BK // B_WIDTH)),
                  pl.BlockSpec((1, seq, B_WIDTH), lambda b, i: (b, 0, PB_BV // B_WIDTH)),
                  pl.BlockSpec((1, Q_BLOCK, 256), lambda b, i: (b, i, PB_IQ // 256)),
                  pl.BlockSpec((1, seq, LANES), lambda b, i: (b, 0, PB_IK // LANES)),
                  pl.BlockSpec((1, Q_BLOCK, LANES), lambda b, i: (b, i, PF_MISC // LANES)),
                  pl.BlockSpec(tri.shape, lambda b, i: (0, 0))],
        out_specs=pl.BlockSpec((1, Q_BLOCK, B_WIDTH), lambda b, i: (b, i, 0)),
        out_shape=jax.ShapeDtypeStruct((batch, seq, B_WIDTH), BF16),
        scratch_shapes=[pltpu.VMEM((Q_BLOCK, seq), I32), pltpu.VMEM((Q_BLOCK, seq), F32)],
        compiler_params=_params("parallel", "arbitrary", vmem_mb=48),
        name="mixer_b",
    )(pb, pb, pb, pb, pb, pf, tri)


def _head_consts():
    lane = np.arange(C_WIDTH)
    same = (lane[:, None] // HEAD_DIM == lane[None, :] // HEAD_DIM).astype(np.float32)
    return jnp.asarray(same)


def _rwkv1_kernel(cin_ref, prev_ref, vf_ref, mu_ref, vec_ref, lora_ref, vl_ref, bd_ref,
                  phi_ref, psi_ref, rp_ref, y0_ref, g_ref, bonus_ref, v_ref, *, rows, use_v_lora):
    i = pl.program_id(1)
    pc = cin_ref[0][:, :C_IN]
    prev = jnp.where(i > 0, prev_ref[0][SUBLANES - 1:SUBLANES, :C_IN], 0.0)
    row = lax.broadcasted_iota(I32, (rows, 1), 0)
    shifted = jnp.where(row == 0, prev, pltpu.roll(pc, 1, 0))
    pc = pc + (shifted - pc) * mu_ref[...]
    r = pc[:, 0:C_WIDTH]
    k = pc[:, C_WIDTH:2 * C_WIDTH]
    v = pc[:, 2 * C_WIDTH:3 * C_WIDTH]
    xl = pc[:, 3 * C_WIDTH:C_IN]
    lane = lax.broadcasted_iota(I32, xl.shape, 1)
    feat = jnp.where(lane < C_LORA_W, jnp.tanh(xl), jnp.where(lane < C_LORA_W + C_LORA_A, xl, _sigmoid(xl)))
    w0, a0, k_k, k_a, r_k = (vec_ref[n:n + 1, :] for n in range(5))
    bd = bd_ref[...]
    w_raw = -_softplus(-(w0 + _doth(feat, lora_ref[0]))) - 0.5
    a = _sigmoid(a0 + _doth(feat, lora_ref[1]))
    g_ref[0] = _doth(feat, lora_ref[2])
    if use_v_lora:
        v0 = vec_ref[5:6, :]
        v = v + (vf_ref[0] - v) * _sigmoid(v0 + _doth(_doth(v, vl_ref[0]), vl_ref[1]))
    v_ref[0] = v
    kk = k * k_k
    kk = kk / jnp.maximum(jnp.sqrt(_doth(kk * kk, bd)), 1e-12)
    k = k * (1.0 + (a - 1.0) * k_a)
    logw = -jnp.exp(w_raw)
    bonus_ref[0] = _doth(r * k * r_k, bd) * v
    av = -kk
    bv = kk * a

    cc = C_CHUNK
    ri = lax.broadcasted_iota(I32, (cc, cc), 0)
    ci = lax.broadcasted_iota(I32, (cc, cc), 1)
    tril = jnp.where(ci <= ri, 1.0, 0.0)
    strict = ci < ri
    incl = ci <= ri
    eye_c = jnp.where(ci == ri, 1.0, 0.0)
    lane_w = lax.broadcasted_iota(I32, (1, C_WIDTH), 1)
    ri2 = lax.broadcasted_iota(I32, (C_WIDTH, C_WIDTH), 0)
    ci2 = lax.broadcasted_iota(I32, (C_WIDTH, C_WIDTH), 1)
    eye_w = jnp.where(ri2 == ci2, 1.0, 0.0)
    for c in range(rows // cc):
        sl = slice(c * cc, (c + 1) * cc)
        lw, rc, kc, vc, ac, bc = logw[sl], r[sl], k[sl], v[sl], av[sl], bv[sl]
        cs = _doth(tril, lw)
        cs_end = cs[cc - 1:cc, :]
        at = ac * jnp.exp(cs - lw)
        rt = rc * jnp.exp(cs)
        inv = jnp.exp(-cs)
        rhs = jnp.concatenate([bc * inv, kc * inv], axis=0)
        tail = jnp.exp(cs_end - cs)
        bh = bc * tail
        kh = kc * tail
        ap = jnp.zeros((cc, C_WIDTH), F32)
        w2 = jnp.zeros((cc, C_WIDTH), F32)
        rp = rt
        y0 = jnp.zeros((cc, C_WIDTH), F32)
        for h in range(C_HEADS):
            mh = jnp.where(lane_w // HEAD_DIM == h, 1.0, 0.0)
            ath = at * mh
            aa = _dot_nt(jnp.concatenate([ath, rt * mh], axis=0), rhs, HIGHEST)
            a_ab = jnp.where(strict, aa[:cc, :cc], 0.0)
            a_ak = jnp.where(strict, aa[:cc, cc:], 0.0)
            a_rb = jnp.where(incl, aa[cc:, :cc], 0.0)
            a_rk = jnp.where(incl, aa[cc:, cc:], 0.0)
            x = eye_c + a_ab
            pw = a_ab
            for _ in range(int(math.log2(cc)) - 1):
                pw = _doth(pw, pw)
                x = x + _doth(x, pw)
            ap_h = _doth(x, ath)
            w2_h = _doth(x, _doth(a_ak, vc)) * mh
            ap = ap + ap_h
            w2 = w2 + w2_h
            rp = rp + _doth(a_rb, ap_h)
            y0 = y0 + (_doth(a_rb, w2_h) + _doth(a_rk, vc)) * mh
        rp_ref[0, sl, :] = rp
        y0_ref[0, sl, :] = y0
        phi_ref[0, c] = eye_w * jnp.exp(cs_end) + _doth(bh.T, ap) * bd
        psi_ref[0, c] = _doth(jnp.concatenate([bh, kh], axis=0).T, jnp.concatenate([w2, vc], axis=0)) * bd


def _rwkv2_kernel(phi_ref, psi_ref, rp_ref, y0_ref, g_ref, bonus_ref, gn_ref, bd_ref, o_ref, s_ref):
    @pl.when(pl.program_id(1) == 0)
    def _():
        s_ref[...] = jnp.zeros_like(s_ref)

    s = s_ref[...]
    y = _doth(rp_ref[0], s) + y0_ref[0]
    s_ref[...] = _doth(phi_ref[0, 0], s) + psi_ref[0, 0]
    bd = bd_ref[...] * (1.0 / HEAD_DIM)
    mu = _doth(y, bd)
    d = y - mu
    var = _doth(d * d, bd)
    yn = d * lax.rsqrt(var + C_GN_EPS) * gn_ref[0:1, :] + gn_ref[1:2, :]
    o_ref[0] = ((yn + bonus_ref[0]) * g_ref[0]).astype(o_ref.dtype)


def _mixer_c(pf, v_first, mu, vec, lora, vl, gn, bd, batch, seq, use_v_lora):
    rows = 128
    nblk = seq // rows
    nch = seq // C_CHUNK
    cpb = rows // C_CHUNK
    kern = functools.partial(_rwkv1_kernel, rows=rows, use_v_lora=use_v_lora)
    full2 = lambda a: pl.BlockSpec(a.shape, lambda b, i: (0,) * a.ndim)
    seq_spec = pl.BlockSpec((1, rows, C_WIDTH), lambda b, i: (b, i, 0))
    mat_spec = pl.BlockSpec((1, cpb, C_WIDTH, C_WIDTH), lambda b, i: (b, i, 0, 0))
    seq_shape = jax.ShapeDtypeStruct((batch, seq, C_WIDTH), F32)
    mat_shape = jax.ShapeDtypeStruct((batch, nch, C_WIDTH, C_WIDTH), F32)
    prev_blk = rows // SUBLANES
    phi, psi, rp, y0, g, bonus, v = pl.pallas_call(
        kern,
        grid=(batch, nblk),
        in_specs=[pl.BlockSpec((1, rows, 1024), lambda b, i: (b, i, PF_CIN // 1024)),
                  pl.BlockSpec((1, SUBLANES, 1024), lambda b, i: (b, jnp.maximum(i * prev_blk - 1, 0), PF_CIN // 1024)),
                  seq_spec, full2(mu), full2(vec), full2(lora), full2(vl), full2(bd)],
        out_specs=[mat_spec, mat_spec, seq_spec, seq_spec, seq_spec, seq_spec, seq_spec],
        out_shape=[mat_shape, mat_shape, seq_shape, seq_shape, seq_shape, seq_shape, seq_shape],
        compiler_params=_params("parallel", "parallel", vmem_mb=48),
        name="rwkv_chunks",
    )(pf, pf, v_first, mu, vec, lora, vl, bd)
    cseq = pl.BlockSpec((1, C_CHUNK, C_WIDTH), lambda b, c: (b, c, 0))
    cmat = pl.BlockSpec((1, 1, C_WIDTH, C_WIDTH), lambda b, c: (b, c, 0, 0))
    o = pl.pallas_call(
        _rwkv2_kernel,
        grid=(batch, nch),
        in_specs=[cmat, cmat, cseq, cseq, cseq, cseq,
                  pl.BlockSpec(gn.shape, lambda b, c: (0, 0)), pl.BlockSpec(bd.shape, lambda b, c: (0, 0))],
        out_specs=cseq,
        out_shape=jax.ShapeDtypeStruct((batch, seq, C_WIDTH), BF16),
        scratch_shapes=[pltpu.VMEM((C_WIDTH, C_WIDTH), F32)],
        compiler_params=_params("parallel", "arbitrary"),
        name="rwkv_scan",
    )(phi, psi, rp, y0, g, bonus, gn, bd)
    return o, v


def _ssd_consts():
    expand = np.zeros((LANES, D_INNER), np.float32)
    for h in range(D_HEADS):
        expand[MISC_DT + h, h * HEAD_DIM:(h + 1) * HEAD_DIM] = 1.0
    return jnp.asarray(expand)


def _ssd_kernel(xbc_ref, z_ref, misc_ref, conv_ref, vec_ref, hp_ref, expand_ref, bd_ref, o_ref, st_ref, prev_ref):
    @pl.when(pl.program_id(1) == 0)
    def _():
        st_ref[...] = jnp.zeros_like(st_ref)
        prev_ref[...] = jnp.zeros_like(prev_ref)

    q = D_CHUNK
    x_raw = xbc_ref[0]
    ext = jnp.concatenate([prev_ref[...], x_raw], axis=0)
    conv = jnp.zeros((q, D_XBC), F32)
    for t in range(D_CONV):
        lo = SUBLANES - (D_CONV - 1) + t
        conv = conv + ext[lo:lo + q, :] * conv_ref[t:t + 1, :]
    prev_ref[...] = x_raw[q - SUBLANES:, :]
    xbc = _silu(conv + conv_ref[D_CONV:D_CONV + 1, :])
    xs = xbc[:, :D_INNER]
    bm = xbc[:, D_INNER:D_INNER + D_GROUPS * D_STATE]
    cm = xbc[:, D_INNER + D_GROUPS * D_STATE:]

    dt_col = _softplus(misc_ref[0] + hp_ref[0:1, :])
    a_col = dt_col * hp_ref[1:2, :]
    ri = lax.broadcasted_iota(I32, (q, q), 0)
    ci = lax.broadcasted_iota(I32, (q, q), 1)
    causal = ci <= ri
    tril = jnp.where(causal, 1.0, 0.0)
    acs_col = _doth(tril, a_col)
    acs_row = acs_col.T
    expand = expand_ref[...]
    acs = _doth(acs_col, expand)
    dt = _doth(dt_col, expand)
    acs_end = acs[q - 1:q, :]
    xdt = xs * dt

    lane = lax.broadcasted_iota(I32, (1, LANES), 1)
    lane_w = lax.broadcasted_iota(I32, (1, D_INNER), 1)
    left = lane < D_STATE
    bm_sw = pltpu.roll(bm, D_STATE, 1)
    cm_sw = pltpu.roll(cm, D_STATE, 1)
    b_exp = jnp.concatenate([jnp.where(left, bm, bm_sw), jnp.where(left, bm_sw, bm)], axis=1)
    c_exp = jnp.concatenate([jnp.where(left, cm, cm_sw), jnp.where(left, cm_sw, cm)], axis=1)
    cb = [_dot_nt(jnp.where(left == (g == 0), cm, 0.0), bm, HIGHEST) for g in range(D_GROUPS)]

    y = jnp.zeros((q, D_INNER), F32)
    for h in range(D_HEADS):
        col = acs_col[:, MISC_DT + h:MISC_DT + h + 1]
        rw = acs_row[MISC_DT + h:MISC_DT + h + 1, :]
        decay = jnp.exp(jnp.where(causal, col - rw, NEG_INF))
        scores = cb[h // (D_HEADS // D_GROUPS)] * decay
        y = y + jnp.where(lane_w // HEAD_DIM == h, _doth(scores, xdt), 0.0)
    st = st_ref[...]
    y = y + _doth(c_exp, st) * jnp.exp(acs)
    st_ref[...] = st * jnp.exp(acs_end) + _doth(b_exp.T, xdt * jnp.exp(acs_end - acs)) * bd_ref[...]
    y = y + xs * vec_ref[0:1, :]
    y = y * _silu(z_ref[0])
    half = D_INNER // D_GROUPS
    outs = []
    for g in range(D_GROUPS):
        yg = y[:, g * half:(g + 1) * half]
        outs.append(yg * lax.rsqrt(jnp.mean(yg * yg, axis=-1, keepdims=True) + D_NORM_EPS))
    o_ref[0] = (jnp.concatenate(outs, axis=1) * vec_ref[1:2, :]).astype(o_ref.dtype)


def _mixer_d(pf, conv, vec, hp, expand, bd, batch, seq):
    q = D_CHUNK
    full = lambda a: pl.BlockSpec(a.shape, lambda b, i: (0, 0))
    return pl.pallas_call(
        _ssd_kernel,
        grid=(batch, seq // q),
        in_specs=[pl.BlockSpec((1, q, D_XBC), lambda b, i: (b, i, PF_XBC // D_XBC)),
                  pl.BlockSpec((1, q, D_INNER), lambda b, i: (b, i, PF_Z // D_INNER)),
                  pl.BlockSpec((1, q, LANES), lambda b, i: (b, i, PF_MISC // LANES)),
                  full(conv), full(vec), full(hp), full(expand), full(bd)],
        out_specs=pl.BlockSpec((1, q, D_INNER), lambda b, i: (b, i, 0)),
        out_shape=jax.ShapeDtypeStruct((batch, seq, D_INNER), BF16),
        scratch_shapes=[pltpu.VMEM((D_INNER, D_INNER), F32), pltpu.VMEM((SUBLANES, D_XBC), F32)],
        compiler_params=_params("parallel", "arbitrary"),
        name="ssd",
    )(pf, pf, pf, conv, vec, hp, expand, bd)


def _merge_kernel(x_ref, gates_ref, oa_ref, ob_ref, oc_ref, od_ref, wbr_ref, wout_ref, ln_ref, wr_ref, br_ref,
                  x1_ref, route_ref):
    acc = None
    for n, o_ref in enumerate((oa_ref, ob_ref, oc_ref, od_ref)):
        term = _sigmoid(gates_ref[:, n * D_MODEL:(n + 1) * D_MODEL]) * _dot(o_ref[...], wbr_ref[n])
        acc = term if acc is None else acc + term
    h = DEEPNORM_ALPHA * x_ref[...] + _dot(acc.astype(BF16), wout_ref[...])
    x1 = _layer_norm(h, ln_ref[0:1, :], ln_ref[1:2, :])
    x1_ref[...] = x1

    logits = _doth(x1, wr_ref[...]) + br_ref[...]
    lane = lax.broadcasted_iota(I32, logits.shape, 1)
    big = jnp.int32(LANES)
    gl = jnp.where(lane < N_EXPERT_GROUPS, logits, NEG_INF)
    gm = jnp.max(gl, axis=-1, keepdims=True)
    pg_top = 1.0 / jnp.sum(jnp.exp(gl - gm), axis=-1, keepdims=True)
    g_sel = jnp.min(jnp.where(gl == gm, lane, big), axis=-1, keepdims=True)
    off = N_EXPERT_GROUPS + g_sel * EXPERTS_PER_GROUP
    el = jnp.where((lane >= off) & (lane < off + EXPERTS_PER_GROUP), logits, NEG_INF)
    em = jnp.max(el, axis=-1, keepdims=True)
    es = jnp.sum(jnp.exp(el - em), axis=-1, keepdims=True)
    idx1 = jnp.min(jnp.where(el == em, lane, big), axis=-1, keepdims=True)
    el2 = jnp.where(lane == idx1, NEG_INF, el)
    em2 = jnp.max(el2, axis=-1, keepdims=True)
    idx2 = jnp.min(jnp.where(el2 == em2, lane, big), axis=-1, keepdims=True)
    p1 = 1.0 / es
    p2 = jnp.exp(em2 - em) / es
    gate1 = pg_top * p1 / (p1 + p2)
    gate2 = pg_top * p2 / (p1 + p2)
    e1 = (idx1 - N_EXPERT_GROUPS).astype(F32)
    e2 = (idx2 - N_EXPERT_GROUPS).astype(F32)
    route_ref[...] = jnp.where(lane == 0, e1, jnp.where(lane == 1, e2, jnp.where(lane == 2, gate1,
                               jnp.where(lane == 3, gate2, 0.0))))


def _merge(x2d, pf, oa, ob, oc, od, wbr, wout, ln, wr, br, tm):
    m = x2d.shape[0]
    row = lambda w: pl.BlockSpec((tm, w), lambda i: (i, 0))
    full = lambda a: pl.BlockSpec(a.shape, lambda i: (0,) * a.ndim)
    return pl.pallas_call(
        _merge_kernel,
        grid=(m // tm,),
        in_specs=[row(D_MODEL), pl.BlockSpec((tm, N_BRANCH * D_MODEL), lambda i: (i, PF_GATES)),
                  row(A_WIDTH), row(B_WIDTH), row(C_WIDTH), row(D_INNER),
                  full(wbr), full(wout), full(ln), full(wr), full(br)],
        out_specs=[row(D_MODEL), row(LANES)],
        out_shape=[jax.ShapeDtypeStruct((m, D_MODEL), F32), jax.ShapeDtypeStruct((m, LANES), F32)],
        compiler_params=_params("parallel", vmem_mb=48),
        name="merge_route",
    )(x2d, pf, oa, ob, oc, od, wbr, wout, ln, wr, br)


def _ffn_kernel(be_ref, nv_ref, xs_ref, wg_ref, wu_ref, wd_ref, o_ref):
    i = pl.program_id(0)

    @pl.when(i < nv_ref[0])
    def _():
        xb = xs_ref[...]
        hg = _dot(xb, wg_ref[0, 0].astype(BF16))
        hu = _dot(xb, wu_ref[0, 0].astype(BF16))
        o_ref[...] = _dot((_silu(hg) * hu).astype(BF16), wd_ref[0, 0].astype(BF16))

    @pl.when(i >= nv_ref[0])
    def _():
        o_ref[...] = jnp.zeros_like(o_ref)


def _ffn(blk_expert, n_valid, xs, e_gate, e_up, e_down, layer):
    cap = xs.shape[0]
    wspec = lambda a: pl.BlockSpec((1, 1) + a.shape[2:], lambda i, be, nv: (layer, be[i], 0, 0))
    return pl.pallas_call(
        _ffn_kernel,
        grid_spec=pltpu.PrefetchScalarGridSpec(
            num_scalar_prefetch=2,
            grid=(cap // MOE_ROWS,),
            in_specs=[pl.BlockSpec((MOE_ROWS, D_MODEL), lambda i, be, nv: (i, 0)),
                      wspec(e_gate), wspec(e_up), wspec(e_down)],
            out_specs=pl.BlockSpec((MOE_ROWS, D_MODEL), lambda i, be, nv: (i, 0)),
        ),
        out_shape=jax.ShapeDtypeStruct((cap, D_MODEL), F32),
        compiler_params=_params("arbitrary", vmem_mb=48),
        name="expert_ffn",
    )(blk_expert, n_valid, xs, e_gate, e_up, e_down)


def _combine_kernel(x_ref, y0_ref, y1_ref, route_ref, ln_ref, o_ref):
    g0 = route_ref[:, 2:3]
    g1 = route_ref[:, 3:4]
    h = DEEPNORM_ALPHA * x_ref[...] + (y0_ref[...] * g0 + y1_ref[...] * g1)
    o_ref[...] = _layer_norm(h, ln_ref[0:1, :], ln_ref[1:2, :])


def _combine(x1, ysel, route, ln, tm):
    m = x1.shape[0]
    return pl.pallas_call(
        _combine_kernel,
        grid=(m // tm,),
        in_specs=[pl.BlockSpec((tm, D_MODEL), lambda i: (i, 0)),
                  pl.BlockSpec((tm, D_MODEL), lambda i: (i, 0)),
                  pl.BlockSpec((tm, D_MODEL), lambda i: (i, 1)),
                  pl.BlockSpec((tm, LANES), lambda i: (i, 0)),
                  pl.BlockSpec(ln.shape, lambda i: (0, 0))],
        out_specs=pl.BlockSpec((tm, D_MODEL), lambda i: (i, 0)),
        out_shape=jax.ShapeDtypeStruct((m, D_MODEL), F32),
        compiler_params=_params("parallel"),
        name="combine",
    )(x1, ysel, ysel, route, ln)


def _dispatch_tables(route, m):
    flat_e = route[:, 0:TOP_K_EXPERTS].astype(I32).reshape(-1)
    n_assign = m * TOP_K_EXPERTS
    onehot = (flat_e[:, None] == jnp.arange(N_EXPERTS, dtype=I32)[None, :]).astype(I32)
    csum = jnp.cumsum(onehot, axis=0)
    rank = jnp.sum(csum * onehot, axis=1) - 1
    counts = csum[-1]
    padded = (counts + MOE_ROWS - 1) // MOE_ROWS * MOE_ROWS
    pad_end = jnp.cumsum(padded)
    pad_start = pad_end - padded
    dest = pad_start[flat_e] + rank
    cap = (n_assign + N_EXPERTS * (MOE_ROWS - 1) + MOE_ROWS - 1) // MOE_ROWS * MOE_ROWS
    n_blocks = cap // MOE_ROWS
    blk_expert = jnp.minimum(jnp.searchsorted(pad_end, jnp.arange(n_blocks, dtype=I32) * MOE_ROWS, side="right"),
                             N_EXPERTS - 1).astype(I32)
    buf_tok = jnp.zeros((cap,), I32).at[dest].set(jnp.arange(n_assign, dtype=I32) // TOP_K_EXPERTS)
    n_valid = (pad_end[-1] // MOE_ROWS).astype(I32).reshape(1)
    return dest, buf_tok, blk_expert, n_valid


def _row_pad(a, rows):
    return jnp.pad(a, ((0, rows - a.shape[0]), (0, 0)))


def kernel(x, w_in, c_mu, c_w0, c_w2, c_a0, c_a2, c_g2, c_kk, c_ka, c_rk, c_gn_w, c_gn_b, c_v0, c_v1, c_v2,
           d_conv_w, d_conv_b, d_dt_bias, d_a_log, d_skip, d_norm_w, w_branch, w_out, ln1_g, ln1_b,
           r_group, r_group_b, r_expert, r_expert_b, e_gate, e_up, e_down, ln2_g, ln2_b):
    batch, seq, _ = x.shape
    m = batch * seq
    biases = _mixa_bias()
    tw = min(256, seq)
    tri = jnp.asarray(np.triu(np.ones((tw, tw), np.float32))).astype(BF16)
    bd = _head_consts()
    expand = _ssd_consts()
    x2d = x.reshape(m, D_MODEL)
    v_first = jnp.zeros((batch, seq, C_WIDTH), F32)
    tm_proj = min(1024, m)
    tm_tok = min(512, m)
    for l in range(DEPTH):
        wb, wf = _proj_weights(w_in[l])
        pb = _proj(x2d, wb, BF16, tm_proj, 256).reshape(batch, seq, PB_WIDTH)
        pf = _proj(x2d, wf, F32, tm_proj, 256).reshape(batch, seq, PF_WIDTH)

        kv_pad = jnp.pad(pb[:, :, PB_AKV:PB_AQ], ((0, 0), (A_MAXWIN, 0), (0, 0)))
        o_a = _mixer_a(pb, kv_pad, biases, batch, seq)
        o_b = _mixer_b(pb, pf, tri, batch, seq)

        use_v_lora = l > 0
        vec_rows = [c_w0[l], c_a0[l], c_kk[l], c_ka[l], c_rk[l].reshape(-1)]
        vec_rows.append(c_v0[l - 1] if use_v_lora else jnp.zeros((C_WIDTH,), F32))
        vec = _row_pad(jnp.stack(vec_rows), SUBLANES)
        lora = jnp.stack([
            jnp.pad(c_w2[l], ((0, LANES - C_LORA_W), (0, 0))),
            jnp.pad(c_a2[l], ((C_LORA_W, LANES - C_LORA_W - C_LORA_A), (0, 0))),
            jnp.pad(c_g2[l], ((C_LORA_W + C_LORA_A, 0), (0, 0)))])
        if use_v_lora:
            vl = jnp.stack([jnp.pad(c_v1[l - 1], ((0, 0), (0, C_WIDTH - C_LORA_V))),
                            jnp.pad(c_v2[l - 1], ((0, C_WIDTH - C_LORA_V), (0, 0)))])
        else:
            vl = jnp.zeros((2, C_WIDTH, C_WIDTH), F32)
        gn = _row_pad(jnp.stack([c_gn_w[l], c_gn_b[l]]), SUBLANES)
        o_c, v_c = _mixer_c(pf, v_first, c_mu[l].reshape(1, C_IN), vec, lora, vl, gn, bd, batch, seq, use_v_lora)
        if l == 0:
            v_first = v_c

        conv = _row_pad(jnp.concatenate([d_conv_w[l], d_conv_b[l][None, :]], axis=0), SUBLANES)
        dvec = _row_pad(jnp.stack([jnp.repeat(d_skip[l], HEAD_DIM), d_norm_w[l]]), SUBLANES)
        place = lambda a: jnp.pad(a, (MISC_DT, LANES - MISC_DT - D_HEADS))
        hp = _row_pad(jnp.stack([place(d_dt_bias[l]), place(-jnp.exp(d_a_log[l]))]), SUBLANES)
        o_d = _mixer_d(pf, conv, dvec, hp, expand, bd, batch, seq)

        wr = jnp.pad(jnp.concatenate([r_group[l], r_expert[l]], axis=1),
                     ((0, 0), (0, LANES - N_EXPERT_GROUPS - N_EXPERTS)))
        br = jnp.pad(jnp.concatenate([r_group_b[l], r_expert_b[l]]), (0, LANES - N_EXPERT_GROUPS - N_EXPERTS))
        ln1 = _row_pad(jnp.stack([ln1_g[l], ln1_b[l]]), SUBLANES)
        x1, route = _merge(x2d, pf.reshape(m, PF_WIDTH), o_a.reshape(m, -1), o_b.reshape(m, -1),
                           o_c.reshape(m, -1), o_d.reshape(m, -1), w_branch[l].astype(BF16),
                           w_out[l].astype(BF16), ln1, wr, br.reshape(1, LANES), tm_tok)

        dest, buf_tok, blk_expert, n_valid = _dispatch_tables(route, m)
        xs = jnp.take(x1.astype(BF16), buf_tok, axis=0)
        yb = _ffn(blk_expert, n_valid, xs, e_gate, e_up, e_down, l)
        ysel = jnp.take(yb, dest, axis=0).reshape(m, TOP_K_EXPERTS * D_MODEL)
        ln2 = _row_pad(jnp.stack([ln2_g[l], ln2_b[l]]), SUBLANES)
        x2d = _combine(x1, ysel, route, ln2, tm_tok)
    return x2d.reshape(batch, seq, D_MODEL)
```

```python
import functools
import math

import jax
import jax.numpy as jnp
import numpy as np
from jax import lax
from jax.experimental import pallas as pl
from jax.experimental.pallas import tpu as pltpu

F32 = jnp.float32
BF16 = jnp.bfloat16
I32 = jnp.int32
HIGHEST = lax.Precision.HIGHEST
NEG_INF = float("-inf")
INT_MIN = -(2 ** 31)

LANES = 128
SUBLANES = 8

D_MODEL = 1024
DEPTH = 2
HEAD_DIM = 64
Q_BLOCK = 128

A_HEADS = 4
A_PATTERNS = ((128, 1), (512, 4), (2048, 16))
A_GROUPS = len(A_PATTERNS)
A_WIDTH = A_HEADS * HEAD_DIM
A_MAXWIN = max(w for w, _ in A_PATTERNS)

B_HEADS = 4
B_WIDTH = B_HEADS * HEAD_DIM
IDX_HEADS = 4
IDX_DIM = 64
TOPK_MAX = 256

C_HEADS = 4
C_WIDTH = C_HEADS * HEAD_DIM
C_LORA_W = 32
C_LORA_A = 32
C_LORA_G = 64
C_LORA_V = 16
C_IN = 3 * C_WIDTH + C_LORA_W + C_LORA_A + C_LORA_G
C_GN_EPS = 64e-5
C_CHUNK = 64

D_HEADS = 4
D_INNER = D_HEADS * HEAD_DIM
D_GROUPS = 2
D_STATE = 64
D_CONV = 4
D_CHUNK = 128
D_XBC = D_INNER + 2 * D_GROUPS * D_STATE
D_NORM_EPS = 1e-5

N_BRANCH = 4
N_EXPERT_GROUPS = 4
EXPERTS_PER_GROUP = 8
N_EXPERTS = N_EXPERT_GROUPS * EXPERTS_PER_GROUP
TOP_K_EXPERTS = 2
D_EXPERT = 512
MOE_ROWS = 256

LN_EPS = 1e-5
DEEPNORM_ALPHA = (2 * DEPTH) ** 0.25

PB_AKV = 0
PB_AQ = PB_AKV + A_HEADS * A_GROUPS * 2 * HEAD_DIM
PB_BQ = PB_AQ + A_HEADS * 4 * HEAD_DIM
PB_BK = PB_BQ + B_WIDTH
PB_BV = PB_BK + B_WIDTH
PB_IQ = PB_BV + B_WIDTH
PB_IK = PB_IQ + IDX_HEADS * IDX_DIM
PB_WIDTH = PB_IK + 256
PF_GATES = 0
PF_CIN = PF_GATES + N_BRANCH * D_MODEL
PF_MISC = PF_CIN + C_IN
PF_XBC = PF_CIN + 1024
PF_Z = PF_XBC + D_XBC
PF_WIDTH = PF_Z + D_INNER
MISC_DT = 4


def _dot(a, b, precision=None):
    return jnp.dot(a, b, preferred_element_type=F32, precision=precision)


def _dot_nt(a, b, precision=None):
    return lax.dot_general(a, b, (((1,), (1,)), ((), ())), preferred_element_type=F32, precision=precision)


def _doth(a, b):
    return _dot(a, b, HIGHEST)


def _split2(a):
    hi = a.astype(BF16)
    return hi, (a - hi.astype(F32)).astype(BF16)


def _dot3(a, b, nt=False):
    mm = _dot_nt if nt else _dot
    ah, al = _split2(a)
    bh, bl = _split2(b)
    return mm(ah, bh) + (mm(ah, bl) + mm(al, bh))


def _dot_sel(sel, x, left=True):
    hi = x.astype(BF16)
    r1 = x - hi.astype(F32)
    mid = r1.astype(BF16)
    lo = (r1 - mid.astype(F32)).astype(BF16)
    if left:
        return _dot(sel, hi) + (_dot(sel, mid) + _dot(sel, lo))
    return _dot(hi, sel) + (_dot(mid, sel) + _dot(lo, sel))


def _sigmoid(x):
    return 1.0 / (1.0 + jnp.exp(-x))


def _softplus(x):
    return jnp.maximum(x, 0.0) + jnp.log1p(jnp.exp(-jnp.abs(x)))


def _silu(x):
    return x * _sigmoid(x)


def _layer_norm(h, g, b):
    mu = jnp.mean(h, axis=-1, keepdims=True)
    d = h - mu
    var = jnp.mean(d * d, axis=-1, keepdims=True)
    return d * lax.rsqrt(var + LN_EPS) * g + b


def _params(*sem, vmem_mb=None):
    kw = {}
    if vmem_mb is not None:
        kw["vmem_limit_bytes"] = vmem_mb * 1024 * 1024
    return pltpu.CompilerParams(dimension_semantics=sem, **kw)


def _proj_kernel(x_ref, w_ref, o_ref):
    o_ref[...] = _dot(x_ref[...].astype(BF16), w_ref[...]).astype(o_ref.dtype)


def _proj(x2d, w, out_dtype, tm, tn):
    m, k = x2d.shape
    n = w.shape[1]
    return pl.pallas_call(
        _proj_kernel,
        grid=(m // tm, n // tn),
        in_specs=[pl.BlockSpec((tm, k), lambda i, j: (i, 0)),
                  pl.BlockSpec((k, tn), lambda i, j: (0, j))],
        out_specs=pl.BlockSpec((tm, tn), lambda i, j: (i, j)),
        out_shape=jax.ShapeDtypeStruct((m, n), out_dtype),
        compiler_params=_params("parallel", "arbitrary", vmem_mb=48),
        name="in_proj",
    )(x2d, w)


def _proj_weights(w):
    k = w.shape[0]
    off = 0
    seg = {}
    for name, size in (("a_qkv", 3 * A_GROUPS * A_WIDTH), ("b_qkv", 3 * B_WIDTH), ("b_idx_q", IDX_HEADS * IDX_DIM),
                       ("b_idx_k", IDX_DIM), ("b_idx_w", IDX_HEADS), ("c_in", C_IN), ("d_z", D_INNER),
                       ("d_xbc", D_XBC), ("d_dt", D_HEADS), ("gates", N_BRANCH * D_MODEL)):
        seg[name] = w[:, off:off + size]
        off += size
    a = seg["a_qkv"].reshape(k, 3, A_GROUPS, A_HEADS, HEAD_DIM)
    a_kv = jnp.transpose(a[:, 1:3], (0, 3, 2, 1, 4)).reshape(k, -1)
    a_q = jnp.transpose(a[:, 0], (0, 2, 1, 3))
    a_q = jnp.pad(a_q, ((0, 0), (0, 0), (0, 4 - A_GROUPS), (0, 0))).reshape(k, -1)
    zeros = lambda n: jnp.zeros((k, n), w.dtype)
    wb = jnp.concatenate([a_kv, a_q, seg["b_qkv"], seg["b_idx_q"], seg["b_idx_k"], zeros(256 - IDX_DIM)], axis=1)
    misc = jnp.concatenate([seg["b_idx_w"], seg["d_dt"], zeros(LANES - IDX_HEADS - D_HEADS)], axis=1)
    wf = jnp.concatenate([seg["gates"], seg["c_in"], misc, seg["d_xbc"], seg["d_z"]], axis=1)
    assert wb.shape[1] == PB_WIDTH and wf.shape[1] == PF_WIDTH
    return wb.astype(BF16), wf.astype(BF16)


def _mixa_bias():
    out = []
    for win, dil in A_PATTERNS:
        r = np.arange(Q_BLOCK)[:, None]
        c = np.arange(win + Q_BLOCK)[None, :]
        d = r + win - c
        ok = (d >= 0) & (d <= win) & (d % dil == 0)
        out.append(jnp.asarray(np.where(ok, 0.0, -np.inf).astype(np.float32)))
    return out


def _mixa_kernel(q_ref, kv_ref, b0_ref, b1_ref, b2_ref, o_ref):
    start = pl.program_id(2) * Q_BLOCK
    scale = HEAD_DIM ** -0.5
    bias_refs = (b0_ref, b1_ref, b2_ref)
    outs = []
    for hh in range(2):
        ms, ls, os_ = [], [], []
        for g, (win, _) in enumerate(A_PATTERNS):
            span = win + Q_BLOCK
            q = q_ref[0, :, hh * 256 + g * HEAD_DIM: hh * 256 + (g + 1) * HEAD_DIM]
            base = pl.multiple_of(start + (A_MAXWIN - win), Q_BLOCK)
            c0 = hh * (A_GROUPS * 2 * HEAD_DIM) + g * 2 * HEAD_DIM
            k = kv_ref[0, pl.ds(base, span), c0:c0 + HEAD_DIM]
            v = kv_ref[0, pl.ds(base, span), c0 + HEAD_DIM:c0 + 2 * HEAD_DIM]
            s = _dot_nt(q, k) * scale + bias_refs[g][...]
            col = lax.broadcasted_iota(I32, (Q_BLOCK, span), 1)
            s = jnp.where(col >= win - start, s, NEG_INF)
            m = jnp.max(s, axis=-1, keepdims=True)
            p = jnp.exp(s - m)
            ls.append(jnp.sum(p, axis=-1, keepdims=True))
            os_.append(_dot(p.astype(BF16), v))
            ms.append(m)
        m_all = jnp.maximum(jnp.maximum(ms[0], ms[1]), ms[2])
        num = jnp.zeros((Q_BLOCK, HEAD_DIM), F32)
        den = jnp.zeros((Q_BLOCK, 1), F32)
        for g in range(A_GROUPS):
            wg = jnp.exp(ms[g] - m_all)
            num = num + wg * os_[g]
            den = den + wg * ls[g]
        outs.append(num / den)
    o_ref[0] = jnp.concatenate(outs, axis=1).astype(o_ref.dtype)


def _mixer_a(pb, kv_pad, biases, batch, seq):
    nblk = seq // Q_BLOCK
    pair_kv = 2 * A_GROUPS * 2 * HEAD_DIM
    in_specs = [pl.BlockSpec((1, Q_BLOCK, 512), lambda b, hp, i: (b, i, PB_AQ // 512 + hp)),
                pl.BlockSpec((1, seq + A_MAXWIN, pair_kv), lambda b, hp, i: (b, 0, hp))]
    for bias in biases:
        in_specs.append(pl.BlockSpec(bias.shape, lambda b, hp, i: (0, 0)))
    return pl.pallas_call(
        _mixa_kernel,
        grid=(batch, A_HEADS // 2, nblk),
        in_specs=in_specs,
        out_specs=pl.BlockSpec((1, Q_BLOCK, 2 * HEAD_DIM), lambda b, hp, i: (b, i, hp)),
        out_shape=jax.ShapeDtypeStruct((batch, seq, A_WIDTH), BF16),
        compiler_params=_params("parallel", "parallel", "arbitrary", vmem_mb=48),
        name="mixer_a",
    )(pb, kv_pad, *biases)


def _mixb_kernel(qb_ref, kb_ref, vb_ref, iq_ref, ik_ref, misc_ref, tri_ref, o_ref, keys_ref, selb_ref, *, top_k):
    start = pl.program_id(1) * Q_BLOCK
    ch = keys_ref.shape[2]
    nch = (start + Q_BLOCK + ch - 1) // ch
    qpos = start + lax.broadcasted_iota(I32, (Q_BLOCK, 1), 0)
    w = misc_ref[0][:, 0:IDX_HEADS] * (IDX_DIM ** -0.5 * IDX_HEADS ** -0.5)
    iq = iq_ref[0]
    zero_col = jnp.zeros((Q_BLOCK, 1), F32)

    def score_chunk(c, carry):
        off = pl.multiple_of(c * ch, ch)
        ik = ik_ref[0, pl.ds(off, ch), 0:IDX_DIM]
        acc = jnp.zeros((Q_BLOCK, ch), F32)
        for h in range(IDX_HEADS):
            lg = _dot_nt(iq[:, h * IDX_DIM:(h + 1) * IDX_DIM], ik)
            acc = acc + jnp.maximum(lg, 0.0) * w[:, h:h + 1]
        acc = acc + 0.0
        bits = lax.bitcast_convert_type(acc, I32)
        key = bits ^ ((bits >> 31) & 0x7FFFFFFF)
        kpos = off + lax.broadcasted_iota(I32, (Q_BLOCK, ch), 1)
        keys_ref[c] = jnp.where(kpos <= qpos, key, INT_MIN)
        return carry

    lax.fori_loop(0, nch, score_chunk, 0)

    def count(pred):
        def chunk(c, cnt):
            kc = keys_ref[c]
            for j in range(ch // LANES):
                cnt = cnt + jnp.where(pred(kc[:, j * LANES:(j + 1) * LANES]), 1.0, 0.0)
            return cnt
        cnt = lax.fori_loop(0, nch, chunk, jnp.zeros((Q_BLOCK, LANES), F32))
        return jnp.sum(cnt, axis=-1, keepdims=True)

    kf = float(top_k)
    ans = jnp.where(count(lambda kc: kc >= 0) >= kf, 0, INT_MIN).astype(I32)

    def body(it, ans):
        cand = ans + jnp.left_shift(jnp.int32(1), 30 - it)
        return jnp.where(count(lambda kc: kc >= cand) >= kf, cand, ans)

    thr = lax.fori_loop(0, 31, body, ans)
    need = kf - count(lambda kc: kc > thr)

    tw = tri_ref.shape[0]

    def select_chunk(c, run):
        kc_all = keys_ref[c]
        for j in range(ch // tw):
            kc = kc_all[:, j * tw:(j + 1) * tw]
            eq = kc == thr
            pre = _dot(jnp.where(eq, 1.0, 0.0).astype(BF16), tri_ref[...])
            take = jnp.where(eq, jnp.where(pre + run <= need, 1.0, 0.0), 0.0)
            sel = jnp.where(kc > thr, 1.0, take)
            sel = jnp.where(kc != INT_MIN, sel, 0.0)
            selb_ref[c, :, j * tw:(j + 1) * tw] = jnp.where(sel > 0.5, 0.0, NEG_INF)
            run = run + pre[:, tw - 1:tw]
        return run

    lax.fori_loop(0, nch, select_chunk, zero_col)

    qs = (qb_ref[0].astype(F32) * HEAD_DIM ** -0.5).astype(BF16)

    def attend_chunk(c, carry):
        off = pl.multiple_of(c * ch, ch)
        bias = selb_ref[c]
        kc = kb_ref[0, pl.ds(off, ch), :]
        vc = vb_ref[0, pl.ds(off, ch), :]
        heads = range(B_HEADS)
        hs = [slice(h * HEAD_DIM, (h + 1) * HEAD_DIM) for h in heads]
        s = [_dot_nt(qs[:, hs[h]], kc[:, hs[h]]) + bias for h in heads]
        m_new = [jnp.maximum(carry[3 * h], jnp.max(s[h], axis=-1, keepdims=True)) for h in heads]
        m_safe = [jnp.where(m_new[h] == NEG_INF, 0.0, m_new[h]) for h in heads]
        alpha = [jnp.exp(carry[3 * h] - m_safe[h]) for h in heads]
        p = [jnp.exp(s[h] - m_safe[h]) for h in heads]
        l_new = [alpha[h] * carry[3 * h + 1] + jnp.sum(p[h], axis=-1, keepdims=True) for h in heads]
        pv = [_dot(p[h].astype(BF16), vc[:, hs[h]]) for h in heads]
        out = []
        for h in heads:
            out += [m_new[h], l_new[h], alpha[h] * carry[3 * h + 2] + pv[h]]
        return tuple(out)

    init = (jnp.full((Q_BLOCK, 1), NEG_INF, F32), zero_col, jnp.zeros((Q_BLOCK, HEAD_DIM), F32)) * B_HEADS
    res = lax.fori_loop(0, nch, attend_chunk, init)
    o_ref[0] = jnp.concatenate([res[3 * h + 2] / res[3 * h + 1] for h in range(B_HEADS)],
                               axis=1).astype(o_ref.dtype)


def _mixer_b(pb, pf, tri, batch, seq):
    nblk = seq // Q_BLOCK
    top_k = min(TOPK_MAX, seq // 4)
    ch = min(512, seq)
    kern = functools.partial(_mixb_kernel, top_k=top_k)
    return pl.pallas_call(
        kern,
        grid=(batch, nblk),
        in_specs=[pl.BlockSpec((1, Q_BLOCK, B_WIDTH), lambda b, i: (b, i, PB_BQ // B_WIDTH)),
                  pl.BlockSpec((1, seq, B_WIDTH), lambda b, i: (b, 0, PB_BK // B_WIDTH)),
                  pl.BlockSpec((1, seq, B_WIDTH), lambda b, i: (b, 0, PB_BV // B_WIDTH)),
                  pl.BlockSpec((1, Q_BLOCK, 256), lambda b, i: (b, i, PB_IQ // 256)),
                  pl.BlockSpec((1, seq, LANES), lambda b, i: (b, 0, PB_IK // LANES)),
                  pl.BlockSpec((1, Q_BLOCK, LANES), lambda b, i: (b, i, PF_MISC // LANES)),
                  pl.BlockSpec(tri.shape, lambda b, i: (0, 0))],
        out_specs=pl.BlockSpec((1, Q_BLOCK, B_WIDTH), lambda b, i: (b, i, 0)),
        out_shape=jax.ShapeDtypeStruct((batch, seq, B_WIDTH), BF16),
        scratch_shapes=[pltpu.VMEM((seq // ch, Q_BLOCK, ch), I32), pltpu.VMEM((seq // ch, Q_BLOCK, ch), F32)],
        compiler_params=_params("parallel", "arbitrary", vmem_mb=48),
        name="mixer_b",
    )(pb, pb, pb, pb, pb, pf, tri)


def _head_consts():
    lane = np.arange(C_WIDTH)
    same = (lane[:, None] // HEAD_DIM == lane[None, :] // HEAD_DIM).astype(np.float32)
    return jnp.asarray(same)


def _rwkv1_kernel(cin_ref, prev_ref, vf_ref, mu_ref, vec_ref, lora_ref, vl_ref, bd_ref,
                  phi_ref, psi_ref, rp_ref, y0_ref, g_ref, bonus_ref, v_ref, *, rows, use_v_lora):
    i = pl.program_id(1)
    pc = cin_ref[0][:, :C_IN]
    prev = jnp.where(i > 0, prev_ref[0][SUBLANES - 1:SUBLANES, :C_IN], 0.0)
    row = lax.broadcasted_iota(I32, (rows, 1), 0)
    shifted = jnp.where(row == 0, prev, pltpu.roll(pc, 1, 0))
    pc = pc + (shifted - pc) * mu_ref[...]
    r = pc[:, 0:C_WIDTH]
    k = pc[:, C_WIDTH:2 * C_WIDTH]
    v = pc[:, 2 * C_WIDTH:3 * C_WIDTH]
    xl = pc[:, 3 * C_WIDTH:C_IN]
    lane = lax.broadcasted_iota(I32, xl.shape, 1)
    feat = jnp.where(lane < C_LORA_W, jnp.tanh(xl), jnp.where(lane < C_LORA_W + C_LORA_A, xl, _sigmoid(xl)))
    w0, a0, k_k, k_a, r_k = (vec_ref[n:n + 1, :] for n in range(5))
    bd = bd_ref[...]
    bd_sel = bd.astype(BF16)
    w_raw = -_softplus(-(w0 + _dot3(feat, lora_ref[0]))) - 0.5
    a = _sigmoid(a0 + _dot3(feat, lora_ref[1]))
    g_ref[0] = _dot3(feat, lora_ref[2])
    if use_v_lora:
        v0 = vec_ref[5:6, :]
        v = v + (vf_ref[0] - v) * _sigmoid(v0 + _dot3(_dot3(v, vl_ref[0]), vl_ref[1]))
    v_ref[0] = v
    kk = k * k_k
    kk = kk / jnp.maximum(jnp.sqrt(_dot_sel(bd_sel, kk * kk, left=False)), 1e-12)
    k = k * (1.0 + (a - 1.0) * k_a)
    logw = -jnp.exp(w_raw)
    bonus_ref[0] = _dot_sel(bd_sel, r * k * r_k, left=False) * v
    av = -kk
    bv = kk * a

    cc = C_CHUNK
    ri = lax.broadcasted_iota(I32, (cc, cc), 0)
    ci = lax.broadcasted_iota(I32, (cc, cc), 1)
    tril = jnp.where(ci <= ri, 1.0, 0.0).astype(BF16)
    strict = ci < ri
    incl = ci <= ri
    eye_c = jnp.where(ci == ri, 1.0, 0.0)
    lane_w = lax.broadcasted_iota(I32, (1, C_WIDTH), 1)
    ri2 = lax.broadcasted_iota(I32, (C_WIDTH, C_WIDTH), 0)
    ci2 = lax.broadcasted_iota(I32, (C_WIDTH, C_WIDTH), 1)
    eye_w = jnp.where(ri2 == ci2, 1.0, 0.0)
    chunks = range(rows // cc)
    heads = range(C_HEADS)
    pairs = [(c, h) for c in chunks for h in heads]
    mh = [jnp.where(lane_w // HEAD_DIM == h, 1.0, 0.0) for h in heads]
    sl = [slice(c * cc, (c + 1) * cc) for c in chunks]
    cs = [_dot_sel(tril, logw[sl[c]]) for c in chunks]
    cs_end = [cs[c][cc - 1:cc, :] for c in chunks]
    at = [av[sl[c]] * jnp.exp(cs[c] - logw[sl[c]]) for c in chunks]
    rt = [r[sl[c]] * jnp.exp(cs[c]) for c in chunks]
    inv = [jnp.exp(-cs[c]) for c in chunks]
    rhs = [jnp.concatenate([bv[sl[c]] * inv[c], k[sl[c]] * inv[c]], axis=0) for c in chunks]
    tail = [jnp.exp(cs_end[c] - cs[c]) for c in chunks]
    vc = [v[sl[c]] for c in chunks]
    vc_b = [vc[c].astype(BF16) for c in chunks]
    ath = {(c, h): at[c] * mh[h] for c, h in pairs}
    aa = {(c, h): _dot3(jnp.concatenate([ath[c, h], rt[c] * mh[h]], axis=0), rhs[c], nt=True) for c, h in pairs}
    a_ab = {p: jnp.where(strict, aa[p][:cc, :cc], 0.0) for p in pairs}
    a_ak = {p: jnp.where(strict, aa[p][:cc, cc:], 0.0).astype(BF16) for p in pairs}
    a_rb = {p: jnp.where(incl, aa[p][cc:, :cc], 0.0).astype(BF16) for p in pairs}
    a_rk = {p: jnp.where(incl, aa[p][cc:, cc:], 0.0).astype(BF16) for p in pairs}
    x = {p: eye_c + a_ab[p] for p in pairs}
    pw = a_ab
    for _ in range(int(math.log2(cc)) - 1):
        pw = {p: _dot3(pw[p], pw[p]) for p in pairs}
        x = {p: x[p] + _dot3(x[p], pw[p]) for p in pairs}
    akv = {(c, h): _dot(a_ak[c, h], vc_b[c]) for c, h in pairs}
    ap_h = {p: _dot3(x[p], ath[p]) for p in pairs}
    w2_h = {(c, h): _dot3(x[c, h], akv[c, h]) * mh[h] for c, h in pairs}
    rp_h = {p: _dot(a_rb[p], ap_h[p].astype(BF16)) for p in pairs}
    y0_h = {(c, h): (_dot(a_rb[c, h], w2_h[c, h].astype(BF16)) + _dot(a_rk[c, h], vc_b[c])) * mh[h] for c, h in pairs}
    for c in chunks:
        ap = sum(ap_h[c, h] for h in heads)
        w2 = sum(w2_h[c, h] for h in heads)
        rp_ref[0, sl[c], :] = rt[c] + sum(rp_h[c, h] for h in heads)
        y0_ref[0, sl[c], :] = sum(y0_h[c, h] for h in heads)
        bh = bv[sl[c]] * tail[c]
        kh = k[sl[c]] * tail[c]
        phi_ref[0, c] = eye_w * jnp.exp(cs_end[c]) + _dot3(bh.T, ap) * bd
        psi_ref[0, c] = _dot(jnp.concatenate([bh, kh], axis=0).T.astype(BF16),
                             jnp.concatenate([w2, vc[c]], axis=0).astype(BF16)) * bd


def _rwkv2_kernel(phi_ref, psi_ref, rp_ref, y0_ref, g_ref, bonus_ref, gn_ref, bd_ref, o_ref, s_ref):
    @pl.when(pl.program_id(1) == 0)
    def _():
        s_ref[...] = jnp.zeros_like(s_ref)

    s = s_ref[...]
    y = _dot3(rp_ref[0], s) + y0_ref[0]
    s_ref[...] = _dot3(phi_ref[0, 0], s) + psi_ref[0, 0]
    bd_sel = bd_ref[...].astype(BF16)
    mu = _dot_sel(bd_sel, y, left=False) * (1.0 / HEAD_DIM)
    d = y - mu
    var = _dot_sel(bd_sel, d * d, left=False) * (1.0 / HEAD_DIM)
    yn = d * lax.rsqrt(var + C_GN_EPS) * gn_ref[0:1, :] + gn_ref[1:2, :]
    o_ref[0] = ((yn + bonus_ref[0]) * g_ref[0]).astype(o_ref.dtype)


def _mixer_c(pf, v_first, mu, vec, lora, vl, gn, bd, batch, seq, use_v_lora):
    rows = min(256, seq)
    nblk = seq // rows
    nch = seq // C_CHUNK
    cpb = rows // C_CHUNK
    kern = functools.partial(_rwkv1_kernel, rows=rows, use_v_lora=use_v_lora)
    full2 = lambda a: pl.BlockSpec(a.shape, lambda b, i: (0,) * a.ndim)
    seq_spec = pl.BlockSpec((1, rows, C_WIDTH), lambda b, i: (b, i, 0))
    mat_spec = pl.BlockSpec((1, cpb, C_WIDTH, C_WIDTH), lambda b, i: (b, i, 0, 0))
    seq_shape = jax.ShapeDtypeStruct((batch, seq, C_WIDTH), F32)
    mat_shape = jax.ShapeDtypeStruct((batch, nch, C_WIDTH, C_WIDTH), F32)
    prev_blk = rows // SUBLANES
    phi, psi, rp, y0, g, bonus, v = pl.pallas_call(
        kern,
        grid=(batch, nblk),
        in_specs=[pl.BlockSpec((1, rows, 1024), lambda b, i: (b, i, PF_CIN // 1024)),
                  pl.BlockSpec((1, SUBLANES, 1024), lambda b, i: (b, jnp.maximum(i * prev_blk - 1, 0), PF_CIN // 1024)),
                  seq_spec, full2(mu), full2(vec), full2(lora), full2(vl), full2(bd)],
        out_specs=[mat_spec, mat_spec, seq_spec, seq_spec, seq_spec, seq_spec, seq_spec],
        out_shape=[mat_shape, mat_shape, seq_shape, seq_shape, seq_shape, seq_shape, seq_shape],
        compiler_params=_params("parallel", "parallel", vmem_mb=48),
        name="rwkv_chunks",
    )(pf, pf, v_first, mu, vec, lora, vl, bd)
    cseq = pl.BlockSpec((1, C_CHUNK, C_WIDTH), lambda b, c: (b, c, 0))
    cmat = pl.BlockSpec((1, 1, C_WIDTH, C_WIDTH), lambda b, c: (b, c, 0, 0))
    o = pl.pallas_call(
        _rwkv2_kernel,
        grid=(batch, nch),
        in_specs=[cmat, cmat, cseq, cseq, cseq, cseq,
                  pl.BlockSpec(gn.shape, lambda b, c: (0, 0)), pl.BlockSpec(bd.shape, lambda b, c: (0, 0))],
        out_specs=cseq,
        out_shape=jax.ShapeDtypeStruct((batch, seq, C_WIDTH), BF16),
        scratch_shapes=[pltpu.VMEM((C_WIDTH, C_WIDTH), F32)],
        compiler_params=_params("parallel", "arbitrary"),
        name="rwkv_scan",
    )(phi, psi, rp, y0, g, bonus, gn, bd)
    return o, v


def _ssd_consts():
    expand = np.zeros((LANES, D_INNER), np.float32)
    for h in range(D_HEADS):
        expand[MISC_DT + h, h * HEAD_DIM:(h + 1) * HEAD_DIM] = 1.0
    return jnp.asarray(expand)


def _ssd_kernel(xbc_ref, z_ref, misc_ref, conv_ref, vec_ref, hp_ref, expand_ref, bd_ref, o_ref, st_ref, prev_ref):
    @pl.when(pl.program_id(1) == 0)
    def _():
        st_ref[...] = jnp.zeros_like(st_ref)
        prev_ref[...] = jnp.zeros_like(prev_ref)

    q = D_CHUNK
    x_raw = xbc_ref[0]
    ext = jnp.concatenate([prev_ref[...], x_raw], axis=0)
    conv = jnp.zeros((q, D_XBC), F32)
    for t in range(D_CONV):
        lo = SUBLANES - (D_CONV - 1) + t
        conv = conv + ext[lo:lo + q, :] * conv_ref[t:t + 1, :]
    prev_ref[...] = x_raw[q - SUBLANES:, :]
    xbc = _silu(conv + conv_ref[D_CONV:D_CONV + 1, :])
    xs = xbc[:, :D_INNER]
    bm = xbc[:, D_INNER:D_INNER + D_GROUPS * D_STATE]
    cm = xbc[:, D_INNER + D_GROUPS * D_STATE:]

    dt_col = _softplus(misc_ref[0] + hp_ref[0:1, :])
    a_col = dt_col * hp_ref[1:2, :]
    ri = lax.broadcasted_iota(I32, (q, q), 0)
    ci = lax.broadcasted_iota(I32, (q, q), 1)
    causal = ci <= ri
    tril = jnp.where(causal, 1.0, 0.0)
    acs_col = _doth(tril, a_col)
    acs_row = acs_col.T
    expand = expand_ref[...]
    acs = _doth(acs_col, expand)
    dt = _doth(dt_col, expand)
    acs_end = acs[q - 1:q, :]
    xdt = xs * dt

    lane = lax.broadcasted_iota(I32, (1, LANES), 1)
    lane_w = lax.broadcasted_iota(I32, (1, D_INNER), 1)
    left = lane < D_STATE
    bm_sw = pltpu.roll(bm, D_STATE, 1)
    cm_sw = pltpu.roll(cm, D_STATE, 1)
    b_exp = jnp.concatenate([jnp.where(left, bm, bm_sw), jnp.where(left, bm_sw, bm)], axis=1)
    c_exp = jnp.concatenate([jnp.where(left, cm, cm_sw), jnp.where(left, cm_sw, cm)], axis=1)
    cb = [_dot_nt(jnp.where(left == (g == 0), cm, 0.0), bm, HIGHEST) for g in range(D_GROUPS)]

    y = jnp.zeros((q, D_INNER), F32)
    for h in range(D_HEADS):
        col = acs_col[:, MISC_DT + h:MISC_DT + h + 1]
        rw = acs_row[MISC_DT + h:MISC_DT + h + 1, :]
        decay = jnp.exp(jnp.where(causal, col - rw, NEG_INF))
        scores = cb[h // (D_HEADS // D_GROUPS)] * decay
        y = y + jnp.where(lane_w // HEAD_DIM == h, _doth(scores, xdt), 0.0)
    st = st_ref[...]
    y = y + _doth(c_exp, st) * jnp.exp(acs)
    st_ref[...] = st * jnp.exp(acs_end) + _doth(b_exp.T, xdt * jnp.exp(acs_end - acs)) * bd_ref[...]
    y = y + xs * vec_ref[0:1, :]
    y = y * _silu(z_ref[0])
    half = D_INNER // D_GROUPS
    outs = []
    for g in range(D_GROUPS):
        yg = y[:, g * half:(g + 1) * half]
        outs.append(yg * lax.rsqrt(jnp.mean(yg * yg, axis=-1, keepdims=True) + D_NORM_EPS))
    o_ref[0] = (jnp.concatenate(outs, axis=1) * vec_ref[1:2, :]).astype(o_ref.dtype)


def _mixer_d(pf, conv, vec, hp, expand, bd, batch, seq):
    q = D_CHUNK
    full = lambda a: pl.BlockSpec(a.shape, lambda b, i: (0, 0))
    return pl.pallas_call(
        _ssd_kernel,
        grid=(batch, seq // q),
        in_specs=[pl.BlockSpec((1, q, D_XBC), lambda b, i: (b, i, PF_XBC // D_XBC)),
                  pl.BlockSpec((1, q, D_INNER), lambda b, i: (b, i, PF_Z // D_INNER)),
                  pl.BlockSpec((1, q, LANES), lambda b, i: (b, i, PF_MISC // LANES)),
                  full(conv), full(vec), full(hp), full(expand), full(bd)],
        out_specs=pl.BlockSpec((1, q, D_INNER), lambda b, i: (b, i, 0)),
        out_shape=jax.ShapeDtypeStruct((batch, seq, D_INNER), BF16),
        scratch_shapes=[pltpu.VMEM((D_INNER, D_INNER), F32), pltpu.VMEM((SUBLANES, D_XBC), F32)],
        compiler_params=_params("parallel", "arbitrary"),
        name="ssd",
    )(pf, pf, pf, conv, vec, hp, expand, bd)


def _merge_kernel(x_ref, gates_ref, oa_ref, ob_ref, oc_ref, od_ref, wbr_ref, wout_ref, ln_ref, wr_ref, br_ref,
                  x1_ref, route_ref):
    acc = None
    for n, o_ref in enumerate((oa_ref, ob_ref, oc_ref, od_ref)):
        term = _sigmoid(gates_ref[:, n * D_MODEL:(n + 1) * D_MODEL]) * _dot(o_ref[...], wbr_ref[n])
        acc = term if acc is None else acc + term
    h = DEEPNORM_ALPHA * x_ref[...] + _dot(acc.astype(BF16), wout_ref[...])
    x1 = _layer_norm(h, ln_ref[0:1, :], ln_ref[1:2, :])
    x1_ref[...] = x1

    logits = _doth(x1, wr_ref[...]) + br_ref[...]
    lane = lax.broadcasted_iota(I32, logits.shape, 1)
    big = jnp.int32(LANES)
    gl = jnp.where(lane < N_EXPERT_GROUPS, logits, NEG_INF)
    gm = jnp.max(gl, axis=-1, keepdims=True)
    pg_top = 1.0 / jnp.sum(jnp.exp(gl - gm), axis=-1, keepdims=True)
    g_sel = jnp.min(jnp.where(gl == gm, lane, big), axis=-1, keepdims=True)
    off = N_EXPERT_GROUPS + g_sel * EXPERTS_PER_GROUP
    el = jnp.where((lane >= off) & (lane < off + EXPERTS_PER_GROUP), logits, NEG_INF)
    em = jnp.max(el, axis=-1, keepdims=True)
    es = jnp.sum(jnp.exp(el - em), axis=-1, keepdims=True)
    idx1 = jnp.min(jnp.where(el == em, lane, big), axis=-1, keepdims=True)
    el2 = jnp.where(lane == idx1, NEG_INF, el)
    em2 = jnp.max(el2, axis=-1, keepdims=True)
    idx2 = jnp.min(jnp.where(el2 == em2, lane, big), axis=-1, keepdims=True)
    p1 = 1.0 / es
    p2 = jnp.exp(em2 - em) / es
    gate1 = pg_top * p1 / (p1 + p2)
    gate2 = pg_top * p2 / (p1 + p2)
    e1 = (idx1 - N_EXPERT_GROUPS).astype(F32)
    e2 = (idx2 - N_EXPERT_GROUPS).astype(F32)
    route_ref[...] = jnp.where(lane == 0, e1, jnp.where(lane == 1, e2, jnp.where(lane == 2, gate1,
                               jnp.where(lane == 3, gate2, 0.0))))


def _merge(x2d, pf, oa, ob, oc, od, wbr, wout, ln, wr, br, tm):
    m = x2d.shape[0]
    row = lambda w: pl.BlockSpec((tm, w), lambda i: (i, 0))
    full = lambda a: pl.BlockSpec(a.shape, lambda i: (0,) * a.ndim)
    return pl.pallas_call(
        _merge_kernel,
        grid=(m // tm,),
        in_specs=[row(D_MODEL), pl.BlockSpec((tm, N_BRANCH * D_MODEL), lambda i: (i, PF_GATES)),
                  row(A_WIDTH), row(B_WIDTH), row(C_WIDTH), row(D_INNER),
                  full(wbr), full(wout), full(ln), full(wr), full(br)],
        out_specs=[row(D_MODEL), row(LANES)],
        out_shape=[jax.ShapeDtypeStruct((m, D_MODEL), F32), jax.ShapeDtypeStruct((m, LANES), F32)],
        compiler_params=_params("parallel", vmem_mb=48),
        name="merge_route",
    )(x2d, pf, oa, ob, oc, od, wbr, wout, ln, wr, br)


def _ffn_kernel(be_ref, nv_ref, xs_ref, wg_ref, wu_ref, wd_ref, o_ref):
    i = pl.program_id(0)

    @pl.when(i < nv_ref[0])
    def _():
        xb = xs_ref[...]
        hg = _dot(xb, wg_ref[0, 0].astype(BF16))
        hu = _dot(xb, wu_ref[0, 0].astype(BF16))
        o_ref[...] = _dot((_silu(hg) * hu).astype(BF16), wd_ref[0, 0].astype(BF16))

    @pl.when(i >= nv_ref[0])
    def _():
        o_ref[...] = jnp.zeros_like(o_ref)


def _ffn(blk_expert, n_valid, xs, e_gate, e_up, e_down, layer):
    cap = xs.shape[0]
    wspec = lambda a: pl.BlockSpec((1, 1) + a.shape[2:], lambda i, be, nv: (layer, be[i], 0, 0))
    return pl.pallas_call(
        _ffn_kernel,
        grid_spec=pltpu.PrefetchScalarGridSpec(
            num_scalar_prefetch=2,
            grid=(cap // MOE_ROWS,),
            in_specs=[pl.BlockSpec((MOE_ROWS, D_MODEL), lambda i, be, nv: (i, 0)),
                      wspec(e_gate), wspec(e_up), wspec(e_down)],
            out_specs=pl.BlockSpec((MOE_ROWS, D_MODEL), lambda i, be, nv: (i, 0)),
        ),
        out_shape=jax.ShapeDtypeStruct((cap, D_MODEL), F32),
        compiler_params=_params("arbitrary", vmem_mb=48),
        name="expert_ffn",
    )(blk_expert, n_valid, xs, e_gate, e_up, e_down)


def _combine_kernel(x_ref, y0_ref, y1_ref, route_ref, ln_ref, o_ref):
    g0 = route_ref[:, 2:3]
    g1 = route_ref[:, 3:4]
    h = DEEPNORM_ALPHA * x_ref[...] + (y0_ref[...] * g0 + y1_ref[...] * g1)
    o_ref[...] = _layer_norm(h, ln_ref[0:1, :], ln_ref[1:2, :])


def _combine(x1, ysel, route, ln, tm):
    m = x1.shape[0]
    return pl.pallas_call(
        _combine_kernel,
        grid=(m // tm,),
        in_specs=[pl.BlockSpec((tm, D_MODEL), lambda i: (i, 0)),
                  pl.BlockSpec((tm, D_MODEL), lambda i: (i, 0)),
                  pl.BlockSpec((tm, D_MODEL), lambda i: (i, 1)),
                  pl.BlockSpec((tm, LANES), lambda i: (i, 0)),
                  pl.BlockSpec(ln.shape, lambda i: (0, 0))],
        out_specs=pl.BlockSpec((tm, D_MODEL), lambda i: (i, 0)),
        out_shape=jax.ShapeDtypeStruct((m, D_MODEL), F32),
        compiler_params=_params("parallel"),
        name="combine",
    )(x1, ysel, ysel, route, ln)


def _dispatch_tables(route, m):
    flat_e = route[:, 0:TOP_K_EXPERTS].astype(I32).reshape(-1)
    n_assign = m * TOP_K_EXPERTS
    onehot = (flat_e[:, None] == jnp.arange(N_EXPERTS, dtype=I32)[None, :]).astype(I32)
    csum = jnp.cumsum(onehot, axis=0)
    rank = jnp.sum(csum * onehot, axis=1) - 1
    counts = csum[-1]
    padded = (counts + MOE_ROWS - 1) // MOE_ROWS * MOE_ROWS
    pad_end = jnp.cumsum(padded)
    pad_start = pad_end - padded
    dest = pad_start[flat_e] + rank
    cap = (n_assign + N_EXPERTS * (MOE_ROWS - 1) + MOE_ROWS - 1) // MOE_ROWS * MOE_ROWS
    n_blocks = cap // MOE_ROWS
    blk_start = jnp.arange(n_blocks, dtype=I32) * MOE_ROWS
    blk_expert = jnp.minimum(jnp.sum((pad_end[None, :] <= blk_start[:, None]).astype(I32), axis=1), N_EXPERTS - 1)
    buf_tok = jnp.zeros((cap,), I32).at[dest].set(jnp.arange(n_assign, dtype=I32) // TOP_K_EXPERTS)
    n_valid = (pad_end[-1] // MOE_ROWS).astype(I32).reshape(1)
    return dest, buf_tok, blk_expert, n_valid


def _row_pad(a, rows):
    return jnp.pad(a, ((0, rows - a.shape[0]), (0, 0)))


def kernel(x, w_in, c_mu, c_w0, c_w2, c_a0, c_a2, c_g2, c_kk, c_ka, c_rk, c_gn_w, c_gn_b, c_v0, c_v1, c_v2,
           d_conv_w, d_conv_b, d_dt_bias, d_a_log, d_skip, d_norm_w, w_branch, w_out, ln1_g, ln1_b,
           r_group, r_group_b, r_expert, r_expert_b, e_gate, e_up, e_down, ln2_g, ln2_b):
    batch, seq, _ = x.shape
    m = batch * seq
    biases = _mixa_bias()
    tw = min(256, seq)
    tri = jnp.asarray(np.triu(np.ones((tw, tw), np.float32))).astype(BF16)
    bd = _head_consts()
    expand = _ssd_consts()
    x2d = x.reshape(m, D_MODEL)
    v_first = jnp.zeros((batch, seq, C_WIDTH), F32)
    tm_proj = min(1024, m)
    tm_tok = min(512, m)
    for l in range(DEPTH):
        wb, wf = _proj_weights(w_in[l])
        pb = _proj(x2d, wb, BF16, tm_proj, 256).reshape(batch, seq, PB_WIDTH)
        pf = _proj(x2d, wf, F32, tm_proj, 256).reshape(batch, seq, PF_WIDTH)

        kv_pad = jnp.pad(pb[:, :, PB_AKV:PB_AQ], ((0, 0), (A_MAXWIN, 0), (0, 0)))
        o_a = _mixer_a(pb, kv_pad, biases, batch, seq)
        o_b = _mixer_b(pb, pf, tri, batch, seq)

        use_v_lora = l > 0
        vec_rows = [c_w0[l], c_a0[l], c_kk[l], c_ka[l], c_rk[l].reshape(-1)]
        vec_rows.append(c_v0[l - 1] if use_v_lora else jnp.zeros((C_WIDTH,), F32))
        vec = _row_pad(jnp.stack(vec_rows), SUBLANES)
        lora = jnp.stack([
            jnp.pad(c_w2[l], ((0, LANES - C_LORA_W), (0, 0))),
            jnp.pad(c_a2[l], ((C_LORA_W, LANES - C_LORA_W - C_LORA_A), (0, 0))),
            jnp.pad(c_g2[l], ((C_LORA_W + C_LORA_A, 0), (0, 0)))])
        if use_v_lora:
            vl = jnp.stack([jnp.pad(c_v1[l - 1], ((0, 0), (0, C_WIDTH - C_LORA_V))),
                            jnp.pad(c_v2[l - 1], ((0, C_WIDTH - C_LORA_V), (0, 0)))])
        else:
            vl = jnp.zeros((2, C_WIDTH, C_WIDTH), F32)
        gn = _row_pad(jnp.stack([c_gn_w[l], c_gn_b[l]]), SUBLANES)
        o_c, v_c = _mixer_c(pf, v_first, c_mu[l].reshape(1, C_IN), vec, lora, vl, gn, bd, batch, seq, use_v_lora)
        if l == 0:
            v_first = v_c

        conv = _row_pad(jnp.concatenate([d_conv_w[l], d_conv_b[l][None, :]], axis=0), SUBLANES)
        dvec = _row_pad(jnp.stack([jnp.repeat(d_skip[l], HEAD_DIM), d_norm_w[l]]), SUBLANES)
        place = lambda a: jnp.pad(a, (MISC_DT, LANES - MISC_DT - D_HEADS))
        hp = _row_pad(jnp.stack([place(d_dt_bias[l]), place(-jnp.exp(d_a_log[l]))]), SUBLANES)
        o_d = _mixer_d(pf, conv, dvec, hp, expand, bd, batch, seq)

        wr = jnp.pad(jnp.concatenate([r_group[l], r_expert[l]], axis=1),
                     ((0, 0), (0, LANES - N_EXPERT_GROUPS - N_EXPERTS)))
        br = jnp.pad(jnp.concatenate([r_group_b[l], r_expert_b[l]]), (0, LANES - N_EXPERT_GROUPS - N_EXPERTS))
        ln1 = _row_pad(jnp.stack([ln1_g[l], ln1_b[l]]), SUBLANES)
        x1, route = _merge(x2d, pf.reshape(m, PF_WIDTH), o_a.reshape(m, -1), o_b.reshape(m, -1),
                           o_c.reshape(m, -1), o_d.reshape(m, -1), w_branch[l].astype(BF16),
                           w_out[l].astype(BF16), ln1, wr, br.reshape(1, LANES), tm_tok)

        dest, buf_tok, blk_expert, n_valid = _dispatch_tables(route, m)
        xs = jnp.take(x1.astype(BF16), buf_tok, axis=0)
        yb = _ffn(blk_expert, n_valid, xs, e_gate, e_up, e_down, l)
        ysel = jnp.take(yb, dest, axis=0).reshape(m, TOP_K_EXPERTS * D_MODEL)
        ln2 = _row_pad(jnp.stack([ln2_g[l], ln2_b[l]]), SUBLANES)
        x2d = _combine(x1, ysel, route, ln2, tm_tok)
    return x2d.reshape(batch, seq, D_MODEL)
```

```python
import functools
import math

import jax
import jax.numpy as jnp
import numpy as np
from jax import lax
from jax.experimental import pallas as pl
from jax.experimental.pallas import tpu as pltpu

F32 = jnp.float32
BF16 = jnp.bfloat16
I32 = jnp.int32
HIGHEST = lax.Precision.HIGHEST
NEG_INF = float("-inf")
INT_MIN = -(2 ** 31)

LANES = 128
SUBLANES = 8

D_MODEL = 1024
DEPTH = 2
HEAD_DIM = 64
Q_BLOCK = 128

A_HEADS = 4
A_PATTERNS = ((128, 1), (512, 4), (2048, 16))
A_GROUPS = len(A_PATTERNS)
A_WIDTH = A_HEADS * HEAD_DIM
A_MAXWIN = max(w for w, _ in A_PATTERNS)

B_HEADS = 4
B_WIDTH = B_HEADS * HEAD_DIM
IDX_HEADS = 4
IDX_DIM = 64
TOPK_MAX = 256

C_HEADS = 4
C_WIDTH = C_HEADS * HEAD_DIM
C_LORA_W = 32
C_LORA_A = 32
C_LORA_G = 64
C_LORA_V = 16
C_IN = 3 * C_WIDTH + C_LORA_W + C_LORA_A + C_LORA_G
C_GN_EPS = 64e-5
C_CHUNK = 64

D_HEADS = 4
D_INNER = D_HEADS * HEAD_DIM
D_GROUPS = 2
D_STATE = 64
D_CONV = 4
D_CHUNK = 128
D_XBC = D_INNER + 2 * D_GROUPS * D_STATE
D_NORM_EPS = 1e-5

N_BRANCH = 4
N_EXPERT_GROUPS = 4
EXPERTS_PER_GROUP = 8
N_EXPERTS = N_EXPERT_GROUPS * EXPERTS_PER_GROUP
TOP_K_EXPERTS = 2
D_EXPERT = 512
MOE_ROWS = 256

LN_EPS = 1e-5
DEEPNORM_ALPHA = (2 * DEPTH) ** 0.25

PB_AKV = 0
PB_AQ = PB_AKV + A_HEADS * A_GROUPS * 2 * HEAD_DIM
PB_BQ = PB_AQ + A_HEADS * 4 * HEAD_DIM
PB_BK = PB_BQ + B_WIDTH
PB_BV = PB_BK + B_WIDTH
PB_IQ = PB_BV + B_WIDTH
PB_IK = PB_IQ + IDX_HEADS * IDX_DIM
PB_WIDTH = PB_IK + 256
PF_GATES = 0
PF_CIN = PF_GATES + N_BRANCH * D_MODEL
PF_MISC = PF_CIN + C_IN
PF_XBC = PF_CIN + 1024
PF_Z = PF_XBC + D_XBC
PROJ_TN = 768
PF_WIDTH = -(-(PF_Z + D_INNER) // PROJ_TN) * PROJ_TN
MISC_DT = 4


def _dot(a, b, precision=None):
    return jnp.dot(a, b, preferred_element_type=F32, precision=precision)


def _dot_nt(a, b, precision=None):
    return lax.dot_general(a, b, (((1,), (1,)), ((), ())), preferred_element_type=F32, precision=precision)


def _doth(a, b):
    return _dot(a, b, HIGHEST)


def _split2(a):
    hi = a.astype(BF16)
    return hi, (a - hi.astype(F32)).astype(BF16)


def _dot3(a, b, nt=False):
    mm = _dot_nt if nt else _dot
    ah, al = _split2(a)
    bh, bl = _split2(b)
    return mm(ah, bh) + (mm(ah, bl) + mm(al, bh))


def _dot_sel(sel, x, left=True):
    hi = x.astype(BF16)
    r1 = x - hi.astype(F32)
    mid = r1.astype(BF16)
    lo = (r1 - mid.astype(F32)).astype(BF16)
    if left:
        return _dot(sel, hi) + (_dot(sel, mid) + _dot(sel, lo))
    return _dot(hi, sel) + (_dot(mid, sel) + _dot(lo, sel))


def _sigmoid(x):
    return 1.0 / (1.0 + jnp.exp(-x))


def _softplus(x):
    return jnp.maximum(x, 0.0) + jnp.log1p(jnp.exp(-jnp.abs(x)))


def _silu(x):
    return x * _sigmoid(x)


def _layer_norm(h, g, b):
    mu = jnp.mean(h, axis=-1, keepdims=True)
    d = h - mu
    var = jnp.mean(d * d, axis=-1, keepdims=True)
    return d * lax.rsqrt(var + LN_EPS) * g + b


def _params(*sem, vmem_mb=None):
    kw = {}
    if vmem_mb is not None:
        kw["vmem_limit_bytes"] = vmem_mb * 1024 * 1024
    return pltpu.CompilerParams(dimension_semantics=sem, **kw)


def _proj_kernel(x_ref, w_ref, o_ref):
    o_ref[...] = _dot(x_ref[...].astype(BF16), w_ref[...]).astype(o_ref.dtype)


def _proj(x2d, w, out_dtype, tm, tn):
    m, k = x2d.shape
    n = w.shape[1]
    return pl.pallas_call(
        _proj_kernel,
        grid=(m // tm, n // tn),
        in_specs=[pl.BlockSpec((tm, k), lambda i, j: (i, 0)),
                  pl.BlockSpec((k, tn), lambda i, j: (0, j))],
        out_specs=pl.BlockSpec((tm, tn), lambda i, j: (i, j)),
        out_shape=jax.ShapeDtypeStruct((m, n), out_dtype),
        compiler_params=_params("parallel", "arbitrary", vmem_mb=48),
        name="in_proj",
    )(x2d, w)


def _proj_weights(w):
    k = w.shape[0]
    off = 0
    seg = {}
    for name, size in (("a_qkv", 3 * A_GROUPS * A_WIDTH), ("b_qkv", 3 * B_WIDTH), ("b_idx_q", IDX_HEADS * IDX_DIM),
                       ("b_idx_k", IDX_DIM), ("b_idx_w", IDX_HEADS), ("c_in", C_IN), ("d_z", D_INNER),
                       ("d_xbc", D_XBC), ("d_dt", D_HEADS), ("gates", N_BRANCH * D_MODEL)):
        seg[name] = w[:, off:off + size]
        off += size
    a = seg["a_qkv"].reshape(k, 3, A_GROUPS, A_HEADS, HEAD_DIM)
    a_kv = jnp.transpose(a[:, 1:3], (0, 3, 2, 1, 4)).reshape(k, -1)
    a_q = jnp.transpose(a[:, 0], (0, 2, 1, 3))
    a_q = jnp.pad(a_q, ((0, 0), (0, 0), (0, 4 - A_GROUPS), (0, 0))).reshape(k, -1)
    zeros = lambda n: jnp.zeros((k, n), w.dtype)
    wb = jnp.concatenate([a_kv, a_q, seg["b_qkv"], seg["b_idx_q"], seg["b_idx_k"], zeros(256 - IDX_DIM)], axis=1)
    misc = jnp.concatenate([seg["b_idx_w"], seg["d_dt"], zeros(LANES - IDX_HEADS - D_HEADS)], axis=1)
    wf = jnp.concatenate([seg["gates"], seg["c_in"], misc, seg["d_xbc"], seg["d_z"]], axis=1)
    wf = jnp.concatenate([wf, zeros(PF_WIDTH - wf.shape[1])], axis=1)
    assert wb.shape[1] == PB_WIDTH and PB_WIDTH % PROJ_TN == 0
    return wb.astype(BF16), wf.astype(BF16)


def _mixa_bias():
    out = []
    for win, dil in A_PATTERNS:
        r = np.arange(Q_BLOCK)[:, None]
        c = np.arange(win + Q_BLOCK)[None, :]
        d = r + win - c
        ok = (d >= 0) & (d <= win) & (d % dil == 0)
        out.append(jnp.asarray(np.where(ok, 0.0, -np.inf).astype(np.float32)))
    return out


def _mixa_kernel(q_ref, kv_ref, b0_ref, b1_ref, b2_ref, o_ref):
    start = pl.program_id(2) * Q_BLOCK
    scale = HEAD_DIM ** -0.5
    bias_refs = (b0_ref, b1_ref, b2_ref)
    outs = []
    for hh in range(2):
        ms, ls, os_ = [], [], []
        for g, (win, _) in enumerate(A_PATTERNS):
            span = win + Q_BLOCK
            q = q_ref[0, :, hh * 256 + g * HEAD_DIM: hh * 256 + (g + 1) * HEAD_DIM]
            base = pl.multiple_of(start + (A_MAXWIN - win), Q_BLOCK)
            c0 = hh * (A_GROUPS * 2 * HEAD_DIM) + g * 2 * HEAD_DIM
            k = kv_ref[0, pl.ds(base, span), c0:c0 + HEAD_DIM]
            v = kv_ref[0, pl.ds(base, span), c0 + HEAD_DIM:c0 + 2 * HEAD_DIM]
            s = _dot_nt(q, k) * scale + bias_refs[g][...]
            col = lax.broadcasted_iota(I32, (Q_BLOCK, span), 1)
            s = jnp.where(col >= win - start, s, NEG_INF)
            m = jnp.max(s, axis=-1, keepdims=True)
            p = jnp.exp(s - m)
            ls.append(jnp.sum(p, axis=-1, keepdims=True))
            os_.append(_dot(p.astype(BF16), v))
            ms.append(m)
        m_all = jnp.maximum(jnp.maximum(ms[0], ms[1]), ms[2])
        num = jnp.zeros((Q_BLOCK, HEAD_DIM), F32)
        den = jnp.zeros((Q_BLOCK, 1), F32)
        for g in range(A_GROUPS):
            wg = jnp.exp(ms[g] - m_all)
            num = num + wg * os_[g]
            den = den + wg * ls[g]
        outs.append(num / den)
    o_ref[0] = jnp.concatenate(outs, axis=1).astype(o_ref.dtype)


def _mixer_a(pb, kv_pad, biases, batch, seq):
    nblk = seq // Q_BLOCK
    pair_kv = 2 * A_GROUPS * 2 * HEAD_DIM
    in_specs = [pl.BlockSpec((1, Q_BLOCK, 512), lambda b, hp, i: (b, i, PB_AQ // 512 + hp)),
                pl.BlockSpec((1, seq + A_MAXWIN, pair_kv), lambda b, hp, i: (b, 0, hp))]
    for bias in biases:
        in_specs.append(pl.BlockSpec(bias.shape, lambda b, hp, i: (0, 0)))
    return pl.pallas_call(
        _mixa_kernel,
        grid=(batch, A_HEADS // 2, nblk),
        in_specs=in_specs,
        out_specs=pl.BlockSpec((1, Q_BLOCK, 2 * HEAD_DIM), lambda b, hp, i: (b, i, hp)),
        out_shape=jax.ShapeDtypeStruct((batch, seq, A_WIDTH), BF16),
        compiler_params=_params("parallel", "parallel", "arbitrary", vmem_mb=48),
        name="mixer_a",
    )(pb, kv_pad, *biases)


def _mixb_kernel(qb_ref, kb_ref, vb_ref, iq_ref, ik_ref, misc_ref, tri_ref, o_ref, keys_ref, selb_ref, *, top_k):
    _, rows, ch = keys_ref.shape
    start = pl.program_id(1) * rows
    nch = (start + rows + ch - 1) // ch
    qpos = start + lax.broadcasted_iota(I32, (rows, 1), 0)
    w = misc_ref[0][:, 0:IDX_HEADS] * (IDX_DIM ** -0.5 * IDX_HEADS ** -0.5)
    iq = iq_ref[0]
    zero_col = jnp.zeros((rows, 1), F32)

    def score_chunk(c, carry):
        off = pl.multiple_of(c * ch, ch)
        ik = ik_ref[0, pl.ds(off, ch), 0:IDX_DIM]
        acc = jnp.zeros((rows, ch), F32)
        for h in range(IDX_HEADS):
            lg = _dot_nt(iq[:, h * IDX_DIM:(h + 1) * IDX_DIM], ik)
            acc = acc + jnp.maximum(lg, 0.0) * w[:, h:h + 1]
        acc = acc + 0.0
        bits = lax.bitcast_convert_type(acc, I32)
        key = bits ^ ((bits >> 31) & 0x7FFFFFFF)
        kpos = off + lax.broadcasted_iota(I32, (rows, ch), 1)
        keys_ref[c] = jnp.where(kpos <= qpos, key, INT_MIN)
        return carry

    lax.fori_loop(0, nch, score_chunk, 0)

    def count(pred):
        def chunk(c, cnt):
            kc = keys_ref[c]
            for j in range(ch // LANES):
                cnt = cnt + jnp.where(pred(kc[:, j * LANES:(j + 1) * LANES]), 1.0, 0.0)
            return cnt
        cnt = lax.fori_loop(0, nch, chunk, jnp.zeros((rows, LANES), F32))
        return jnp.sum(cnt, axis=-1, keepdims=True)

    kf = float(top_k)
    ans = jnp.where(count(lambda kc: kc >= 0) >= kf, 0, INT_MIN).astype(I32)

    def body(it, ans):
        cand = ans + jnp.left_shift(jnp.int32(1), 30 - it)
        return jnp.where(count(lambda kc: kc >= cand) >= kf, cand, ans)

    thr = lax.fori_loop(0, 31, body, ans)
    need = kf - count(lambda kc: kc > thr)

    tw = tri_ref.shape[0]

    def select_chunk(c, run):
        kc_all = keys_ref[c]
        for j in range(ch // tw):
            kc = kc_all[:, j * tw:(j + 1) * tw]
            eq = kc == thr
            pre = _dot(jnp.where(eq, 1.0, 0.0).astype(BF16), tri_ref[...])
            take = jnp.where(eq, jnp.where(pre + run <= need, 1.0, 0.0), 0.0)
            sel = jnp.where(kc > thr, 1.0, take)
            sel = jnp.where(kc != INT_MIN, sel, 0.0)
            selb_ref[c, :, j * tw:(j + 1) * tw] = jnp.where(sel > 0.5, 0.0, NEG_INF)
            run = run + pre[:, tw - 1:tw]
        return run

    def select_chunk_no_ties(c, carry):
        kc = keys_ref[c]
        selb_ref[c] = jnp.where(kc >= thr, jnp.where(kc != INT_MIN, 0.0, NEG_INF), NEG_INF)
        return carry

    excess = jnp.max(count(lambda kc: kc >= thr)) > kf
    lax.cond(excess,
             lambda: lax.fori_loop(0, nch, select_chunk, zero_col),
             lambda: lax.fori_loop(0, nch, select_chunk_no_ties, zero_col))

    heads = range(B_HEADS)
    lane = lax.broadcasted_iota(I32, (1, B_WIDTH), 1) // HEAD_DIM
    qs = (qb_ref[0].astype(F32) * HEAD_DIM ** -0.5).astype(BF16)
    qm = [jnp.where(lane == h, qs, jnp.zeros_like(qs)) for h in heads]

    def per_lane(cols):
        out = cols[B_HEADS - 1]
        for h in reversed(range(B_HEADS - 1)):
            out = jnp.where(lane <= h, cols[h], out)
        return out

    def attend_chunk(c, carry):
        off = pl.multiple_of(c * ch, ch)
        bias = selb_ref[c]
        kc = kb_ref[0, pl.ds(off, ch), :]
        vc = vb_ref[0, pl.ds(off, ch), :]
        ms, ls, acc = carry[:B_HEADS], carry[B_HEADS:2 * B_HEADS], carry[2 * B_HEADS]
        s = [_dot_nt(qm[h], kc) + bias for h in heads]
        m_new = [jnp.maximum(ms[h], jnp.max(s[h], axis=-1, keepdims=True)) for h in heads]
        m_safe = [jnp.where(m_new[h] == NEG_INF, 0.0, m_new[h]) for h in heads]
        alpha = [jnp.exp(ms[h] - m_safe[h]) for h in heads]
        p = [jnp.exp(s[h] - m_safe[h]) for h in heads]
        l_new = [alpha[h] * ls[h] + jnp.sum(p[h], axis=-1, keepdims=True) for h in heads]
        pv = [_dot(p[h].astype(BF16), vc) for h in heads]
        acc = per_lane(alpha) * acc + per_lane(pv)
        return tuple(m_new) + tuple(l_new) + (acc,)

    init = ((jnp.full((rows, 1), NEG_INF, F32),) * B_HEADS + (zero_col,) * B_HEADS
            + (jnp.zeros((rows, B_WIDTH), F32),))
    res = lax.fori_loop(0, nch, attend_chunk, init)
    o_ref[0] = (res[2 * B_HEADS] / per_lane(res[B_HEADS:2 * B_HEADS])).astype(o_ref.dtype)


def _mixer_b(pb, pf, tri, batch, seq):
    rows = Q_BLOCK
    top_k = min(TOPK_MAX, seq // 4)
    ch = min(512, seq)
    kern = functools.partial(_mixb_kernel, top_k=top_k)
    return pl.pallas_call(
        kern,
        grid=(batch, seq // rows),
        in_specs=[pl.BlockSpec((1, rows, B_WIDTH), lambda b, i: (b, i, PB_BQ // B_WIDTH)),
                  pl.BlockSpec((1, seq, B_WIDTH), lambda b, i: (b, 0, PB_BK // B_WIDTH)),
                  pl.BlockSpec((1, seq, B_WIDTH), lambda b, i: (b, 0, PB_BV // B_WIDTH)),
                  pl.BlockSpec((1, rows, 256), lambda b, i: (b, i, PB_IQ // 256)),
                  pl.BlockSpec((1, seq, LANES), lambda b, i: (b, 0, PB_IK // LANES)),
                  pl.BlockSpec((1, rows, LANES), lambda b, i: (b, i, PF_MISC // LANES)),
                  pl.BlockSpec(tri.shape, lambda b, i: (0, 0))],
        out_specs=pl.BlockSpec((1, rows, B_WIDTH), lambda b, i: (b, i, 0)),
        out_shape=jax.ShapeDtypeStruct((batch, seq, B_WIDTH), BF16),
        scratch_shapes=[pltpu.VMEM((seq // ch, rows, ch), I32), pltpu.VMEM((seq // ch, rows, ch), F32)],
        compiler_params=_params("parallel", "arbitrary", vmem_mb=48),
        name="mixer_b",
    )(pb, pb, pb, pb, pb, pf, tri)


def _head_consts():
    lane = np.arange(C_WIDTH)
    same = (lane[:, None] // HEAD_DIM == lane[None, :] // HEAD_DIM).astype(np.float32)
    return jnp.asarray(same)


def _rwkv1_kernel(cin_ref, prev_ref, vf_ref, mu_ref, vec_ref, lora_ref, vl_ref, bd_ref,
                  phi_ref, psi_ref, rp_ref, y0_ref, g_ref, bonus_ref, v_ref, *, rows, use_v_lora):
    i = pl.program_id(1)
    pc = cin_ref[0][:, :C_IN]
    prev = jnp.where(i > 0, prev_ref[0][SUBLANES - 1:SUBLANES, :C_IN], 0.0)
    row = lax.broadcasted_iota(I32, (rows, 1), 0)
    shifted = jnp.where(row == 0, prev, pltpu.roll(pc, 1, 0))
    pc = pc + (shifted - pc) * mu_ref[...]
    r = pc[:, 0:C_WIDTH]
    k = pc[:, C_WIDTH:2 * C_WIDTH]
    v = pc[:, 2 * C_WIDTH:3 * C_WIDTH]
    xl = pc[:, 3 * C_WIDTH:C_IN]
    lane = lax.broadcasted_iota(I32, xl.shape, 1)
    feat = jnp.where(lane < C_LORA_W, jnp.tanh(xl), jnp.where(lane < C_LORA_W + C_LORA_A, xl, _sigmoid(xl)))
    w0, a0, k_k, k_a, r_k = (vec_ref[n:n + 1, :] for n in range(5))
    bd = bd_ref[...]
    bd_sel = bd.astype(BF16)
    w_raw = -_softplus(-(w0 + _dot3(feat, lora_ref[0]))) - 0.5
    a = _sigmoid(a0 + _dot3(feat, lora_ref[1]))
    g_ref[0] = _dot3(feat, lora_ref[2])
    if use_v_lora:
        v0 = vec_ref[5:6, :]
        v = v + (vf_ref[0] - v) * _sigmoid(v0 + _dot3(_dot3(v, vl_ref[0]), vl_ref[1]))
    v_ref[0] = v
    kk = k * k_k
    kk = kk / jnp.maximum(jnp.sqrt(_dot_sel(bd_sel, kk * kk, left=False)), 1e-12)
    k = k * (1.0 + (a - 1.0) * k_a)
    logw = -jnp.exp(w_raw)
    bonus_ref[0] = _dot_sel(bd_sel, r * k * r_k, left=False) * v
    av = -kk
    bv = kk * a

    cc = C_CHUNK
    ri = lax.broadcasted_iota(I32, (cc, cc), 0)
    ci = lax.broadcasted_iota(I32, (cc, cc), 1)
    tril = jnp.where(ci <= ri, 1.0, 0.0).astype(BF16)
    strict = ci < ri
    incl = ci <= ri
    eye_c = jnp.where(ci == ri, 1.0, 0.0)
    lane_w = lax.broadcasted_iota(I32, (1, C_WIDTH), 1)
    ri2 = lax.broadcasted_iota(I32, (C_WIDTH, C_WIDTH), 0)
    ci2 = lax.broadcasted_iota(I32, (C_WIDTH, C_WIDTH), 1)
    eye_w = jnp.where(ri2 == ci2, 1.0, 0.0)
    chunks = range(rows // cc)
    heads = range(C_HEADS)
    pairs = [(c, h) for c in chunks for h in heads]
    mh = [jnp.where(lane_w // HEAD_DIM == h, 1.0, 0.0) for h in heads]
    sl = [slice(c * cc, (c + 1) * cc) for c in chunks]
    cs = [_dot_sel(tril, logw[sl[c]]) for c in chunks]
    cs_end = [cs[c][cc - 1:cc, :] for c in chunks]
    at = [av[sl[c]] * jnp.exp(cs[c] - logw[sl[c]]) for c in chunks]
    rt = [r[sl[c]] * jnp.exp(cs[c]) for c in chunks]
    inv = [jnp.exp(-cs[c]) for c in chunks]
    rhs = [jnp.concatenate([bv[sl[c]] * inv[c], k[sl[c]] * inv[c]], axis=0) for c in chunks]
    tail = [jnp.exp(cs_end[c] - cs[c]) for c in chunks]
    vc = [v[sl[c]] for c in chunks]
    vc_b = [vc[c].astype(BF16) for c in chunks]
    ath = {(c, h): at[c] * mh[h] for c, h in pairs}
    aa = {(c, h): _dot3(jnp.concatenate([ath[c, h], rt[c] * mh[h]], axis=0), rhs[c], nt=True) for c, h in pairs}
    a_ab = {p: jnp.where(strict, aa[p][:cc, :cc], 0.0) for p in pairs}
    a_ak = {p: jnp.where(strict, aa[p][:cc, cc:], 0.0).astype(BF16) for p in pairs}
    a_rb = {p: jnp.where(incl, aa[p][cc:, :cc], 0.0).astype(BF16) for p in pairs}
    a_rk = {p: jnp.where(incl, aa[p][cc:, cc:], 0.0).astype(BF16) for p in pairs}
    x = {p: eye_c + a_ab[p] for p in pairs}
    pw = a_ab
    for _ in range(int(math.log2(cc)) - 1):
        pw = {p: _dot3(pw[p], pw[p]) for p in pairs}
        x = {p: x[p] + _dot3(x[p], pw[p]) for p in pairs}
    akv = {(c, h): _dot(a_ak[c, h], vc_b[c]) for c, h in pairs}
    ap_h = {p: _dot3(x[p], ath[p]) for p in pairs}
    w2_h = {(c, h): _dot3(x[c, h], akv[c, h]) * mh[h] for c, h in pairs}
    rp_h = {p: _dot(a_rb[p], ap_h[p].astype(BF16)) for p in pairs}
    y0_h = {(c, h): (_dot(a_rb[c, h], w2_h[c, h].astype(BF16)) + _dot(a_rk[c, h], vc_b[c])) * mh[h] for c, h in pairs}
    for c in chunks:
        ap = sum(ap_h[c, h] for h in heads)
        w2 = sum(w2_h[c, h] for h in heads)
        rp_ref[0, sl[c], :] = rt[c] + sum(rp_h[c, h] for h in heads)
        y0_ref[0, sl[c], :] = sum(y0_h[c, h] for h in heads)
        bh = bv[sl[c]] * tail[c]
        kh = k[sl[c]] * tail[c]
        phi_ref[0, c] = eye_w * jnp.exp(cs_end[c]) + _dot3(bh.T, ap) * bd
        psi_ref[0, c] = _dot(jnp.concatenate([bh, kh], axis=0).T.astype(BF16),
                             jnp.concatenate([w2, vc[c]], axis=0).astype(BF16)) * bd


def _rwkv2_kernel(phi_ref, psi_ref, rp_ref, y0_ref, g_ref, bonus_ref, gn_ref, bd_ref, o_ref, s_ref):
    @pl.when(pl.program_id(1) == 0)
    def _():
        s_ref[...] = jnp.zeros_like(s_ref)

    s = s_ref[...]
    y = _dot3(rp_ref[0], s) + y0_ref[0]
    s_ref[...] = _dot3(phi_ref[0, 0], s) + psi_ref[0, 0]
    bd_sel = bd_ref[...].astype(BF16)
    mu = _dot_sel(bd_sel, y, left=False) * (1.0 / HEAD_DIM)
    d = y - mu
    var = _dot_sel(bd_sel, d * d, left=False) * (1.0 / HEAD_DIM)
    yn = d * lax.rsqrt(var + C_GN_EPS) * gn_ref[0:1, :] + gn_ref[1:2, :]
    o_ref[0] = ((yn + bonus_ref[0]) * g_ref[0]).astype(o_ref.dtype)


def _mixer_c(pf, v_first, mu, vec, lora, vl, gn, bd, batch, seq, use_v_lora):
    rows = min(256, seq)
    nblk = seq // rows
    nch = seq // C_CHUNK
    cpb = rows // C_CHUNK
    kern = functools.partial(_rwkv1_kernel, rows=rows, use_v_lora=use_v_lora)
    full2 = lambda a: pl.BlockSpec(a.shape, lambda b, i: (0,) * a.ndim)
    seq_spec = pl.BlockSpec((1, rows, C_WIDTH), lambda b, i: (b, i, 0))
    mat_spec = pl.BlockSpec((1, cpb, C_WIDTH, C_WIDTH), lambda b, i: (b, i, 0, 0))
    seq_shape = jax.ShapeDtypeStruct((batch, seq, C_WIDTH), F32)
    mat_shape = jax.ShapeDtypeStruct((batch, nch, C_WIDTH, C_WIDTH), F32)
    prev_blk = rows // SUBLANES
    phi, psi, rp, y0, g, bonus, v = pl.pallas_call(
        kern,
        grid=(batch, nblk),
        in_specs=[pl.BlockSpec((1, rows, 1024), lambda b, i: (b, i, PF_CIN // 1024)),
                  pl.BlockSpec((1, SUBLANES, 1024), lambda b, i: (b, jnp.maximum(i * prev_blk - 1, 0), PF_CIN // 1024)),
                  seq_spec, full2(mu), full2(vec), full2(lora), full2(vl), full2(bd)],
        out_specs=[mat_spec, mat_spec, seq_spec, seq_spec, seq_spec, seq_spec, seq_spec],
        out_shape=[mat_shape, mat_shape, seq_shape, seq_shape, seq_shape, seq_shape, seq_shape],
        compiler_params=_params("parallel", "parallel", vmem_mb=48),
        name="rwkv_chunks",
    )(pf, pf, v_first, mu, vec, lora, vl, bd)
    cseq = pl.BlockSpec((1, C_CHUNK, C_WIDTH), lambda b, c: (b, c, 0))
    cmat = pl.BlockSpec((1, 1, C_WIDTH, C_WIDTH), lambda b, c: (b, c, 0, 0))
    o = pl.pallas_call(
        _rwkv2_kernel,
        grid=(batch, nch),
        in_specs=[cmat, cmat, cseq, cseq, cseq, cseq,
                  pl.BlockSpec(gn.shape, lambda b, c: (0, 0)), pl.BlockSpec(bd.shape, lambda b, c: (0, 0))],
        out_specs=cseq,
        out_shape=jax.ShapeDtypeStruct((batch, seq, C_WIDTH), BF16),
        scratch_shapes=[pltpu.VMEM((C_WIDTH, C_WIDTH), F32)],
        compiler_params=_params("parallel", "arbitrary"),
        name="rwkv_scan",
    )(phi, psi, rp, y0, g, bonus, gn, bd)
    return o, v


def _ssd_consts():
    expand = np.zeros((LANES, D_INNER), np.float32)
    for h in range(D_HEADS):
        expand[MISC_DT + h, h * HEAD_DIM:(h + 1) * HEAD_DIM] = 1.0
    return jnp.asarray(expand)


def _ssd_kernel(xbc_ref, z_ref, misc_ref, conv_ref, vec_ref, hp_ref, expand_ref, bd_ref, o_ref, st_ref, prev_ref):
    @pl.when(pl.program_id(1) == 0)
    def _():
        st_ref[...] = jnp.zeros_like(st_ref)
        prev_ref[...] = jnp.zeros_like(prev_ref)

    q = D_CHUNK
    x_raw = xbc_ref[0]
    ext = jnp.concatenate([prev_ref[...], x_raw], axis=0)
    conv = jnp.zeros((q, D_XBC), F32)
    for t in range(D_CONV):
        lo = SUBLANES - (D_CONV - 1) + t
        conv = conv + ext[lo:lo + q, :] * conv_ref[t:t + 1, :]
    prev_ref[...] = x_raw[q - SUBLANES:, :]
    xbc = _silu(conv + conv_ref[D_CONV:D_CONV + 1, :])
    xs = xbc[:, :D_INNER]
    bm = xbc[:, D_INNER:D_INNER + D_GROUPS * D_STATE]
    cm = xbc[:, D_INNER + D_GROUPS * D_STATE:]

    dt_col = _softplus(misc_ref[0] + hp_ref[0:1, :])
    a_col = dt_col * hp_ref[1:2, :]
    ri = lax.broadcasted_iota(I32, (q, q), 0)
    ci = lax.broadcasted_iota(I32, (q, q), 1)
    causal = ci <= ri
    tril = jnp.where(causal, 1.0, 0.0)
    acs_col = _doth(tril, a_col)
    acs_row = acs_col.T
    expand = expand_ref[...]
    acs = _doth(acs_col, expand)
    dt = _doth(dt_col, expand)
    acs_end = acs[q - 1:q, :]
    xdt = xs * dt

    lane = lax.broadcasted_iota(I32, (1, LANES), 1)
    lane_w = lax.broadcasted_iota(I32, (1, D_INNER), 1)
    left = lane < D_STATE
    bm_sw = pltpu.roll(bm, D_STATE, 1)
    cm_sw = pltpu.roll(cm, D_STATE, 1)
    b_exp = jnp.concatenate([jnp.where(left, bm, bm_sw), jnp.where(left, bm_sw, bm)], axis=1)
    c_exp = jnp.concatenate([jnp.where(left, cm, cm_sw), jnp.where(left, cm_sw, cm)], axis=1)
    cb = [_dot_nt(jnp.where(left == (g == 0), cm, 0.0), bm, HIGHEST) for g in range(D_GROUPS)]

    y = jnp.zeros((q, D_INNER), F32)
    for h in range(D_HEADS):
        col = acs_col[:, MISC_DT + h:MISC_DT + h + 1]
        rw = acs_row[MISC_DT + h:MISC_DT + h + 1, :]
        decay = jnp.exp(jnp.where(causal, col - rw, NEG_INF))
        scores = cb[h // (D_HEADS // D_GROUPS)] * decay
        y = y + jnp.where(lane_w // HEAD_DIM == h, _doth(scores, xdt), 0.0)
    st = st_ref[...]
    y = y + _doth(c_exp, st) * jnp.exp(acs)
    st_ref[...] = st * jnp.exp(acs_end) + _doth(b_exp.T, xdt * jnp.exp(acs_end - acs)) * bd_ref[...]
    y = y + xs * vec_ref[0:1, :]
    y = y * _silu(z_ref[0])
    half = D_INNER // D_GROUPS
    outs = []
    for g in range(D_GROUPS):
        yg = y[:, g * half:(g + 1) * half]
        outs.append(yg * lax.rsqrt(jnp.mean(yg * yg, axis=-1, keepdims=True) + D_NORM_EPS))
    o_ref[0] = (jnp.concatenate(outs, axis=1) * vec_ref[1:2, :]).astype(o_ref.dtype)


def _mixer_d(pf, conv, vec, hp, expand, bd, batch, seq):
    q = D_CHUNK
    full = lambda a: pl.BlockSpec(a.shape, lambda b, i: (0, 0))
    return pl.pallas_call(
        _ssd_kernel,
        grid=(batch, seq // q),
        in_specs=[pl.BlockSpec((1, q, D_XBC), lambda b, i: (b, i, PF_XBC // D_XBC)),
                  pl.BlockSpec((1, q, D_INNER), lambda b, i: (b, i, PF_Z // D_INNER)),
                  pl.BlockSpec((1, q, LANES), lambda b, i: (b, i, PF_MISC // LANES)),
                  full(conv), full(vec), full(hp), full(expand), full(bd)],
        out_specs=pl.BlockSpec((1, q, D_INNER), lambda b, i: (b, i, 0)),
        out_shape=jax.ShapeDtypeStruct((batch, seq, D_INNER), BF16),
        scratch_shapes=[pltpu.VMEM((D_INNER, D_INNER), F32), pltpu.VMEM((SUBLANES, D_XBC), F32)],
        compiler_params=_params("parallel", "arbitrary"),
        name="ssd",
    )(pf, pf, pf, conv, vec, hp, expand, bd)


def _merge_kernel(x_ref, gates_ref, oa_ref, ob_ref, oc_ref, od_ref, wbr_ref, wout_ref, ln_ref, wr_ref, br_ref,
                  x1_ref, route_ref):
    acc = None
    for n, o_ref in enumerate((oa_ref, ob_ref, oc_ref, od_ref)):
        term = _sigmoid(gates_ref[:, n * D_MODEL:(n + 1) * D_MODEL]) * _dot(o_ref[...], wbr_ref[n])
        acc = term if acc is None else acc + term
    h = DEEPNORM_ALPHA * x_ref[...] + _dot(acc.astype(BF16), wout_ref[...])
    x1 = _layer_norm(h, ln_ref[0:1, :], ln_ref[1:2, :])
    x1_ref[...] = x1

    logits = _doth(x1, wr_ref[...]) + br_ref[...]
    lane = lax.broadcasted_iota(I32, logits.shape, 1)
    big = jnp.int32(LANES)
    gl = jnp.where(lane < N_EXPERT_GROUPS, logits, NEG_INF)
    gm = jnp.max(gl, axis=-1, keepdims=True)
    pg_top = 1.0 / jnp.sum(jnp.exp(gl - gm), axis=-1, keepdims=True)
    g_sel = jnp.min(jnp.where(gl == gm, lane, big), axis=-1, keepdims=True)
    off = N_EXPERT_GROUPS + g_sel * EXPERTS_PER_GROUP
    el = jnp.where((lane >= off) & (lane < off + EXPERTS_PER_GROUP), logits, NEG_INF)
    em = jnp.max(el, axis=-1, keepdims=True)
    es = jnp.sum(jnp.exp(el - em), axis=-1, keepdims=True)
    idx1 = jnp.min(jnp.where(el == em, lane, big), axis=-1, keepdims=True)
    el2 = jnp.where(lane == idx1, NEG_INF, el)
    em2 = jnp.max(el2, axis=-1, keepdims=True)
    idx2 = jnp.min(jnp.where(el2 == em2, lane, big), axis=-1, keepdims=True)
    p1 = 1.0 / es
    p2 = jnp.exp(em2 - em) / es
    gate1 = pg_top * p1 / (p1 + p2)
    gate2 = pg_top * p2 / (p1 + p2)
    e1 = (idx1 - N_EXPERT_GROUPS).astype(F32)
    e2 = (idx2 - N_EXPERT_GROUPS).astype(F32)
    route_ref[...] = jnp.where(lane == 0, e1, jnp.where(lane == 1, e2, jnp.where(lane == 2, gate1,
                               jnp.where(lane == 3, gate2, 0.0))))


def _merge(x2d, pf, oa, ob, oc, od, wbr, wout, ln, wr, br, tm):
    m = x2d.shape[0]
    row = lambda w: pl.BlockSpec((tm, w), lambda i: (i, 0))
    full = lambda a: pl.BlockSpec(a.shape, lambda i: (0,) * a.ndim)
    return pl.pallas_call(
        _merge_kernel,
        grid=(m // tm,),
        in_specs=[row(D_MODEL), pl.BlockSpec((tm, N_BRANCH * D_MODEL), lambda i: (i, PF_GATES)),
                  row(A_WIDTH), row(B_WIDTH), row(C_WIDTH), row(D_INNER),
                  full(wbr), full(wout), full(ln), full(wr), full(br)],
        out_specs=[row(D_MODEL), row(LANES)],
        out_shape=[jax.ShapeDtypeStruct((m, D_MODEL), F32), jax.ShapeDtypeStruct((m, LANES), F32)],
        compiler_params=_params("parallel", vmem_mb=48),
        name="merge_route",
    )(x2d, pf, oa, ob, oc, od, wbr, wout, ln, wr, br)


def _ffn_kernel(be_ref, nv_ref, xs_ref, wg_ref, wu_ref, wd_ref, o_ref):
    i = pl.program_id(0)

    @pl.when(i < nv_ref[0])
    def _():
        xb = xs_ref[...]
        hg = _dot(xb, wg_ref[0, 0].astype(BF16))
        hu = _dot(xb, wu_ref[0, 0].astype(BF16))
        o_ref[...] = _dot((_silu(hg) * hu).astype(BF16), wd_ref[0, 0].astype(BF16))

    @pl.when(i >= nv_ref[0])
    def _():
        o_ref[...] = jnp.zeros_like(o_ref)


def _ffn(blk_expert, n_valid, xs, e_gate, e_up, e_down, layer):
    cap = xs.shape[0]
    wspec = lambda a: pl.BlockSpec((1, 1) + a.shape[2:], lambda i, be, nv: (layer, be[i], 0, 0))
    return pl.pallas_call(
        _ffn_kernel,
        grid_spec=pltpu.PrefetchScalarGridSpec(
            num_scalar_prefetch=2,
            grid=(cap // MOE_ROWS,),
            in_specs=[pl.BlockSpec((MOE_ROWS, D_MODEL), lambda i, be, nv: (i, 0)),
                      wspec(e_gate), wspec(e_up), wspec(e_down)],
            out_specs=pl.BlockSpec((MOE_ROWS, D_MODEL), lambda i, be, nv: (i, 0)),
        ),
        out_shape=jax.ShapeDtypeStruct((cap, D_MODEL), F32),
        compiler_params=_params("arbitrary", vmem_mb=48),
        name="expert_ffn",
    )(blk_expert, n_valid, xs, e_gate, e_up, e_down)


def _combine_kernel(x_ref, y0_ref, y1_ref, route_ref, ln_ref, o_ref):
    g0 = route_ref[:, 2:3]
    g1 = route_ref[:, 3:4]
    h = DEEPNORM_ALPHA * x_ref[...] + (y0_ref[...] * g0 + y1_ref[...] * g1)
    o_ref[...] = _layer_norm(h, ln_ref[0:1, :], ln_ref[1:2, :])


def _combine(x1, ysel, route, ln, tm):
    m = x1.shape[0]
    return pl.pallas_call(
        _combine_kernel,
        grid=(m // tm,),
        in_specs=[pl.BlockSpec((tm, D_MODEL), lambda i: (i, 0)),
                  pl.BlockSpec((tm, D_MODEL), lambda i: (i, 0)),
                  pl.BlockSpec((tm, D_MODEL), lambda i: (i, 1)),
                  pl.BlockSpec((tm, LANES), lambda i: (i, 0)),
                  pl.BlockSpec(ln.shape, lambda i: (0, 0))],
        out_specs=pl.BlockSpec((tm, D_MODEL), lambda i: (i, 0)),
        out_shape=jax.ShapeDtypeStruct((m, D_MODEL), F32),
        compiler_params=_params("parallel"),
        name="combine",
    )(x1, ysel, ysel, route, ln)


def _dispatch_tables(route, m):
    flat_e = route[:, 0:TOP_K_EXPERTS].astype(I32).reshape(-1)
    n_assign = m * TOP_K_EXPERTS
    onehot = (flat_e[:, None] == jnp.arange(N_EXPERTS, dtype=I32)[None, :]).astype(I32)
    csum = jnp.cumsum(onehot, axis=0)
    rank = jnp.sum(csum * onehot, axis=1) - 1
    counts = csum[-1]
    padded = (counts + MOE_ROWS - 1) // MOE_ROWS * MOE_ROWS
    pad_end = jnp.cumsum(padded)
    pad_start = pad_end - padded
    dest = pad_start[flat_e] + rank
    cap = (n_assign + N_EXPERTS * (MOE_ROWS - 1) + MOE_ROWS - 1) // MOE_ROWS * MOE_ROWS
    n_blocks = cap // MOE_ROWS
    blk_start = jnp.arange(n_blocks, dtype=I32) * MOE_ROWS
    blk_expert = jnp.minimum(jnp.sum((pad_end[None, :] <= blk_start[:, None]).astype(I32), axis=1), N_EXPERTS - 1)
    buf_tok = jnp.zeros((cap,), I32).at[dest].set(jnp.arange(n_assign, dtype=I32) // TOP_K_EXPERTS)
    n_valid = (pad_end[-1] // MOE_ROWS).astype(I32).reshape(1)
    return dest, buf_tok, blk_expert, n_valid


def _row_pad(a, rows):
    return jnp.pad(a, ((0, rows - a.shape[0]), (0, 0)))


def kernel(x, w_in, c_mu, c_w0, c_w2, c_a0, c_a2, c_g2, c_kk, c_ka, c_rk, c_gn_w, c_gn_b, c_v0, c_v1, c_v2,
           d_conv_w, d_conv_b, d_dt_bias, d_a_log, d_skip, d_norm_w, w_branch, w_out, ln1_g, ln1_b,
           r_group, r_group_b, r_expert, r_expert_b, e_gate, e_up, e_down, ln2_g, ln2_b):
    batch, seq, _ = x.shape
    m = batch * seq
    biases = _mixa_bias()
    tw = min(256, seq)
    tri = jnp.asarray(np.triu(np.ones((tw, tw), np.float32))).astype(BF16)
    bd = _head_consts()
    expand = _ssd_consts()
    x2d = x.reshape(m, D_MODEL)
    v_first = jnp.zeros((batch, seq, C_WIDTH), F32)
    tm_proj = min(2048, m)
    tm_tok = min(512, m)
    for l in range(DEPTH):
        wb, wf = _proj_weights(w_in[l])
        pb = _proj(x2d, wb, BF16, tm_proj, PROJ_TN).reshape(batch, seq, PB_WIDTH)
        pf = _proj(x2d, wf, F32, tm_proj, PROJ_TN).reshape(batch, seq, PF_WIDTH)

        kv_pad = jnp.pad(pb[:, :, PB_AKV:PB_AQ], ((0, 0), (A_MAXWIN, 0), (0, 0)))
        o_a = _mixer_a(pb, kv_pad, biases, batch, seq)
        o_b = _mixer_b(pb, pf, tri, batch, seq)

        use_v_lora = l > 0
        vec_rows = [c_w0[l], c_a0[l], c_kk[l], c_ka[l], c_rk[l].reshape(-1)]
        vec_rows.append(c_v0[l - 1] if use_v_lora else jnp.zeros((C_WIDTH,), F32))
        vec = _row_pad(jnp.stack(vec_rows), SUBLANES)
        lora = jnp.stack([
            jnp.pad(c_w2[l], ((0, LANES - C_LORA_W), (0, 0))),
            jnp.pad(c_a2[l], ((C_LORA_W, LANES - C_LORA_W - C_LORA_A), (0, 0))),
            jnp.pad(c_g2[l], ((C_LORA_W + C_LORA_A, 0), (0, 0)))])
        if use_v_lora:
            vl = jnp.stack([jnp.pad(c_v1[l - 1], ((0, 0), (0, C_WIDTH - C_LORA_V))),
                            jnp.pad(c_v2[l - 1], ((0, C_WIDTH - C_LORA_V), (0, 0)))])
        else:
            vl = jnp.zeros((2, C_WIDTH, C_WIDTH), F32)
        gn = _row_pad(jnp.stack([c_gn_w[l], c_gn_b[l]]), SUBLANES)
        o_c, v_c = _mixer_c(pf, v_first, c_mu[l].reshape(1, C_IN), vec, lora, vl, gn, bd, batch, seq, use_v_lora)
        if l == 0:
            v_first = v_c

        conv = _row_pad(jnp.concatenate([d_conv_w[l], d_conv_b[l][None, :]], axis=0), SUBLANES)
        dvec = _row_pad(jnp.stack([jnp.repeat(d_skip[l], HEAD_DIM), d_norm_w[l]]), SUBLANES)
        place = lambda a: jnp.pad(a, (MISC_DT, LANES - MISC_DT - D_HEADS))
        hp = _row_pad(jnp.stack([place(d_dt_bias[l]), place(-jnp.exp(d_a_log[l]))]), SUBLANES)
        o_d = _mixer_d(pf, conv, dvec, hp, expand, bd, batch, seq)

        wr = jnp.pad(jnp.concatenate([r_group[l], r_expert[l]], axis=1),
                     ((0, 0), (0, LANES - N_EXPERT_GROUPS - N_EXPERTS)))
        br = jnp.pad(jnp.concatenate([r_group_b[l], r_expert_b[l]]), (0, LANES - N_EXPERT_GROUPS - N_EXPERTS))
        ln1 = _row_pad(jnp.stack([ln1_g[l], ln1_b[l]]), SUBLANES)
        x1, route = _merge(x2d, pf.reshape(m, PF_WIDTH), o_a.reshape(m, -1), o_b.reshape(m, -1),
                           o_c.reshape(m, -1), o_d.reshape(m, -1), w_branch[l].astype(BF16),
                           w_out[l].astype(BF16), ln1, wr, br.reshape(1, LANES), tm_tok)

        dest, buf_tok, blk_expert, n_valid = _dispatch_tables(route, m)
        xs = jnp.take(x1.astype(BF16), buf_tok, axis=0)
        yb = _ffn(blk_expert, n_valid, xs, e_gate, e_up, e_down, l)
        ysel = jnp.take(yb, dest, axis=0).reshape(m, TOP_K_EXPERTS * D_MODEL)
        ln2 = _row_pad(jnp.stack([ln2_g[l], ln2_b[l]]), SUBLANES)
        x2d = _combine(x1, ysel, route, ln2, tm_tok)
    return x2d.reshape(batch, seq, D_MODEL)
```

```python
import functools
import math

import jax
import jax.numpy as jnp
import numpy as np
from jax import lax
from jax.experimental import pallas as pl
from jax.experimental.pallas import tpu as pltpu

F32 = jnp.float32
BF16 = jnp.bfloat16
I32 = jnp.int32
NEG_INF = float("-inf")
INT_MIN = -(2 ** 31)

LANES = 128
SUBLANES = 8

D_MODEL = 1024
DEPTH = 2
HEAD_DIM = 64
Q_BLOCK = 128

A_HEADS = 4
A_PATTERNS = ((128, 1), (512, 4), (2048, 16))
A_GROUPS = len(A_PATTERNS)
A_WIDTH = A_HEADS * HEAD_DIM
A_MAXWIN = max(w for w, _ in A_PATTERNS)

B_HEADS = 4
B_WIDTH = B_HEADS * HEAD_DIM
IDX_HEADS = 4
IDX_DIM = 64
TOPK_MAX = 256

C_HEADS = 4
C_WIDTH = C_HEADS * HEAD_DIM
C_LORA_W = 32
C_LORA_A = 32
C_LORA_G = 64
C_LORA_V = 16
C_IN = 3 * C_WIDTH + C_LORA_W + C_LORA_A + C_LORA_G
C_GN_EPS = 64e-5
C_CHUNK = 64

D_HEADS = 4
D_INNER = D_HEADS * HEAD_DIM
D_GROUPS = 2
D_STATE = 64
D_CONV = 4
D_CHUNK = 128
D_XBC = D_INNER + 2 * D_GROUPS * D_STATE
D_NORM_EPS = 1e-5

N_BRANCH = 4
N_EXPERT_GROUPS = 4
EXPERTS_PER_GROUP = 8
N_EXPERTS = N_EXPERT_GROUPS * EXPERTS_PER_GROUP
TOP_K_EXPERTS = 2
D_EXPERT = 512
MOE_ROWS = 512

LN_EPS = 1e-5
DEEPNORM_ALPHA = (2 * DEPTH) ** 0.25

PB_AKV = 0
PB_AQ = PB_AKV + A_HEADS * A_GROUPS * 2 * HEAD_DIM
PB_BQ = PB_AQ + A_HEADS * 4 * HEAD_DIM
PB_BK = PB_BQ + B_WIDTH
PB_BV = PB_BK + B_WIDTH
PB_IQ = PB_BV + B_WIDTH
PB_IK = PB_IQ + IDX_HEADS * IDX_DIM
PB_WIDTH = PB_IK + 256
PF_GATES = 0
PF_CIN = PF_GATES + N_BRANCH * D_MODEL
PF_MISC = PF_CIN + C_IN
PF_XBC = PF_CIN + 1024
PF_Z = PF_XBC + D_XBC
PROJ_TN = 768
PF_WIDTH = -(-(PF_Z + D_INNER) // PROJ_TN) * PROJ_TN
MISC_DT = 4


def _dot(a, b, precision=None):
    return jnp.dot(a, b, preferred_element_type=F32, precision=precision)


def _dot_nt(a, b, precision=None):
    return lax.dot_general(a, b, (((1,), (1,)), ((), ())), preferred_element_type=F32, precision=precision)


def _split2(a):
    hi = a.astype(BF16)
    return hi, (a - hi.astype(F32)).astype(BF16)


def _dot3(a, b, nt=False):
    mm = _dot_nt if nt else _dot
    ah, al = _split2(a)
    bh, bl = _split2(b)
    return mm(ah, bh) + (mm(ah, bl) + mm(al, bh))


def _dot_sel(sel, x, left=True):
    hi = x.astype(BF16)
    r1 = x - hi.astype(F32)
    mid = r1.astype(BF16)
    lo = (r1 - mid.astype(F32)).astype(BF16)
    if left:
        return _dot(sel, hi) + (_dot(sel, mid) + _dot(sel, lo))
    return _dot(hi, sel) + (_dot(mid, sel) + _dot(lo, sel))


def _sigmoid(x):
    return 1.0 / (1.0 + jnp.exp(-x))


def _softplus(x):
    return jnp.maximum(x, 0.0) + jnp.log1p(jnp.exp(-jnp.abs(x)))


def _silu(x):
    return x * _sigmoid(x)


def _layer_norm(h, g, b):
    mu = jnp.mean(h, axis=-1, keepdims=True)
    d = h - mu
    var = jnp.mean(d * d, axis=-1, keepdims=True)
    return d * lax.rsqrt(var + LN_EPS) * g + b


def _params(*sem, vmem_mb=None):
    kw = {}
    if vmem_mb is not None:
        kw["vmem_limit_bytes"] = vmem_mb * 1024 * 1024
    return pltpu.CompilerParams(dimension_semantics=sem, **kw)


def _proj_kernel(x_ref, w_ref, o_ref):
    o_ref[...] = _dot(x_ref[...].astype(BF16), w_ref[...]).astype(o_ref.dtype)


def _proj(x2d, w, out_dtype, tm, tn):
    m, k = x2d.shape
    n = w.shape[1]
    return pl.pallas_call(
        _proj_kernel,
        grid=(m // tm, n // tn),
        in_specs=[pl.BlockSpec((tm, k), lambda i, j: (i, 0)),
                  pl.BlockSpec((k, tn), lambda i, j: (0, j))],
        out_specs=pl.BlockSpec((tm, tn), lambda i, j: (i, j)),
        out_shape=jax.ShapeDtypeStruct((m, n), out_dtype),
        compiler_params=_params("parallel", "arbitrary", vmem_mb=48),
        name="in_proj",
    )(x2d, w)


def _proj_weights(w):
    k = w.shape[0]
    off = 0
    seg = {}
    for name, size in (("a_qkv", 3 * A_GROUPS * A_WIDTH), ("b_qkv", 3 * B_WIDTH), ("b_idx_q", IDX_HEADS * IDX_DIM),
                       ("b_idx_k", IDX_DIM), ("b_idx_w", IDX_HEADS), ("c_in", C_IN), ("d_z", D_INNER),
                       ("d_xbc", D_XBC), ("d_dt", D_HEADS), ("gates", N_BRANCH * D_MODEL)):
        seg[name] = w[:, off:off + size]
        off += size
    a = seg["a_qkv"].reshape(k, 3, A_GROUPS, A_HEADS, HEAD_DIM)
    a_kv = jnp.transpose(a[:, 1:3], (0, 3, 2, 1, 4)).reshape(k, -1)
    a_q = jnp.transpose(a[:, 0], (0, 2, 1, 3))
    a_q = jnp.pad(a_q, ((0, 0), (0, 0), (0, 4 - A_GROUPS), (0, 0))).reshape(k, -1)
    zeros = lambda n: jnp.zeros((k, n), w.dtype)
    wb = jnp.concatenate([a_kv, a_q, seg["b_qkv"], seg["b_idx_q"], seg["b_idx_k"], zeros(256 - IDX_DIM)], axis=1)
    misc = jnp.concatenate([seg["b_idx_w"], seg["d_dt"], zeros(LANES - IDX_HEADS - D_HEADS)], axis=1)
    wf = jnp.concatenate([seg["gates"], seg["c_in"], misc, seg["d_xbc"], seg["d_z"]], axis=1)
    wf = jnp.concatenate([wf, zeros(PF_WIDTH - wf.shape[1])], axis=1)
    assert wb.shape[1] == PB_WIDTH and PB_WIDTH % PROJ_TN == 0
    return wb.astype(BF16), wf.astype(BF16)


def _mixa_bias():
    out = []
    for win, dil in A_PATTERNS:
        r = np.arange(Q_BLOCK)[:, None]
        c = np.arange(win + Q_BLOCK)[None, :]
        d = r + win - c
        ok = (d >= 0) & (d <= win) & (d % dil == 0)
        out.append(jnp.asarray(np.where(ok, 0.0, -np.inf).astype(np.float32)))
    return out


def _mixa_kernel(q_ref, kv_ref, b0_ref, b1_ref, b2_ref, o_ref):
    start = pl.program_id(2) * Q_BLOCK
    scale = HEAD_DIM ** -0.5
    bias_refs = (b0_ref, b1_ref, b2_ref)
    outs = []
    for hh in range(2):
        ms, ls, os_ = [], [], []
        for g, (win, _) in enumerate(A_PATTERNS):
            span = win + Q_BLOCK
            q = q_ref[0, :, hh * 256 + g * HEAD_DIM: hh * 256 + (g + 1) * HEAD_DIM]
            base = pl.multiple_of(start + (A_MAXWIN - win), Q_BLOCK)
            c0 = hh * (A_GROUPS * 2 * HEAD_DIM) + g * 2 * HEAD_DIM
            k = kv_ref[0, pl.ds(base, span), c0:c0 + HEAD_DIM]
            v = kv_ref[0, pl.ds(base, span), c0 + HEAD_DIM:c0 + 2 * HEAD_DIM]
            s = _dot_nt(q, k) * scale + bias_refs[g][...]
            col = lax.broadcasted_iota(I32, (Q_BLOCK, span), 1)
            s = jnp.where(col >= win - start, s, NEG_INF)
            m = jnp.max(s, axis=-1, keepdims=True)
            p = jnp.exp(s - m)
            ls.append(jnp.sum(p, axis=-1, keepdims=True))
            os_.append(_dot(p.astype(BF16), v))
            ms.append(m)
        m_all = jnp.maximum(jnp.maximum(ms[0], ms[1]), ms[2])
        num = jnp.zeros((Q_BLOCK, HEAD_DIM), F32)
        den = jnp.zeros((Q_BLOCK, 1), F32)
        for g in range(A_GROUPS):
            wg = jnp.exp(ms[g] - m_all)
            num = num + wg * os_[g]
            den = den + wg * ls[g]
        outs.append(num / den)
    o_ref[0] = jnp.concatenate(outs, axis=1).astype(o_ref.dtype)


def _mixer_a(pb, kv_pad, biases, batch, seq):
    nblk = seq // Q_BLOCK
    pair_kv = 2 * A_GROUPS * 2 * HEAD_DIM
    in_specs = [pl.BlockSpec((1, Q_BLOCK, 512), lambda b, hp, i: (b, i, PB_AQ // 512 + hp)),
                pl.BlockSpec((1, seq + A_MAXWIN, pair_kv), lambda b, hp, i: (b, 0, hp))]
    for bias in biases:
        in_specs.append(pl.BlockSpec(bias.shape, lambda b, hp, i: (0, 0)))
    return pl.pallas_call(
        _mixa_kernel,
        grid=(batch, A_HEADS // 2, nblk),
        in_specs=in_specs,
        out_specs=pl.BlockSpec((1, Q_BLOCK, 2 * HEAD_DIM), lambda b, hp, i: (b, i, hp)),
        out_shape=jax.ShapeDtypeStruct((batch, seq, A_WIDTH), BF16),
        compiler_params=_params("parallel", "parallel", "arbitrary", vmem_mb=48),
        name="mixer_a",
    )(pb, kv_pad, *biases)


def _mixb_kernel(qb_ref, kb_ref, vb_ref, iq_ref, ik_ref, misc_ref, tri_ref, o_ref, keys_ref, selb_ref, *, top_k):
    _, rows, ch = keys_ref.shape
    start = pl.program_id(1) * rows
    nch = (start + rows + ch - 1) // ch
    qpos = start + lax.broadcasted_iota(I32, (rows, 1), 0)
    w = misc_ref[0][:, 0:IDX_HEADS] * (IDX_DIM ** -0.5 * IDX_HEADS ** -0.5)
    iq = iq_ref[0]
    zero_col = jnp.zeros((rows, 1), F32)

    def score_chunk(c, carry):
        off = pl.multiple_of(c * ch, ch)
        ik = ik_ref[0, pl.ds(off, ch), 0:IDX_DIM]
        acc = jnp.zeros((rows, ch), F32)
        for h in range(IDX_HEADS):
            lg = _dot_nt(iq[:, h * IDX_DIM:(h + 1) * IDX_DIM], ik)
            acc = acc + jnp.maximum(lg, 0.0) * w[:, h:h + 1]
        acc = acc + 0.0
        bits = lax.bitcast_convert_type(acc, I32)
        key = bits ^ ((bits >> 31) & 0x7FFFFFFF)
        kpos = off + lax.broadcasted_iota(I32, (rows, ch), 1)
        keys_ref[c] = jnp.where(kpos <= qpos, key, INT_MIN)
        return carry

    lax.fori_loop(0, nch, score_chunk, 0)

    def count(pred):
        def chunk(c, cnt):
            kc = keys_ref[c]
            for j in range(ch // LANES):
                cnt = cnt + jnp.where(pred(kc[:, j * LANES:(j + 1) * LANES]), 1.0, 0.0)
            return cnt
        cnt = lax.fori_loop(0, nch, chunk, jnp.zeros((rows, LANES), F32))
        return jnp.sum(cnt, axis=-1, keepdims=True)

    kf = float(top_k)
    ans = jnp.where(count(lambda kc: kc >= 0) >= kf, 0, INT_MIN).astype(I32)

    def body(it, ans):
        cand = ans + jnp.left_shift(jnp.int32(1), 30 - it)
        return jnp.where(count(lambda kc: kc >= cand) >= kf, cand, ans)

    thr = lax.fori_loop(0, 31, body, ans)
    need = kf - count(lambda kc: kc > thr)

    tw = tri_ref.shape[0]

    def select_chunk(c, run):
        kc_all = keys_ref[c]
        for j in range(ch // tw):
            kc = kc_all[:, j * tw:(j + 1) * tw]
            eq = kc == thr
            pre = _dot(jnp.where(eq, 1.0, 0.0).astype(BF16), tri_ref[...])
            take = jnp.where(eq, jnp.where(pre + run <= need, 1.0, 0.0), 0.0)
            sel = jnp.where(kc > thr, 1.0, take)
            sel = jnp.where(kc != INT_MIN, sel, 0.0)
            selb_ref[c, :, j * tw:(j + 1) * tw] = jnp.where(sel > 0.5, 0.0, NEG_INF)
            run = run + pre[:, tw - 1:tw]
        return run

    def select_chunk_no_ties(c, carry):
        kc = keys_ref[c]
        selb_ref[c] = jnp.where(kc >= thr, jnp.where(kc != INT_MIN, 0.0, NEG_INF), NEG_INF)
        return carry

    excess = jnp.max(count(lambda kc: kc >= thr)) > kf
    lax.cond(excess,
             lambda: lax.fori_loop(0, nch, select_chunk, zero_col),
             lambda: lax.fori_loop(0, nch, select_chunk_no_ties, zero_col))

    heads = range(B_HEADS)
    lane = lax.broadcasted_iota(I32, (1, B_WIDTH), 1) // HEAD_DIM
    qs = (qb_ref[0].astype(F32) * HEAD_DIM ** -0.5).astype(BF16)
    qm = [jnp.where(lane == h, qs, jnp.zeros_like(qs)) for h in heads]

    def per_lane(cols):
        out = cols[B_HEADS - 1]
        for h in reversed(range(B_HEADS - 1)):
            out = jnp.where(lane <= h, cols[h], out)
        return out

    def attend_chunk(c, carry):
        off = pl.multiple_of(c * ch, ch)
        bias = selb_ref[c]
        kc = kb_ref[0, pl.ds(off, ch), :]
        vc = vb_ref[0, pl.ds(off, ch), :]
        ms, ls, acc = carry[:B_HEADS], carry[B_HEADS:2 * B_HEADS], carry[2 * B_HEADS]
        s = [_dot_nt(qm[h], kc) + bias for h in heads]
        m_new = [jnp.maximum(ms[h], jnp.max(s[h], axis=-1, keepdims=True)) for h in heads]
        m_safe = [jnp.where(m_new[h] == NEG_INF, 0.0, m_new[h]) for h in heads]
        alpha = [jnp.exp(ms[h] - m_safe[h]) for h in heads]
        p = [jnp.exp(s[h] - m_safe[h]) for h in heads]
        l_new = [alpha[h] * ls[h] + jnp.sum(p[h], axis=-1, keepdims=True) for h in heads]
        pv = [_dot(p[h].astype(BF16), vc) for h in heads]
        acc = per_lane(alpha) * acc + per_lane(pv)
        return tuple(m_new) + tuple(l_new) + (acc,)

    init = ((jnp.full((rows, 1), NEG_INF, F32),) * B_HEADS + (zero_col,) * B_HEADS
            + (jnp.zeros((rows, B_WIDTH), F32),))
    res = lax.fori_loop(0, nch, attend_chunk, init)
    o_ref[0] = (res[2 * B_HEADS] / per_lane(res[B_HEADS:2 * B_HEADS])).astype(o_ref.dtype)


def _mixer_b(pb, pf, tri, batch, seq):
    rows = Q_BLOCK
    top_k = min(TOPK_MAX, seq // 4)
    ch = min(512, seq)
    kern = functools.partial(_mixb_kernel, top_k=top_k)
    return pl.pallas_call(
        kern,
        grid=(batch, seq // rows),
        in_specs=[pl.BlockSpec((1, rows, B_WIDTH), lambda b, i: (b, i, PB_BQ // B_WIDTH)),
                  pl.BlockSpec((1, seq, B_WIDTH), lambda b, i: (b, 0, PB_BK // B_WIDTH)),
                  pl.BlockSpec((1, seq, B_WIDTH), lambda b, i: (b, 0, PB_BV // B_WIDTH)),
                  pl.BlockSpec((1, rows, 256), lambda b, i: (b, i, PB_IQ // 256)),
                  pl.BlockSpec((1, seq, LANES), lambda b, i: (b, 0, PB_IK // LANES)),
                  pl.BlockSpec((1, rows, LANES), lambda b, i: (b, i, PF_MISC // LANES)),
                  pl.BlockSpec(tri.shape, lambda b, i: (0, 0))],
        out_specs=pl.BlockSpec((1, rows, B_WIDTH), lambda b, i: (b, i, 0)),
        out_shape=jax.ShapeDtypeStruct((batch, seq, B_WIDTH), BF16),
        scratch_shapes=[pltpu.VMEM((seq // ch, rows, ch), I32), pltpu.VMEM((seq // ch, rows, ch), F32)],
        compiler_params=_params("parallel", "arbitrary", vmem_mb=48),
        name="mixer_b",
    )(pb, pb, pb, pb, pb, pf, tri)


def _head_consts():
    lane = np.arange(C_WIDTH)
    same = (lane[:, None] // HEAD_DIM == lane[None, :] // HEAD_DIM).astype(np.float32)
    return jnp.asarray(same)


def _rwkv1_kernel(cin_ref, prev_ref, vf_ref, mu_ref, vec_ref, lora_ref, vl_ref, bd_ref,
                  phi_ref, psi_ref, rp_ref, y0_ref, g_ref, bonus_ref, v_ref, *, rows, use_v_lora):
    i = pl.program_id(1)
    pc = cin_ref[0][:, :C_IN]
    prev = jnp.where(i > 0, prev_ref[0][SUBLANES - 1:SUBLANES, :C_IN], 0.0)
    row = lax.broadcasted_iota(I32, (rows, 1), 0)
    shifted = jnp.where(row == 0, prev, pltpu.roll(pc, 1, 0))
    pc = pc + (shifted - pc) * mu_ref[...]
    r = pc[:, 0:C_WIDTH]
    k = pc[:, C_WIDTH:2 * C_WIDTH]
    v = pc[:, 2 * C_WIDTH:3 * C_WIDTH]
    xl = pc[:, 3 * C_WIDTH:C_IN]
    lane = lax.broadcasted_iota(I32, xl.shape, 1)
    feat = jnp.where(lane < C_LORA_W, jnp.tanh(xl), jnp.where(lane < C_LORA_W + C_LORA_A, xl, _sigmoid(xl)))
    w0, a0, k_k, k_a, r_k = (vec_ref[n:n + 1, :] for n in range(5))
    bd = bd_ref[...]
    bd_sel = bd.astype(BF16)
    w_raw = -_softplus(-(w0 + _dot3(feat, lora_ref[0]))) - 0.5
    a = _sigmoid(a0 + _dot3(feat, lora_ref[1]))
    g_ref[0] = _dot3(feat, lora_ref[2])
    if use_v_lora:
        v0 = vec_ref[5:6, :]
        v = v + (vf_ref[0] - v) * _sigmoid(v0 + _dot3(_dot3(v, vl_ref[0]), vl_ref[1]))
    v_ref[0] = v
    kk = k * k_k
    kk = kk / jnp.maximum(jnp.sqrt(_dot_sel(bd_sel, kk * kk, left=False)), 1e-12)
    k = k * (1.0 + (a - 1.0) * k_a)
    logw = -jnp.exp(w_raw)
    bonus_ref[0] = _dot_sel(bd_sel, r * k * r_k, left=False) * v
    av = -kk
    bv = kk * a

    cc = C_CHUNK
    ri = lax.broadcasted_iota(I32, (cc, cc), 0)
    ci = lax.broadcasted_iota(I32, (cc, cc), 1)
    tril = jnp.where(ci <= ri, 1.0, 0.0).astype(BF16)
    strict = ci < ri
    incl = ci <= ri
    eye_c = jnp.where(ci == ri, 1.0, 0.0)
    lane_w = lax.broadcasted_iota(I32, (1, C_WIDTH), 1)
    ri2 = lax.broadcasted_iota(I32, (C_WIDTH, C_WIDTH), 0)
    ci2 = lax.broadcasted_iota(I32, (C_WIDTH, C_WIDTH), 1)
    eye_w = jnp.where(ri2 == ci2, 1.0, 0.0)
    chunks = range(rows // cc)
    heads = range(C_HEADS)
    pairs = [(c, h) for c in chunks for h in heads]
    mh = [jnp.where(lane_w // HEAD_DIM == h, 1.0, 0.0) for h in heads]
    sl = [slice(c * cc, (c + 1) * cc) for c in chunks]
    cs = [_dot_sel(tril, logw[sl[c]]) for c in chunks]
    cs_end = [cs[c][cc - 1:cc, :] for c in chunks]
    at = [av[sl[c]] * jnp.exp(cs[c] - logw[sl[c]]) for c in chunks]
    rt = [r[sl[c]] * jnp.exp(cs[c]) for c in chunks]
    inv = [jnp.exp(-cs[c]) for c in chunks]
    rhs = [jnp.concatenate([bv[sl[c]] * inv[c], k[sl[c]] * inv[c]], axis=0) for c in chunks]
    tail = [jnp.exp(cs_end[c] - cs[c]) for c in chunks]
    vc = [v[sl[c]] for c in chunks]
    vc_b = [vc[c].astype(BF16) for c in chunks]
    ath = {(c, h): at[c] * mh[h] for c, h in pairs}
    aa = {(c, h): _dot3(jnp.concatenate([ath[c, h], rt[c] * mh[h]], axis=0), rhs[c], nt=True) for c, h in pairs}
    a_ab = {p: jnp.where(strict, aa[p][:cc, :cc], 0.0) for p in pairs}
    a_ak = {p: jnp.where(strict, aa[p][:cc, cc:], 0.0).astype(BF16) for p in pairs}
    a_rb = {p: jnp.where(incl, aa[p][cc:, :cc], 0.0).astype(BF16) for p in pairs}
    a_rk = {p: jnp.where(incl, aa[p][cc:, cc:], 0.0).astype(BF16) for p in pairs}
    x = {p: eye_c + a_ab[p] for p in pairs}
    pw = a_ab
    for _ in range(int(math.log2(cc)) - 1):
        pw = {p: _dot3(pw[p], pw[p]) for p in pairs}
        x = {p: x[p] + _dot3(x[p], pw[p]) for p in pairs}
    akv = {(c, h): _dot(a_ak[c, h], vc_b[c]) for c, h in pairs}
    ap_h = {p: _dot3(x[p], ath[p]) for p in pairs}
    w2_h = {(c, h): _dot3(x[c, h], akv[c, h]) * mh[h] for c, h in pairs}
    rp_h = {p: _dot(a_rb[p], ap_h[p].astype(BF16)) for p in pairs}
    y0_h = {(c, h): (_dot(a_rb[c, h], w2_h[c, h].astype(BF16)) + _dot(a_rk[c, h], vc_b[c])) * mh[h] for c, h in pairs}
    for c in chunks:
        ap = sum(ap_h[c, h] for h in heads)
        w2 = sum(w2_h[c, h] for h in heads)
        rp_ref[0, sl[c], :] = rt[c] + sum(rp_h[c, h] for h in heads)
        y0_ref[0, sl[c], :] = sum(y0_h[c, h] for h in heads)
        bh = bv[sl[c]] * tail[c]
        kh = k[sl[c]] * tail[c]
        phi_ref[0, c] = eye_w * jnp.exp(cs_end[c]) + _dot3(bh.T, ap) * bd
        psi_ref[0, c] = _dot(jnp.concatenate([bh, kh], axis=0).T.astype(BF16),
                             jnp.concatenate([w2, vc[c]], axis=0).astype(BF16)) * bd


def _rwkv2_kernel(phi_ref, psi_ref, rp_ref, y0_ref, g_ref, bonus_ref, gn_ref, bd_ref, o_ref, s_ref):
    @pl.when(pl.program_id(1) == 0)
    def _():
        s_ref[...] = jnp.zeros_like(s_ref)

    states = [s_ref[...]]
    for c in range(phi_ref.shape[1]):
        states.append(_dot3(phi_ref[0, c], states[c]) + psi_ref[0, c])
    s_ref[...] = states[-1]
    cc = C_CHUNK
    y = jnp.concatenate([_dot3(rp_ref[0, c * cc:(c + 1) * cc, :], states[c]) for c in range(phi_ref.shape[1])],
                        axis=0) + y0_ref[0]
    bd_sel = bd_ref[...].astype(BF16)
    mu = _dot_sel(bd_sel, y, left=False) * (1.0 / HEAD_DIM)
    d = y - mu
    var = _dot_sel(bd_sel, d * d, left=False) * (1.0 / HEAD_DIM)
    yn = d * lax.rsqrt(var + C_GN_EPS) * gn_ref[0:1, :] + gn_ref[1:2, :]
    o_ref[0] = ((yn + bonus_ref[0]) * g_ref[0]).astype(o_ref.dtype)


def _mixer_c(pf, v_first, mu, vec, lora, vl, gn, bd, batch, seq, use_v_lora):
    rows = min(256, seq)
    nblk = seq // rows
    nch = seq // C_CHUNK
    cpb = rows // C_CHUNK
    kern = functools.partial(_rwkv1_kernel, rows=rows, use_v_lora=use_v_lora)
    full2 = lambda a: pl.BlockSpec(a.shape, lambda b, i: (0,) * a.ndim)
    seq_spec = pl.BlockSpec((1, rows, C_WIDTH), lambda b, i: (b, i, 0))
    mat_spec = pl.BlockSpec((1, cpb, C_WIDTH, C_WIDTH), lambda b, i: (b, i, 0, 0))
    seq_shape = jax.ShapeDtypeStruct((batch, seq, C_WIDTH), F32)
    mat_shape = jax.ShapeDtypeStruct((batch, nch, C_WIDTH, C_WIDTH), F32)
    prev_blk = rows // SUBLANES
    phi, psi, rp, y0, g, bonus, v = pl.pallas_call(
        kern,
        grid=(batch, nblk),
        in_specs=[pl.BlockSpec((1, rows, 1024), lambda b, i: (b, i, PF_CIN // 1024)),
                  pl.BlockSpec((1, SUBLANES, 1024), lambda b, i: (b, jnp.maximum(i * prev_blk - 1, 0), PF_CIN // 1024)),
                  seq_spec, full2(mu), full2(vec), full2(lora), full2(vl), full2(bd)],
        out_specs=[mat_spec, mat_spec, seq_spec, seq_spec, seq_spec, seq_spec, seq_spec],
        out_shape=[mat_shape, mat_shape, seq_shape, seq_shape, seq_shape, seq_shape, seq_shape],
        compiler_params=_params("parallel", "parallel", vmem_mb=48),
        name="rwkv_chunks",
    )(pf, pf, v_first, mu, vec, lora, vl, bd)
    scan_chunks = min(4, nch)
    cseq = pl.BlockSpec((1, scan_chunks * C_CHUNK, C_WIDTH), lambda b, c: (b, c, 0))
    cmat = pl.BlockSpec((1, scan_chunks, C_WIDTH, C_WIDTH), lambda b, c: (b, c, 0, 0))
    o = pl.pallas_call(
        _rwkv2_kernel,
        grid=(batch, nch // scan_chunks),
        in_specs=[cmat, cmat, cseq, cseq, cseq, cseq,
                  pl.BlockSpec(gn.shape, lambda b, c: (0, 0)), pl.BlockSpec(bd.shape, lambda b, c: (0, 0))],
        out_specs=cseq,
        out_shape=jax.ShapeDtypeStruct((batch, seq, C_WIDTH), BF16),
        scratch_shapes=[pltpu.VMEM((C_WIDTH, C_WIDTH), F32)],
        compiler_params=_params("parallel", "arbitrary"),
        name="rwkv_scan",
    )(phi, psi, rp, y0, g, bonus, gn, bd)
    return o, v


def _ssd_consts():
    expand = np.zeros((LANES, D_INNER), np.float32)
    for h in range(D_HEADS):
        expand[MISC_DT + h, h * HEAD_DIM:(h + 1) * HEAD_DIM] = 1.0
    return jnp.asarray(expand)


def _ssd_kernel(xbc_ref, z_ref, misc_ref, conv_ref, vec_ref, hp_ref, expand_ref, bd_ref, o_ref, st_ref, prev_ref):
    @pl.when(pl.program_id(1) == 0)
    def _():
        st_ref[...] = jnp.zeros_like(st_ref)
        prev_ref[...] = jnp.zeros_like(prev_ref)

    q = D_CHUNK
    x_raw = xbc_ref[0]
    ext = jnp.concatenate([prev_ref[...], x_raw], axis=0)
    conv = jnp.zeros((q, D_XBC), F32)
    for t in range(D_CONV):
        lo = SUBLANES - (D_CONV - 1) + t
        conv = conv + ext[lo:lo + q, :] * conv_ref[t:t + 1, :]
    prev_ref[...] = x_raw[q - SUBLANES:, :]
    xbc = _silu(conv + conv_ref[D_CONV:D_CONV + 1, :])
    xs = xbc[:, :D_INNER]
    bm = xbc[:, D_INNER:D_INNER + D_GROUPS * D_STATE]
    cm = xbc[:, D_INNER + D_GROUPS * D_STATE:]

    dt_col = _softplus(misc_ref[0] + hp_ref[0:1, :])
    a_col = dt_col * hp_ref[1:2, :]
    ri = lax.broadcasted_iota(I32, (q, q), 0)
    ci = lax.broadcasted_iota(I32, (q, q), 1)
    causal = ci <= ri
    tril = jnp.where(causal, 1.0, 0.0).astype(BF16)
    acs_col = _dot_sel(tril, a_col)
    acs_row = acs_col.T
    expand = expand_ref[...].astype(BF16)
    acs = _dot_sel(expand, acs_col, left=False)
    dt = _dot_sel(expand, dt_col, left=False)
    acs_end = acs[q - 1:q, :]
    xdt = xs * dt

    lane = lax.broadcasted_iota(I32, (1, LANES), 1)
    lane_w = lax.broadcasted_iota(I32, (1, D_INNER), 1)
    left = lane < D_STATE
    bm_sw = pltpu.roll(bm, D_STATE, 1)
    cm_sw = pltpu.roll(cm, D_STATE, 1)
    b_exp = jnp.concatenate([jnp.where(left, bm, bm_sw), jnp.where(left, bm_sw, bm)], axis=1)
    c_exp = jnp.concatenate([jnp.where(left, cm, cm_sw), jnp.where(left, cm_sw, cm)], axis=1)
    cb = [_dot3(jnp.where(left == (g == 0), cm, 0.0), bm, nt=True) for g in range(D_GROUPS)]

    scores = []
    for h in range(D_HEADS):
        col = acs_col[:, MISC_DT + h:MISC_DT + h + 1]
        rw = acs_row[MISC_DT + h:MISC_DT + h + 1, :]
        decay = jnp.exp(jnp.where(causal, col - rw, NEG_INF))
        scores.append(cb[h // (D_HEADS // D_GROUPS)] * decay)
    y_h = [_dot3(scores[h], xdt) for h in range(D_HEADS)]
    y = jnp.zeros((q, D_INNER), F32)
    for h in range(D_HEADS):
        y = y + jnp.where(lane_w // HEAD_DIM == h, y_h[h], 0.0)
    st = st_ref[...]
    y = y + _dot3(c_exp, st) * jnp.exp(acs)
    st_ref[...] = st * jnp.exp(acs_end) + _dot3(b_exp.T, xdt * jnp.exp(acs_end - acs)) * bd_ref[...]
    y = y + xs * vec_ref[0:1, :]
    y = y * _silu(z_ref[0])
    half = D_INNER // D_GROUPS
    outs = []
    for g in range(D_GROUPS):
        yg = y[:, g * half:(g + 1) * half]
        outs.append(yg * lax.rsqrt(jnp.mean(yg * yg, axis=-1, keepdims=True) + D_NORM_EPS))
    o_ref[0] = (jnp.concatenate(outs, axis=1) * vec_ref[1:2, :]).astype(o_ref.dtype)


def _mixer_d(pf, conv, vec, hp, expand, bd, batch, seq):
    q = D_CHUNK
    full = lambda a: pl.BlockSpec(a.shape, lambda b, i: (0, 0))
    return pl.pallas_call(
        _ssd_kernel,
        grid=(batch, seq // q),
        in_specs=[pl.BlockSpec((1, q, D_XBC), lambda b, i: (b, i, PF_XBC // D_XBC)),
                  pl.BlockSpec((1, q, D_INNER), lambda b, i: (b, i, PF_Z // D_INNER)),
                  pl.BlockSpec((1, q, LANES), lambda b, i: (b, i, PF_MISC // LANES)),
                  full(conv), full(vec), full(hp), full(expand), full(bd)],
        out_specs=pl.BlockSpec((1, q, D_INNER), lambda b, i: (b, i, 0)),
        out_shape=jax.ShapeDtypeStruct((batch, seq, D_INNER), BF16),
        scratch_shapes=[pltpu.VMEM((D_INNER, D_INNER), F32), pltpu.VMEM((SUBLANES, D_XBC), F32)],
        compiler_params=_params("parallel", "arbitrary"),
        name="ssd",
    )(pf, pf, pf, conv, vec, hp, expand, bd)


def _merge_kernel(x_ref, gates_ref, oa_ref, ob_ref, oc_ref, od_ref, wbr_ref, wout_ref, ln_ref, wr_ref, br_ref,
                  x1_ref, route_ref):
    acc = None
    for n, o_ref in enumerate((oa_ref, ob_ref, oc_ref, od_ref)):
        term = _sigmoid(gates_ref[:, n * D_MODEL:(n + 1) * D_MODEL]) * _dot(o_ref[...], wbr_ref[n])
        acc = term if acc is None else acc + term
    h = DEEPNORM_ALPHA * x_ref[...] + _dot(acc.astype(BF16), wout_ref[...])
    x1 = _layer_norm(h, ln_ref[0:1, :], ln_ref[1:2, :])
    x1_ref[...] = x1

    logits = _dot3(x1, wr_ref[...]) + br_ref[...]
    lane = lax.broadcasted_iota(I32, logits.shape, 1)
    big = jnp.int32(LANES)
    gl = jnp.where(lane < N_EXPERT_GROUPS, logits, NEG_INF)
    gm = jnp.max(gl, axis=-1, keepdims=True)
    pg_top = 1.0 / jnp.sum(jnp.exp(gl - gm), axis=-1, keepdims=True)
    g_sel = jnp.min(jnp.where(gl == gm, lane, big), axis=-1, keepdims=True)
    off = N_EXPERT_GROUPS + g_sel * EXPERTS_PER_GROUP
    el = jnp.where((lane >= off) & (lane < off + EXPERTS_PER_GROUP), logits, NEG_INF)
    em = jnp.max(el, axis=-1, keepdims=True)
    es = jnp.sum(jnp.exp(el - em), axis=-1, keepdims=True)
    idx1 = jnp.min(jnp.where(el == em, lane, big), axis=-1, keepdims=True)
    el2 = jnp.where(lane == idx1, NEG_INF, el)
    em2 = jnp.max(el2, axis=-1, keepdims=True)
    idx2 = jnp.min(jnp.where(el2 == em2, lane, big), axis=-1, keepdims=True)
    p1 = 1.0 / es
    p2 = jnp.exp(em2 - em) / es
    gate1 = pg_top * p1 / (p1 + p2)
    gate2 = pg_top * p2 / (p1 + p2)
    e1 = (idx1 - N_EXPERT_GROUPS).astype(F32)
    e2 = (idx2 - N_EXPERT_GROUPS).astype(F32)
    route_ref[...] = jnp.where(lane == 0, e1, jnp.where(lane == 1, e2, jnp.where(lane == 2, gate1,
                               jnp.where(lane == 3, gate2, 0.0))))


def _merge(x2d, pf, oa, ob, oc, od, wbr, wout, ln, wr, br, tm):
    m = x2d.shape[0]
    row = lambda w: pl.BlockSpec((tm, w), lambda i: (i, 0))
    full = lambda a: pl.BlockSpec(a.shape, lambda i: (0,) * a.ndim)
    return pl.pallas_call(
        _merge_kernel,
        grid=(m // tm,),
        in_specs=[row(D_MODEL), pl.BlockSpec((tm, N_BRANCH * D_MODEL), lambda i: (i, PF_GATES)),
                  row(A_WIDTH), row(B_WIDTH), row(C_WIDTH), row(D_INNER),
                  full(wbr), full(wout), full(ln), full(wr), full(br)],
        out_specs=[row(D_MODEL), row(LANES)],
        out_shape=[jax.ShapeDtypeStruct((m, D_MODEL), F32), jax.ShapeDtypeStruct((m, LANES), F32)],
        compiler_params=_params("parallel", vmem_mb=48),
        name="merge_route",
    )(x2d, pf, oa, ob, oc, od, wbr, wout, ln, wr, br)


def _ffn_kernel(be_ref, nv_ref, xs_ref, wg_ref, wu_ref, wd_ref, o_ref):
    i = pl.program_id(0)

    @pl.when(i < nv_ref[0])
    def _():
        xb = xs_ref[...]
        hg = _dot(xb, wg_ref[0, 0].astype(BF16))
        hu = _dot(xb, wu_ref[0, 0].astype(BF16))
        o_ref[...] = _dot((_silu(hg) * hu).astype(BF16), wd_ref[0, 0].astype(BF16))

    @pl.when(i >= nv_ref[0])
    def _():
        o_ref[...] = jnp.zeros_like(o_ref)


def _ffn(blk_expert, n_valid, xs, e_gate, e_up, e_down, layer):
    cap = xs.shape[0]
    wspec = lambda a: pl.BlockSpec((1, 1) + a.shape[2:], lambda i, be, nv: (layer, be[i], 0, 0))
    return pl.pallas_call(
        _ffn_kernel,
        grid_spec=pltpu.PrefetchScalarGridSpec(
            num_scalar_prefetch=2,
            grid=(cap // MOE_ROWS,),
            in_specs=[pl.BlockSpec((MOE_ROWS, D_MODEL), lambda i, be, nv: (i, 0)),
                      wspec(e_gate), wspec(e_up), wspec(e_down)],
            out_specs=pl.BlockSpec((MOE_ROWS, D_MODEL), lambda i, be, nv: (i, 0)),
        ),
        out_shape=jax.ShapeDtypeStruct((cap, D_MODEL), F32),
        compiler_params=_params("arbitrary", vmem_mb=48),
        name="expert_ffn",
    )(blk_expert, n_valid, xs, e_gate, e_up, e_down)


def _combine_kernel(x_ref, y0_ref, y1_ref, route_ref, ln_ref, o_ref):
    g0 = route_ref[:, 2:3]
    g1 = route_ref[:, 3:4]
    h = DEEPNORM_ALPHA * x_ref[...] + (y0_ref[...] * g0 + y1_ref[...] * g1)
    o_ref[...] = _layer_norm(h, ln_ref[0:1, :], ln_ref[1:2, :])


def _combine(x1, ysel, route, ln, tm):
    m = x1.shape[0]
    return pl.pallas_call(
        _combine_kernel,
        grid=(m // tm,),
        in_specs=[pl.BlockSpec((tm, D_MODEL), lambda i: (i, 0)),
                  pl.BlockSpec((tm, D_MODEL), lambda i: (i, 0)),
                  pl.BlockSpec((tm, D_MODEL), lambda i: (i, 1)),
                  pl.BlockSpec((tm, LANES), lambda i: (i, 0)),
                  pl.BlockSpec(ln.shape, lambda i: (0, 0))],
        out_specs=pl.BlockSpec((tm, D_MODEL), lambda i: (i, 0)),
        out_shape=jax.ShapeDtypeStruct((m, D_MODEL), F32),
        compiler_params=_params("parallel"),
        name="combine",
    )(x1, ysel, ysel, route, ln)


def _dispatch_tables(route, m):
    flat_e = route[:, 0:TOP_K_EXPERTS].astype(I32).reshape(-1)
    n_assign = m * TOP_K_EXPERTS
    onehot = (flat_e[:, None] == jnp.arange(N_EXPERTS, dtype=I32)[None, :]).astype(I32)
    csum = jnp.cumsum(onehot, axis=0)
    rank = jnp.sum(csum * onehot, axis=1) - 1
    counts = csum[-1]
    padded = (counts + MOE_ROWS - 1) // MOE_ROWS * MOE_ROWS
    pad_end = jnp.cumsum(padded)
    pad_start = pad_end - padded
    dest = pad_start[flat_e] + rank
    cap = (n_assign + N_EXPERTS * (MOE_ROWS - 1) + MOE_ROWS - 1) // MOE_ROWS * MOE_ROWS
    n_blocks = cap // MOE_ROWS
    blk_start = jnp.arange(n_blocks, dtype=I32) * MOE_ROWS
    blk_expert = jnp.minimum(jnp.sum((pad_end[None, :] <= blk_start[:, None]).astype(I32), axis=1), N_EXPERTS - 1)
    buf_tok = jnp.zeros((cap,), I32).at[dest].set(jnp.arange(n_assign, dtype=I32) // TOP_K_EXPERTS)
    n_valid = (pad_end[-1] // MOE_ROWS).astype(I32).reshape(1)
    return dest, buf_tok, blk_expert, n_valid


def _row_pad(a, rows):
    return jnp.pad(a, ((0, rows - a.shape[0]), (0, 0)))


def kernel(x, w_in, c_mu, c_w0, c_w2, c_a0, c_a2, c_g2, c_kk, c_ka, c_rk, c_gn_w, c_gn_b, c_v0, c_v1, c_v2,
           d_conv_w, d_conv_b, d_dt_bias, d_a_log, d_skip, d_norm_w, w_branch, w_out, ln1_g, ln1_b,
           r_group, r_group_b, r_expert, r_expert_b, e_gate, e_up, e_down, ln2_g, ln2_b):
    batch, seq, _ = x.shape
    m = batch * seq
    biases = _mixa_bias()
    tw = min(256, seq)
    tri = jnp.asarray(np.triu(np.ones((tw, tw), np.float32))).astype(BF16)
    bd = _head_consts()
    expand = _ssd_consts()
    x2d = x.reshape(m, D_MODEL)
    v_first = jnp.zeros((batch, seq, C_WIDTH), F32)
    tm_proj = min(2048, m)
    tm_tok = min(512, m)
    for l in range(DEPTH):
        wb, wf = _proj_weights(w_in[l])
        pb = _proj(x2d, wb, BF16, tm_proj, PROJ_TN).reshape(batch, seq, PB_WIDTH)
        pf = _proj(x2d, wf, F32, tm_proj, PROJ_TN).reshape(batch, seq, PF_WIDTH)

        kv_pad = jnp.pad(pb[:, :, PB_AKV:PB_AQ], ((0, 0), (A_MAXWIN, 0), (0, 0)))
        o_a = _mixer_a(pb, kv_pad, biases, batch, seq)
        o_b = _mixer_b(pb, pf, tri, batch, seq)

        use_v_lora = l > 0
        vec_rows = [c_w0[l], c_a0[l], c_kk[l], c_ka[l], c_rk[l].reshape(-1)]
        vec_rows.append(c_v0[l - 1] if use_v_lora else jnp.zeros((C_WIDTH,), F32))
        vec = _row_pad(jnp.stack(vec_rows), SUBLANES)
        lora = jnp.stack([
            jnp.pad(c_w2[l], ((0, LANES - C_LORA_W), (0, 0))),
            jnp.pad(c_a2[l], ((C_LORA_W, LANES - C_LORA_W - C_LORA_A), (0, 0))),
            jnp.pad(c_g2[l], ((C_LORA_W + C_LORA_A, 0), (0, 0)))])
        if use_v_lora:
            vl = jnp.stack([jnp.pad(c_v1[l - 1], ((0, 0), (0, C_WIDTH - C_LORA_V))),
                            jnp.pad(c_v2[l - 1], ((0, C_WIDTH - C_LORA_V), (0, 0)))])
        else:
            vl = jnp.zeros((2, C_WIDTH, C_WIDTH), F32)
        gn = _row_pad(jnp.stack([c_gn_w[l], c_gn_b[l]]), SUBLANES)
        o_c, v_c = _mixer_c(pf, v_first, c_mu[l].reshape(1, C_IN), vec, lora, vl, gn, bd, batch, seq, use_v_lora)
        if l == 0:
            v_first = v_c

        conv = _row_pad(jnp.concatenate([d_conv_w[l], d_conv_b[l][None, :]], axis=0), SUBLANES)
        dvec = _row_pad(jnp.stack([jnp.repeat(d_skip[l], HEAD_DIM), d_norm_w[l]]), SUBLANES)
        place = lambda a: jnp.pad(a, (MISC_DT, LANES - MISC_DT - D_HEADS))
        hp = _row_pad(jnp.stack([place(d_dt_bias[l]), place(-jnp.exp(d_a_log[l]))]), SUBLANES)
        o_d = _mixer_d(pf, conv, dvec, hp, expand, bd, batch, seq)

        wr = jnp.pad(jnp.concatenate([r_group[l], r_expert[l]], axis=1),
                     ((0, 0), (0, LANES - N_EXPERT_GROUPS - N_EXPERTS)))
        br = jnp.pad(jnp.concatenate([r_group_b[l], r_expert_b[l]]), (0, LANES - N_EXPERT_GROUPS - N_EXPERTS))
        ln1 = _row_pad(jnp.stack([ln1_g[l], ln1_b[l]]), SUBLANES)
        x1, route = _merge(x2d, pf.reshape(m, PF_WIDTH), o_a.reshape(m, -1), o_b.reshape(m, -1),
                           o_c.reshape(m, -1), o_d.reshape(m, -1), w_branch[l].astype(BF16),
                           w_out[l].astype(BF16), ln1, wr, br.reshape(1, LANES), tm_tok)

        dest, buf_tok, blk_expert, n_valid = _dispatch_tables(route, m)
        xs = jnp.take(x1.astype(BF16), buf_tok, axis=0)
        yb = _ffn(blk_expert, n_valid, xs, e_gate, e_up, e_down, l)
        ysel = jnp.take(yb, dest, axis=0).reshape(m, TOP_K_EXPERTS * D_MODEL)
        ln2 = _row_pad(jnp.stack([ln2_g[l], ln2_b[l]]), SUBLANES)
        x2d = _combine(x1, ysel, route, ln2, tm_tok)
    return x2d.reshape(batch, seq, D_MODEL)
```

```python
import functools
import math

import jax
import jax.numpy as jnp
import numpy as np
from jax import lax
from jax.experimental import pallas as pl
from jax.experimental.pallas import tpu as pltpu

F32 = jnp.float32
BF16 = jnp.bfloat16
I32 = jnp.int32
NEG_INF = float("-inf")
INT_MIN = -(2 ** 31)

LANES = 128
SUBLANES = 8

D_MODEL = 1024
DEPTH = 2
HEAD_DIM = 64
Q_BLOCK = 128

A_HEADS = 4
A_PATTERNS = ((128, 1), (512, 4), (2048, 16))
A_GROUPS = len(A_PATTERNS)
A_WIDTH = A_HEADS * HEAD_DIM
A_MAXWIN = max(w for w, _ in A_PATTERNS)

B_HEADS = 4
B_WIDTH = B_HEADS * HEAD_DIM
IDX_HEADS = 4
IDX_DIM = 64
TOPK_MAX = 256

C_HEADS = 4
C_WIDTH = C_HEADS * HEAD_DIM
C_LORA_W = 32
C_LORA_A = 32
C_LORA_G = 64
C_LORA_V = 16
C_IN = 3 * C_WIDTH + C_LORA_W + C_LORA_A + C_LORA_G
C_GN_EPS = 64e-5
C_CHUNK = 64

D_HEADS = 4
D_INNER = D_HEADS * HEAD_DIM
D_GROUPS = 2
D_STATE = 64
D_CONV = 4
D_CHUNK = 128
D_XBC = D_INNER + 2 * D_GROUPS * D_STATE
D_NORM_EPS = 1e-5

N_BRANCH = 4
N_EXPERT_GROUPS = 4
EXPERTS_PER_GROUP = 8
N_EXPERTS = N_EXPERT_GROUPS * EXPERTS_PER_GROUP
TOP_K_EXPERTS = 2
D_EXPERT = 512
MOE_ROWS = 512

LN_EPS = 1e-5
DEEPNORM_ALPHA = (2 * DEPTH) ** 0.25

PB_AKV = 0
PB_AQ = PB_AKV + A_HEADS * A_GROUPS * 2 * HEAD_DIM
PB_BQ = PB_AQ + A_HEADS * 4 * HEAD_DIM
PB_BK = PB_BQ + B_WIDTH
PB_BV = PB_BK + B_WIDTH
PB_IQ = PB_BV + B_WIDTH
PB_IK = PB_IQ + IDX_HEADS * IDX_DIM
PB_WIDTH = PB_IK + 256
PF_GATES = 0
PF_CIN = PF_GATES + N_BRANCH * D_MODEL
PF_MISC = PF_CIN + C_IN
PF_XBC = PF_CIN + 1024
PF_Z = PF_XBC + D_XBC
PROJ_TN = 768
PF_WIDTH = -(-(PF_Z + D_INNER) // PROJ_TN) * PROJ_TN
MISC_DT = 4


def _dot(a, b, precision=None):
    return jnp.dot(a, b, preferred_element_type=F32, precision=precision)


def _dot_nt(a, b, precision=None):
    return lax.dot_general(a, b, (((1,), (1,)), ((), ())), preferred_element_type=F32, precision=precision)


def _split2(a):
    hi = a.astype(BF16)
    return hi, (a - hi.astype(F32)).astype(BF16)


def _dot3(a, b, nt=False):
    mm = _dot_nt if nt else _dot
    ah, al = _split2(a)
    bh, bl = _split2(b)
    return mm(ah, bh) + (mm(ah, bl) + mm(al, bh))


def _dot_sel(sel, x, left=True):
    hi = x.astype(BF16)
    r1 = x - hi.astype(F32)
    mid = r1.astype(BF16)
    lo = (r1 - mid.astype(F32)).astype(BF16)
    if left:
        return _dot(sel, hi) + (_dot(sel, mid) + _dot(sel, lo))
    return _dot(hi, sel) + (_dot(mid, sel) + _dot(lo, sel))


def _sigmoid(x):
    return 1.0 / (1.0 + jnp.exp(-x))


def _softplus(x):
    return jnp.maximum(x, 0.0) + jnp.log1p(jnp.exp(-jnp.abs(x)))


def _silu(x):
    return x * _sigmoid(x)


def _layer_norm(h, g, b):
    mu = jnp.mean(h, axis=-1, keepdims=True)
    d = h - mu
    var = jnp.mean(d * d, axis=-1, keepdims=True)
    return d * lax.rsqrt(var + LN_EPS) * g + b


def _params(*sem, vmem_mb=None):
    kw = {}
    if vmem_mb is not None:
        kw["vmem_limit_bytes"] = vmem_mb * 1024 * 1024
    return pltpu.CompilerParams(dimension_semantics=sem, **kw)


def _proj_kernel(x_ref, w_ref, o_ref):
    o_ref[...] = _dot(x_ref[...].astype(BF16), w_ref[...]).astype(o_ref.dtype)


def _proj(x2d, w, out_dtype, tm, tn):
    m, k = x2d.shape
    n = w.shape[1]
    return pl.pallas_call(
        _proj_kernel,
        grid=(m // tm, n // tn),
        in_specs=[pl.BlockSpec((tm, k), lambda i, j: (i, 0)),
                  pl.BlockSpec((k, tn), lambda i, j: (0, j))],
        out_specs=pl.BlockSpec((tm, tn), lambda i, j: (i, j)),
        out_shape=jax.ShapeDtypeStruct((m, n), out_dtype),
        compiler_params=_params("parallel", "arbitrary", vmem_mb=48),
        name="in_proj",
    )(x2d, w)


def _proj_weights(w):
    k = w.shape[0]
    off = 0
    seg = {}
    for name, size in (("a_qkv", 3 * A_GROUPS * A_WIDTH), ("b_qkv", 3 * B_WIDTH), ("b_idx_q", IDX_HEADS * IDX_DIM),
                       ("b_idx_k", IDX_DIM), ("b_idx_w", IDX_HEADS), ("c_in", C_IN), ("d_z", D_INNER),
                       ("d_xbc", D_XBC), ("d_dt", D_HEADS), ("gates", N_BRANCH * D_MODEL)):
        seg[name] = w[:, off:off + size]
        off += size
    a = seg["a_qkv"].reshape(k, 3, A_GROUPS, A_HEADS, HEAD_DIM)
    a_kv = jnp.transpose(a[:, 1:3], (0, 3, 2, 1, 4)).reshape(k, -1)
    a_q = jnp.transpose(a[:, 0], (0, 2, 1, 3))
    a_q = jnp.pad(a_q, ((0, 0), (0, 0), (0, 4 - A_GROUPS), (0, 0))).reshape(k, -1)
    zeros = lambda n: jnp.zeros((k, n), w.dtype)
    wb = jnp.concatenate([a_kv, a_q, seg["b_qkv"], seg["b_idx_q"], seg["b_idx_k"], zeros(256 - IDX_DIM)], axis=1)
    misc = jnp.concatenate([seg["b_idx_w"], seg["d_dt"], zeros(LANES - IDX_HEADS - D_HEADS)], axis=1)
    wf = jnp.concatenate([seg["gates"], seg["c_in"], misc, seg["d_xbc"], seg["d_z"]], axis=1)
    wf = jnp.concatenate([wf, zeros(PF_WIDTH - wf.shape[1])], axis=1)
    assert wb.shape[1] == PB_WIDTH and PB_WIDTH % PROJ_TN == 0
    return wb.astype(BF16), wf.astype(BF16)


def _mixa_bias():
    out = []
    for win, dil in A_PATTERNS:
        r = np.arange(Q_BLOCK)[:, None]
        c = np.arange(win + Q_BLOCK)[None, :]
        d = r + win - c
        ok = (d >= 0) & (d <= win) & (d % dil == 0)
        out.append(jnp.asarray(np.where(ok, 0.0, -np.inf).astype(np.float32)))
    return out


def _mixa_kernel(q_ref, kv_ref, b0_ref, b1_ref, b2_ref, o_ref):
    start = pl.program_id(2) * Q_BLOCK
    scale = HEAD_DIM ** -0.5
    bias_refs = (b0_ref, b1_ref, b2_ref)
    outs = []
    for hh in range(2):
        ms, ls, os_ = [], [], []
        for g, (win, _) in enumerate(A_PATTERNS):
            span = win + Q_BLOCK
            q = q_ref[0, :, hh * 256 + g * HEAD_DIM: hh * 256 + (g + 1) * HEAD_DIM]
            base = pl.multiple_of(start + (A_MAXWIN - win), Q_BLOCK)
            c0 = hh * (A_GROUPS * 2 * HEAD_DIM) + g * 2 * HEAD_DIM
            k = kv_ref[0, pl.ds(base, span), c0:c0 + HEAD_DIM]
            v = kv_ref[0, pl.ds(base, span), c0 + HEAD_DIM:c0 + 2 * HEAD_DIM]
            s = _dot_nt(q, k) * scale + bias_refs[g][...]
            col = lax.broadcasted_iota(I32, (Q_BLOCK, span), 1)
            s = jnp.where(col >= win - start, s, NEG_INF)
            m = jnp.max(s, axis=-1, keepdims=True)
            p = jnp.exp(s - m)
            ls.append(jnp.sum(p, axis=-1, keepdims=True))
            os_.append(_dot(p.astype(BF16), v))
            ms.append(m)
        m_all = jnp.maximum(jnp.maximum(ms[0], ms[1]), ms[2])
        num = jnp.zeros((Q_BLOCK, HEAD_DIM), F32)
        den = jnp.zeros((Q_BLOCK, 1), F32)
        for g in range(A_GROUPS):
            wg = jnp.exp(ms[g] - m_all)
            num = num + wg * os_[g]
            den = den + wg * ls[g]
        outs.append(num / den)
    o_ref[0] = jnp.concatenate(outs, axis=1).astype(o_ref.dtype)


def _mixer_a(pb, kv_pad, biases, batch, seq):
    nblk = seq // Q_BLOCK
    pair_kv = 2 * A_GROUPS * 2 * HEAD_DIM
    in_specs = [pl.BlockSpec((1, Q_BLOCK, 512), lambda b, hp, i: (b, i, PB_AQ // 512 + hp)),
                pl.BlockSpec((1, seq + A_MAXWIN, pair_kv), lambda b, hp, i: (b, 0, hp))]
    for bias in biases:
        in_specs.append(pl.BlockSpec(bias.shape, lambda b, hp, i: (0, 0)))
    return pl.pallas_call(
        _mixa_kernel,
        grid=(batch, A_HEADS // 2, nblk),
        in_specs=in_specs,
        out_specs=pl.BlockSpec((1, Q_BLOCK, 2 * HEAD_DIM), lambda b, hp, i: (b, i, hp)),
        out_shape=jax.ShapeDtypeStruct((batch, seq, A_WIDTH), BF16),
        compiler_params=_params("parallel", "parallel", "arbitrary", vmem_mb=48),
        name="mixer_a",
    )(pb, kv_pad, *biases)


def _mixb_kernel(qb_ref, kb_ref, vb_ref, iq_ref, ik_ref, misc_ref, tri_ref, o_ref, keys_ref, selb_ref, *, top_k):
    _, rows, ch = keys_ref.shape
    start = pl.program_id(1) * rows
    nch = (start + rows + ch - 1) // ch
    qpos = start + lax.broadcasted_iota(I32, (rows, 1), 0)
    w = misc_ref[0][:, 0:IDX_HEADS] * (IDX_DIM ** -0.5 * IDX_HEADS ** -0.5)
    iq = iq_ref[0]
    zero_col = jnp.zeros((rows, 1), F32)

    def score_chunk(c, carry):
        off = pl.multiple_of(c * ch, ch)
        ik = ik_ref[0, pl.ds(off, ch), 0:IDX_DIM]
        acc = jnp.zeros((rows, ch), F32)
        for h in range(IDX_HEADS):
            lg = _dot_nt(iq[:, h * IDX_DIM:(h + 1) * IDX_DIM], ik)
            acc = acc + jnp.maximum(lg, 0.0) * w[:, h:h + 1]
        acc = acc + 0.0
        bits = lax.bitcast_convert_type(acc, I32)
        key = bits ^ ((bits >> 31) & 0x7FFFFFFF)
        kpos = off + lax.broadcasted_iota(I32, (rows, ch), 1)
        keys_ref[c] = jnp.where(kpos <= qpos, key, INT_MIN)
        return carry

    lax.fori_loop(0, nch, score_chunk, 0)

    def count(pred):
        def chunk(c, cnt):
            kc = keys_ref[c]
            for j in range(ch // LANES):
                cnt = cnt + jnp.where(pred(kc[:, j * LANES:(j + 1) * LANES]), 1.0, 0.0)
            return cnt
        cnt = lax.fori_loop(0, nch, chunk, jnp.zeros((rows, LANES), F32))
        return jnp.sum(cnt, axis=-1, keepdims=True)

    kf = float(top_k)
    ans = jnp.where(count(lambda kc: kc >= 0) >= kf, 0, INT_MIN).astype(I32)

    def body(it, ans):
        cand = ans + jnp.left_shift(jnp.int32(1), 30 - it)
        return jnp.where(count(lambda kc: kc >= cand) >= kf, cand, ans)

    thr = lax.fori_loop(0, 31, body, ans)
    need = kf - count(lambda kc: kc > thr)

    tw = tri_ref.shape[0]

    def select_chunk(c, run):
        kc_all = keys_ref[c]
        for j in range(ch // tw):
            kc = kc_all[:, j * tw:(j + 1) * tw]
            eq = kc == thr
            pre = _dot(jnp.where(eq, 1.0, 0.0).astype(BF16), tri_ref[...])
            take = jnp.where(eq, jnp.where(pre + run <= need, 1.0, 0.0), 0.0)
            sel = jnp.where(kc > thr, 1.0, take)
            sel = jnp.where(kc != INT_MIN, sel, 0.0)
            selb_ref[c, :, j * tw:(j + 1) * tw] = jnp.where(sel > 0.5, 0.0, NEG_INF)
            run = run + pre[:, tw - 1:tw]
        return run

    def select_chunk_no_ties(c, carry):
        kc = keys_ref[c]
        selb_ref[c] = jnp.where(kc >= thr, jnp.where(kc != INT_MIN, 0.0, NEG_INF), NEG_INF)
        return carry

    excess = jnp.max(count(lambda kc: kc >= thr)) > kf
    lax.cond(excess,
             lambda: lax.fori_loop(0, nch, select_chunk, zero_col),
             lambda: lax.fori_loop(0, nch, select_chunk_no_ties, zero_col))

    heads = range(B_HEADS)
    lane = lax.broadcasted_iota(I32, (1, B_WIDTH), 1) // HEAD_DIM
    qs = (qb_ref[0].astype(F32) * HEAD_DIM ** -0.5).astype(BF16)
    qm = [jnp.where(lane == h, qs, jnp.zeros_like(qs)) for h in heads]

    def per_lane(cols):
        out = cols[B_HEADS - 1]
        for h in reversed(range(B_HEADS - 1)):
            out = jnp.where(lane <= h, cols[h], out)
        return out

    def attend_chunk(c, carry):
        off = pl.multiple_of(c * ch, ch)
        bias = selb_ref[c]
        kc = kb_ref[0, pl.ds(off, ch), :]
        vc = vb_ref[0, pl.ds(off, ch), :]
        ms, ls, acc = carry[:B_HEADS], carry[B_HEADS:2 * B_HEADS], carry[2 * B_HEADS]
        s = [_dot_nt(qm[h], kc) + bias for h in heads]
        m_new = [jnp.maximum(ms[h], jnp.max(s[h], axis=-1, keepdims=True)) for h in heads]
        m_safe = [jnp.where(m_new[h] == NEG_INF, 0.0, m_new[h]) for h in heads]
        alpha = [jnp.exp(ms[h] - m_safe[h]) for h in heads]
        p = [jnp.exp(s[h] - m_safe[h]) for h in heads]
        l_new = [alpha[h] * ls[h] + jnp.sum(p[h], axis=-1, keepdims=True) for h in heads]
        pv = [_dot(p[h].astype(BF16), vc) for h in heads]
        acc = per_lane(alpha) * acc + per_lane(pv)
        return tuple(m_new) + tuple(l_new) + (acc,)

    init = ((jnp.full((rows, 1), NEG_INF, F32),) * B_HEADS + (zero_col,) * B_HEADS
            + (jnp.zeros((rows, B_WIDTH), F32),))
    res = lax.fori_loop(0, nch, attend_chunk, init)
    o_ref[0] = (res[2 * B_HEADS] / per_lane(res[B_HEADS:2 * B_HEADS])).astype(o_ref.dtype)


def _mixer_b(pb, pf, tri, batch, seq):
    rows = Q_BLOCK
    top_k = min(TOPK_MAX, seq // 4)
    ch = min(512, seq)
    kern = functools.partial(_mixb_kernel, top_k=top_k)
    return pl.pallas_call(
        kern,
        grid=(batch, seq // rows),
        in_specs=[pl.BlockSpec((1, rows, B_WIDTH), lambda b, i: (b, i, PB_BQ // B_WIDTH)),
                  pl.BlockSpec((1, seq, B_WIDTH), lambda b, i: (b, 0, PB_BK // B_WIDTH)),
                  pl.BlockSpec((1, seq, B_WIDTH), lambda b, i: (b, 0, PB_BV // B_WIDTH)),
                  pl.BlockSpec((1, rows, 256), lambda b, i: (b, i, PB_IQ // 256)),
                  pl.BlockSpec((1, seq, LANES), lambda b, i: (b, 0, PB_IK // LANES)),
                  pl.BlockSpec((1, rows, LANES), lambda b, i: (b, i, PF_MISC // LANES)),
                  pl.BlockSpec(tri.shape, lambda b, i: (0, 0))],
        out_specs=pl.BlockSpec((1, rows, B_WIDTH), lambda b, i: (b, i, 0)),
        out_shape=jax.ShapeDtypeStruct((batch, seq, B_WIDTH), BF16),
        scratch_shapes=[pltpu.VMEM((seq // ch, rows, ch), I32), pltpu.VMEM((seq // ch, rows, ch), F32)],
        compiler_params=_params("parallel", "arbitrary", vmem_mb=48),
        name="mixer_b",
    )(pb, pb, pb, pb, pb, pf, tri)


def _head_consts():
    lane = np.arange(C_WIDTH)
    same = (lane[:, None] // HEAD_DIM == lane[None, :] // HEAD_DIM).astype(np.float32)
    return jnp.asarray(same)


def _rwkv1_kernel(cin_ref, prev_ref, vf_ref, mu_ref, vec_ref, lora_ref, vl_ref, bd_ref,
                  phi_ref, psi_ref, rp_ref, y0_ref, g_ref, bonus_ref, v_ref, *, rows, use_v_lora):
    i = pl.program_id(1)
    pc = cin_ref[0][:, :C_IN]
    prev = jnp.where(i > 0, prev_ref[0][SUBLANES - 1:SUBLANES, :C_IN], 0.0)
    row = lax.broadcasted_iota(I32, (rows, 1), 0)
    shifted = jnp.where(row == 0, prev, pltpu.roll(pc, 1, 0))
    pc = pc + (shifted - pc) * mu_ref[...]
    r = pc[:, 0:C_WIDTH]
    k = pc[:, C_WIDTH:2 * C_WIDTH]
    v = pc[:, 2 * C_WIDTH:3 * C_WIDTH]
    xl = pc[:, 3 * C_WIDTH:C_IN]
    lane = lax.broadcasted_iota(I32, xl.shape, 1)
    feat = jnp.where(lane < C_LORA_W, jnp.tanh(xl), jnp.where(lane < C_LORA_W + C_LORA_A, xl, _sigmoid(xl)))
    w0, a0, k_k, k_a, r_k = (vec_ref[n:n + 1, :] for n in range(5))
    bd = bd_ref[...]
    bd_sel = bd.astype(BF16)
    w_raw = -_softplus(-(w0 + _dot3(feat, lora_ref[0]))) - 0.5
    a = _sigmoid(a0 + _dot3(feat, lora_ref[1]))
    g_ref[0] = _dot3(feat, lora_ref[2])
    if use_v_lora:
        v0 = vec_ref[5:6, :]
        v = v + (vf_ref[0] - v) * _sigmoid(v0 + _dot3(_dot3(v, vl_ref[0]), vl_ref[1]))
    v_ref[0] = v
    kk = k * k_k
    kk = kk / jnp.maximum(jnp.sqrt(_dot_sel(bd_sel, kk * kk, left=False)), 1e-12)
    k = k * (1.0 + (a - 1.0) * k_a)
    logw = -jnp.exp(w_raw)
    bonus_ref[0] = _dot_sel(bd_sel, r * k * r_k, left=False) * v
    av = -kk
    bv = kk * a

    cc = C_CHUNK
    ri = lax.broadcasted_iota(I32, (cc, cc), 0)
    ci = lax.broadcasted_iota(I32, (cc, cc), 1)
    tril = jnp.where(ci <= ri, 1.0, 0.0).astype(BF16)
    strict = ci < ri
    incl = ci <= ri
    eye_c = jnp.where(ci == ri, 1.0, 0.0)
    lane_w = lax.broadcasted_iota(I32, (1, C_WIDTH), 1)
    ri2 = lax.broadcasted_iota(I32, (C_WIDTH, C_WIDTH), 0)
    ci2 = lax.broadcasted_iota(I32, (C_WIDTH, C_WIDTH), 1)
    eye_w = jnp.where(ri2 == ci2, 1.0, 0.0)
    chunks = range(rows // cc)
    heads = range(C_HEADS)
    pairs = [(c, h) for c in chunks for h in heads]
    mh = [jnp.where(lane_w // HEAD_DIM == h, 1.0, 0.0) for h in heads]
    sl = [slice(c * cc, (c + 1) * cc) for c in chunks]
    cs = [_dot_sel(tril, logw[sl[c]]) for c in chunks]
    cs_end = [cs[c][cc - 1:cc, :] for c in chunks]
    at = [av[sl[c]] * jnp.exp(cs[c] - logw[sl[c]]) for c in chunks]
    rt = [r[sl[c]] * jnp.exp(cs[c]) for c in chunks]
    inv = [jnp.exp(-cs[c]) for c in chunks]
    rhs = [jnp.concatenate([bv[sl[c]] * inv[c], k[sl[c]] * inv[c]], axis=0) for c in chunks]
    tail = [jnp.exp(cs_end[c] - cs[c]) for c in chunks]
    vc = [v[sl[c]] for c in chunks]
    vc_b = [vc[c].astype(BF16) for c in chunks]
    ath = {(c, h): at[c] * mh[h] for c, h in pairs}
    aa = {(c, h): _dot3(jnp.concatenate([ath[c, h], rt[c] * mh[h]], axis=0), rhs[c], nt=True) for c, h in pairs}
    a_ab = {p: jnp.where(strict, aa[p][:cc, :cc], 0.0) for p in pairs}
    a_ak = {p: jnp.where(strict, aa[p][:cc, cc:], 0.0).astype(BF16) for p in pairs}
    a_rb = {p: jnp.where(incl, aa[p][cc:, :cc], 0.0).astype(BF16) for p in pairs}
    a_rk = {p: jnp.where(incl, aa[p][cc:, cc:], 0.0).astype(BF16) for p in pairs}
    x = {p: eye_c + a_ab[p] for p in pairs}
    pw = a_ab
    for _ in range(int(math.log2(cc)) - 1):
        pw = {p: _dot3(pw[p], pw[p]) for p in pairs}
        x = {p: x[p] + _dot3(x[p], pw[p]) for p in pairs}
    akv = {(c, h): _dot(a_ak[c, h], vc_b[c]) for c, h in pairs}
    ap_h = {p: _dot3(x[p], ath[p]) for p in pairs}
    w2_h = {(c, h): _dot3(x[c, h], akv[c, h]) * mh[h] for c, h in pairs}
    rp_h = {p: _dot(a_rb[p], ap_h[p].astype(BF16)) for p in pairs}
    y0_h = {(c, h): (_dot(a_rb[c, h], w2_h[c, h].astype(BF16)) + _dot(a_rk[c, h], vc_b[c])) * mh[h] for c, h in pairs}
    for c in chunks:
        ap = sum(ap_h[c, h] for h in heads)
        w2 = sum(w2_h[c, h] for h in heads)
        rp_ref[0, sl[c], :] = rt[c] + sum(rp_h[c, h] for h in heads)
        y0_ref[0, sl[c], :] = sum(y0_h[c, h] for h in heads)
        bh = bv[sl[c]] * tail[c]
        kh = k[sl[c]] * tail[c]
        phi_ref[0, c] = eye_w * jnp.exp(cs_end[c]) + _dot3(bh.T, ap) * bd
        psi_ref[0, c] = _dot(jnp.concatenate([bh, kh], axis=0).T.astype(BF16),
                             jnp.concatenate([w2, vc[c]], axis=0).astype(BF16)) * bd


def _rwkv2_kernel(phi_ref, psi_ref, rp_ref, y0_ref, g_ref, bonus_ref, gn_ref, bd_ref, o_ref, s_ref):
    @pl.when(pl.program_id(1) == 0)
    def _():
        s_ref[...] = jnp.zeros_like(s_ref)

    states = [s_ref[...]]
    for c in range(phi_ref.shape[1]):
        states.append(_dot3(phi_ref[0, c], states[c]) + psi_ref[0, c])
    s_ref[...] = states[-1]
    cc = C_CHUNK
    y = jnp.concatenate([_dot3(rp_ref[0, c * cc:(c + 1) * cc, :], states[c]) for c in range(phi_ref.shape[1])],
                        axis=0) + y0_ref[0]
    bd_sel = bd_ref[...].astype(BF16)
    mu = _dot_sel(bd_sel, y, left=False) * (1.0 / HEAD_DIM)
    d = y - mu
    var = _dot_sel(bd_sel, d * d, left=False) * (1.0 / HEAD_DIM)
    yn = d * lax.rsqrt(var + C_GN_EPS) * gn_ref[0:1, :] + gn_ref[1:2, :]
    o_ref[0] = ((yn + bonus_ref[0]) * g_ref[0]).astype(o_ref.dtype)


def _mixer_c(pf, v_first, mu, vec, lora, vl, gn, bd, batch, seq, use_v_lora):
    rows = min(256, seq)
    nblk = seq // rows
    nch = seq // C_CHUNK
    cpb = rows // C_CHUNK
    kern = functools.partial(_rwkv1_kernel, rows=rows, use_v_lora=use_v_lora)
    full2 = lambda a: pl.BlockSpec(a.shape, lambda b, i: (0,) * a.ndim)
    seq_spec = pl.BlockSpec((1, rows, C_WIDTH), lambda b, i: (b, i, 0))
    mat_spec = pl.BlockSpec((1, cpb, C_WIDTH, C_WIDTH), lambda b, i: (b, i, 0, 0))
    seq_shape = jax.ShapeDtypeStruct((batch, seq, C_WIDTH), F32)
    mat_shape = jax.ShapeDtypeStruct((batch, nch, C_WIDTH, C_WIDTH), F32)
    prev_blk = rows // SUBLANES
    phi, psi, rp, y0, g, bonus, v = pl.pallas_call(
        kern,
        grid=(batch, nblk),
        in_specs=[pl.BlockSpec((1, rows, 1024), lambda b, i: (b, i, PF_CIN // 1024)),
                  pl.BlockSpec((1, SUBLANES, 1024), lambda b, i: (b, jnp.maximum(i * prev_blk - 1, 0), PF_CIN // 1024)),
                  seq_spec, full2(mu), full2(vec), full2(lora), full2(vl), full2(bd)],
        out_specs=[mat_spec, mat_spec, seq_spec, seq_spec, seq_spec, seq_spec, seq_spec],
        out_shape=[mat_shape, mat_shape, seq_shape, seq_shape, seq_shape, seq_shape, seq_shape],
        compiler_params=_params("parallel", "parallel", vmem_mb=48),
        name="rwkv_chunks",
    )(pf, pf, v_first, mu, vec, lora, vl, bd)
    scan_chunks = min(4, nch)
    cseq = pl.BlockSpec((1, scan_chunks * C_CHUNK, C_WIDTH), lambda b, c: (b, c, 0))
    cmat = pl.BlockSpec((1, scan_chunks, C_WIDTH, C_WIDTH), lambda b, c: (b, c, 0, 0))
    o = pl.pallas_call(
        _rwkv2_kernel,
        grid=(batch, nch // scan_chunks),
        in_specs=[cmat, cmat, cseq, cseq, cseq, cseq,
                  pl.BlockSpec(gn.shape, lambda b, c: (0, 0)), pl.BlockSpec(bd.shape, lambda b, c: (0, 0))],
        out_specs=cseq,
        out_shape=jax.ShapeDtypeStruct((batch, seq, C_WIDTH), BF16),
        scratch_shapes=[pltpu.VMEM((C_WIDTH, C_WIDTH), F32)],
        compiler_params=_params("parallel", "arbitrary"),
        name="rwkv_scan",
    )(phi, psi, rp, y0, g, bonus, gn, bd)
    return o, v


def _ssd_consts():
    expand = np.zeros((LANES, D_INNER), np.float32)
    for h in range(D_HEADS):
        expand[MISC_DT + h, h * HEAD_DIM:(h + 1) * HEAD_DIM] = 1.0
    return jnp.asarray(expand)


def _ssd_kernel(xbc_ref, z_ref, misc_ref, conv_ref, vec_ref, hp_ref, expand_ref, bd_ref, o_ref, st_ref, prev_ref):
    @pl.when(pl.program_id(1) == 0)
    def _():
        st_ref[...] = jnp.zeros_like(st_ref)
        prev_ref[...] = jnp.zeros_like(prev_ref)

    q = D_CHUNK
    x_raw = xbc_ref[0]
    ext = jnp.concatenate([prev_ref[...], x_raw], axis=0)
    conv = jnp.zeros((q, D_XBC), F32)
    for t in range(D_CONV):
        lo = SUBLANES - (D_CONV - 1) + t
        conv = conv + ext[lo:lo + q, :] * conv_ref[t:t + 1, :]
    prev_ref[...] = x_raw[q - SUBLANES:, :]
    xbc = _silu(conv + conv_ref[D_CONV:D_CONV + 1, :])
    xs = xbc[:, :D_INNER]
    bm = xbc[:, D_INNER:D_INNER + D_GROUPS * D_STATE]
    cm = xbc[:, D_INNER + D_GROUPS * D_STATE:]

    dt_col = _softplus(misc_ref[0] + hp_ref[0:1, :])
    a_col = dt_col * hp_ref[1:2, :]
    ri = lax.broadcasted_iota(I32, (q, q), 0)
    ci = lax.broadcasted_iota(I32, (q, q), 1)
    causal = ci <= ri
    tril = jnp.where(causal, 1.0, 0.0).astype(BF16)
    acs_col = _dot_sel(tril, a_col)
    acs_row = acs_col.T
    expand = expand_ref[...].astype(BF16)
    acs = _dot_sel(expand, acs_col, left=False)
    dt = _dot_sel(expand, dt_col, left=False)
    acs_end = acs[q - 1:q, :]
    xdt = xs * dt

    lane = lax.broadcasted_iota(I32, (1, LANES), 1)
    lane_w = lax.broadcasted_iota(I32, (1, D_INNER), 1)
    left = lane < D_STATE
    bm_sw = pltpu.roll(bm, D_STATE, 1)
    cm_sw = pltpu.roll(cm, D_STATE, 1)
    b_exp = jnp.concatenate([jnp.where(left, bm, bm_sw), jnp.where(left, bm_sw, bm)], axis=1)
    c_exp = jnp.concatenate([jnp.where(left, cm, cm_sw), jnp.where(left, cm_sw, cm)], axis=1)
    cb = [_dot3(jnp.where(left == (g == 0), cm, 0.0), bm, nt=True) for g in range(D_GROUPS)]

    scores = []
    for h in range(D_HEADS):
        col = acs_col[:, MISC_DT + h:MISC_DT + h + 1]
        rw = acs_row[MISC_DT + h:MISC_DT + h + 1, :]
        decay = jnp.exp(jnp.where(causal, col - rw, NEG_INF))
        scores.append(cb[h // (D_HEADS // D_GROUPS)] * decay)
    y_h = [_dot3(scores[h], xdt) for h in range(D_HEADS)]
    y = jnp.zeros((q, D_INNER), F32)
    for h in range(D_HEADS):
        y = y + jnp.where(lane_w // HEAD_DIM == h, y_h[h], 0.0)
    st = st_ref[...]
    y = y + _dot3(c_exp, st) * jnp.exp(acs)
    st_ref[...] = st * jnp.exp(acs_end) + _dot3(b_exp.T, xdt * jnp.exp(acs_end - acs)) * bd_ref[...]
    y = y + xs * vec_ref[0:1, :]
    y = y * _silu(z_ref[0])
    half = D_INNER // D_GROUPS
    outs = []
    for g in range(D_GROUPS):
        yg = y[:, g * half:(g + 1) * half]
        outs.append(yg * lax.rsqrt(jnp.mean(yg * yg, axis=-1, keepdims=True) + D_NORM_EPS))
    o_ref[0] = (jnp.concatenate(outs, axis=1) * vec_ref[1:2, :]).astype(o_ref.dtype)


def _mixer_d(pf, conv, vec, hp, expand, bd, batch, seq):
    q = D_CHUNK
    full = lambda a: pl.BlockSpec(a.shape, lambda b, i: (0, 0))
    return pl.pallas_call(
        _ssd_kernel,
        grid=(batch, seq // q),
        in_specs=[pl.BlockSpec((1, q, D_XBC), lambda b, i: (b, i, PF_XBC // D_XBC)),
                  pl.BlockSpec((1, q, D_INNER), lambda b, i: (b, i, PF_Z // D_INNER)),
                  pl.BlockSpec((1, q, LANES), lambda b, i: (b, i, PF_MISC // LANES)),
                  full(conv), full(vec), full(hp), full(expand), full(bd)],
        out_specs=pl.BlockSpec((1, q, D_INNER), lambda b, i: (b, i, 0)),
        out_shape=jax.ShapeDtypeStruct((batch, seq, D_INNER), BF16),
        scratch_shapes=[pltpu.VMEM((D_INNER, D_INNER), F32), pltpu.VMEM((SUBLANES, D_XBC), F32)],
        compiler_params=_params("parallel", "arbitrary"),
        name="ssd",
    )(pf, pf, pf, conv, vec, hp, expand, bd)


def _merge_kernel(x_ref, gates_ref, oa_ref, ob_ref, oc_ref, od_ref, wbr_ref, wout_ref, ln_ref, wr_ref, br_ref,
                  x1_ref, route_ref):
    acc = None
    for n, o_ref in enumerate((oa_ref, ob_ref, oc_ref, od_ref)):
        term = _sigmoid(gates_ref[:, n * D_MODEL:(n + 1) * D_MODEL]) * _dot(o_ref[...], wbr_ref[n])
        acc = term if acc is None else acc + term
    h = DEEPNORM_ALPHA * x_ref[...] + _dot(acc.astype(BF16), wout_ref[...])
    x1 = _layer_norm(h, ln_ref[0:1, :], ln_ref[1:2, :])
    x1_ref[...] = x1

    logits = _dot3(x1, wr_ref[...]) + br_ref[...]
    lane = lax.broadcasted_iota(I32, logits.shape, 1)
    big = jnp.int32(LANES)
    gl = jnp.where(lane < N_EXPERT_GROUPS, logits, NEG_INF)
    gm = jnp.max(gl, axis=-1, keepdims=True)
    pg_top = 1.0 / jnp.sum(jnp.exp(gl - gm), axis=-1, keepdims=True)
    g_sel = jnp.min(jnp.where(gl == gm, lane, big), axis=-1, keepdims=True)
    off = N_EXPERT_GROUPS + g_sel * EXPERTS_PER_GROUP
    el = jnp.where((lane >= off) & (lane < off + EXPERTS_PER_GROUP), logits, NEG_INF)
    em = jnp.max(el, axis=-1, keepdims=True)
    es = jnp.sum(jnp.exp(el - em), axis=-1, keepdims=True)
    idx1 = jnp.min(jnp.where(el == em, lane, big), axis=-1, keepdims=True)
    el2 = jnp.where(lane == idx1, NEG_INF, el)
    em2 = jnp.max(el2, axis=-1, keepdims=True)
    idx2 = jnp.min(jnp.where(el2 == em2, lane, big), axis=-1, keepdims=True)
    p1 = 1.0 / es
    p2 = jnp.exp(em2 - em) / es
    gate1 = pg_top * p1 / (p1 + p2)
    gate2 = pg_top * p2 / (p1 + p2)
    e1 = (idx1 - N_EXPERT_GROUPS).astype(F32)
    e2 = (idx2 - N_EXPERT_GROUPS).astype(F32)
    route_ref[...] = jnp.where(lane == 0, e1, jnp.where(lane == 1, e2, jnp.where(lane == 2, gate1,
                               jnp.where(lane == 3, gate2, 0.0))))


def _merge(x2d, pf, oa, ob, oc, od, wbr, wout, ln, wr, br, tm):
    m = x2d.shape[0]
    row = lambda w: pl.BlockSpec((tm, w), lambda i: (i, 0))
    full = lambda a: pl.BlockSpec(a.shape, lambda i: (0,) * a.ndim)
    return pl.pallas_call(
        _merge_kernel,
        grid=(m // tm,),
        in_specs=[row(D_MODEL), pl.BlockSpec((tm, N_BRANCH * D_MODEL), lambda i: (i, PF_GATES)),
                  row(A_WIDTH), row(B_WIDTH), row(C_WIDTH), row(D_INNER),
                  full(wbr), full(wout), full(ln), full(wr), full(br)],
        out_specs=[row(D_MODEL), row(LANES)],
        out_shape=[jax.ShapeDtypeStruct((m, D_MODEL), F32), jax.ShapeDtypeStruct((m, LANES), F32)],
        compiler_params=_params("parallel", vmem_mb=48),
        name="merge_route",
    )(x2d, pf, oa, ob, oc, od, wbr, wout, ln, wr, br)


def _ffn_kernel(be_ref, nv_ref, xs_ref, wg_ref, wu_ref, wd_ref, o_ref):
    i = pl.program_id(0)

    @pl.when(i < nv_ref[0])
    def _():
        xb = xs_ref[...]
        hg = _dot(xb, wg_ref[0, 0].astype(BF16))
        hu = _dot(xb, wu_ref[0, 0].astype(BF16))
        o_ref[...] = _dot((_silu(hg) * hu).astype(BF16), wd_ref[0, 0].astype(BF16))

    @pl.when(i >= nv_ref[0])
    def _():
        o_ref[...] = jnp.zeros_like(o_ref)


def _ffn(blk_expert, n_valid, xs, e_gate, e_up, e_down, layer):
    cap = xs.shape[0]
    wspec = lambda a: pl.BlockSpec((1, 1) + a.shape[2:], lambda i, be, nv: (layer, be[i], 0, 0))
    return pl.pallas_call(
        _ffn_kernel,
        grid_spec=pltpu.PrefetchScalarGridSpec(
            num_scalar_prefetch=2,
            grid=(cap // MOE_ROWS,),
            in_specs=[pl.BlockSpec((MOE_ROWS, D_MODEL), lambda i, be, nv: (i, 0)),
                      wspec(e_gate), wspec(e_up), wspec(e_down)],
            out_specs=pl.BlockSpec((MOE_ROWS, D_MODEL), lambda i, be, nv: (i, 0)),
        ),
        out_shape=jax.ShapeDtypeStruct((cap, D_MODEL), F32),
        compiler_params=_params("arbitrary", vmem_mb=48),
        name="expert_ffn",
    )(blk_expert, n_valid, xs, e_gate, e_up, e_down)


def _combine_kernel(dest_ref, x_ref, route_ref, ln_ref, yb_ref, o_ref, ybuf, sem):
    i = pl.program_id(0)
    tm = x_ref.shape[0]

    def gather(tile, slot):
        base = tile * (TOP_K_EXPERTS * tm)

        def body(r, carry):
            for s in range(TOP_K_EXPERTS):
                row = dest_ref[base + TOP_K_EXPERTS * r + s]
                pltpu.make_async_copy(yb_ref.at[pl.ds(row, 1)], ybuf.at[slot, pl.ds(s * tm + r, 1)],
                                      sem.at[slot]).start()
            return carry

        lax.fori_loop(0, tm, body, 0)

    @pl.when(i == 0)
    def _():
        gather(0, 0)

    @pl.when(i + 1 < pl.num_programs(0))
    def _():
        gather(i + 1, (i + 1) % 2)

    slot = i % 2
    pltpu.make_async_copy(yb_ref.at[pl.ds(0, TOP_K_EXPERTS * tm)], ybuf.at[slot], sem.at[slot]).wait()
    g0 = route_ref[:, 2:3]
    g1 = route_ref[:, 3:4]
    h = DEEPNORM_ALPHA * x_ref[...] + (ybuf[slot, 0:tm, :] * g0 + ybuf[slot, tm:2 * tm, :] * g1)
    o_ref[...] = _layer_norm(h, ln_ref[0:1, :], ln_ref[1:2, :])


def _combine(x1, yb, dest, route, ln, tm):
    m = x1.shape[0]
    return pl.pallas_call(
        _combine_kernel,
        grid_spec=pltpu.PrefetchScalarGridSpec(
            num_scalar_prefetch=1,
            grid=(m // tm,),
            in_specs=[pl.BlockSpec((tm, D_MODEL), lambda i, d: (i, 0)),
                      pl.BlockSpec((tm, LANES), lambda i, d: (i, 0)),
                      pl.BlockSpec(ln.shape, lambda i, d: (0, 0)),
                      pl.BlockSpec(memory_space=pl.ANY)],
            out_specs=pl.BlockSpec((tm, D_MODEL), lambda i, d: (i, 0)),
            scratch_shapes=[pltpu.VMEM((2, TOP_K_EXPERTS * tm, D_MODEL), F32), pltpu.SemaphoreType.DMA((2,))],
        ),
        out_shape=jax.ShapeDtypeStruct((m, D_MODEL), F32),
        compiler_params=_params("arbitrary"),
        name="combine",
    )(dest, x1, route, ln, yb)


def _dispatch_tables(route, m):
    flat_e = route[:, 0:TOP_K_EXPERTS].astype(I32).reshape(-1)
    n_assign = m * TOP_K_EXPERTS
    onehot = (flat_e[:, None] == jnp.arange(N_EXPERTS, dtype=I32)[None, :]).astype(I32)
    csum = jnp.cumsum(onehot, axis=0)
    rank = jnp.sum(csum * onehot, axis=1) - 1
    counts = csum[-1]
    padded = (counts + MOE_ROWS - 1) // MOE_ROWS * MOE_ROWS
    pad_end = jnp.cumsum(padded)
    pad_start = pad_end - padded
    dest = pad_start[flat_e] + rank
    cap = (n_assign + N_EXPERTS * (MOE_ROWS - 1) + MOE_ROWS - 1) // MOE_ROWS * MOE_ROWS
    n_blocks = cap // MOE_ROWS
    blk_start = jnp.arange(n_blocks, dtype=I32) * MOE_ROWS
    blk_expert = jnp.minimum(jnp.sum((pad_end[None, :] <= blk_start[:, None]).astype(I32), axis=1), N_EXPERTS - 1)
    buf_tok = jnp.zeros((cap,), I32).at[dest].set(jnp.arange(n_assign, dtype=I32) // TOP_K_EXPERTS)
    n_valid = (pad_end[-1] // MOE_ROWS).astype(I32).reshape(1)
    return dest, buf_tok, blk_expert, n_valid


def _row_pad(a, rows):
    return jnp.pad(a, ((0, rows - a.shape[0]), (0, 0)))


def kernel(x, w_in, c_mu, c_w0, c_w2, c_a0, c_a2, c_g2, c_kk, c_ka, c_rk, c_gn_w, c_gn_b, c_v0, c_v1, c_v2,
           d_conv_w, d_conv_b, d_dt_bias, d_a_log, d_skip, d_norm_w, w_branch, w_out, ln1_g, ln1_b,
           r_group, r_group_b, r_expert, r_expert_b, e_gate, e_up, e_down, ln2_g, ln2_b):
    batch, seq, _ = x.shape
    m = batch * seq
    biases = _mixa_bias()
    tw = min(256, seq)
    tri = jnp.asarray(np.triu(np.ones((tw, tw), np.float32))).astype(BF16)
    bd = _head_consts()
    expand = _ssd_consts()
    x2d = x.reshape(m, D_MODEL)
    v_first = jnp.zeros((batch, seq, C_WIDTH), F32)
    tm_proj = min(2048, m)
    tm_tok = min(512, m)
    for l in range(DEPTH):
        wb, wf = _proj_weights(w_in[l])
        pb = _proj(x2d, wb, BF16, tm_proj, PROJ_TN).reshape(batch, seq, PB_WIDTH)
        pf = _proj(x2d, wf, F32, tm_proj, PROJ_TN).reshape(batch, seq, PF_WIDTH)

        kv_pad = jnp.pad(pb[:, :, PB_AKV:PB_AQ], ((0, 0), (A_MAXWIN, 0), (0, 0)))
        o_a = _mixer_a(pb, kv_pad, biases, batch, seq)
        o_b = _mixer_b(pb, pf, tri, batch, seq)

        use_v_lora = l > 0
        vec_rows = [c_w0[l], c_a0[l], c_kk[l], c_ka[l], c_rk[l].reshape(-1)]
        vec_rows.append(c_v0[l - 1] if use_v_lora else jnp.zeros((C_WIDTH,), F32))
        vec = _row_pad(jnp.stack(vec_rows), SUBLANES)
        lora = jnp.stack([
            jnp.pad(c_w2[l], ((0, LANES - C_LORA_W), (0, 0))),
            jnp.pad(c_a2[l], ((C_LORA_W, LANES - C_LORA_W - C_LORA_A), (0, 0))),
            jnp.pad(c_g2[l], ((C_LORA_W + C_LORA_A, 0), (0, 0)))])
        if use_v_lora:
            vl = jnp.stack([jnp.pad(c_v1[l - 1], ((0, 0), (0, C_WIDTH - C_LORA_V))),
                            jnp.pad(c_v2[l - 1], ((0, C_WIDTH - C_LORA_V), (0, 0)))])
        else:
            vl = jnp.zeros((2, C_WIDTH, C_WIDTH), F32)
        gn = _row_pad(jnp.stack([c_gn_w[l], c_gn_b[l]]), SUBLANES)
        o_c, v_c = _mixer_c(pf, v_first, c_mu[l].reshape(1, C_IN), vec, lora, vl, gn, bd, batch, seq, use_v_lora)
        if l == 0:
            v_first = v_c

        conv = _row_pad(jnp.concatenate([d_conv_w[l], d_conv_b[l][None, :]], axis=0), SUBLANES)
        dvec = _row_pad(jnp.stack([jnp.repeat(d_skip[l], HEAD_DIM), d_norm_w[l]]), SUBLANES)
        place = lambda a: jnp.pad(a, (MISC_DT, LANES - MISC_DT - D_HEADS))
        hp = _row_pad(jnp.stack([place(d_dt_bias[l]), place(-jnp.exp(d_a_log[l]))]), SUBLANES)
        o_d = _mixer_d(pf, conv, dvec, hp, expand, bd, batch, seq)

        wr = jnp.pad(jnp.concatenate([r_group[l], r_expert[l]], axis=1),
                     ((0, 0), (0, LANES - N_EXPERT_GROUPS - N_EXPERTS)))
        br = jnp.pad(jnp.concatenate([r_group_b[l], r_expert_b[l]]), (0, LANES - N_EXPERT_GROUPS - N_EXPERTS))
        ln1 = _row_pad(jnp.stack([ln1_g[l], ln1_b[l]]), SUBLANES)
        x1, route = _merge(x2d, pf.reshape(m, PF_WIDTH), o_a.reshape(m, -1), o_b.reshape(m, -1),
                           o_c.reshape(m, -1), o_d.reshape(m, -1), w_branch[l].astype(BF16),
                           w_out[l].astype(BF16), ln1, wr, br.reshape(1, LANES), tm_tok)

        dest, buf_tok, blk_expert, n_valid = _dispatch_tables(route, m)
        xs = jnp.take(x1.astype(BF16), buf_tok, axis=0)
        yb = _ffn(blk_expert, n_valid, xs, e_gate, e_up, e_down, l)
        ln2 = _row_pad(jnp.stack([ln2_g[l], ln2_b[l]]), SUBLANES)
        x2d = _combine(x1, yb, dest, route, ln2, min(256, m))
    return x2d.reshape(batch, seq, D_MODEL)
```

```python
import functools
import math

import jax
import jax.numpy as jnp
import numpy as np
from jax import lax
from jax.experimental import pallas as pl
from jax.experimental.pallas import tpu as pltpu

F32 = jnp.float32
BF16 = jnp.bfloat16
I32 = jnp.int32
NEG_INF = float("-inf")
INT_MIN = -(2 ** 31)

LANES = 128
SUBLANES = 8

D_MODEL = 1024
DEPTH = 2
HEAD_DIM = 64
Q_BLOCK = 128

A_HEADS = 4
A_PATTERNS = ((128, 1), (512, 4), (2048, 16))
A_GROUPS = len(A_PATTERNS)
A_WIDTH = A_HEADS * HEAD_DIM
A_MAXWIN = max(w for w, _ in A_PATTERNS)

B_HEADS = 4
B_WIDTH = B_HEADS * HEAD_DIM
IDX_HEADS = 4
IDX_DIM = 64
TOPK_MAX = 256

C_HEADS = 4
C_WIDTH = C_HEADS * HEAD_DIM
C_LORA_W = 32
C_LORA_A = 32
C_LORA_G = 64
C_LORA_V = 16
C_IN = 3 * C_WIDTH + C_LORA_W + C_LORA_A + C_LORA_G
C_GN_EPS = 64e-5
C_CHUNK = 64

D_HEADS = 4
D_INNER = D_HEADS * HEAD_DIM
D_GROUPS = 2
D_STATE = 64
D_CONV = 4
D_CHUNK = 128
D_XBC = D_INNER + 2 * D_GROUPS * D_STATE
D_NORM_EPS = 1e-5

N_BRANCH = 4
N_EXPERT_GROUPS = 4
EXPERTS_PER_GROUP = 8
N_EXPERTS = N_EXPERT_GROUPS * EXPERTS_PER_GROUP
TOP_K_EXPERTS = 2
D_EXPERT = 512
MOE_ROWS = 512

LN_EPS = 1e-5
DEEPNORM_ALPHA = (2 * DEPTH) ** 0.25

PB_AKV = 0
PB_AQ = PB_AKV + A_HEADS * A_GROUPS * 2 * HEAD_DIM
PB_BQ = PB_AQ + A_HEADS * 4 * HEAD_DIM
PB_BK = PB_BQ + B_WIDTH
PB_BV = PB_BK + B_WIDTH
PB_IQ = PB_BV + B_WIDTH
PB_IK = PB_IQ + IDX_HEADS * IDX_DIM
PB_WIDTH = PB_IK + 256
PF_GATES = 0
PF_CIN = PF_GATES + N_BRANCH * D_MODEL
PF_MISC = PF_CIN + C_IN
PF_XBC = PF_CIN + 1024
PF_Z = PF_XBC + D_XBC
PROJ_TN = 768
PF_WIDTH = -(-(PF_Z + D_INNER) // PROJ_TN) * PROJ_TN
MISC_DT = 4


def _dot(a, b, precision=None):
    return jnp.dot(a, b, preferred_element_type=F32, precision=precision)


def _dot_nt(a, b, precision=None):
    return lax.dot_general(a, b, (((1,), (1,)), ((), ())), preferred_element_type=F32, precision=precision)


def _split2(a):
    hi = a.astype(BF16)
    return hi, (a - hi.astype(F32)).astype(BF16)


def _dot3(a, b, nt=False):
    mm = _dot_nt if nt else _dot
    ah, al = _split2(a)
    bh, bl = _split2(b)
    return mm(ah, bh) + (mm(ah, bl) + mm(al, bh))


def _dot_sel(sel, x, left=True):
    hi = x.astype(BF16)
    r1 = x - hi.astype(F32)
    mid = r1.astype(BF16)
    lo = (r1 - mid.astype(F32)).astype(BF16)
    if left:
        return _dot(sel, hi) + (_dot(sel, mid) + _dot(sel, lo))
    return _dot(hi, sel) + (_dot(mid, sel) + _dot(lo, sel))


def _sigmoid(x):
    return 1.0 / (1.0 + jnp.exp(-x))


def _softplus(x):
    return jnp.maximum(x, 0.0) + jnp.log1p(jnp.exp(-jnp.abs(x)))


def _silu(x):
    return x * _sigmoid(x)


def _layer_norm(h, g, b):
    mu = jnp.mean(h, axis=-1, keepdims=True)
    d = h - mu
    var = jnp.mean(d * d, axis=-1, keepdims=True)
    return d * lax.rsqrt(var + LN_EPS) * g + b


def _params(*sem, vmem_mb=None):
    kw = {}
    if vmem_mb is not None:
        kw["vmem_limit_bytes"] = vmem_mb * 1024 * 1024
    return pltpu.CompilerParams(dimension_semantics=sem, **kw)


def _proj_kernel(x_ref, w_ref, o_ref):
    o_ref[...] = _dot(x_ref[...].astype(BF16), w_ref[...]).astype(o_ref.dtype)


def _proj(x2d, w, out_dtype, tm, tn):
    m, k = x2d.shape
    n = w.shape[1]
    return pl.pallas_call(
        _proj_kernel,
        grid=(m // tm, n // tn),
        in_specs=[pl.BlockSpec((tm, k), lambda i, j: (i, 0)),
                  pl.BlockSpec((k, tn), lambda i, j: (0, j))],
        out_specs=pl.BlockSpec((tm, tn), lambda i, j: (i, j)),
        out_shape=jax.ShapeDtypeStruct((m, n), out_dtype),
        compiler_params=_params("parallel", "arbitrary", vmem_mb=48),
        name="in_proj",
    )(x2d, w)


def _proj_weights(w):
    k = w.shape[0]
    off = 0
    seg = {}
    for name, size in (("a_qkv", 3 * A_GROUPS * A_WIDTH), ("b_qkv", 3 * B_WIDTH), ("b_idx_q", IDX_HEADS * IDX_DIM),
                       ("b_idx_k", IDX_DIM), ("b_idx_w", IDX_HEADS), ("c_in", C_IN), ("d_z", D_INNER),
                       ("d_xbc", D_XBC), ("d_dt", D_HEADS), ("gates", N_BRANCH * D_MODEL)):
        seg[name] = w[:, off:off + size]
        off += size
    a = seg["a_qkv"].reshape(k, 3, A_GROUPS, A_HEADS, HEAD_DIM)
    a_kv = jnp.transpose(a[:, 1:3], (0, 3, 2, 1, 4)).reshape(k, -1)
    a_q = jnp.transpose(a[:, 0], (0, 2, 1, 3))
    a_q = jnp.pad(a_q, ((0, 0), (0, 0), (0, 4 - A_GROUPS), (0, 0))).reshape(k, -1)
    zeros = lambda n: jnp.zeros((k, n), w.dtype)
    wb = jnp.concatenate([a_kv, a_q, seg["b_qkv"], seg["b_idx_q"], seg["b_idx_k"], zeros(256 - IDX_DIM)], axis=1)
    misc = jnp.concatenate([seg["b_idx_w"], seg["d_dt"], zeros(LANES - IDX_HEADS - D_HEADS)], axis=1)
    wf = jnp.concatenate([seg["gates"], seg["c_in"], misc, seg["d_xbc"], seg["d_z"]], axis=1)
    wf = jnp.concatenate([wf, zeros(PF_WIDTH - wf.shape[1])], axis=1)
    assert wb.shape[1] == PB_WIDTH and PB_WIDTH % PROJ_TN == 0
    return wb.astype(BF16), wf.astype(BF16)


def _mixa_bias():
    out = []
    for win, dil in A_PATTERNS:
        r = np.arange(Q_BLOCK)[:, None]
        c = np.arange(win + Q_BLOCK)[None, :]
        d = r + win - c
        ok = (d >= 0) & (d <= win) & (d % dil == 0)
        out.append(jnp.asarray(np.where(ok, 0.0, -np.inf).astype(np.float32)))
    return out


def _mixa_kernel(q_ref, kv_ref, b0_ref, b1_ref, b2_ref, o_ref):
    start = pl.program_id(2) * Q_BLOCK
    scale = HEAD_DIM ** -0.5
    bias_refs = (b0_ref, b1_ref, b2_ref)
    outs = []
    for hh in range(2):
        ms, ls, os_ = [], [], []
        for g, (win, _) in enumerate(A_PATTERNS):
            span = win + Q_BLOCK
            q = q_ref[0, :, hh * 256 + g * HEAD_DIM: hh * 256 + (g + 1) * HEAD_DIM]
            base = pl.multiple_of(start + (A_MAXWIN - win), Q_BLOCK)
            c0 = hh * (A_GROUPS * 2 * HEAD_DIM) + g * 2 * HEAD_DIM
            k = kv_ref[0, pl.ds(base, span), c0:c0 + HEAD_DIM]
            v = kv_ref[0, pl.ds(base, span), c0 + HEAD_DIM:c0 + 2 * HEAD_DIM]
            s = _dot_nt(q, k) * scale + bias_refs[g][...]
            col = lax.broadcasted_iota(I32, (Q_BLOCK, span), 1)
            s = jnp.where(col >= win - start, s, NEG_INF)
            m = jnp.max(s, axis=-1, keepdims=True)
            p = jnp.exp(s - m)
            ls.append(jnp.sum(p, axis=-1, keepdims=True))
            os_.append(_dot(p.astype(BF16), v))
            ms.append(m)
        m_all = jnp.maximum(jnp.maximum(ms[0], ms[1]), ms[2])
        num = jnp.zeros((Q_BLOCK, HEAD_DIM), F32)
        den = jnp.zeros((Q_BLOCK, 1), F32)
        for g in range(A_GROUPS):
            wg = jnp.exp(ms[g] - m_all)
            num = num + wg * os_[g]
            den = den + wg * ls[g]
        outs.append(num / den)
    o_ref[0] = jnp.concatenate(outs, axis=1).astype(o_ref.dtype)


def _mixer_a(pb, kv_pad, biases, batch, seq):
    nblk = seq // Q_BLOCK
    pair_kv = 2 * A_GROUPS * 2 * HEAD_DIM
    in_specs = [pl.BlockSpec((1, Q_BLOCK, 512), lambda b, hp, i: (b, i, PB_AQ // 512 + hp)),
                pl.BlockSpec((1, seq + A_MAXWIN, pair_kv), lambda b, hp, i: (b, 0, hp))]
    for bias in biases:
        in_specs.append(pl.BlockSpec(bias.shape, lambda b, hp, i: (0, 0)))
    return pl.pallas_call(
        _mixa_kernel,
        grid=(batch, A_HEADS // 2, nblk),
        in_specs=in_specs,
        out_specs=pl.BlockSpec((1, Q_BLOCK, 2 * HEAD_DIM), lambda b, hp, i: (b, i, hp)),
        out_shape=jax.ShapeDtypeStruct((batch, seq, A_WIDTH), BF16),
        compiler_params=_params("parallel", "parallel", "arbitrary", vmem_mb=48),
        name="mixer_a",
    )(pb, kv_pad, *biases)


def _mixb_kernel(qb_ref, kb_ref, vb_ref, iq_ref, ik_ref, misc_ref, tri_ref, o_ref, keys_ref, selb_ref, *, top_k):
    _, rows, ch = keys_ref.shape
    start = pl.program_id(1) * rows
    nch = (start + rows + ch - 1) // ch
    qpos = start + lax.broadcasted_iota(I32, (rows, 1), 0)
    w = misc_ref[0][:, 0:IDX_HEADS] * (IDX_DIM ** -0.5 * IDX_HEADS ** -0.5)
    iq = iq_ref[0]
    zero_col = jnp.zeros((rows, 1), F32)

    def score_chunk(c, carry):
        off = pl.multiple_of(c * ch, ch)
        ik = ik_ref[0, pl.ds(off, ch), 0:IDX_DIM]
        acc = jnp.zeros((rows, ch), F32)
        for h in range(IDX_HEADS):
            lg = _dot_nt(iq[:, h * IDX_DIM:(h + 1) * IDX_DIM], ik)
            acc = acc + jnp.maximum(lg, 0.0) * w[:, h:h + 1]
        acc = acc + 0.0
        bits = lax.bitcast_convert_type(acc, I32)
        key = bits ^ ((bits >> 31) & 0x7FFFFFFF)
        kpos = off + lax.broadcasted_iota(I32, (rows, ch), 1)
        keys_ref[c] = jnp.where(kpos <= qpos, key, INT_MIN)
        return carry

    lax.fori_loop(0, nch, score_chunk, 0)

    def count(pred):
        def chunk(c, cnt):
            kc = keys_ref[c]
            for j in range(ch // LANES):
                cnt = cnt + jnp.where(pred(kc[:, j * LANES:(j + 1) * LANES]), 1.0, 0.0)
            return cnt
        cnt = lax.fori_loop(0, nch, chunk, jnp.zeros((rows, LANES), F32))
        return jnp.sum(cnt, axis=-1, keepdims=True)

    kf = float(top_k)
    ans = jnp.where(count(lambda kc: kc >= 0) >= kf, 0, INT_MIN).astype(I32)

    def body(it, ans):
        cand = ans + jnp.left_shift(jnp.int32(1), 30 - it)
        return jnp.where(count(lambda kc: kc >= cand) >= kf, cand, ans)

    thr = lax.fori_loop(0, 31, body, ans)
    need = kf - count(lambda kc: kc > thr)

    tw = tri_ref.shape[0]

    def select_chunk(c, run):
        kc_all = keys_ref[c]
        for j in range(ch // tw):
            kc = kc_all[:, j * tw:(j + 1) * tw]
            eq = kc == thr
            pre = _dot(jnp.where(eq, 1.0, 0.0).astype(BF16), tri_ref[...])
            take = jnp.where(eq, jnp.where(pre + run <= need, 1.0, 0.0), 0.0)
            sel = jnp.where(kc > thr, 1.0, take)
            sel = jnp.where(kc != INT_MIN, sel, 0.0)
            selb_ref[c, :, j * tw:(j + 1) * tw] = jnp.where(sel > 0.5, 0.0, NEG_INF)
            run = run + pre[:, tw - 1:tw]
        return run

    def select_chunk_no_ties(c, carry):
        kc = keys_ref[c]
        selb_ref[c] = jnp.where(kc >= thr, jnp.where(kc != INT_MIN, 0.0, NEG_INF), NEG_INF)
        return carry

    excess = jnp.max(count(lambda kc: kc >= thr)) > kf
    lax.cond(excess,
             lambda: lax.fori_loop(0, nch, select_chunk, zero_col),
             lambda: lax.fori_loop(0, nch, select_chunk_no_ties, zero_col))

    heads = range(B_HEADS)
    lane = lax.broadcasted_iota(I32, (1, B_WIDTH), 1) // HEAD_DIM
    qs = (qb_ref[0].astype(F32) * HEAD_DIM ** -0.5).astype(BF16)
    qm = [jnp.where(lane == h, qs, jnp.zeros_like(qs)) for h in heads]

    def per_lane(cols):
        out = cols[B_HEADS - 1]
        for h in reversed(range(B_HEADS - 1)):
            out = jnp.where(lane <= h, cols[h], out)
        return out

    def attend_chunk(c, carry):
        off = pl.multiple_of(c * ch, ch)
        bias = selb_ref[c]
        kc = kb_ref[0, pl.ds(off, ch), :]
        vc = vb_ref[0, pl.ds(off, ch), :]
        ms, ls, acc = carry[:B_HEADS], carry[B_HEADS:2 * B_HEADS], carry[2 * B_HEADS]
        s = [_dot_nt(qm[h], kc) + bias for h in heads]
        m_new = [jnp.maximum(ms[h], jnp.max(s[h], axis=-1, keepdims=True)) for h in heads]
        m_safe = [jnp.where(m_new[h] == NEG_INF, 0.0, m_new[h]) for h in heads]
        alpha = [jnp.exp(ms[h] - m_safe[h]) for h in heads]
        p = [jnp.exp(s[h] - m_safe[h]) for h in heads]
        l_new = [alpha[h] * ls[h] + jnp.sum(p[h], axis=-1, keepdims=True) for h in heads]
        pv = [_dot(p[h].astype(BF16), vc) for h in heads]
        acc = per_lane(alpha) * acc + per_lane(pv)
        return tuple(m_new) + tuple(l_new) + (acc,)

    init = ((jnp.full((rows, 1), NEG_INF, F32),) * B_HEADS + (zero_col,) * B_HEADS
            + (jnp.zeros((rows, B_WIDTH), F32),))
    res = lax.fori_loop(0, nch, attend_chunk, init)
    o_ref[0] = (res[2 * B_HEADS] / per_lane(res[B_HEADS:2 * B_HEADS])).astype(o_ref.dtype)


def _mixer_b(pb, pf, tri, batch, seq):
    rows = Q_BLOCK
    top_k = min(TOPK_MAX, seq // 4)
    ch = min(512, seq)
    kern = functools.partial(_mixb_kernel, top_k=top_k)
    return pl.pallas_call(
        kern,
        grid=(batch, seq // rows),
        in_specs=[pl.BlockSpec((1, rows, B_WIDTH), lambda b, i: (b, i, PB_BQ // B_WIDTH)),
                  pl.BlockSpec((1, seq, B_WIDTH), lambda b, i: (b, 0, PB_BK // B_WIDTH)),
                  pl.BlockSpec((1, seq, B_WIDTH), lambda b, i: (b, 0, PB_BV // B_WIDTH)),
                  pl.BlockSpec((1, rows, 256), lambda b, i: (b, i, PB_IQ // 256)),
                  pl.BlockSpec((1, seq, LANES), lambda b, i: (b, 0, PB_IK // LANES)),
                  pl.BlockSpec((1, rows, LANES), lambda b, i: (b, i, PF_MISC // LANES)),
                  pl.BlockSpec(tri.shape, lambda b, i: (0, 0))],
        out_specs=pl.BlockSpec((1, rows, B_WIDTH), lambda b, i: (b, i, 0)),
        out_shape=jax.ShapeDtypeStruct((batch, seq, B_WIDTH), BF16),
        scratch_shapes=[pltpu.VMEM((seq // ch, rows, ch), I32), pltpu.VMEM((seq // ch, rows, ch), F32)],
        compiler_params=_params("parallel", "arbitrary", vmem_mb=48),
        name="mixer_b",
    )(pb, pb, pb, pb, pb, pf, tri)


def _head_consts():
    lane = np.arange(C_WIDTH)
    same = (lane[:, None] // HEAD_DIM == lane[None, :] // HEAD_DIM).astype(np.float32)
    return jnp.asarray(same)


def _rwkv1_kernel(cin_ref, prev_ref, vf_ref, mu_ref, vec_ref, lora_ref, vl_ref, bd_ref,
                  phi_ref, psi_ref, rp_ref, y0_ref, g_ref, bonus_ref, v_ref, *, rows, use_v_lora):
    i = pl.program_id(1)
    pc = cin_ref[0][:, :C_IN]
    prev = jnp.where(i > 0, prev_ref[0][SUBLANES - 1:SUBLANES, :C_IN], 0.0)
    row = lax.broadcasted_iota(I32, (rows, 1), 0)
    shifted = jnp.where(row == 0, prev, pltpu.roll(pc, 1, 0))
    pc = pc + (shifted - pc) * mu_ref[...]
    r = pc[:, 0:C_WIDTH]
    k = pc[:, C_WIDTH:2 * C_WIDTH]
    v = pc[:, 2 * C_WIDTH:3 * C_WIDTH]
    xl = pc[:, 3 * C_WIDTH:C_IN]
    lane = lax.broadcasted_iota(I32, xl.shape, 1)
    feat = jnp.where(lane < C_LORA_W, jnp.tanh(xl), jnp.where(lane < C_LORA_W + C_LORA_A, xl, _sigmoid(xl)))
    w0, a0, k_k, k_a, r_k = (vec_ref[n:n + 1, :] for n in range(5))
    bd = bd_ref[...]
    bd_sel = bd.astype(BF16)
    w_raw = -_softplus(-(w0 + _dot3(feat, lora_ref[0]))) - 0.5
    a = _sigmoid(a0 + _dot3(feat, lora_ref[1]))
    g_ref[0] = _dot3(feat, lora_ref[2])
    if use_v_lora:
        v0 = vec_ref[5:6, :]
        v = v + (vf_ref[0] - v) * _sigmoid(v0 + _dot3(_dot3(v, vl_ref[0]), vl_ref[1]))
    v_ref[0] = v
    kk = k * k_k
    kk = kk / jnp.maximum(jnp.sqrt(_dot_sel(bd_sel, kk * kk, left=False)), 1e-12)
    k = k * (1.0 + (a - 1.0) * k_a)
    logw = -jnp.exp(w_raw)
    bonus_ref[0] = _dot_sel(bd_sel, r * k * r_k, left=False) * v
    av = -kk
    bv = kk * a

    cc = C_CHUNK
    ri = lax.broadcasted_iota(I32, (cc, cc), 0)
    ci = lax.broadcasted_iota(I32, (cc, cc), 1)
    tril = jnp.where(ci <= ri, 1.0, 0.0).astype(BF16)
    strict = ci < ri
    incl = ci <= ri
    eye_c = jnp.where(ci == ri, 1.0, 0.0)
    lane_w = lax.broadcasted_iota(I32, (1, C_WIDTH), 1)
    ri2 = lax.broadcasted_iota(I32, (C_WIDTH, C_WIDTH), 0)
    ci2 = lax.broadcasted_iota(I32, (C_WIDTH, C_WIDTH), 1)
    eye_w = jnp.where(ri2 == ci2, 1.0, 0.0)
    chunks = range(rows // cc)
    heads = range(C_HEADS)
    pairs = [(c, h) for c in chunks for h in heads]
    mh = [jnp.where(lane_w // HEAD_DIM == h, 1.0, 0.0) for h in heads]
    sl = [slice(c * cc, (c + 1) * cc) for c in chunks]
    cs = [_dot_sel(tril, logw[sl[c]]) for c in chunks]
    cs_end = [cs[c][cc - 1:cc, :] for c in chunks]
    at = [av[sl[c]] * jnp.exp(cs[c] - logw[sl[c]]) for c in chunks]
    rt = [r[sl[c]] * jnp.exp(cs[c]) for c in chunks]
    inv = [jnp.exp(-cs[c]) for c in chunks]
    rhs = [jnp.concatenate([bv[sl[c]] * inv[c], k[sl[c]] * inv[c]], axis=0) for c in chunks]
    tail = [jnp.exp(cs_end[c] - cs[c]) for c in chunks]
    vc = [v[sl[c]] for c in chunks]
    vc_b = [vc[c].astype(BF16) for c in chunks]
    ath = {(c, h): at[c] * mh[h] for c, h in pairs}
    aa = {(c, h): _dot3(jnp.concatenate([ath[c, h], rt[c] * mh[h]], axis=0), rhs[c], nt=True) for c, h in pairs}
    a_ab = {p: jnp.where(strict, aa[p][:cc, :cc], 0.0) for p in pairs}
    a_ak = {p: jnp.where(strict, aa[p][:cc, cc:], 0.0).astype(BF16) for p in pairs}
    a_rb = {p: jnp.where(incl, aa[p][cc:, :cc], 0.0).astype(BF16) for p in pairs}
    a_rk = {p: jnp.where(incl, aa[p][cc:, cc:], 0.0).astype(BF16) for p in pairs}
    x = {p: eye_c + a_ab[p] for p in pairs}
    pw = a_ab
    for _ in range(int(math.log2(cc)) - 1):
        pw = {p: _dot3(pw[p], pw[p]) for p in pairs}
        x = {p: x[p] + _dot3(x[p], pw[p]) for p in pairs}
    akv = {(c, h): _dot(a_ak[c, h], vc_b[c]) for c, h in pairs}
    ap_h = {p: _dot3(x[p], ath[p]) for p in pairs}
    w2_h = {(c, h): _dot3(x[c, h], akv[c, h]) * mh[h] for c, h in pairs}
    rp_h = {p: _dot(a_rb[p], ap_h[p].astype(BF16)) for p in pairs}
    y0_h = {(c, h): (_dot(a_rb[c, h], w2_h[c, h].astype(BF16)) + _dot(a_rk[c, h], vc_b[c])) * mh[h] for c, h in pairs}
    for c in chunks:
        ap = sum(ap_h[c, h] for h in heads)
        w2 = sum(w2_h[c, h] for h in heads)
        rp_ref[0, sl[c], :] = rt[c] + sum(rp_h[c, h] for h in heads)
        y0_ref[0, sl[c], :] = sum(y0_h[c, h] for h in heads)
        bh = bv[sl[c]] * tail[c]
        kh = k[sl[c]] * tail[c]
        phi_ref[0, c] = eye_w * jnp.exp(cs_end[c]) + _dot3(bh.T, ap) * bd
        psi_ref[0, c] = _dot(jnp.concatenate([bh, kh], axis=0).T.astype(BF16),
                             jnp.concatenate([w2, vc[c]], axis=0).astype(BF16)) * bd


def _rwkv2_kernel(phi_ref, psi_ref, rp_ref, y0_ref, g_ref, bonus_ref, gn_ref, bd_ref, o_ref, s_ref):
    @pl.when(pl.program_id(1) == 0)
    def _():
        s_ref[...] = jnp.zeros_like(s_ref)

    states = [s_ref[...]]
    for c in range(phi_ref.shape[1]):
        states.append(_dot3(phi_ref[0, c], states[c]) + psi_ref[0, c])
    s_ref[...] = states[-1]
    cc = C_CHUNK
    y = jnp.concatenate([_dot3(rp_ref[0, c * cc:(c + 1) * cc, :], states[c]) for c in range(phi_ref.shape[1])],
                        axis=0) + y0_ref[0]
    bd_sel = bd_ref[...].astype(BF16)
    mu = _dot_sel(bd_sel, y, left=False) * (1.0 / HEAD_DIM)
    d = y - mu
    var = _dot_sel(bd_sel, d * d, left=False) * (1.0 / HEAD_DIM)
    yn = d * lax.rsqrt(var + C_GN_EPS) * gn_ref[0:1, :] + gn_ref[1:2, :]
    o_ref[0] = ((yn + bonus_ref[0]) * g_ref[0]).astype(o_ref.dtype)


def _mixer_c(pf, v_first, mu, vec, lora, vl, gn, bd, batch, seq, use_v_lora):
    rows = min(256, seq)
    nblk = seq // rows
    nch = seq // C_CHUNK
    cpb = rows // C_CHUNK
    kern = functools.partial(_rwkv1_kernel, rows=rows, use_v_lora=use_v_lora)
    full2 = lambda a: pl.BlockSpec(a.shape, lambda b, i: (0,) * a.ndim)
    seq_spec = pl.BlockSpec((1, rows, C_WIDTH), lambda b, i: (b, i, 0))
    mat_spec = pl.BlockSpec((1, cpb, C_WIDTH, C_WIDTH), lambda b, i: (b, i, 0, 0))
    seq_shape = jax.ShapeDtypeStruct((batch, seq, C_WIDTH), F32)
    mat_shape = jax.ShapeDtypeStruct((batch, nch, C_WIDTH, C_WIDTH), F32)
    prev_blk = rows // SUBLANES
    phi, psi, rp, y0, g, bonus, v = pl.pallas_call(
        kern,
        grid=(batch, nblk),
        in_specs=[pl.BlockSpec((1, rows, 1024), lambda b, i: (b, i, PF_CIN // 1024)),
                  pl.BlockSpec((1, SUBLANES, 1024), lambda b, i: (b, jnp.maximum(i * prev_blk - 1, 0), PF_CIN // 1024)),
                  seq_spec, full2(mu), full2(vec), full2(lora), full2(vl), full2(bd)],
        out_specs=[mat_spec, mat_spec, seq_spec, seq_spec, seq_spec, seq_spec, seq_spec],
        out_shape=[mat_shape, mat_shape, seq_shape, seq_shape, seq_shape, seq_shape, seq_shape],
        compiler_params=_params("parallel", "parallel", vmem_mb=48),
        name="rwkv_chunks",
    )(pf, pf, v_first, mu, vec, lora, vl, bd)
    scan_chunks = min(4, nch)
    cseq = pl.BlockSpec((1, scan_chunks * C_CHUNK, C_WIDTH), lambda b, c: (b, c, 0))
    cmat = pl.BlockSpec((1, scan_chunks, C_WIDTH, C_WIDTH), lambda b, c: (b, c, 0, 0))
    o = pl.pallas_call(
        _rwkv2_kernel,
        grid=(batch, nch // scan_chunks),
        in_specs=[cmat, cmat, cseq, cseq, cseq, cseq,
                  pl.BlockSpec(gn.shape, lambda b, c: (0, 0)), pl.BlockSpec(bd.shape, lambda b, c: (0, 0))],
        out_specs=cseq,
        out_shape=jax.ShapeDtypeStruct((batch, seq, C_WIDTH), BF16),
        scratch_shapes=[pltpu.VMEM((C_WIDTH, C_WIDTH), F32)],
        compiler_params=_params("parallel", "arbitrary"),
        name="rwkv_scan",
    )(phi, psi, rp, y0, g, bonus, gn, bd)
    return o, v


def _ssd_consts():
    expand = np.zeros((LANES, D_INNER), np.float32)
    for h in range(D_HEADS):
        expand[MISC_DT + h, h * HEAD_DIM:(h + 1) * HEAD_DIM] = 1.0
    return jnp.asarray(expand)


def _ssd_kernel(xbc_ref, z_ref, misc_ref, conv_ref, vec_ref, hp_ref, expand_ref, bd_ref, o_ref, st_ref, prev_ref):
    @pl.when(pl.program_id(1) == 0)
    def _():
        st_ref[...] = jnp.zeros_like(st_ref)
        prev_ref[...] = jnp.zeros_like(prev_ref)

    q = D_CHUNK
    x_raw = xbc_ref[0]
    ext = jnp.concatenate([prev_ref[...], x_raw], axis=0)
    conv = jnp.zeros((q, D_XBC), F32)
    for t in range(D_CONV):
        lo = SUBLANES - (D_CONV - 1) + t
        conv = conv + ext[lo:lo + q, :] * conv_ref[t:t + 1, :]
    prev_ref[...] = x_raw[q - SUBLANES:, :]
    xbc = _silu(conv + conv_ref[D_CONV:D_CONV + 1, :])
    xs = xbc[:, :D_INNER]
    bm = xbc[:, D_INNER:D_INNER + D_GROUPS * D_STATE]
    cm = xbc[:, D_INNER + D_GROUPS * D_STATE:]

    dt_col = _softplus(misc_ref[0] + hp_ref[0:1, :])
    a_col = dt_col * hp_ref[1:2, :]
    ri = lax.broadcasted_iota(I32, (q, q), 0)
    ci = lax.broadcasted_iota(I32, (q, q), 1)
    causal = ci <= ri
    tril = jnp.where(causal, 1.0, 0.0).astype(BF16)
    acs_col = _dot_sel(tril, a_col)
    acs_row = acs_col.T
    expand = expand_ref[...].astype(BF16)
    acs = _dot_sel(expand, acs_col, left=False)
    dt = _dot_sel(expand, dt_col, left=False)
    acs_end = acs[q - 1:q, :]
    xdt = xs * dt

    lane = lax.broadcasted_iota(I32, (1, LANES), 1)
    lane_w = lax.broadcasted_iota(I32, (1, D_INNER), 1)
    left = lane < D_STATE
    bm_sw = pltpu.roll(bm, D_STATE, 1)
    cm_sw = pltpu.roll(cm, D_STATE, 1)
    b_exp = jnp.concatenate([jnp.where(left, bm, bm_sw), jnp.where(left, bm_sw, bm)], axis=1)
    c_exp = jnp.concatenate([jnp.where(left, cm, cm_sw), jnp.where(left, cm_sw, cm)], axis=1)
    cb = [_dot3(jnp.where(left == (g == 0), cm, 0.0), bm, nt=True) for g in range(D_GROUPS)]

    scores = []
    for h in range(D_HEADS):
        col = acs_col[:, MISC_DT + h:MISC_DT + h + 1]
        rw = acs_row[MISC_DT + h:MISC_DT + h + 1, :]
        decay = jnp.exp(jnp.where(causal, col - rw, NEG_INF))
        scores.append(cb[h // (D_HEADS // D_GROUPS)] * decay)
    y_h = [_dot3(scores[h], xdt) for h in range(D_HEADS)]
    y = jnp.zeros((q, D_INNER), F32)
    for h in range(D_HEADS):
        y = y + jnp.where(lane_w // HEAD_DIM == h, y_h[h], 0.0)
    st = st_ref[...]
    y = y + _dot3(c_exp, st) * jnp.exp(acs)
    st_ref[...] = st * jnp.exp(acs_end) + _dot3(b_exp.T, xdt * jnp.exp(acs_end - acs)) * bd_ref[...]
    y = y + xs * vec_ref[0:1, :]
    y = y * _silu(z_ref[0])
    half = D_INNER // D_GROUPS
    outs = []
    for g in range(D_GROUPS):
        yg = y[:, g * half:(g + 1) * half]
        outs.append(yg * lax.rsqrt(jnp.mean(yg * yg, axis=-1, keepdims=True) + D_NORM_EPS))
    o_ref[0] = (jnp.concatenate(outs, axis=1) * vec_ref[1:2, :]).astype(o_ref.dtype)


def _mixer_d(pf, conv, vec, hp, expand, bd, batch, seq):
    q = D_CHUNK
    full = lambda a: pl.BlockSpec(a.shape, lambda b, i: (0, 0))
    return pl.pallas_call(
        _ssd_kernel,
        grid=(batch, seq // q),
        in_specs=[pl.BlockSpec((1, q, D_XBC), lambda b, i: (b, i, PF_XBC // D_XBC)),
                  pl.BlockSpec((1, q, D_INNER), lambda b, i: (b, i, PF_Z // D_INNER)),
                  pl.BlockSpec((1, q, LANES), lambda b, i: (b, i, PF_MISC // LANES)),
                  full(conv), full(vec), full(hp), full(expand), full(bd)],
        out_specs=pl.BlockSpec((1, q, D_INNER), lambda b, i: (b, i, 0)),
        out_shape=jax.ShapeDtypeStruct((batch, seq, D_INNER), BF16),
        scratch_shapes=[pltpu.VMEM((D_INNER, D_INNER), F32), pltpu.VMEM((SUBLANES, D_XBC), F32)],
        compiler_params=_params("parallel", "arbitrary"),
        name="ssd",
    )(pf, pf, pf, conv, vec, hp, expand, bd)


def _merge_kernel(x_ref, gates_ref, oa_ref, ob_ref, oc_ref, od_ref, wbr_ref, wout_ref, ln_ref, wr_ref, br_ref,
                  x1_ref, route_ref):
    acc = None
    for n, o_ref in enumerate((oa_ref, ob_ref, oc_ref, od_ref)):
        term = _sigmoid(gates_ref[:, n * D_MODEL:(n + 1) * D_MODEL]) * _dot(o_ref[...], wbr_ref[n])
        acc = term if acc is None else acc + term
    h = DEEPNORM_ALPHA * x_ref[...] + _dot(acc.astype(BF16), wout_ref[...])
    x1 = _layer_norm(h, ln_ref[0:1, :], ln_ref[1:2, :])
    x1_ref[...] = x1

    logits = _dot3(x1, wr_ref[...]) + br_ref[...]
    lane = lax.broadcasted_iota(I32, logits.shape, 1)
    big = jnp.int32(LANES)
    gl = jnp.where(lane < N_EXPERT_GROUPS, logits, NEG_INF)
    gm = jnp.max(gl, axis=-1, keepdims=True)
    pg_top = 1.0 / jnp.sum(jnp.exp(gl - gm), axis=-1, keepdims=True)
    g_sel = jnp.min(jnp.where(gl == gm, lane, big), axis=-1, keepdims=True)
    off = N_EXPERT_GROUPS + g_sel * EXPERTS_PER_GROUP
    el = jnp.where((lane >= off) & (lane < off + EXPERTS_PER_GROUP), logits, NEG_INF)
    em = jnp.max(el, axis=-1, keepdims=True)
    es = jnp.sum(jnp.exp(el - em), axis=-1, keepdims=True)
    idx1 = jnp.min(jnp.where(el == em, lane, big), axis=-1, keepdims=True)
    el2 = jnp.where(lane == idx1, NEG_INF, el)
    em2 = jnp.max(el2, axis=-1, keepdims=True)
    idx2 = jnp.min(jnp.where(el2 == em2, lane, big), axis=-1, keepdims=True)
    p1 = 1.0 / es
    p2 = jnp.exp(em2 - em) / es
    gate1 = pg_top * p1 / (p1 + p2)
    gate2 = pg_top * p2 / (p1 + p2)
    e1 = (idx1 - N_EXPERT_GROUPS).astype(F32)
    e2 = (idx2 - N_EXPERT_GROUPS).astype(F32)
    route_ref[...] = jnp.where(lane == 0, e1, jnp.where(lane == 1, e2, jnp.where(lane == 2, gate1,
                               jnp.where(lane == 3, gate2, 0.0))))


def _merge(x2d, pf, oa, ob, oc, od, wbr, wout, ln, wr, br, tm):
    m = x2d.shape[0]
    row = lambda w: pl.BlockSpec((tm, w), lambda i: (i, 0))
    full = lambda a: pl.BlockSpec(a.shape, lambda i: (0,) * a.ndim)
    return pl.pallas_call(
        _merge_kernel,
        grid=(m // tm,),
        in_specs=[row(D_MODEL), pl.BlockSpec((tm, N_BRANCH * D_MODEL), lambda i: (i, PF_GATES)),
                  row(A_WIDTH), row(B_WIDTH), row(C_WIDTH), row(D_INNER),
                  full(wbr), full(wout), full(ln), full(wr), full(br)],
        out_specs=[row(D_MODEL), row(LANES)],
        out_shape=[jax.ShapeDtypeStruct((m, D_MODEL), F32), jax.ShapeDtypeStruct((m, LANES), F32)],
        compiler_params=_params("parallel", vmem_mb=48),
        name="merge_route",
    )(x2d, pf, oa, ob, oc, od, wbr, wout, ln, wr, br)


def _ffn_kernel(be_ref, nv_ref, xs_ref, wg_ref, wu_ref, wd_ref, o_ref):
    i = pl.program_id(0)

    @pl.when(i < nv_ref[0])
    def _():
        xb = xs_ref[...]
        hg = _dot(xb, wg_ref[0, 0].astype(BF16))
        hu = _dot(xb, wu_ref[0, 0].astype(BF16))
        o_ref[...] = _dot((_silu(hg) * hu).astype(BF16), wd_ref[0, 0].astype(BF16))

    @pl.when(i >= nv_ref[0])
    def _():
        o_ref[...] = jnp.zeros_like(o_ref)


def _ffn(blk_expert, n_valid, xs, e_gate, e_up, e_down, layer):
    cap = xs.shape[0]
    wspec = lambda a: pl.BlockSpec((1, 1) + a.shape[2:], lambda i, be, nv: (layer, be[i], 0, 0))
    return pl.pallas_call(
        _ffn_kernel,
        grid_spec=pltpu.PrefetchScalarGridSpec(
            num_scalar_prefetch=2,
            grid=(cap // MOE_ROWS,),
            in_specs=[pl.BlockSpec((MOE_ROWS, D_MODEL), lambda i, be, nv: (i, 0)),
                      wspec(e_gate), wspec(e_up), wspec(e_down)],
            out_specs=pl.BlockSpec((MOE_ROWS, D_MODEL), lambda i, be, nv: (i, 0)),
        ),
        out_shape=jax.ShapeDtypeStruct((cap, D_MODEL), F32),
        compiler_params=_params("arbitrary", vmem_mb=48),
        name="expert_ffn",
    )(blk_expert, n_valid, xs, e_gate, e_up, e_down)


def _combine_kernel(dest_ref, x_ref, route_ref, ln_ref, yb_ref, o_ref, ybuf, sem):
    i = pl.program_id(0)
    tm = x_ref.shape[0]

    def gather(tile, slot):
        base = tile * (TOP_K_EXPERTS * tm)

        def body(r, carry):
            for s in range(TOP_K_EXPERTS):
                row = dest_ref[base + TOP_K_EXPERTS * r + s]
                pltpu.make_async_copy(yb_ref.at[pl.ds(row, 1)], ybuf.at[slot, pl.ds(s * tm + r, 1)],
                                      sem.at[slot]).start()
            return carry

        lax.fori_loop(0, tm, body, 0, unroll=8)

    @pl.when(i == 0)
    def _():
        gather(0, 0)

    @pl.when(i + 1 < pl.num_programs(0))
    def _():
        gather(i + 1, (i + 1) % 2)

    slot = i % 2
    pltpu.make_async_copy(yb_ref.at[pl.ds(0, TOP_K_EXPERTS * tm)], ybuf.at[slot], sem.at[slot]).wait()
    g0 = route_ref[:, 2:3]
    g1 = route_ref[:, 3:4]
    h = DEEPNORM_ALPHA * x_ref[...] + (ybuf[slot, 0:tm, :] * g0 + ybuf[slot, tm:2 * tm, :] * g1)
    o_ref[...] = _layer_norm(h, ln_ref[0:1, :], ln_ref[1:2, :])


def _combine(x1, yb, dest, route, ln, tm):
    m = x1.shape[0]
    return pl.pallas_call(
        _combine_kernel,
        grid_spec=pltpu.PrefetchScalarGridSpec(
            num_scalar_prefetch=1,
            grid=(m // tm,),
            in_specs=[pl.BlockSpec((tm, D_MODEL), lambda i, d: (i, 0)),
                      pl.BlockSpec((tm, LANES), lambda i, d: (i, 0)),
                      pl.BlockSpec(ln.shape, lambda i, d: (0, 0)),
                      pl.BlockSpec(memory_space=pl.ANY)],
            out_specs=pl.BlockSpec((tm, D_MODEL), lambda i, d: (i, 0)),
            scratch_shapes=[pltpu.VMEM((2, TOP_K_EXPERTS * tm, D_MODEL), F32), pltpu.SemaphoreType.DMA((2,))],
        ),
        out_shape=jax.ShapeDtypeStruct((m, D_MODEL), F32),
        compiler_params=_params("arbitrary"),
        name="combine",
    )(dest, x1, route, ln, yb)


def _dispatch_tables(route, m):
    flat_e = route[:, 0:TOP_K_EXPERTS].astype(I32).reshape(-1)
    n_assign = m * TOP_K_EXPERTS
    onehot = (flat_e[:, None] == jnp.arange(N_EXPERTS, dtype=I32)[None, :]).astype(I32)
    csum = jnp.cumsum(onehot, axis=0)
    rank = jnp.sum(csum * onehot, axis=1) - 1
    counts = csum[-1]
    padded = (counts + MOE_ROWS - 1) // MOE_ROWS * MOE_ROWS
    pad_end = jnp.cumsum(padded)
    pad_start = pad_end - padded
    dest = pad_start[flat_e] + rank
    cap = (n_assign + N_EXPERTS * (MOE_ROWS - 1) + MOE_ROWS - 1) // MOE_ROWS * MOE_ROWS
    n_blocks = cap // MOE_ROWS
    blk_start = jnp.arange(n_blocks, dtype=I32) * MOE_ROWS
    blk_expert = jnp.minimum(jnp.sum((pad_end[None, :] <= blk_start[:, None]).astype(I32), axis=1), N_EXPERTS - 1)
    buf_tok = (jnp.arange(cap, dtype=I32) % m).at[dest].set(jnp.arange(n_assign, dtype=I32) // TOP_K_EXPERTS)
    n_valid = (pad_end[-1] // MOE_ROWS).astype(I32).reshape(1)
    return dest, buf_tok, blk_expert, n_valid


def _row_pad(a, rows):
    return jnp.pad(a, ((0, rows - a.shape[0]), (0, 0)))


def kernel(x, w_in, c_mu, c_w0, c_w2, c_a0, c_a2, c_g2, c_kk, c_ka, c_rk, c_gn_w, c_gn_b, c_v0, c_v1, c_v2,
           d_conv_w, d_conv_b, d_dt_bias, d_a_log, d_skip, d_norm_w, w_branch, w_out, ln1_g, ln1_b,
           r_group, r_group_b, r_expert, r_expert_b, e_gate, e_up, e_down, ln2_g, ln2_b):
    batch, seq, _ = x.shape
    m = batch * seq
    biases = _mixa_bias()
    tw = min(256, seq)
    tri = jnp.asarray(np.triu(np.ones((tw, tw), np.float32))).astype(BF16)
    bd = _head_consts()
    expand = _ssd_consts()
    x2d = x.reshape(m, D_MODEL)
    v_first = jnp.zeros((batch, seq, C_WIDTH), F32)
    tm_proj = min(2048, m)
    tm_tok = min(512, m)
    for l in range(DEPTH):
        wb, wf = _proj_weights(w_in[l])
        pb = _proj(x2d, wb, BF16, tm_proj, PROJ_TN).reshape(batch, seq, PB_WIDTH)
        pf = _proj(x2d, wf, F32, tm_proj, PROJ_TN).reshape(batch, seq, PF_WIDTH)

        kv_pad = jnp.pad(pb[:, :, PB_AKV:PB_AQ], ((0, 0), (A_MAXWIN, 0), (0, 0)))
        o_a = _mixer_a(pb, kv_pad, biases, batch, seq)
        o_b = _mixer_b(pb, pf, tri, batch, seq)

        use_v_lora = l > 0
        vec_rows = [c_w0[l], c_a0[l], c_kk[l], c_ka[l], c_rk[l].reshape(-1)]
        vec_rows.append(c_v0[l - 1] if use_v_lora else jnp.zeros((C_WIDTH,), F32))
        vec = _row_pad(jnp.stack(vec_rows), SUBLANES)
        lora = jnp.stack([
            jnp.pad(c_w2[l], ((0, LANES - C_LORA_W), (0, 0))),
            jnp.pad(c_a2[l], ((C_LORA_W, LANES - C_LORA_W - C_LORA_A), (0, 0))),
            jnp.pad(c_g2[l], ((C_LORA_W + C_LORA_A, 0), (0, 0)))])
        if use_v_lora:
            vl = jnp.stack([jnp.pad(c_v1[l - 1], ((0, 0), (0, C_WIDTH - C_LORA_V))),
                            jnp.pad(c_v2[l - 1], ((0, C_WIDTH - C_LORA_V), (0, 0)))])
        else:
            vl = jnp.zeros((2, C_WIDTH, C_WIDTH), F32)
        gn = _row_pad(jnp.stack([c_gn_w[l], c_gn_b[l]]), SUBLANES)
        o_c, v_c = _mixer_c(pf, v_first, c_mu[l].reshape(1, C_IN), vec, lora, vl, gn, bd, batch, seq, use_v_lora)
        if l == 0:
            v_first = v_c

        conv = _row_pad(jnp.concatenate([d_conv_w[l], d_conv_b[l][None, :]], axis=0), SUBLANES)
        dvec = _row_pad(jnp.stack([jnp.repeat(d_skip[l], HEAD_DIM), d_norm_w[l]]), SUBLANES)
        place = lambda a: jnp.pad(a, (MISC_DT, LANES - MISC_DT - D_HEADS))
        hp = _row_pad(jnp.stack([place(d_dt_bias[l]), place(-jnp.exp(d_a_log[l]))]), SUBLANES)
        o_d = _mixer_d(pf, conv, dvec, hp, expand, bd, batch, seq)

        wr = jnp.pad(jnp.concatenate([r_group[l], r_expert[l]], axis=1),
                     ((0, 0), (0, LANES - N_EXPERT_GROUPS - N_EXPERTS)))
        br = jnp.pad(jnp.concatenate([r_group_b[l], r_expert_b[l]]), (0, LANES - N_EXPERT_GROUPS - N_EXPERTS))
        ln1 = _row_pad(jnp.stack([ln1_g[l], ln1_b[l]]), SUBLANES)
        x1, route = _merge(x2d, pf.reshape(m, PF_WIDTH), o_a.reshape(m, -1), o_b.reshape(m, -1),
                           o_c.reshape(m, -1), o_d.reshape(m, -1), w_branch[l].astype(BF16),
                           w_out[l].astype(BF16), ln1, wr, br.reshape(1, LANES), tm_tok)

        dest, buf_tok, blk_expert, n_valid = _dispatch_tables(route, m)
        xs = jnp.take(x1.astype(BF16), buf_tok, axis=0)
        yb = _ffn(blk_expert, n_valid, xs, e_gate, e_up, e_down, l)
        ln2 = _row_pad(jnp.stack([ln2_g[l], ln2_b[l]]), SUBLANES)
        x2d = _combine(x1, yb, dest, route, ln2, min(256, m))
    return x2d.reshape(batch, seq, D_MODEL)
```

```python
import functools
import math

import jax
import jax.numpy as jnp
import numpy as np
from jax import lax
from jax.experimental import pallas as pl
from jax.experimental.pallas import tpu as pltpu

F32 = jnp.float32
BF16 = jnp.bfloat16
I32 = jnp.int32
NEG_INF = float("-inf")
INT_MIN = -(2 ** 31)

LANES = 128
SUBLANES = 8

D_MODEL = 1024
DEPTH = 2
HEAD_DIM = 64
Q_BLOCK = 128

A_HEADS = 4
A_PATTERNS = ((128, 1), (512, 4), (2048, 16))
A_GROUPS = len(A_PATTERNS)
A_WIDTH = A_HEADS * HEAD_DIM
A_MAXWIN = max(w for w, _ in A_PATTERNS)

B_HEADS = 4
B_WIDTH = B_HEADS * HEAD_DIM
IDX_HEADS = 4
IDX_DIM = 64
TOPK_MAX = 256

C_HEADS = 4
C_WIDTH = C_HEADS * HEAD_DIM
C_LORA_W = 32
C_LORA_A = 32
C_LORA_G = 64
C_LORA_V = 16
C_IN = 3 * C_WIDTH + C_LORA_W + C_LORA_A + C_LORA_G
C_GN_EPS = 64e-5
C_CHUNK = 64

D_HEADS = 4
D_INNER = D_HEADS * HEAD_DIM
D_GROUPS = 2
D_STATE = 64
D_CONV = 4
D_CHUNK = 128
D_XBC = D_INNER + 2 * D_GROUPS * D_STATE
D_NORM_EPS = 1e-5

N_BRANCH = 4
N_EXPERT_GROUPS = 4
EXPERTS_PER_GROUP = 8
N_EXPERTS = N_EXPERT_GROUPS * EXPERTS_PER_GROUP
TOP_K_EXPERTS = 2
D_EXPERT = 512
MOE_ROWS = 512

LN_EPS = 1e-5
DEEPNORM_ALPHA = (2 * DEPTH) ** 0.25

PB_AKV = 0
PB_AQ = PB_AKV + A_HEADS * A_GROUPS * 2 * HEAD_DIM
PB_BQ = PB_AQ + A_HEADS * 4 * HEAD_DIM
PB_BK = PB_BQ + B_WIDTH
PB_BV = PB_BK + B_WIDTH
PB_IQ = PB_BV + B_WIDTH
PB_IK = PB_IQ + IDX_HEADS * IDX_DIM
PB_WIDTH = PB_IK + 256
PF_GATES = 0
PF_CIN = PF_GATES + N_BRANCH * D_MODEL
PF_MISC = PF_CIN + C_IN
PF_XBC = PF_CIN + 1024
PF_Z = PF_XBC + D_XBC
PROJ_TN = 768
PF_WIDTH = -(-(PF_Z + D_INNER) // PROJ_TN) * PROJ_TN
MISC_DT = 4


def _dot(a, b, precision=None):
    return jnp.dot(a, b, preferred_element_type=F32, precision=precision)


def _dot_nt(a, b, precision=None):
    return lax.dot_general(a, b, (((1,), (1,)), ((), ())), preferred_element_type=F32, precision=precision)


def _split2(a):
    hi = a.astype(BF16)
    return hi, (a - hi.astype(F32)).astype(BF16)


def _dot3(a, b, nt=False):
    mm = _dot_nt if nt else _dot
    ah, al = _split2(a)
    bh, bl = _split2(b)
    return mm(ah, bh) + (mm(ah, bl) + mm(al, bh))


def _dot_sel(sel, x, left=True):
    hi = x.astype(BF16)
    r1 = x - hi.astype(F32)
    mid = r1.astype(BF16)
    lo = (r1 - mid.astype(F32)).astype(BF16)
    if left:
        return _dot(sel, hi) + (_dot(sel, mid) + _dot(sel, lo))
    return _dot(hi, sel) + (_dot(mid, sel) + _dot(lo, sel))


def _sigmoid(x):
    return 1.0 / (1.0 + jnp.exp(-x))


def _softplus(x):
    return jnp.maximum(x, 0.0) + jnp.log1p(jnp.exp(-jnp.abs(x)))


def _silu(x):
    return x * _sigmoid(x)


def _layer_norm(h, g, b):
    mu = jnp.mean(h, axis=-1, keepdims=True)
    d = h - mu
    var = jnp.mean(d * d, axis=-1, keepdims=True)
    return d * lax.rsqrt(var + LN_EPS) * g + b


def _params(*sem, vmem_mb=None):
    kw = {}
    if vmem_mb is not None:
        kw["vmem_limit_bytes"] = vmem_mb * 1024 * 1024
    return pltpu.CompilerParams(dimension_semantics=sem, **kw)


def _proj_kernel(x_ref, w_ref, o_ref):
    o_ref[...] = _dot(x_ref[...].astype(BF16), w_ref[...]).astype(o_ref.dtype)


def _proj(x2d, w, out_dtype, tm, tn):
    m, k = x2d.shape
    n = w.shape[1]
    return pl.pallas_call(
        _proj_kernel,
        grid=(m // tm, n // tn),
        in_specs=[pl.BlockSpec((tm, k), lambda i, j: (i, 0)),
                  pl.BlockSpec((k, tn), lambda i, j: (0, j))],
        out_specs=pl.BlockSpec((tm, tn), lambda i, j: (i, j)),
        out_shape=jax.ShapeDtypeStruct((m, n), out_dtype),
        compiler_params=_params("parallel", "arbitrary", vmem_mb=48),
        name="in_proj",
    )(x2d, w)


def _proj_weights(w):
    k = w.shape[0]
    off = 0
    seg = {}
    for name, size in (("a_qkv", 3 * A_GROUPS * A_WIDTH), ("b_qkv", 3 * B_WIDTH), ("b_idx_q", IDX_HEADS * IDX_DIM),
                       ("b_idx_k", IDX_DIM), ("b_idx_w", IDX_HEADS), ("c_in", C_IN), ("d_z", D_INNER),
                       ("d_xbc", D_XBC), ("d_dt", D_HEADS), ("gates", N_BRANCH * D_MODEL)):
        seg[name] = w[:, off:off + size]
        off += size
    a = seg["a_qkv"].reshape(k, 3, A_GROUPS, A_HEADS, HEAD_DIM)
    a_kv = jnp.transpose(a[:, 1:3], (0, 3, 2, 1, 4)).reshape(k, -1)
    a_q = jnp.transpose(a[:, 0], (0, 2, 1, 3))
    a_q = jnp.pad(a_q, ((0, 0), (0, 0), (0, 4 - A_GROUPS), (0, 0))).reshape(k, -1)
    zeros = lambda n: jnp.zeros((k, n), w.dtype)
    wb = jnp.concatenate([a_kv, a_q, seg["b_qkv"], seg["b_idx_q"], seg["b_idx_k"], zeros(256 - IDX_DIM)], axis=1)
    misc = jnp.concatenate([seg["b_idx_w"], seg["d_dt"], zeros(LANES - IDX_HEADS - D_HEADS)], axis=1)
    wf = jnp.concatenate([seg["gates"], seg["c_in"], misc, seg["d_xbc"], seg["d_z"]], axis=1)
    wf = jnp.concatenate([wf, zeros(PF_WIDTH - wf.shape[1])], axis=1)
    assert wb.shape[1] == PB_WIDTH and PB_WIDTH % PROJ_TN == 0
    return wb.astype(BF16), wf.astype(BF16)


def _mixa_bias():
    out = []
    for win, dil in A_PATTERNS:
        r = np.arange(Q_BLOCK)[:, None]
        c = np.arange(win + Q_BLOCK)[None, :]
        d = r + win - c
        ok = (d >= 0) & (d <= win) & (d % dil == 0)
        out.append(jnp.asarray(np.where(ok, 0.0, -np.inf).astype(np.float32)))
    return out


def _mixa_kernel(q_ref, kv_ref, b0_ref, b1_ref, b2_ref, o_ref):
    start = pl.program_id(2) * Q_BLOCK
    scale = HEAD_DIM ** -0.5
    bias_refs = (b0_ref, b1_ref, b2_ref)
    outs = []
    for hh in range(2):
        ms, ls, os_ = [], [], []
        for g, (win, _) in enumerate(A_PATTERNS):
            span = win + Q_BLOCK
            q = q_ref[0, :, hh * 256 + g * HEAD_DIM: hh * 256 + (g + 1) * HEAD_DIM]
            base = pl.multiple_of(start + (A_MAXWIN - win), Q_BLOCK)
            c0 = hh * (A_GROUPS * 2 * HEAD_DIM) + g * 2 * HEAD_DIM
            k = kv_ref[0, pl.ds(base, span), c0:c0 + HEAD_DIM]
            v = kv_ref[0, pl.ds(base, span), c0 + HEAD_DIM:c0 + 2 * HEAD_DIM]
            s = _dot_nt(q, k) * scale + bias_refs[g][...]
            col = lax.broadcasted_iota(I32, (Q_BLOCK, span), 1)
            s = jnp.where(col >= win - start, s, NEG_INF)
            m = jnp.max(s, axis=-1, keepdims=True)
            p = jnp.exp(s - m)
            ls.append(jnp.sum(p, axis=-1, keepdims=True))
            os_.append(_dot(p.astype(BF16), v))
            ms.append(m)
        m_all = jnp.maximum(jnp.maximum(ms[0], ms[1]), ms[2])
        num = jnp.zeros((Q_BLOCK, HEAD_DIM), F32)
        den = jnp.zeros((Q_BLOCK, 1), F32)
        for g in range(A_GROUPS):
            wg = jnp.exp(ms[g] - m_all)
            num = num + wg * os_[g]
            den = den + wg * ls[g]
        outs.append(num / den)
    o_ref[0] = jnp.concatenate(outs, axis=1).astype(o_ref.dtype)


def _mixer_a(pb, kv_pad, biases, batch, seq):
    nblk = seq // Q_BLOCK
    pair_kv = 2 * A_GROUPS * 2 * HEAD_DIM
    in_specs = [pl.BlockSpec((1, Q_BLOCK, 512), lambda b, hp, i: (b, i, PB_AQ // 512 + hp)),
                pl.BlockSpec((1, seq + A_MAXWIN, pair_kv), lambda b, hp, i: (b, 0, hp))]
    for bias in biases:
        in_specs.append(pl.BlockSpec(bias.shape, lambda b, hp, i: (0, 0)))
    return pl.pallas_call(
        _mixa_kernel,
        grid=(batch, A_HEADS // 2, nblk),
        in_specs=in_specs,
        out_specs=pl.BlockSpec((1, Q_BLOCK, 2 * HEAD_DIM), lambda b, hp, i: (b, i, hp)),
        out_shape=jax.ShapeDtypeStruct((batch, seq, A_WIDTH), BF16),
        compiler_params=_params("parallel", "parallel", "arbitrary", vmem_mb=48),
        name="mixer_a",
    )(pb, kv_pad, *biases)


A_ROWS2 = Q_BLOCK // 16
A_SPAN1 = 512 // 4 + Q_BLOCK // 4
A_SPAN2 = 144


def _mixa_class_bias():
    x = (np.arange(Q_BLOCK) % A_ROWS2) * 16 + np.arange(Q_BLOCK) // A_ROWS2
    c0 = np.arange(2 * Q_BLOCK)[None, :]
    b0 = np.where((c0 >= x[:, None]) & (c0 <= x[:, None] + Q_BLOCK), 0.0, -np.inf)
    u, a16 = np.arange(32) // A_ROWS2, np.arange(32) % A_ROWS2
    a4 = 4 * a16 + u
    c1 = np.arange(A_SPAN1)[None, :]
    b1 = np.where((c1 >= a4[:, None]) & (c1 <= a4[:, None] + Q_BLOCK), 0.0, -np.inf)
    a2 = np.arange(A_ROWS2)
    c2 = np.arange(A_SPAN2)[None, :]
    b2 = np.where((c2 >= a2[:, None]) & (c2 <= a2[:, None] + Q_BLOCK), 0.0, -np.inf)
    return [jnp.asarray(b.astype(np.float32)) for b in (b0, b1, b2)]


def _mixa2_kernel(q0_ref, q1_ref, q2_ref, k0_ref, v0_ref, k1_ref, v1_ref, k2_ref, v2_ref,
                  b0_ref, b1_ref, b2_ref, o_ref):
    pid = pl.program_id(2)
    start = pid * Q_BLOCK
    r8 = A_ROWS2
    scale = HEAD_DIM ** -0.5
    head_of_lane = lax.broadcasted_iota(I32, (1, 2 * HEAD_DIM), 1) // HEAD_DIM
    qf = [q_ref[0].astype(F32) * scale for q_ref in (q0_ref, q1_ref, q2_ref)]

    def masked_bias(b_ref, first_valid):
        col = lax.broadcasted_iota(I32, b_ref.shape, 1)
        return jnp.where(col >= first_valid, b_ref[...], NEG_INF)

    w0 = pl.ds(pl.multiple_of(start, Q_BLOCK), 2 * Q_BLOCK)
    classes = [(qf[0], k0_ref[0, w0, :], v0_ref[0, w0, :], masked_bias(b0_ref, Q_BLOCK - start))]
    w1 = pl.ds(pl.multiple_of(pid * (Q_BLOCK // 4), Q_BLOCK // 4), A_SPAN1)
    bias1 = masked_bias(b1_ref, Q_BLOCK - pid * (Q_BLOCK // 4))
    for r4 in range(4):
        q1 = jnp.concatenate([qf[1][(r4 + 4 * u) * r8:(r4 + 4 * u + 1) * r8] for u in range(4)], axis=0)
        classes.append((q1, k1_ref[0, r4, w1, :], v1_ref[0, r4, w1, :], bias1))
    w2 = pl.ds(pl.multiple_of(pid * r8, r8), A_SPAN2)
    bias2 = masked_bias(b2_ref, Q_BLOCK - pid * r8)
    for r16 in range(16):
        classes.append((qf[2][r16 * r8:(r16 + 1) * r8], k2_ref[0, r16, w2, :].astype(BF16),
                        v2_ref[0, r16, w2, :].astype(BF16), bias2))

    def rows_of(vals):
        g1 = jnp.concatenate([vals[1 + r16 % 4][(r16 // 4) * r8:(r16 // 4 + 1) * r8] for r16 in range(16)], axis=0)
        return [vals[0], g1, jnp.concatenate(vals[5:21], axis=0)]

    first = head_of_lane == 0

    def stacked(q):
        return jnp.concatenate([jnp.where(first, q, 0.0), jnp.where(first, 0.0, q)], axis=0).astype(BF16)

    s = [_dot_nt(stacked(q), k) + jnp.concatenate([bias, bias], axis=0) for q, k, _, bias in classes]
    m = [jnp.max(si, axis=-1, keepdims=True) for si in s]
    p = [jnp.exp(si - mi) for si, mi in zip(s, m)]
    l = [jnp.sum(pi, axis=-1, keepdims=True) for pi in p]
    o = [_dot(pi.astype(BF16), c[2]) for pi, c in zip(p, classes)]

    res = []
    for hh in range(2):
        half = lambda vals: [v[hh * (v.shape[0] // 2):(hh + 1) * (v.shape[0] // 2)] for v in vals]
        ms, ls, os_ = rows_of(half(m)), rows_of(half(l)), rows_of(half(o))
        m_all = jnp.maximum(jnp.maximum(ms[0], ms[1]), ms[2])
        num = jnp.zeros((Q_BLOCK, 2 * HEAD_DIM), F32)
        den = jnp.zeros((Q_BLOCK, 1), F32)
        for g in range(A_GROUPS):
            wg = jnp.exp(ms[g] - m_all)
            num = num + wg * os_[g]
            den = den + wg * ls[g]
        res.append(num / den)
    o_ref[0] = jnp.where(head_of_lane == 0, res[0], res[1]).astype(o_ref.dtype)


def _mixer_a2(pb, biases, batch, seq):
    nblk = seq // Q_BLOCK
    r8 = A_ROWS2
    a_kv = pb[:, :, PB_AKV:PB_AQ].reshape(batch, seq, A_HEADS, A_GROUPS, 2, HEAD_DIM)
    kv = [jnp.concatenate([a_kv[:, :, :, g, 0, :].reshape(batch, seq, A_WIDTH),
                           a_kv[:, :, :, g, 1, :].reshape(batch, seq, A_WIDTH)], axis=-1) for g in range(A_GROUPS)]
    kv0 = jnp.pad(kv[0], ((0, 0), (Q_BLOCK, 0), (0, 0)))
    kv1 = jnp.pad(kv[1], ((0, 0), (512, 0), (0, 0))).reshape(batch, -1, 4, 2 * A_WIDTH).transpose(0, 2, 1, 3)
    kv2 = jnp.pad(kv[2].astype(F32), ((0, 0), (A_MAXWIN, 16 * (A_SPAN2 - 136)), (0, 0)))
    kv2 = kv2.reshape(batch, -1, 16, 2 * A_WIDTH).transpose(0, 2, 1, 3)
    a_q = pb[:, :, PB_AQ:PB_BQ].reshape(batch, seq, A_HEADS, 4, HEAD_DIM)
    q = jnp.concatenate([a_q[:, :, :, g, :].reshape(batch, seq, A_WIDTH) for g in range(A_GROUPS)], axis=-1)
    q = q.reshape(batch, nblk, r8, 16, -1).transpose(0, 1, 3, 2, 4).reshape(batch, seq, -1)
    pair = 2 * HEAD_DIM
    pairs_per_part = A_WIDTH // pair
    q_spec = lambda g: pl.BlockSpec((1, Q_BLOCK, pair), lambda b, hp, i: (b, i, g * pairs_per_part + hp))
    k3 = lambda a: pl.BlockSpec((1, a.shape[1], pair), lambda b, hp, i: (b, 0, hp))
    v3 = lambda a: pl.BlockSpec((1, a.shape[1], pair), lambda b, hp, i: (b, 0, pairs_per_part + hp))
    k4 = lambda a: pl.BlockSpec((1, a.shape[1], a.shape[2], pair), lambda b, hp, i: (b, 0, 0, hp))
    v4 = lambda a: pl.BlockSpec((1, a.shape[1], a.shape[2], pair), lambda b, hp, i: (b, 0, 0, pairs_per_part + hp))
    in_specs = [q_spec(0), q_spec(1), q_spec(2), k3(kv0), v3(kv0), k4(kv1), v4(kv1), k4(kv2), v4(kv2)]
    for bias in biases:
        in_specs.append(pl.BlockSpec(bias.shape, lambda b, hp, i: (0, 0)))
    o = pl.pallas_call(
        _mixa2_kernel,
        grid=(batch, A_HEADS // 2, nblk),
        in_specs=in_specs,
        out_specs=pl.BlockSpec((1, Q_BLOCK, pair), lambda b, hp, i: (b, i, hp)),
        out_shape=jax.ShapeDtypeStruct((batch, seq, A_WIDTH), BF16),
        compiler_params=_params("parallel", "parallel", "arbitrary", vmem_mb=48),
        name="mixer_a",
    )(q, q, q, kv0, kv0, kv1, kv1, kv2, kv2, *biases)
    return o.reshape(batch, nblk, 16, r8, A_WIDTH).transpose(0, 1, 3, 2, 4).reshape(batch, seq, A_WIDTH)


def _mixb_kernel(qb_ref, kb_ref, vb_ref, iq_ref, ik_ref, misc_ref, tri_ref, o_ref, keys_ref, selb_ref, *, top_k):
    _, rows, ch = keys_ref.shape
    start = pl.program_id(1) * rows
    nch = (start + rows + ch - 1) // ch
    qpos = start + lax.broadcasted_iota(I32, (rows, 1), 0)
    w = misc_ref[0][:, 0:IDX_HEADS] * (IDX_DIM ** -0.5 * IDX_HEADS ** -0.5)
    iq = iq_ref[0]
    zero_col = jnp.zeros((rows, 1), F32)

    def score_chunk(c, carry):
        off = pl.multiple_of(c * ch, ch)
        ik = ik_ref[0, pl.ds(off, ch), 0:IDX_DIM]
        acc = jnp.zeros((rows, ch), F32)
        for h in range(IDX_HEADS):
            lg = _dot_nt(iq[:, h * IDX_DIM:(h + 1) * IDX_DIM], ik)
            acc = acc + jnp.maximum(lg, 0.0) * w[:, h:h + 1]
        acc = acc + 0.0
        bits = lax.bitcast_convert_type(acc, I32)
        key = bits ^ ((bits >> 31) & 0x7FFFFFFF)
        kpos = off + lax.broadcasted_iota(I32, (rows, ch), 1)
        keys_ref[c] = jnp.where(kpos <= qpos, key, INT_MIN)
        return carry

    lax.fori_loop(0, nch, score_chunk, 0)

    def count(pred):
        def chunk(c, cnt):
            kc = keys_ref[c]
            for j in range(ch // LANES):
                cnt = cnt + jnp.where(pred(kc[:, j * LANES:(j + 1) * LANES]), 1.0, 0.0)
            return cnt
        cnt = lax.fori_loop(0, nch, chunk, jnp.zeros((rows, LANES), F32))
        return jnp.sum(cnt, axis=-1, keepdims=True)

    kf = float(top_k)
    ans = jnp.where(count(lambda kc: kc >= 0) >= kf, 0, INT_MIN).astype(I32)

    def body(it, ans):
        cand = ans + jnp.left_shift(jnp.int32(1), 30 - it)
        return jnp.where(count(lambda kc: kc >= cand) >= kf, cand, ans)

    thr = lax.fori_loop(0, 31, body, ans)
    need = kf - count(lambda kc: kc > thr)

    tw = tri_ref.shape[0]

    def select_chunk(c, run):
        kc_all = keys_ref[c]
        for j in range(ch // tw):
            kc = kc_all[:, j * tw:(j + 1) * tw]
            eq = kc == thr
            pre = _dot(jnp.where(eq, 1.0, 0.0).astype(BF16), tri_ref[...])
            take = jnp.where(eq, jnp.where(pre + run <= need, 1.0, 0.0), 0.0)
            sel = jnp.where(kc > thr, 1.0, take)
            sel = jnp.where(kc != INT_MIN, sel, 0.0)
            selb_ref[c, :, j * tw:(j + 1) * tw] = jnp.where(sel > 0.5, 0.0, NEG_INF)
            run = run + pre[:, tw - 1:tw]
        return run

    def select_chunk_no_ties(c, carry):
        kc = keys_ref[c]
        selb_ref[c] = jnp.where(kc >= thr, jnp.where(kc != INT_MIN, 0.0, NEG_INF), NEG_INF)
        return carry

    excess = jnp.max(count(lambda kc: kc >= thr)) > kf
    lax.cond(excess,
             lambda: lax.fori_loop(0, nch, select_chunk, zero_col),
             lambda: lax.fori_loop(0, nch, select_chunk_no_ties, zero_col))

    heads = range(B_HEADS)
    lane = lax.broadcasted_iota(I32, (1, B_WIDTH), 1) // HEAD_DIM
    qs = (qb_ref[0].astype(F32) * HEAD_DIM ** -0.5).astype(BF16)
    qm = [jnp.where(lane == h, qs, jnp.zeros_like(qs)) for h in heads]

    def per_lane(cols):
        out = cols[B_HEADS - 1]
        for h in reversed(range(B_HEADS - 1)):
            out = jnp.where(lane <= h, cols[h], out)
        return out

    def attend_chunk(c, carry):
        off = pl.multiple_of(c * ch, ch)
        bias = selb_ref[c]
        kc = kb_ref[0, pl.ds(off, ch), :]
        vc = vb_ref[0, pl.ds(off, ch), :]
        ms, ls, acc = carry[:B_HEADS], carry[B_HEADS:2 * B_HEADS], carry[2 * B_HEADS]
        s = [_dot_nt(qm[h], kc) + bias for h in heads]
        m_new = [jnp.maximum(ms[h], jnp.max(s[h], axis=-1, keepdims=True)) for h in heads]
        m_safe = [jnp.where(m_new[h] == NEG_INF, 0.0, m_new[h]) for h in heads]
        alpha = [jnp.exp(ms[h] - m_safe[h]) for h in heads]
        p = [jnp.exp(s[h] - m_safe[h]) for h in heads]
        l_new = [alpha[h] * ls[h] + jnp.sum(p[h], axis=-1, keepdims=True) for h in heads]
        pv = [_dot(p[h].astype(BF16), vc) for h in heads]
        acc = per_lane(alpha) * acc + per_lane(pv)
        return tuple(m_new) + tuple(l_new) + (acc,)

    init = ((jnp.full((rows, 1), NEG_INF, F32),) * B_HEADS + (zero_col,) * B_HEADS
            + (jnp.zeros((rows, B_WIDTH), F32),))
    res = lax.fori_loop(0, nch, attend_chunk, init)
    o_ref[0] = (res[2 * B_HEADS] / per_lane(res[B_HEADS:2 * B_HEADS])).astype(o_ref.dtype)


def _mixer_b(pb, pf, tri, batch, seq):
    rows = Q_BLOCK
    top_k = min(TOPK_MAX, seq // 4)
    ch = min(512, seq)
    kern = functools.partial(_mixb_kernel, top_k=top_k)
    return pl.pallas_call(
        kern,
        grid=(batch, seq // rows),
        in_specs=[pl.BlockSpec((1, rows, B_WIDTH), lambda b, i: (b, i, PB_BQ // B_WIDTH)),
                  pl.BlockSpec((1, seq, B_WIDTH), lambda b, i: (b, 0, PB_BK // B_WIDTH)),
                  pl.BlockSpec((1, seq, B_WIDTH), lambda b, i: (b, 0, PB_BV // B_WIDTH)),
                  pl.BlockSpec((1, rows, 256), lambda b, i: (b, i, PB_IQ // 256)),
                  pl.BlockSpec((1, seq, LANES), lambda b, i: (b, 0, PB_IK // LANES)),
                  pl.BlockSpec((1, rows, LANES), lambda b, i: (b, i, PF_MISC // LANES)),
                  pl.BlockSpec(tri.shape, lambda b, i: (0, 0))],
        out_specs=pl.BlockSpec((1, rows, B_WIDTH), lambda b, i: (b, i, 0)),
        out_shape=jax.ShapeDtypeStruct((batch, seq, B_WIDTH), BF16),
        scratch_shapes=[pltpu.VMEM((seq // ch, rows, ch), I32), pltpu.VMEM((seq // ch, rows, ch), F32)],
        compiler_params=_params("parallel", "arbitrary", vmem_mb=48),
        name="mixer_b",
    )(pb, pb, pb, pb, pb, pf, tri)


def _head_consts():
    lane = np.arange(C_WIDTH)
    same = (lane[:, None] // HEAD_DIM == lane[None, :] // HEAD_DIM).astype(np.float32)
    return jnp.asarray(same)


def _rwkv1_kernel(cin_ref, prev_ref, vf_ref, mu_ref, vec_ref, lora_ref, vl_ref, bd_ref,
                  phi_ref, psi_ref, rp_ref, y0_ref, g_ref, bonus_ref, v_ref, *, rows, use_v_lora):
    i = pl.program_id(1)
    pc = cin_ref[0][:, :C_IN]
    prev = jnp.where(i > 0, prev_ref[0][SUBLANES - 1:SUBLANES, :C_IN], 0.0)
    row = lax.broadcasted_iota(I32, (rows, 1), 0)
    shifted = jnp.where(row == 0, prev, pltpu.roll(pc, 1, 0))
    pc = pc + (shifted - pc) * mu_ref[...]
    r = pc[:, 0:C_WIDTH]
    k = pc[:, C_WIDTH:2 * C_WIDTH]
    v = pc[:, 2 * C_WIDTH:3 * C_WIDTH]
    xl = pc[:, 3 * C_WIDTH:C_IN]
    lane = lax.broadcasted_iota(I32, xl.shape, 1)
    feat = jnp.where(lane < C_LORA_W, jnp.tanh(xl), jnp.where(lane < C_LORA_W + C_LORA_A, xl, _sigmoid(xl)))
    w0, a0, k_k, k_a, r_k = (vec_ref[n:n + 1, :] for n in range(5))
    bd = bd_ref[...]
    bd_sel = bd.astype(BF16)
    w_raw = -_softplus(-(w0 + _dot3(feat, lora_ref[0]))) - 0.5
    a = _sigmoid(a0 + _dot3(feat, lora_ref[1]))
    g_ref[0] = _dot3(feat, lora_ref[2])
    if use_v_lora:
        v0 = vec_ref[5:6, :]
        v = v + (vf_ref[0] - v) * _sigmoid(v0 + _dot3(_dot3(v, vl_ref[0]), vl_ref[1]))
    v_ref[0] = v
    kk = k * k_k
    kk = kk / jnp.maximum(jnp.sqrt(_dot_sel(bd_sel, kk * kk, left=False)), 1e-12)
    k = k * (1.0 + (a - 1.0) * k_a)
    logw = -jnp.exp(w_raw)
    bonus_ref[0] = _dot_sel(bd_sel, r * k * r_k, left=False) * v
    av = -kk
    bv = kk * a

    cc = C_CHUNK
    ri = lax.broadcasted_iota(I32, (cc, cc), 0)
    ci = lax.broadcasted_iota(I32, (cc, cc), 1)
    tril = jnp.where(ci <= ri, 1.0, 0.0).astype(BF16)
    strict = ci < ri
    incl = ci <= ri
    eye_c = jnp.where(ci == ri, 1.0, 0.0)
    lane_w = lax.broadcasted_iota(I32, (1, C_WIDTH), 1)
    ri2 = lax.broadcasted_iota(I32, (C_WIDTH, C_WIDTH), 0)
    ci2 = lax.broadcasted_iota(I32, (C_WIDTH, C_WIDTH), 1)
    eye_w = jnp.where(ri2 == ci2, 1.0, 0.0)
    chunks = range(rows // cc)
    heads = range(C_HEADS)
    pairs = [(c, h) for c in chunks for h in heads]
    mh = [jnp.where(lane_w // HEAD_DIM == h, 1.0, 0.0) for h in heads]
    sl = [slice(c * cc, (c + 1) * cc) for c in chunks]
    cs = [_dot_sel(tril, logw[sl[c]]) for c in chunks]
    cs_end = [cs[c][cc - 1:cc, :] for c in chunks]
    at = [av[sl[c]] * jnp.exp(cs[c] - logw[sl[c]]) for c in chunks]
    rt = [r[sl[c]] * jnp.exp(cs[c]) for c in chunks]
    inv = [jnp.exp(-cs[c]) for c in chunks]
    rhs = [jnp.concatenate([bv[sl[c]] * inv[c], k[sl[c]] * inv[c]], axis=0) for c in chunks]
    tail = [jnp.exp(cs_end[c] - cs[c]) for c in chunks]
    vc = [v[sl[c]] for c in chunks]
    vc_b = [vc[c].astype(BF16) for c in chunks]
    ath = {(c, h): at[c] * mh[h] for c, h in pairs}
    aa = {(c, h): _dot3(jnp.concatenate([ath[c, h], rt[c] * mh[h]], axis=0), rhs[c], nt=True) for c, h in pairs}
    a_ab = {p: jnp.where(strict, aa[p][:cc, :cc], 0.0) for p in pairs}
    a_ak = {p: jnp.where(strict, aa[p][:cc, cc:], 0.0).astype(BF16) for p in pairs}
    a_rb = {p: jnp.where(incl, aa[p][cc:, :cc], 0.0).astype(BF16) for p in pairs}
    a_rk = {p: jnp.where(incl, aa[p][cc:, cc:], 0.0).astype(BF16) for p in pairs}
    x = {p: eye_c + a_ab[p] for p in pairs}
    pw = a_ab
    for _ in range(int(math.log2(cc)) - 1):
        pw = {p: _dot3(pw[p], pw[p]) for p in pairs}
        x = {p: x[p] + _dot3(x[p], pw[p]) for p in pairs}
    akv = {(c, h): _dot(a_ak[c, h], vc_b[c]) for c, h in pairs}
    ap_h = {p: _dot3(x[p], ath[p]) for p in pairs}
    w2_h = {(c, h): _dot3(x[c, h], akv[c, h]) * mh[h] for c, h in pairs}
    rp_h = {p: _dot(a_rb[p], ap_h[p].astype(BF16)) for p in pairs}
    y0_h = {(c, h): (_dot(a_rb[c, h], w2_h[c, h].astype(BF16)) + _dot(a_rk[c, h], vc_b[c])) * mh[h] for c, h in pairs}
    for c in chunks:
        ap = sum(ap_h[c, h] for h in heads)
        w2 = sum(w2_h[c, h] for h in heads)
        rp_ref[0, sl[c], :] = rt[c] + sum(rp_h[c, h] for h in heads)
        y0_ref[0, sl[c], :] = sum(y0_h[c, h] for h in heads)
        bh = bv[sl[c]] * tail[c]
        kh = k[sl[c]] * tail[c]
        phi_ref[0, c] = eye_w * jnp.exp(cs_end[c]) + _dot3(bh.T, ap) * bd
        psi_ref[0, c] = _dot(jnp.concatenate([bh, kh], axis=0).T.astype(BF16),
                             jnp.concatenate([w2, vc[c]], axis=0).astype(BF16)) * bd


def _rwkv2_kernel(phi_ref, psi_ref, rp_ref, y0_ref, g_ref, bonus_ref, gn_ref, bd_ref, o_ref, s_ref):
    @pl.when(pl.program_id(1) == 0)
    def _():
        s_ref[...] = jnp.zeros_like(s_ref)

    states = [s_ref[...]]
    for c in range(phi_ref.shape[1]):
        states.append(_dot3(phi_ref[0, c], states[c]) + psi_ref[0, c])
    s_ref[...] = states[-1]
    cc = C_CHUNK
    y = jnp.concatenate([_dot3(rp_ref[0, c * cc:(c + 1) * cc, :], states[c]) for c in range(phi_ref.shape[1])],
                        axis=0) + y0_ref[0]
    bd_sel = bd_ref[...].astype(BF16)
    mu = _dot_sel(bd_sel, y, left=False) * (1.0 / HEAD_DIM)
    d = y - mu
    var = _dot_sel(bd_sel, d * d, left=False) * (1.0 / HEAD_DIM)
    yn = d * lax.rsqrt(var + C_GN_EPS) * gn_ref[0:1, :] + gn_ref[1:2, :]
    o_ref[0] = ((yn + bonus_ref[0]) * g_ref[0]).astype(o_ref.dtype)


def _mixer_c(pf, v_first, mu, vec, lora, vl, gn, bd, batch, seq, use_v_lora):
    rows = min(256, seq)
    nblk = seq // rows
    nch = seq // C_CHUNK
    cpb = rows // C_CHUNK
    kern = functools.partial(_rwkv1_kernel, rows=rows, use_v_lora=use_v_lora)
    full2 = lambda a: pl.BlockSpec(a.shape, lambda b, i: (0,) * a.ndim)
    seq_spec = pl.BlockSpec((1, rows, C_WIDTH), lambda b, i: (b, i, 0))
    mat_spec = pl.BlockSpec((1, cpb, C_WIDTH, C_WIDTH), lambda b, i: (b, i, 0, 0))
    seq_shape = jax.ShapeDtypeStruct((batch, seq, C_WIDTH), F32)
    mat_shape = jax.ShapeDtypeStruct((batch, nch, C_WIDTH, C_WIDTH), F32)
    prev_blk = rows // SUBLANES
    phi, psi, rp, y0, g, bonus, v = pl.pallas_call(
        kern,
        grid=(batch, nblk),
        in_specs=[pl.BlockSpec((1, rows, 1024), lambda b, i: (b, i, PF_CIN // 1024)),
                  pl.BlockSpec((1, SUBLANES, 1024), lambda b, i: (b, jnp.maximum(i * prev_blk - 1, 0), PF_CIN // 1024)),
                  seq_spec, full2(mu), full2(vec), full2(lora), full2(vl), full2(bd)],
        out_specs=[mat_spec, mat_spec, seq_spec, seq_spec, seq_spec, seq_spec, seq_spec],
        out_shape=[mat_shape, mat_shape, seq_shape, seq_shape, seq_shape, seq_shape, seq_shape],
        compiler_params=_params("parallel", "parallel", vmem_mb=48),
        name="rwkv_chunks",
    )(pf, pf, v_first, mu, vec, lora, vl, bd)
    scan_chunks = min(4, nch)
    cseq = pl.BlockSpec((1, scan_chunks * C_CHUNK, C_WIDTH), lambda b, c: (b, c, 0))
    cmat = pl.BlockSpec((1, scan_chunks, C_WIDTH, C_WIDTH), lambda b, c: (b, c, 0, 0))
    o = pl.pallas_call(
        _rwkv2_kernel,
        grid=(batch, nch // scan_chunks),
        in_specs=[cmat, cmat, cseq, cseq, cseq, cseq,
                  pl.BlockSpec(gn.shape, lambda b, c: (0, 0)), pl.BlockSpec(bd.shape, lambda b, c: (0, 0))],
        out_specs=cseq,
        out_shape=jax.ShapeDtypeStruct((batch, seq, C_WIDTH), BF16),
        scratch_shapes=[pltpu.VMEM((C_WIDTH, C_WIDTH), F32)],
        compiler_params=_params("parallel", "arbitrary"),
        name="rwkv_scan",
    )(phi, psi, rp, y0, g, bonus, gn, bd)
    return o, v


def _ssd_consts():
    expand = np.zeros((LANES, D_INNER), np.float32)
    for h in range(D_HEADS):
        expand[MISC_DT + h, h * HEAD_DIM:(h + 1) * HEAD_DIM] = 1.0
    return jnp.asarray(expand)


def _ssd_kernel(xbc_ref, z_ref, misc_ref, conv_ref, vec_ref, hp_ref, expand_ref, bd_ref, o_ref, st_ref, prev_ref):
    @pl.when(pl.program_id(1) == 0)
    def _():
        st_ref[...] = jnp.zeros_like(st_ref)
        prev_ref[...] = jnp.zeros_like(prev_ref)

    q = D_CHUNK
    x_raw = xbc_ref[0]
    ext = jnp.concatenate([prev_ref[...], x_raw], axis=0)
    conv = jnp.zeros((q, D_XBC), F32)
    for t in range(D_CONV):
        lo = SUBLANES - (D_CONV - 1) + t
        conv = conv + ext[lo:lo + q, :] * conv_ref[t:t + 1, :]
    prev_ref[...] = x_raw[q - SUBLANES:, :]
    xbc = _silu(conv + conv_ref[D_CONV:D_CONV + 1, :])
    xs = xbc[:, :D_INNER]
    bm = xbc[:, D_INNER:D_INNER + D_GROUPS * D_STATE]
    cm = xbc[:, D_INNER + D_GROUPS * D_STATE:]

    dt_col = _softplus(misc_ref[0] + hp_ref[0:1, :])
    a_col = dt_col * hp_ref[1:2, :]
    ri = lax.broadcasted_iota(I32, (q, q), 0)
    ci = lax.broadcasted_iota(I32, (q, q), 1)
    causal = ci <= ri
    tril = jnp.where(causal, 1.0, 0.0).astype(BF16)
    acs_col = _dot_sel(tril, a_col)
    acs_row = acs_col.T
    expand = expand_ref[...].astype(BF16)
    acs = _dot_sel(expand, acs_col, left=False)
    dt = _dot_sel(expand, dt_col, left=False)
    acs_end = acs[q - 1:q, :]
    xdt = xs * dt

    lane = lax.broadcasted_iota(I32, (1, LANES), 1)
    lane_w = lax.broadcasted_iota(I32, (1, D_INNER), 1)
    left = lane < D_STATE
    bm_sw = pltpu.roll(bm, D_STATE, 1)
    cm_sw = pltpu.roll(cm, D_STATE, 1)
    b_exp = jnp.concatenate([jnp.where(left, bm, bm_sw), jnp.where(left, bm_sw, bm)], axis=1)
    c_exp = jnp.concatenate([jnp.where(left, cm, cm_sw), jnp.where(left, cm_sw, cm)], axis=1)
    cb = [_dot3(jnp.where(left == (g == 0), cm, 0.0), bm, nt=True) for g in range(D_GROUPS)]

    scores = []
    for h in range(D_HEADS):
        col = acs_col[:, MISC_DT + h:MISC_DT + h + 1]
        rw = acs_row[MISC_DT + h:MISC_DT + h + 1, :]
        decay = jnp.exp(jnp.where(causal, col - rw, NEG_INF))
        scores.append(cb[h // (D_HEADS // D_GROUPS)] * decay)
    y_h = [_dot3(scores[h], xdt) for h in range(D_HEADS)]
    y = jnp.zeros((q, D_INNER), F32)
    for h in range(D_HEADS):
        y = y + jnp.where(lane_w // HEAD_DIM == h, y_h[h], 0.0)
    st = st_ref[...]
    y = y + _dot3(c_exp, st) * jnp.exp(acs)
    st_ref[...] = st * jnp.exp(acs_end) + _dot3(b_exp.T, xdt * jnp.exp(acs_end - acs)) * bd_ref[...]
    y = y + xs * vec_ref[0:1, :]
    y = y * _silu(z_ref[0])
    half = D_INNER // D_GROUPS
    outs = []
    for g in range(D_GROUPS):
        yg = y[:, g * half:(g + 1) * half]
        outs.append(yg * lax.rsqrt(jnp.mean(yg * yg, axis=-1, keepdims=True) + D_NORM_EPS))
    o_ref[0] = (jnp.concatenate(outs, axis=1) * vec_ref[1:2, :]).astype(o_ref.dtype)


def _mixer_d(pf, conv, vec, hp, expand, bd, batch, seq):
    q = D_CHUNK
    full = lambda a: pl.BlockSpec(a.shape, lambda b, i: (0, 0))
    return pl.pallas_call(
        _ssd_kernel,
        grid=(batch, seq // q),
        in_specs=[pl.BlockSpec((1, q, D_XBC), lambda b, i: (b, i, PF_XBC // D_XBC)),
                  pl.BlockSpec((1, q, D_INNER), lambda b, i: (b, i, PF_Z // D_INNER)),
                  pl.BlockSpec((1, q, LANES), lambda b, i: (b, i, PF_MISC // LANES)),
                  full(conv), full(vec), full(hp), full(expand), full(bd)],
        out_specs=pl.BlockSpec((1, q, D_INNER), lambda b, i: (b, i, 0)),
        out_shape=jax.ShapeDtypeStruct((batch, seq, D_INNER), BF16),
        scratch_shapes=[pltpu.VMEM((D_INNER, D_INNER), F32), pltpu.VMEM((SUBLANES, D_XBC), F32)],
        compiler_params=_params("parallel", "arbitrary"),
        name="ssd",
    )(pf, pf, pf, conv, vec, hp, expand, bd)


def _merge_kernel(x_ref, gates_ref, oa_ref, ob_ref, oc_ref, od_ref, wbr_ref, wout_ref, ln_ref, wr_ref, br_ref,
                  x1_ref, route_ref):
    acc = None
    for n, o_ref in enumerate((oa_ref, ob_ref, oc_ref, od_ref)):
        term = _sigmoid(gates_ref[:, n * D_MODEL:(n + 1) * D_MODEL]) * _dot(o_ref[...], wbr_ref[n])
        acc = term if acc is None else acc + term
    h = DEEPNORM_ALPHA * x_ref[...] + _dot(acc.astype(BF16), wout_ref[...])
    x1 = _layer_norm(h, ln_ref[0:1, :], ln_ref[1:2, :])
    x1_ref[...] = x1

    logits = _dot3(x1, wr_ref[...]) + br_ref[...]
    lane = lax.broadcasted_iota(I32, logits.shape, 1)
    big = jnp.int32(LANES)
    gl = jnp.where(lane < N_EXPERT_GROUPS, logits, NEG_INF)
    gm = jnp.max(gl, axis=-1, keepdims=True)
    pg_top = 1.0 / jnp.sum(jnp.exp(gl - gm), axis=-1, keepdims=True)
    g_sel = jnp.min(jnp.where(gl == gm, lane, big), axis=-1, keepdims=True)
    off = N_EXPERT_GROUPS + g_sel * EXPERTS_PER_GROUP
    el = jnp.where((lane >= off) & (lane < off + EXPERTS_PER_GROUP), logits, NEG_INF)
    em = jnp.max(el, axis=-1, keepdims=True)
    es = jnp.sum(jnp.exp(el - em), axis=-1, keepdims=True)
    idx1 = jnp.min(jnp.where(el == em, lane, big), axis=-1, keepdims=True)
    el2 = jnp.where(lane == idx1, NEG_INF, el)
    em2 = jnp.max(el2, axis=-1, keepdims=True)
    idx2 = jnp.min(jnp.where(el2 == em2, lane, big), axis=-1, keepdims=True)
    p1 = 1.0 / es
    p2 = jnp.exp(em2 - em) / es
    gate1 = pg_top * p1 / (p1 + p2)
    gate2 = pg_top * p2 / (p1 + p2)
    e1 = (idx1 - N_EXPERT_GROUPS).astype(F32)
    e2 = (idx2 - N_EXPERT_GROUPS).astype(F32)
    route_ref[...] = jnp.where(lane == 0, e1, jnp.where(lane == 1, e2, jnp.where(lane == 2, gate1,
                               jnp.where(lane == 3, gate2, 0.0))))


def _merge(x2d, pf, oa, ob, oc, od, wbr, wout, ln, wr, br, tm):
    m = x2d.shape[0]
    row = lambda w: pl.BlockSpec((tm, w), lambda i: (i, 0))
    full = lambda a: pl.BlockSpec(a.shape, lambda i: (0,) * a.ndim)
    return pl.pallas_call(
        _merge_kernel,
        grid=(m // tm,),
        in_specs=[row(D_MODEL), pl.BlockSpec((tm, N_BRANCH * D_MODEL), lambda i: (i, PF_GATES)),
                  row(A_WIDTH), row(B_WIDTH), row(C_WIDTH), row(D_INNER),
                  full(wbr), full(wout), full(ln), full(wr), full(br)],
        out_specs=[row(D_MODEL), row(LANES)],
        out_shape=[jax.ShapeDtypeStruct((m, D_MODEL), F32), jax.ShapeDtypeStruct((m, LANES), F32)],
        compiler_params=_params("parallel", vmem_mb=48),
        name="merge_route",
    )(x2d, pf, oa, ob, oc, od, wbr, wout, ln, wr, br)


def _ffn_kernel(be_ref, nv_ref, xs_ref, wg_ref, wu_ref, wd_ref, o_ref):
    i = pl.program_id(0)

    @pl.when(i < nv_ref[0])
    def _():
        xb = xs_ref[...]
        hg = _dot(xb, wg_ref[0, 0].astype(BF16))
        hu = _dot(xb, wu_ref[0, 0].astype(BF16))
        o_ref[...] = _dot((_silu(hg) * hu).astype(BF16), wd_ref[0, 0].astype(BF16))

    @pl.when(i >= nv_ref[0])
    def _():
        o_ref[...] = jnp.zeros_like(o_ref)


def _ffn(blk_expert, n_valid, xs, e_gate, e_up, e_down, layer):
    cap = xs.shape[0]
    wspec = lambda a: pl.BlockSpec((1, 1) + a.shape[2:], lambda i, be, nv: (layer, be[i], 0, 0))
    return pl.pallas_call(
        _ffn_kernel,
        grid_spec=pltpu.PrefetchScalarGridSpec(
            num_scalar_prefetch=2,
            grid=(cap // MOE_ROWS,),
            in_specs=[pl.BlockSpec((MOE_ROWS, D_MODEL), lambda i, be, nv: (i, 0)),
                      wspec(e_gate), wspec(e_up), wspec(e_down)],
            out_specs=pl.BlockSpec((MOE_ROWS, D_MODEL), lambda i, be, nv: (i, 0)),
        ),
        out_shape=jax.ShapeDtypeStruct((cap, D_MODEL), F32),
        compiler_params=_params("arbitrary", vmem_mb=48),
        name="expert_ffn",
    )(blk_expert, n_valid, xs, e_gate, e_up, e_down)


def _combine_kernel(dest_ref, x_ref, route_ref, ln_ref, yb_ref, o_ref, ybuf, sem):
    i = pl.program_id(0)
    tm = x_ref.shape[0]

    def gather(tile, slot):
        base = tile * (TOP_K_EXPERTS * tm)

        def body(r, carry):
            for s in range(TOP_K_EXPERTS):
                row = dest_ref[base + TOP_K_EXPERTS * r + s]
                pltpu.make_async_copy(yb_ref.at[pl.ds(row, 1)], ybuf.at[slot, pl.ds(s * tm + r, 1)],
                                      sem.at[slot]).start()
            return carry

        lax.fori_loop(0, tm, body, 0, unroll=8)

    @pl.when(i == 0)
    def _():
        gather(0, 0)

    @pl.when(i + 1 < pl.num_programs(0))
    def _():
        gather(i + 1, (i + 1) % 2)

    slot = i % 2
    pltpu.make_async_copy(yb_ref.at[pl.ds(0, TOP_K_EXPERTS * tm)], ybuf.at[slot], sem.at[slot]).wait()
    g0 = route_ref[:, 2:3]
    g1 = route_ref[:, 3:4]
    h = DEEPNORM_ALPHA * x_ref[...] + (ybuf[slot, 0:tm, :] * g0 + ybuf[slot, tm:2 * tm, :] * g1)
    o_ref[...] = _layer_norm(h, ln_ref[0:1, :], ln_ref[1:2, :])


def _combine(x1, yb, dest, route, ln, tm):
    m = x1.shape[0]
    return pl.pallas_call(
        _combine_kernel,
        grid_spec=pltpu.PrefetchScalarGridSpec(
            num_scalar_prefetch=1,
            grid=(m // tm,),
            in_specs=[pl.BlockSpec((tm, D_MODEL), lambda i, d: (i, 0)),
                      pl.BlockSpec((tm, LANES), lambda i, d: (i, 0)),
                      pl.BlockSpec(ln.shape, lambda i, d: (0, 0)),
                      pl.BlockSpec(memory_space=pl.ANY)],
            out_specs=pl.BlockSpec((tm, D_MODEL), lambda i, d: (i, 0)),
            scratch_shapes=[pltpu.VMEM((2, TOP_K_EXPERTS * tm, D_MODEL), F32), pltpu.SemaphoreType.DMA((2,))],
        ),
        out_shape=jax.ShapeDtypeStruct((m, D_MODEL), F32),
        compiler_params=_params("arbitrary"),
        name="combine",
    )(dest, x1, route, ln, yb)


def _dispatch_tables(route, m):
    flat_e = route[:, 0:TOP_K_EXPERTS].astype(I32).reshape(-1)
    n_assign = m * TOP_K_EXPERTS
    onehot = (flat_e[:, None] == jnp.arange(N_EXPERTS, dtype=I32)[None, :]).astype(I32)
    csum = jnp.cumsum(onehot, axis=0)
    rank = jnp.sum(csum * onehot, axis=1) - 1
    counts = csum[-1]
    padded = (counts + MOE_ROWS - 1) // MOE_ROWS * MOE_ROWS
    pad_end = jnp.cumsum(padded)
    pad_start = pad_end - padded
    dest = pad_start[flat_e] + rank
    cap = (n_assign + N_EXPERTS * (MOE_ROWS - 1) + MOE_ROWS - 1) // MOE_ROWS * MOE_ROWS
    n_blocks = cap // MOE_ROWS
    blk_start = jnp.arange(n_blocks, dtype=I32) * MOE_ROWS
    blk_expert = jnp.minimum(jnp.sum((pad_end[None, :] <= blk_start[:, None]).astype(I32), axis=1), N_EXPERTS - 1)
    buf_tok = (jnp.arange(cap, dtype=I32) % m).at[dest].set(jnp.arange(n_assign, dtype=I32) // TOP_K_EXPERTS)
    n_valid = (pad_end[-1] // MOE_ROWS).astype(I32).reshape(1)
    return dest, buf_tok, blk_expert, n_valid


def _row_pad(a, rows):
    return jnp.pad(a, ((0, rows - a.shape[0]), (0, 0)))


def kernel(x, w_in, c_mu, c_w0, c_w2, c_a0, c_a2, c_g2, c_kk, c_ka, c_rk, c_gn_w, c_gn_b, c_v0, c_v1, c_v2,
           d_conv_w, d_conv_b, d_dt_bias, d_a_log, d_skip, d_norm_w, w_branch, w_out, ln1_g, ln1_b,
           r_group, r_group_b, r_expert, r_expert_b, e_gate, e_up, e_down, ln2_g, ln2_b):
    batch, seq, _ = x.shape
    m = batch * seq
    biases = _mixa_class_bias()
    tw = min(256, seq)
    tri = jnp.asarray(np.triu(np.ones((tw, tw), np.float32))).astype(BF16)
    bd = _head_consts()
    expand = _ssd_consts()
    x2d = x.reshape(m, D_MODEL)
    v_first = jnp.zeros((batch, seq, C_WIDTH), F32)
    tm_proj = min(2048, m)
    tm_tok = min(512, m)
    for l in range(DEPTH):
        wb, wf = _proj_weights(w_in[l])
        pb = _proj(x2d, wb, BF16, tm_proj, PROJ_TN).reshape(batch, seq, PB_WIDTH)
        pf = _proj(x2d, wf, F32, tm_proj, PROJ_TN).reshape(batch, seq, PF_WIDTH)

        o_a = _mixer_a2(pb, biases, batch, seq)
        o_b = _mixer_b(pb, pf, tri, batch, seq)

        use_v_lora = l > 0
        vec_rows = [c_w0[l], c_a0[l], c_kk[l], c_ka[l], c_rk[l].reshape(-1)]
        vec_rows.append(c_v0[l - 1] if use_v_lora else jnp.zeros((C_WIDTH,), F32))
        vec = _row_pad(jnp.stack(vec_rows), SUBLANES)
        lora = jnp.stack([
            jnp.pad(c_w2[l], ((0, LANES - C_LORA_W), (0, 0))),
            jnp.pad(c_a2[l], ((C_LORA_W, LANES - C_LORA_W - C_LORA_A), (0, 0))),
            jnp.pad(c_g2[l], ((C_LORA_W + C_LORA_A, 0), (0, 0)))])
        if use_v_lora:
            vl = jnp.stack([jnp.pad(c_v1[l - 1], ((0, 0), (0, C_WIDTH - C_LORA_V))),
                            jnp.pad(c_v2[l - 1], ((0, C_WIDTH - C_LORA_V), (0, 0)))])
        else:
            vl = jnp.zeros((2, C_WIDTH, C_WIDTH), F32)
        gn = _row_pad(jnp.stack([c_gn_w[l], c_gn_b[l]]), SUBLANES)
        o_c, v_c = _mixer_c(pf, v_first, c_mu[l].reshape(1, C_IN), vec, lora, vl, gn, bd, batch, seq, use_v_lora)
        if l == 0:
            v_first = v_c

        conv = _row_pad(jnp.concatenate([d_conv_w[l], d_conv_b[l][None, :]], axis=0), SUBLANES)
        dvec = _row_pad(jnp.stack([jnp.repeat(d_skip[l], HEAD_DIM), d_norm_w[l]]), SUBLANES)
        place = lambda a: jnp.pad(a, (MISC_DT, LANES - MISC_DT - D_HEADS))
        hp = _row_pad(jnp.stack([place(d_dt_bias[l]), place(-jnp.exp(d_a_log[l]))]), SUBLANES)
        o_d = _mixer_d(pf, conv, dvec, hp, expand, bd, batch, seq)

        wr = jnp.pad(jnp.concatenate([r_group[l], r_expert[l]], axis=1),
                     ((0, 0), (0, LANES - N_EXPERT_GROUPS - N_EXPERTS)))
        br = jnp.pad(jnp.concatenate([r_group_b[l], r_expert_b[l]]), (0, LANES - N_EXPERT_GROUPS - N_EXPERTS))
        ln1 = _row_pad(jnp.stack([ln1_g[l], ln1_b[l]]), SUBLANES)
        x1, route = _merge(x2d, pf.reshape(m, PF_WIDTH), o_a.reshape(m, -1), o_b.reshape(m, -1),
                           o_c.reshape(m, -1), o_d.reshape(m, -1), w_branch[l].astype(BF16),
                           w_out[l].astype(BF16), ln1, wr, br.reshape(1, LANES), tm_tok)

        dest, buf_tok, blk_expert, n_valid = _dispatch_tables(route, m)
        xs = jnp.take(x1.astype(BF16), buf_tok, axis=0)
        yb = _ffn(blk_expert, n_valid, xs, e_gate, e_up, e_down, l)
        ln2 = _row_pad(jnp.stack([ln2_g[l], ln2_b[l]]), SUBLANES)
        x2d = _combine(x1, yb, dest, route, ln2, min(256, m))
    return x2d.reshape(batch, seq, D_MODEL)
```

```python
import functools
import math

import jax
import jax.numpy as jnp
import numpy as np
from jax import lax
from jax.experimental import pallas as pl
from jax.experimental.pallas import tpu as pltpu

F32 = jnp.float32
BF16 = jnp.bfloat16
I32 = jnp.int32
NEG_INF = float("-inf")
INT_MIN = -(2 ** 31)

LANES = 128
SUBLANES = 8

D_MODEL = 1024
DEPTH = 2
HEAD_DIM = 64
Q_BLOCK = 128

A_HEADS = 4
A_PATTERNS = ((128, 1), (512, 4), (2048, 16))
A_GROUPS = len(A_PATTERNS)
A_WIDTH = A_HEADS * HEAD_DIM
A_MAXWIN = max(w for w, _ in A_PATTERNS)

B_HEADS = 4
B_WIDTH = B_HEADS * HEAD_DIM
IDX_HEADS = 4
IDX_DIM = 64
TOPK_MAX = 256

C_HEADS = 4
C_WIDTH = C_HEADS * HEAD_DIM
C_LORA_W = 32
C_LORA_A = 32
C_LORA_G = 64
C_LORA_V = 16
C_IN = 3 * C_WIDTH + C_LORA_W + C_LORA_A + C_LORA_G
C_GN_EPS = 64e-5
C_CHUNK = 64

D_HEADS = 4
D_INNER = D_HEADS * HEAD_DIM
D_GROUPS = 2
D_STATE = 64
D_CONV = 4
D_CHUNK = 128
D_XBC = D_INNER + 2 * D_GROUPS * D_STATE
D_NORM_EPS = 1e-5

N_BRANCH = 4
N_EXPERT_GROUPS = 4
EXPERTS_PER_GROUP = 8
N_EXPERTS = N_EXPERT_GROUPS * EXPERTS_PER_GROUP
TOP_K_EXPERTS = 2
D_EXPERT = 512
MOE_ROWS = 512

LN_EPS = 1e-5
DEEPNORM_ALPHA = (2 * DEPTH) ** 0.25

PB_AKV = 0
PB_AQ = PB_AKV + A_GROUPS * 2 * A_WIDTH
PB_BQ = PB_AQ + A_GROUPS * A_WIDTH
PB_BK = PB_BQ + B_WIDTH
PB_BV = PB_BK + B_WIDTH
PB_IQ = PB_BV + B_WIDTH
PB_IK = PB_IQ + IDX_HEADS * IDX_DIM
PB_WIDTH = PB_IK + 512
PF_GATES = 0
PF_CIN = PF_GATES + N_BRANCH * D_MODEL
PF_MISC = PF_CIN + C_IN
PF_XBC = PF_CIN + 1024
PF_Z = PF_XBC + D_XBC
PROJ_TN = 768
PF_WIDTH = -(-(PF_Z + D_INNER) // PROJ_TN) * PROJ_TN
MISC_DT = 4


def _dot(a, b, precision=None):
    return jnp.dot(a, b, preferred_element_type=F32, precision=precision)


def _dot_nt(a, b, precision=None):
    return lax.dot_general(a, b, (((1,), (1,)), ((), ())), preferred_element_type=F32, precision=precision)


def _split2(a):
    hi = a.astype(BF16)
    return hi, (a - hi.astype(F32)).astype(BF16)


def _dot3(a, b, nt=False):
    mm = _dot_nt if nt else _dot
    ah, al = _split2(a)
    bh, bl = _split2(b)
    return mm(ah, bh) + (mm(ah, bl) + mm(al, bh))


def _dot_sel(sel, x, left=True):
    hi = x.astype(BF16)
    r1 = x - hi.astype(F32)
    mid = r1.astype(BF16)
    lo = (r1 - mid.astype(F32)).astype(BF16)
    if left:
        return _dot(sel, hi) + (_dot(sel, mid) + _dot(sel, lo))
    return _dot(hi, sel) + (_dot(mid, sel) + _dot(lo, sel))


def _sigmoid(x):
    return 1.0 / (1.0 + jnp.exp(-x))


def _softplus(x):
    return jnp.maximum(x, 0.0) + jnp.log1p(jnp.exp(-jnp.abs(x)))


def _silu(x):
    return x * _sigmoid(x)


def _layer_norm(h, g, b):
    mu = jnp.mean(h, axis=-1, keepdims=True)
    d = h - mu
    var = jnp.mean(d * d, axis=-1, keepdims=True)
    return d * lax.rsqrt(var + LN_EPS) * g + b


def _params(*sem, vmem_mb=None):
    kw = {}
    if vmem_mb is not None:
        kw["vmem_limit_bytes"] = vmem_mb * 1024 * 1024
    return pltpu.CompilerParams(dimension_semantics=sem, **kw)


def _proj_kernel(x_ref, w_ref, o_ref):
    o_ref[...] = _dot(x_ref[...].astype(BF16), w_ref[...]).astype(o_ref.dtype)


def _proj(x2d, w, out_dtype, tm, tn):
    m, k = x2d.shape
    n = w.shape[1]
    return pl.pallas_call(
        _proj_kernel,
        grid=(m // tm, n // tn),
        in_specs=[pl.BlockSpec((tm, k), lambda i, j: (i, 0)),
                  pl.BlockSpec((k, tn), lambda i, j: (0, j))],
        out_specs=pl.BlockSpec((tm, tn), lambda i, j: (i, j)),
        out_shape=jax.ShapeDtypeStruct((m, n), out_dtype),
        compiler_params=_params("parallel", "arbitrary", vmem_mb=48),
        name="in_proj",
    )(x2d, w)


def _proj_weights(w):
    k = w.shape[0]
    off = 0
    seg = {}
    for name, size in (("a_qkv", 3 * A_GROUPS * A_WIDTH), ("b_qkv", 3 * B_WIDTH), ("b_idx_q", IDX_HEADS * IDX_DIM),
                       ("b_idx_k", IDX_DIM), ("b_idx_w", IDX_HEADS), ("c_in", C_IN), ("d_z", D_INNER),
                       ("d_xbc", D_XBC), ("d_dt", D_HEADS), ("gates", N_BRANCH * D_MODEL)):
        seg[name] = w[:, off:off + size]
        off += size
    a = seg["a_qkv"].reshape(k, 3, A_GROUPS, A_HEADS, HEAD_DIM)
    a_kv = jnp.transpose(a[:, 1:3], (0, 2, 1, 3, 4)).reshape(k, -1)
    a_q = a[:, 0].reshape(k, -1)
    zeros = lambda n: jnp.zeros((k, n), w.dtype)
    wb = jnp.concatenate([a_kv, a_q, seg["b_qkv"], seg["b_idx_q"], seg["b_idx_k"],
                          zeros(PB_WIDTH - PB_IK - IDX_DIM)], axis=1)
    misc = jnp.concatenate([seg["b_idx_w"], seg["d_dt"], zeros(LANES - IDX_HEADS - D_HEADS)], axis=1)
    wf = jnp.concatenate([seg["gates"], seg["c_in"], misc, seg["d_xbc"], seg["d_z"]], axis=1)
    wf = jnp.concatenate([wf, zeros(PF_WIDTH - wf.shape[1])], axis=1)
    assert wb.shape[1] == PB_WIDTH and PB_WIDTH % PROJ_TN == 0
    return wb.astype(BF16), wf.astype(BF16)


A_ROWS2 = Q_BLOCK // 16
A_SPAN1 = 512 // 4 + Q_BLOCK // 4
A_SPAN2 = 144


def _mixa_class_bias():
    x = (np.arange(Q_BLOCK) % A_ROWS2) * 16 + np.arange(Q_BLOCK) // A_ROWS2
    c0 = np.arange(2 * Q_BLOCK)[None, :]
    b0 = np.where((c0 >= x[:, None]) & (c0 <= x[:, None] + Q_BLOCK), 0.0, -np.inf)
    u, a16 = np.arange(32) // A_ROWS2, np.arange(32) % A_ROWS2
    a4 = 4 * a16 + u
    c1 = np.arange(A_SPAN1)[None, :]
    b1 = np.where((c1 >= a4[:, None]) & (c1 <= a4[:, None] + Q_BLOCK), 0.0, -np.inf)
    a2 = np.arange(A_ROWS2)
    c2 = np.arange(A_SPAN2)[None, :]
    b2 = np.where((c2 >= a2[:, None]) & (c2 <= a2[:, None] + Q_BLOCK), 0.0, -np.inf)
    return [jnp.asarray(b.astype(np.float32)) for b in (b0, b1, b2)]


def _mixa2_kernel(q0_ref, q1_ref, q2_ref, k0p_ref, k0_ref, v0p_ref, v0_ref, k1_ref, v1_ref, k2_ref, v2_ref,
                  b0_ref, b1_ref, b2_ref, o_ref):
    pid = pl.program_id(2)
    start = pid * Q_BLOCK
    r8 = A_ROWS2
    scale = HEAD_DIM ** -0.5
    head_of_lane = lax.broadcasted_iota(I32, (1, 2 * HEAD_DIM), 1) // HEAD_DIM
    qf = [q_ref[0].astype(F32) * scale for q_ref in (q0_ref, q1_ref, q2_ref)]

    def masked_bias(b_ref, first_valid):
        col = lax.broadcasted_iota(I32, b_ref.shape, 1)
        return jnp.where(col >= first_valid, b_ref[...], NEG_INF)

    classes = [(qf[0], jnp.concatenate([k0p_ref[0], k0_ref[0]], axis=0),
                jnp.concatenate([v0p_ref[0], v0_ref[0]], axis=0), masked_bias(b0_ref, Q_BLOCK - start))]
    w1 = pl.ds(pl.multiple_of(pid * (Q_BLOCK // 4), Q_BLOCK // 4), A_SPAN1)
    bias1 = masked_bias(b1_ref, Q_BLOCK - pid * (Q_BLOCK // 4))
    for r4 in range(4):
        q1 = jnp.concatenate([qf[1][(r4 + 4 * u) * r8:(r4 + 4 * u + 1) * r8] for u in range(4)], axis=0)
        classes.append((q1, k1_ref[0, r4, w1, :], v1_ref[0, r4, w1, :], bias1))
    w2 = pl.ds(pl.multiple_of(pid * r8, r8), A_SPAN2)
    bias2 = masked_bias(b2_ref, Q_BLOCK - pid * r8)
    for r16 in range(16):
        classes.append((qf[2][r16 * r8:(r16 + 1) * r8], k2_ref[0, r16, w2, :].astype(BF16),
                        v2_ref[0, r16, w2, :].astype(BF16), bias2))

    def rows_of(vals):
        g1 = jnp.concatenate([vals[1 + r16 % 4][(r16 // 4) * r8:(r16 // 4 + 1) * r8] for r16 in range(16)], axis=0)
        return [vals[0], g1, jnp.concatenate(vals[5:21], axis=0)]

    first = head_of_lane == 0

    def stacked(q):
        return jnp.concatenate([jnp.where(first, q, 0.0), jnp.where(first, 0.0, q)], axis=0).astype(BF16)

    s = [_dot_nt(stacked(q), k) + jnp.concatenate([bias, bias], axis=0) for q, k, _, bias in classes]
    m = [jnp.max(si, axis=-1, keepdims=True) for si in s]
    p = [jnp.exp(si - mi) for si, mi in zip(s, m)]
    l = [jnp.sum(pi, axis=-1, keepdims=True) for pi in p]
    o = [_dot(pi.astype(BF16), c[2]) for pi, c in zip(p, classes)]

    res = []
    for hh in range(2):
        half = lambda vals: [v[hh * (v.shape[0] // 2):(hh + 1) * (v.shape[0] // 2)] for v in vals]
        ms, ls, os_ = rows_of(half(m)), rows_of(half(l)), rows_of(half(o))
        m_all = jnp.maximum(jnp.maximum(ms[0], ms[1]), ms[2])
        num = jnp.zeros((Q_BLOCK, 2 * HEAD_DIM), F32)
        den = jnp.zeros((Q_BLOCK, 1), F32)
        for g in range(A_GROUPS):
            wg = jnp.exp(ms[g] - m_all)
            num = num + wg * os_[g]
            den = den + wg * ls[g]
        res.append(num / den)
    o_ref[0] = jnp.where(head_of_lane == 0, res[0], res[1]).astype(o_ref.dtype)


def _mixer_a2(pb, biases, batch, seq):
    nblk = seq // Q_BLOCK
    r8 = A_ROWS2
    gw = 2 * A_WIDTH
    kv1 = jnp.pad(pb[:, :, PB_AKV + gw:PB_AKV + 2 * gw], ((0, 0), (512, 0), (0, 0)))
    kv1 = kv1.reshape(batch, -1, 4, gw).transpose(0, 2, 1, 3)
    kv2 = jnp.pad(pb[:, :, PB_AKV + 2 * gw:PB_AKV + 3 * gw].astype(F32),
                  ((0, 0), (A_MAXWIN, 16 * (A_SPAN2 - 136)), (0, 0)))
    kv2 = kv2.reshape(batch, -1, 16, gw).transpose(0, 2, 1, 3)
    q = pb[:, :, PB_AQ:PB_BQ].reshape(batch, nblk, r8, 16, -1).transpose(0, 1, 3, 2, 4).reshape(batch, seq, -1)
    pair = 2 * HEAD_DIM
    pairs_per_part = A_WIDTH // pair
    q_spec = lambda g: pl.BlockSpec((1, Q_BLOCK, pair), lambda b, hp, i: (b, i, g * pairs_per_part + hp))
    cur = lambda part: pl.BlockSpec((1, Q_BLOCK, pair), lambda b, hp, i: (b, i, PB_AKV // pair + part * pairs_per_part + hp))
    prev = lambda part: pl.BlockSpec((1, Q_BLOCK, pair),
                                     lambda b, hp, i: (b, jnp.maximum(i - 1, 0), PB_AKV // pair + part * pairs_per_part + hp))
    k4 = lambda a: pl.BlockSpec((1, a.shape[1], a.shape[2], pair), lambda b, hp, i: (b, 0, 0, hp))
    v4 = lambda a: pl.BlockSpec((1, a.shape[1], a.shape[2], pair), lambda b, hp, i: (b, 0, 0, pairs_per_part + hp))
    in_specs = [q_spec(0), q_spec(1), q_spec(2), prev(0), cur(0), prev(1), cur(1), k4(kv1), v4(kv1), k4(kv2), v4(kv2)]
    for bias in biases:
        in_specs.append(pl.BlockSpec(bias.shape, lambda b, hp, i: (0, 0)))
    o = pl.pallas_call(
        _mixa2_kernel,
        grid=(batch, A_HEADS // 2, nblk),
        in_specs=in_specs,
        out_specs=pl.BlockSpec((1, Q_BLOCK, pair), lambda b, hp, i: (b, i, hp)),
        out_shape=jax.ShapeDtypeStruct((batch, seq, A_WIDTH), BF16),
        compiler_params=_params("parallel", "parallel", "arbitrary", vmem_mb=48),
        name="mixer_a",
    )(q, q, q, pb, pb, pb, pb, kv1, kv1, kv2, kv2, *biases)
    return o.reshape(batch, nblk, 16, r8, A_WIDTH).transpose(0, 1, 3, 2, 4).reshape(batch, seq, A_WIDTH)


def _mixb_kernel(qb_ref, kb_ref, vb_ref, iq_ref, ik_ref, misc_ref, tri_ref, o_ref, keys_ref, selb_ref, *, top_k):
    _, rows, ch = keys_ref.shape
    start = pl.program_id(1) * rows
    nch = (start + rows + ch - 1) // ch
    qpos = start + lax.broadcasted_iota(I32, (rows, 1), 0)
    w = misc_ref[0][:, 0:IDX_HEADS] * (IDX_DIM ** -0.5 * IDX_HEADS ** -0.5)
    iq = iq_ref[0]
    zero_col = jnp.zeros((rows, 1), F32)

    def score_chunk(c, carry):
        off = pl.multiple_of(c * ch, ch)
        ik = ik_ref[0, pl.ds(off, ch), 0:IDX_DIM]
        acc = jnp.zeros((rows, ch), F32)
        for h in range(IDX_HEADS):
            lg = _dot_nt(iq[:, h * IDX_DIM:(h + 1) * IDX_DIM], ik)
            acc = acc + jnp.maximum(lg, 0.0) * w[:, h:h + 1]
        acc = acc + 0.0
        bits = lax.bitcast_convert_type(acc, I32)
        key = bits ^ ((bits >> 31) & 0x7FFFFFFF)
        kpos = off + lax.broadcasted_iota(I32, (rows, ch), 1)
        keys_ref[c] = jnp.where(kpos <= qpos, key, INT_MIN)
        return carry

    lax.fori_loop(0, nch, score_chunk, 0)

    def count(pred):
        def chunk(c, cnt):
            kc = keys_ref[c]
            for j in range(ch // LANES):
                cnt = cnt + jnp.where(pred(kc[:, j * LANES:(j + 1) * LANES]), 1.0, 0.0)
            return cnt
        cnt = lax.fori_loop(0, nch, chunk, jnp.zeros((rows, LANES), F32))
        return jnp.sum(cnt, axis=-1, keepdims=True)

    kf = float(top_k)
    ans = jnp.where(count(lambda kc: kc >= 0) >= kf, 0, INT_MIN).astype(I32)

    def body(it, ans):
        cand = ans + jnp.left_shift(jnp.int32(1), 30 - it)
        return jnp.where(count(lambda kc: kc >= cand) >= kf, cand, ans)

    thr = lax.fori_loop(0, 31, body, ans)
    need = kf - count(lambda kc: kc > thr)

    tw = tri_ref.shape[0]

    def select_chunk(c, run):
        kc_all = keys_ref[c]
        for j in range(ch // tw):
            kc = kc_all[:, j * tw:(j + 1) * tw]
            eq = kc == thr
            pre = _dot(jnp.where(eq, 1.0, 0.0).astype(BF16), tri_ref[...])
            take = jnp.where(eq, jnp.where(pre + run <= need, 1.0, 0.0), 0.0)
            sel = jnp.where(kc > thr, 1.0, take)
            sel = jnp.where(kc != INT_MIN, sel, 0.0)
            selb_ref[c, :, j * tw:(j + 1) * tw] = jnp.where(sel > 0.5, 0.0, NEG_INF)
            run = run + pre[:, tw - 1:tw]
        return run

    def select_chunk_no_ties(c, carry):
        kc = keys_ref[c]
        selb_ref[c] = jnp.where(kc >= thr, jnp.where(kc != INT_MIN, 0.0, NEG_INF), NEG_INF)
        return carry

    excess = jnp.max(count(lambda kc: kc >= thr)) > kf
    lax.cond(excess,
             lambda: lax.fori_loop(0, nch, select_chunk, zero_col),
             lambda: lax.fori_loop(0, nch, select_chunk_no_ties, zero_col))

    heads = range(B_HEADS)
    lane = lax.broadcasted_iota(I32, (1, B_WIDTH), 1) // HEAD_DIM
    qs = (qb_ref[0].astype(F32) * HEAD_DIM ** -0.5).astype(BF16)
    qm = [jnp.where(lane == h, qs, jnp.zeros_like(qs)) for h in heads]

    def per_lane(cols):
        out = cols[B_HEADS - 1]
        for h in reversed(range(B_HEADS - 1)):
            out = jnp.where(lane <= h, cols[h], out)
        return out

    def attend_chunk(c, carry):
        off = pl.multiple_of(c * ch, ch)
        bias = selb_ref[c]
        kc = kb_ref[0, pl.ds(off, ch), :]
        vc = vb_ref[0, pl.ds(off, ch), :]
        ms, ls, acc = carry[:B_HEADS], carry[B_HEADS:2 * B_HEADS], carry[2 * B_HEADS]
        s = [_dot_nt(qm[h], kc) + bias for h in heads]
        m_new = [jnp.maximum(ms[h], jnp.max(s[h], axis=-1, keepdims=True)) for h in heads]
        m_safe = [jnp.where(m_new[h] == NEG_INF, 0.0, m_new[h]) for h in heads]
        alpha = [jnp.exp(ms[h] - m_safe[h]) for h in heads]
        p = [jnp.exp(s[h] - m_safe[h]) for h in heads]
        l_new = [alpha[h] * ls[h] + jnp.sum(p[h], axis=-1, keepdims=True) for h in heads]
        pv = [_dot(p[h].astype(BF16), vc) for h in heads]
        acc = per_lane(alpha) * acc + per_lane(pv)
        return tuple(m_new) + tuple(l_new) + (acc,)

    init = ((jnp.full((rows, 1), NEG_INF, F32),) * B_HEADS + (zero_col,) * B_HEADS
            + (jnp.zeros((rows, B_WIDTH), F32),))
    res = lax.fori_loop(0, nch, attend_chunk, init)
    o_ref[0] = (res[2 * B_HEADS] / per_lane(res[B_HEADS:2 * B_HEADS])).astype(o_ref.dtype)


def _mixer_b(pb, pf, tri, batch, seq):
    rows = Q_BLOCK
    top_k = min(TOPK_MAX, seq // 4)
    ch = min(512, seq)
    kern = functools.partial(_mixb_kernel, top_k=top_k)
    return pl.pallas_call(
        kern,
        grid=(batch, seq // rows),
        in_specs=[pl.BlockSpec((1, rows, B_WIDTH), lambda b, i: (b, i, PB_BQ // B_WIDTH)),
                  pl.BlockSpec((1, seq, B_WIDTH), lambda b, i: (b, 0, PB_BK // B_WIDTH)),
                  pl.BlockSpec((1, seq, B_WIDTH), lambda b, i: (b, 0, PB_BV // B_WIDTH)),
                  pl.BlockSpec((1, rows, 256), lambda b, i: (b, i, PB_IQ // 256)),
                  pl.BlockSpec((1, seq, LANES), lambda b, i: (b, 0, PB_IK // LANES)),
                  pl.BlockSpec((1, rows, LANES), lambda b, i: (b, i, PF_MISC // LANES)),
                  pl.BlockSpec(tri.shape, lambda b, i: (0, 0))],
        out_specs=pl.BlockSpec((1, rows, B_WIDTH), lambda b, i: (b, i, 0)),
        out_shape=jax.ShapeDtypeStruct((batch, seq, B_WIDTH), BF16),
        scratch_shapes=[pltpu.VMEM((seq // ch, rows, ch), I32), pltpu.VMEM((seq // ch, rows, ch), F32)],
        compiler_params=_params("parallel", "arbitrary", vmem_mb=48),
        name="mixer_b",
    )(pb, pb, pb, pb, pb, pf, tri)


def _head_consts():
    lane = np.arange(C_WIDTH)
    same = (lane[:, None] // HEAD_DIM == lane[None, :] // HEAD_DIM).astype(np.float32)
    return jnp.asarray(same)


def _rwkv1_kernel(cin_ref, prev_ref, vf_ref, mu_ref, vec_ref, lora_ref, vl_ref, bd_ref,
                  phi_ref, psi_ref, rp_ref, y0_ref, g_ref, bonus_ref, v_ref, *, rows, use_v_lora):
    i = pl.program_id(1)
    pc = cin_ref[0][:, :C_IN]
    prev = jnp.where(i > 0, prev_ref[0][SUBLANES - 1:SUBLANES, :C_IN], 0.0)
    row = lax.broadcasted_iota(I32, (rows, 1), 0)
    shifted = jnp.where(row == 0, prev, pltpu.roll(pc, 1, 0))
    pc = pc + (shifted - pc) * mu_ref[...]
    r = pc[:, 0:C_WIDTH]
    k = pc[:, C_WIDTH:2 * C_WIDTH]
    v = pc[:, 2 * C_WIDTH:3 * C_WIDTH]
    xl = pc[:, 3 * C_WIDTH:C_IN]
    lane = lax.broadcasted_iota(I32, xl.shape, 1)
    feat = jnp.where(lane < C_LORA_W, jnp.tanh(xl), jnp.where(lane < C_LORA_W + C_LORA_A, xl, _sigmoid(xl)))
    w0, a0, k_k, k_a, r_k = (vec_ref[n:n + 1, :] for n in range(5))
    bd = bd_ref[...]
    bd_sel = bd.astype(BF16)
    w_raw = -_softplus(-(w0 + _dot3(feat, lora_ref[0]))) - 0.5
    a = _sigmoid(a0 + _dot3(feat, lora_ref[1]))
    g_ref[0] = _dot3(feat, lora_ref[2])
    if use_v_lora:
        v0 = vec_ref[5:6, :]
        v = v + (vf_ref[0] - v) * _sigmoid(v0 + _dot3(_dot3(v, vl_ref[0]), vl_ref[1]))
    v_ref[0] = v
    kk = k * k_k
    kk = kk / jnp.maximum(jnp.sqrt(_dot_sel(bd_sel, kk * kk, left=False)), 1e-12)
    k = k * (1.0 + (a - 1.0) * k_a)
    logw = -jnp.exp(w_raw)
    bonus_ref[0] = _dot_sel(bd_sel, r * k * r_k, left=False) * v
    av = -kk
    bv = kk * a

    cc = C_CHUNK
    ri = lax.broadcasted_iota(I32, (cc, cc), 0)
    ci = lax.broadcasted_iota(I32, (cc, cc), 1)
    tril = jnp.where(ci <= ri, 1.0, 0.0).astype(BF16)
    strict = ci < ri
    incl = ci <= ri
    eye_c = jnp.where(ci == ri, 1.0, 0.0)
    lane_w = lax.broadcasted_iota(I32, (1, C_WIDTH), 1)
    ri2 = lax.broadcasted_iota(I32, (C_WIDTH, C_WIDTH), 0)
    ci2 = lax.broadcasted_iota(I32, (C_WIDTH, C_WIDTH), 1)
    eye_w = jnp.where(ri2 == ci2, 1.0, 0.0)
    chunks = range(rows // cc)
    heads = range(C_HEADS)
    pairs = [(c, h) for c in chunks for h in heads]
    mh = [jnp.where(lane_w // HEAD_DIM == h, 1.0, 0.0) for h in heads]
    sl = [slice(c * cc, (c + 1) * cc) for c in chunks]
    cs = [_dot_sel(tril, logw[sl[c]]) for c in chunks]
    cs_end = [cs[c][cc - 1:cc, :] for c in chunks]
    at = [av[sl[c]] * jnp.exp(cs[c] - logw[sl[c]]) for c in chunks]
    rt = [r[sl[c]] * jnp.exp(cs[c]) for c in chunks]
    inv = [jnp.exp(-cs[c]) for c in chunks]
    rhs = [jnp.concatenate([bv[sl[c]] * inv[c], k[sl[c]] * inv[c]], axis=0) for c in chunks]
    tail = [jnp.exp(cs_end[c] - cs[c]) for c in chunks]
    vc = [v[sl[c]] for c in chunks]
    vc_b = [vc[c].astype(BF16) for c in chunks]
    ath = {(c, h): at[c] * mh[h] for c, h in pairs}
    aa = {(c, h): _dot3(jnp.concatenate([ath[c, h], rt[c] * mh[h]], axis=0), rhs[c], nt=True) for c, h in pairs}
    a_ab = {p: jnp.where(strict, aa[p][:cc, :cc], 0.0) for p in pairs}
    a_ak = {p: jnp.where(strict, aa[p][:cc, cc:], 0.0).astype(BF16) for p in pairs}
    a_rb = {p: jnp.where(incl, aa[p][cc:, :cc], 0.0).astype(BF16) for p in pairs}
    a_rk = {p: jnp.where(incl, aa[p][cc:, cc:], 0.0).astype(BF16) for p in pairs}
    x = {p: eye_c + a_ab[p] for p in pairs}
    pw = a_ab
    for _ in range(int(math.log2(cc)) - 1):
        pw = {p: _dot3(pw[p], pw[p]) for p in pairs}
        x = {p: x[p] + _dot3(x[p], pw[p]) for p in pairs}
    akv = {(c, h): _dot(a_ak[c, h], vc_b[c]) for c, h in pairs}
    ap_h = {p: _dot3(x[p], ath[p]) for p in pairs}
    w2_h = {(c, h): _dot3(x[c, h], akv[c, h]) * mh[h] for c, h in pairs}
    rp_h = {p: _dot(a_rb[p], ap_h[p].astype(BF16)) for p in pairs}
    y0_h = {(c, h): (_dot(a_rb[c, h], w2_h[c, h].astype(BF16)) + _dot(a_rk[c, h], vc_b[c])) * mh[h] for c, h in pairs}
    for c in chunks:
        ap = sum(ap_h[c, h] for h in heads)
        w2 = sum(w2_h[c, h] for h in heads)
        rp_ref[0, sl[c], :] = rt[c] + sum(rp_h[c, h] for h in heads)
        y0_ref[0, sl[c], :] = sum(y0_h[c, h] for h in heads)
        bh = bv[sl[c]] * tail[c]
        kh = k[sl[c]] * tail[c]
        phi_ref[0, c] = eye_w * jnp.exp(cs_end[c]) + _dot3(bh.T, ap) * bd
        psi_ref[0, c] = _dot(jnp.concatenate([bh, kh], axis=0).T.astype(BF16),
                             jnp.concatenate([w2, vc[c]], axis=0).astype(BF16)) * bd


def _rwkv2_kernel(phi_ref, psi_ref, rp_ref, y0_ref, g_ref, bonus_ref, gn_ref, bd_ref, o_ref, s_ref):
    @pl.when(pl.program_id(1) == 0)
    def _():
        s_ref[...] = jnp.zeros_like(s_ref)

    states = [s_ref[...]]
    for c in range(phi_ref.shape[1]):
        states.append(_dot3(phi_ref[0, c], states[c]) + psi_ref[0, c])
    s_ref[...] = states[-1]
    cc = C_CHUNK
    y = jnp.concatenate([_dot3(rp_ref[0, c * cc:(c + 1) * cc, :], states[c]) for c in range(phi_ref.shape[1])],
                        axis=0) + y0_ref[0]
    bd_sel = bd_ref[...].astype(BF16)
    mu = _dot_sel(bd_sel, y, left=False) * (1.0 / HEAD_DIM)
    d = y - mu
    var = _dot_sel(bd_sel, d * d, left=False) * (1.0 / HEAD_DIM)
    yn = d * lax.rsqrt(var + C_GN_EPS) * gn_ref[0:1, :] + gn_ref[1:2, :]
    o_ref[0] = ((yn + bonus_ref[0]) * g_ref[0]).astype(o_ref.dtype)


def _mixer_c(pf, v_first, mu, vec, lora, vl, gn, bd, batch, seq, use_v_lora):
    rows = min(256, seq)
    nblk = seq // rows
    nch = seq // C_CHUNK
    cpb = rows // C_CHUNK
    kern = functools.partial(_rwkv1_kernel, rows=rows, use_v_lora=use_v_lora)
    full2 = lambda a: pl.BlockSpec(a.shape, lambda b, i: (0,) * a.ndim)
    seq_spec = pl.BlockSpec((1, rows, C_WIDTH), lambda b, i: (b, i, 0))
    mat_spec = pl.BlockSpec((1, cpb, C_WIDTH, C_WIDTH), lambda b, i: (b, i, 0, 0))
    seq_shape = jax.ShapeDtypeStruct((batch, seq, C_WIDTH), F32)
    mat_shape = jax.ShapeDtypeStruct((batch, nch, C_WIDTH, C_WIDTH), F32)
    prev_blk = rows // SUBLANES
    phi, psi, rp, y0, g, bonus, v = pl.pallas_call(
        kern,
        grid=(batch, nblk),
        in_specs=[pl.BlockSpec((1, rows, 1024), lambda b, i: (b, i, PF_CIN // 1024)),
                  pl.BlockSpec((1, SUBLANES, 1024), lambda b, i: (b, jnp.maximum(i * prev_blk - 1, 0), PF_CIN // 1024)),
                  seq_spec, full2(mu), full2(vec), full2(lora), full2(vl), full2(bd)],
        out_specs=[mat_spec, mat_spec, seq_spec, seq_spec, seq_spec, seq_spec, seq_spec],
        out_shape=[mat_shape, mat_shape, seq_shape, seq_shape, seq_shape, seq_shape, seq_shape],
        compiler_params=_params("parallel", "parallel", vmem_mb=48),
        name="rwkv_chunks",
    )(pf, pf, v_first, mu, vec, lora, vl, bd)
    scan_chunks = min(4, nch)
    cseq = pl.BlockSpec((1, scan_chunks * C_CHUNK, C_WIDTH), lambda b, c: (b, c, 0))
    cmat = pl.BlockSpec((1, scan_chunks, C_WIDTH, C_WIDTH), lambda b, c: (b, c, 0, 0))
    o = pl.pallas_call(
        _rwkv2_kernel,
        grid=(batch, nch // scan_chunks),
        in_specs=[cmat, cmat, cseq, cseq, cseq, cseq,
                  pl.BlockSpec(gn.shape, lambda b, c: (0, 0)), pl.BlockSpec(bd.shape, lambda b, c: (0, 0))],
        out_specs=cseq,
        out_shape=jax.ShapeDtypeStruct((batch, seq, C_WIDTH), BF16),
        scratch_shapes=[pltpu.VMEM((C_WIDTH, C_WIDTH), F32)],
        compiler_params=_params("parallel", "arbitrary"),
        name="rwkv_scan",
    )(phi, psi, rp, y0, g, bonus, gn, bd)
    return o, v


def _ssd_consts():
    expand = np.zeros((LANES, D_INNER), np.float32)
    for h in range(D_HEADS):
        expand[MISC_DT + h, h * HEAD_DIM:(h + 1) * HEAD_DIM] = 1.0
    return jnp.asarray(expand)


def _ssd_kernel(xbc_ref, z_ref, misc_ref, conv_ref, vec_ref, hp_ref, expand_ref, bd_ref, o_ref, st_ref, prev_ref):
    @pl.when(pl.program_id(1) == 0)
    def _():
        st_ref[...] = jnp.zeros_like(st_ref)
        prev_ref[...] = jnp.zeros_like(prev_ref)

    q = D_CHUNK
    x_raw = xbc_ref[0]
    ext = jnp.concatenate([prev_ref[...], x_raw], axis=0)
    conv = jnp.zeros((q, D_XBC), F32)
    for t in range(D_CONV):
        lo = SUBLANES - (D_CONV - 1) + t
        conv = conv + ext[lo:lo + q, :] * conv_ref[t:t + 1, :]
    prev_ref[...] = x_raw[q - SUBLANES:, :]
    xbc = _silu(conv + conv_ref[D_CONV:D_CONV + 1, :])
    xs = xbc[:, :D_INNER]
    bm = xbc[:, D_INNER:D_INNER + D_GROUPS * D_STATE]
    cm = xbc[:, D_INNER + D_GROUPS * D_STATE:]

    dt_col = _softplus(misc_ref[0] + hp_ref[0:1, :])
    a_col = dt_col * hp_ref[1:2, :]
    ri = lax.broadcasted_iota(I32, (q, q), 0)
    ci = lax.broadcasted_iota(I32, (q, q), 1)
    causal = ci <= ri
    tril = jnp.where(causal, 1.0, 0.0).astype(BF16)
    acs_col = _dot_sel(tril, a_col)
    acs_row = acs_col.T
    expand = expand_ref[...].astype(BF16)
    acs = _dot_sel(expand, acs_col, left=False)
    dt = _dot_sel(expand, dt_col, left=False)
    acs_end = acs[q - 1:q, :]
    xdt = xs * dt

    lane = lax.broadcasted_iota(I32, (1, LANES), 1)
    lane_w = lax.broadcasted_iota(I32, (1, D_INNER), 1)
    left = lane < D_STATE
    bm_sw = pltpu.roll(bm, D_STATE, 1)
    cm_sw = pltpu.roll(cm, D_STATE, 1)
    b_exp = jnp.concatenate([jnp.where(left, bm, bm_sw), jnp.where(left, bm_sw, bm)], axis=1)
    c_exp = jnp.concatenate([jnp.where(left, cm, cm_sw), jnp.where(left, cm_sw, cm)], axis=1)
    cb = [_dot3(jnp.where(left == (g == 0), cm, 0.0), bm, nt=True) for g in range(D_GROUPS)]

    scores = []
    for h in range(D_HEADS):
        col = acs_col[:, MISC_DT + h:MISC_DT + h + 1]
        rw = acs_row[MISC_DT + h:MISC_DT + h + 1, :]
        decay = jnp.exp(jnp.where(causal, col - rw, NEG_INF))
        scores.append(cb[h // (D_HEADS // D_GROUPS)] * decay)
    y_h = [_dot3(scores[h], xdt) for h in range(D_HEADS)]
    y = jnp.zeros((q, D_INNER), F32)
    for h in range(D_HEADS):
        y = y + jnp.where(lane_w // HEAD_DIM == h, y_h[h], 0.0)
    st = st_ref[...]
    y = y + _dot3(c_exp, st) * jnp.exp(acs)
    st_ref[...] = st * jnp.exp(acs_end) + _dot3(b_exp.T, xdt * jnp.exp(acs_end - acs)) * bd_ref[...]
    y = y + xs * vec_ref[0:1, :]
    y = y * _silu(z_ref[0])
    half = D_INNER // D_GROUPS
    outs = []
    for g in range(D_GROUPS):
        yg = y[:, g * half:(g + 1) * half]
        outs.append(yg * lax.rsqrt(jnp.mean(yg * yg, axis=-1, keepdims=True) + D_NORM_EPS))
    o_ref[0] = (jnp.concatenate(outs, axis=1) * vec_ref[1:2, :]).astype(o_ref.dtype)


def _mixer_d(pf, conv, vec, hp, expand, bd, batch, seq):
    q = D_CHUNK
    full = lambda a: pl.BlockSpec(a.shape, lambda b, i: (0, 0))
    return pl.pallas_call(
        _ssd_kernel,
        grid=(batch, seq // q),
        in_specs=[pl.BlockSpec((1, q, D_XBC), lambda b, i: (b, i, PF_XBC // D_XBC)),
                  pl.BlockSpec((1, q, D_INNER), lambda b, i: (b, i, PF_Z // D_INNER)),
                  pl.BlockSpec((1, q, LANES), lambda b, i: (b, i, PF_MISC // LANES)),
                  full(conv), full(vec), full(hp), full(expand), full(bd)],
        out_specs=pl.BlockSpec((1, q, D_INNER), lambda b, i: (b, i, 0)),
        out_shape=jax.ShapeDtypeStruct((batch, seq, D_INNER), BF16),
        scratch_shapes=[pltpu.VMEM((D_INNER, D_INNER), F32), pltpu.VMEM((SUBLANES, D_XBC), F32)],
        compiler_params=_params("parallel", "arbitrary"),
        name="ssd",
    )(pf, pf, pf, conv, vec, hp, expand, bd)


def _merge_kernel(x_ref, gates_ref, oa_ref, ob_ref, oc_ref, od_ref, wbr_ref, wout_ref, ln_ref, wr_ref, br_ref,
                  x1_ref, route_ref):
    acc = None
    for n, o_ref in enumerate((oa_ref, ob_ref, oc_ref, od_ref)):
        term = _sigmoid(gates_ref[:, n * D_MODEL:(n + 1) * D_MODEL]) * _dot(o_ref[...], wbr_ref[n])
        acc = term if acc is None else acc + term
    h = DEEPNORM_ALPHA * x_ref[...] + _dot(acc.astype(BF16), wout_ref[...])
    x1 = _layer_norm(h, ln_ref[0:1, :], ln_ref[1:2, :])
    x1_ref[...] = x1

    logits = _dot3(x1, wr_ref[...]) + br_ref[...]
    lane = lax.broadcasted_iota(I32, logits.shape, 1)
    big = jnp.int32(LANES)
    gl = jnp.where(lane < N_EXPERT_GROUPS, logits, NEG_INF)
    gm = jnp.max(gl, axis=-1, keepdims=True)
    pg_top = 1.0 / jnp.sum(jnp.exp(gl - gm), axis=-1, keepdims=True)
    g_sel = jnp.min(jnp.where(gl == gm, lane, big), axis=-1, keepdims=True)
    off = N_EXPERT_GROUPS + g_sel * EXPERTS_PER_GROUP
    el = jnp.where((lane >= off) & (lane < off + EXPERTS_PER_GROUP), logits, NEG_INF)
    em = jnp.max(el, axis=-1, keepdims=True)
    es = jnp.sum(jnp.exp(el - em), axis=-1, keepdims=True)
    idx1 = jnp.min(jnp.where(el == em, lane, big), axis=-1, keepdims=True)
    el2 = jnp.where(lane == idx1, NEG_INF, el)
    em2 = jnp.max(el2, axis=-1, keepdims=True)
    idx2 = jnp.min(jnp.where(el2 == em2, lane, big), axis=-1, keepdims=True)
    p1 = 1.0 / es
    p2 = jnp.exp(em2 - em) / es
    gate1 = pg_top * p1 / (p1 + p2)
    gate2 = pg_top * p2 / (p1 + p2)
    e1 = (idx1 - N_EXPERT_GROUPS).astype(F32)
    e2 = (idx2 - N_EXPERT_GROUPS).astype(F32)
    route_ref[...] = jnp.where(lane == 0, e1, jnp.where(lane == 1, e2, jnp.where(lane == 2, gate1,
                               jnp.where(lane == 3, gate2, 0.0))))


def _merge(x2d, pf, oa, ob, oc, od, wbr, wout, ln, wr, br, tm):
    m = x2d.shape[0]
    row = lambda w: pl.BlockSpec((tm, w), lambda i: (i, 0))
    full = lambda a: pl.BlockSpec(a.shape, lambda i: (0,) * a.ndim)
    return pl.pallas_call(
        _merge_kernel,
        grid=(m // tm,),
        in_specs=[row(D_MODEL), pl.BlockSpec((tm, N_BRANCH * D_MODEL), lambda i: (i, PF_GATES)),
                  row(A_WIDTH), row(B_WIDTH), row(C_WIDTH), row(D_INNER),
                  full(wbr), full(wout), full(ln), full(wr), full(br)],
        out_specs=[row(D_MODEL), row(LANES)],
        out_shape=[jax.ShapeDtypeStruct((m, D_MODEL), F32), jax.ShapeDtypeStruct((m, LANES), F32)],
        compiler_params=_params("parallel", vmem_mb=48),
        name="merge_route",
    )(x2d, pf, oa, ob, oc, od, wbr, wout, ln, wr, br)


def _ffn_kernel(be_ref, nv_ref, xs_ref, wg_ref, wu_ref, wd_ref, o_ref):
    i = pl.program_id(0)

    @pl.when(i < nv_ref[0])
    def _():
        xb = xs_ref[...]
        hg = _dot(xb, wg_ref[0, 0].astype(BF16))
        hu = _dot(xb, wu_ref[0, 0].astype(BF16))
        o_ref[...] = _dot((_silu(hg) * hu).astype(BF16), wd_ref[0, 0].astype(BF16))

    @pl.when(i >= nv_ref[0])
    def _():
        o_ref[...] = jnp.zeros_like(o_ref)


def _ffn(blk_expert, n_valid, xs, e_gate, e_up, e_down, layer):
    cap = xs.shape[0]
    wspec = lambda a: pl.BlockSpec((1, 1) + a.shape[2:], lambda i, be, nv: (layer, be[i], 0, 0))
    return pl.pallas_call(
        _ffn_kernel,
        grid_spec=pltpu.PrefetchScalarGridSpec(
            num_scalar_prefetch=2,
            grid=(cap // MOE_ROWS,),
            in_specs=[pl.BlockSpec((MOE_ROWS, D_MODEL), lambda i, be, nv: (i, 0)),
                      wspec(e_gate), wspec(e_up), wspec(e_down)],
            out_specs=pl.BlockSpec((MOE_ROWS, D_MODEL), lambda i, be, nv: (i, 0)),
        ),
        out_shape=jax.ShapeDtypeStruct((cap, D_MODEL), F32),
        compiler_params=_params("arbitrary", vmem_mb=48),
        name="expert_ffn",
    )(blk_expert, n_valid, xs, e_gate, e_up, e_down)


def _combine_kernel(dest_ref, x_ref, route_ref, ln_ref, yb_ref, o_ref, ybuf, sem):
    i = pl.program_id(0)
    tm = x_ref.shape[0]

    def gather(tile, slot):
        base = tile * (TOP_K_EXPERTS * tm)

        def body(r, carry):
            for s in range(TOP_K_EXPERTS):
                row = dest_ref[base + TOP_K_EXPERTS * r + s]
                pltpu.make_async_copy(yb_ref.at[pl.ds(row, 1)], ybuf.at[slot, pl.ds(s * tm + r, 1)],
                                      sem.at[slot]).start()
            return carry

        lax.fori_loop(0, tm, body, 0, unroll=8)

    @pl.when(i == 0)
    def _():
        gather(0, 0)

    @pl.when(i + 1 < pl.num_programs(0))
    def _():
        gather(i + 1, (i + 1) % 2)

    slot = i % 2
    pltpu.make_async_copy(yb_ref.at[pl.ds(0, TOP_K_EXPERTS * tm)], ybuf.at[slot], sem.at[slot]).wait()
    g0 = route_ref[:, 2:3]
    g1 = route_ref[:, 3:4]
    h = DEEPNORM_ALPHA * x_ref[...] + (ybuf[slot, 0:tm, :] * g0 + ybuf[slot, tm:2 * tm, :] * g1)
    o_ref[...] = _layer_norm(h, ln_ref[0:1, :], ln_ref[1:2, :])


def _combine(x1, yb, dest, route, ln, tm):
    m = x1.shape[0]
    return pl.pallas_call(
        _combine_kernel,
        grid_spec=pltpu.PrefetchScalarGridSpec(
            num_scalar_prefetch=1,
            grid=(m // tm,),
            in_specs=[pl.BlockSpec((tm, D_MODEL), lambda i, d: (i, 0)),
                      pl.BlockSpec((tm, LANES), lambda i, d: (i, 0)),
                      pl.BlockSpec(ln.shape, lambda i, d: (0, 0)),
                      pl.BlockSpec(memory_space=pl.ANY)],
            out_specs=pl.BlockSpec((tm, D_MODEL), lambda i, d: (i, 0)),
            scratch_shapes=[pltpu.VMEM((2, TOP_K_EXPERTS * tm, D_MODEL), F32), pltpu.SemaphoreType.DMA((2,))],
        ),
        out_shape=jax.ShapeDtypeStruct((m, D_MODEL), F32),
        compiler_params=_params("arbitrary"),
        name="combine",
    )(dest, x1, route, ln, yb)


def _dispatch_tables(route, m):
    flat_e = route[:, 0:TOP_K_EXPERTS].astype(I32).reshape(-1)
    n_assign = m * TOP_K_EXPERTS
    onehot = (flat_e[:, None] == jnp.arange(N_EXPERTS, dtype=I32)[None, :]).astype(I32)
    csum = jnp.cumsum(onehot, axis=0)
    rank = jnp.sum(csum * onehot, axis=1) - 1
    counts = csum[-1]
    padded = (counts + MOE_ROWS - 1) // MOE_ROWS * MOE_ROWS
    pad_end = jnp.cumsum(padded)
    pad_start = pad_end - padded
    dest = pad_start[flat_e] + rank
    cap = (n_assign + N_EXPERTS * (MOE_ROWS - 1) + MOE_ROWS - 1) // MOE_ROWS * MOE_ROWS
    n_blocks = cap // MOE_ROWS
    blk_start = jnp.arange(n_blocks, dtype=I32) * MOE_ROWS
    blk_expert = jnp.minimum(jnp.sum((pad_end[None, :] <= blk_start[:, None]).astype(I32), axis=1), N_EXPERTS - 1)
    buf_tok = (jnp.arange(cap, dtype=I32) % m).at[dest].set(jnp.arange(n_assign, dtype=I32) // TOP_K_EXPERTS)
    n_valid = (pad_end[-1] // MOE_ROWS).astype(I32).reshape(1)
    return dest, buf_tok, blk_expert, n_valid


def _row_pad(a, rows):
    return jnp.pad(a, ((0, rows - a.shape[0]), (0, 0)))


def kernel(x, w_in, c_mu, c_w0, c_w2, c_a0, c_a2, c_g2, c_kk, c_ka, c_rk, c_gn_w, c_gn_b, c_v0, c_v1, c_v2,
           d_conv_w, d_conv_b, d_dt_bias, d_a_log, d_skip, d_norm_w, w_branch, w_out, ln1_g, ln1_b,
           r_group, r_group_b, r_expert, r_expert_b, e_gate, e_up, e_down, ln2_g, ln2_b):
    batch, seq, _ = x.shape
    m = batch * seq
    biases = _mixa_class_bias()
    tw = min(256, seq)
    tri = jnp.asarray(np.triu(np.ones((tw, tw), np.float32))).astype(BF16)
    bd = _head_consts()
    expand = _ssd_consts()
    x2d = x.reshape(m, D_MODEL)
    v_first = jnp.zeros((batch, seq, C_WIDTH), F32)
    tm_proj = min(2048, m)
    tm_tok = min(512, m)
    for l in range(DEPTH):
        wb, wf = _proj_weights(w_in[l])
        pb = _proj(x2d, wb, BF16, tm_proj, PROJ_TN).reshape(batch, seq, PB_WIDTH)
        pf = _proj(x2d, wf, F32, tm_proj, PROJ_TN).reshape(batch, seq, PF_WIDTH)

        o_a = _mixer_a2(pb, biases, batch, seq)
        o_b = _mixer_b(pb, pf, tri, batch, seq)

        use_v_lora = l > 0
        vec_rows = [c_w0[l], c_a0[l], c_kk[l], c_ka[l], c_rk[l].reshape(-1)]
        vec_rows.append(c_v0[l - 1] if use_v_lora else jnp.zeros((C_WIDTH,), F32))
        vec = _row_pad(jnp.stack(vec_rows), SUBLANES)
        lora = jnp.stack([
            jnp.pad(c_w2[l], ((0, LANES - C_LORA_W), (0, 0))),
            jnp.pad(c_a2[l], ((C_LORA_W, LANES - C_LORA_W - C_LORA_A), (0, 0))),
            jnp.pad(c_g2[l], ((C_LORA_W + C_LORA_A, 0), (0, 0)))])
        if use_v_lora:
            vl = jnp.stack([jnp.pad(c_v1[l - 1], ((0, 0), (0, C_WIDTH - C_LORA_V))),
                            jnp.pad(c_v2[l - 1], ((0, C_WIDTH - C_LORA_V), (0, 0)))])
        else:
            vl = jnp.zeros((2, C_WIDTH, C_WIDTH), F32)
        gn = _row_pad(jnp.stack([c_gn_w[l], c_gn_b[l]]), SUBLANES)
        o_c, v_c = _mixer_c(pf, v_first, c_mu[l].reshape(1, C_IN), vec, lora, vl, gn, bd, batch, seq, use_v_lora)
        if l == 0:
            v_first = v_c

        conv = _row_pad(jnp.concatenate([d_conv_w[l], d_conv_b[l][None, :]], axis=0), SUBLANES)
        dvec = _row_pad(jnp.stack([jnp.repeat(d_skip[l], HEAD_DIM), d_norm_w[l]]), SUBLANES)
        place = lambda a: jnp.pad(a, (MISC_DT, LANES - MISC_DT - D_HEADS))
        hp = _row_pad(jnp.stack([place(d_dt_bias[l]), place(-jnp.exp(d_a_log[l]))]), SUBLANES)
        o_d = _mixer_d(pf, conv, dvec, hp, expand, bd, batch, seq)

        wr = jnp.pad(jnp.concatenate([r_group[l], r_expert[l]], axis=1),
                     ((0, 0), (0, LANES - N_EXPERT_GROUPS - N_EXPERTS)))
        br = jnp.pad(jnp.concatenate([r_group_b[l], r_expert_b[l]]), (0, LANES - N_EXPERT_GROUPS - N_EXPERTS))
        ln1 = _row_pad(jnp.stack([ln1_g[l], ln1_b[l]]), SUBLANES)
        x1, route = _merge(x2d, pf.reshape(m, PF_WIDTH), o_a.reshape(m, -1), o_b.reshape(m, -1),
                           o_c.reshape(m, -1), o_d.reshape(m, -1), w_branch[l].astype(BF16),
                           w_out[l].astype(BF16), ln1, wr, br.reshape(1, LANES), tm_tok)

        dest, buf_tok, blk_expert, n_valid = _dispatch_tables(route, m)
        xs = jnp.take(x1.astype(BF16), buf_tok, axis=0)
        yb = _ffn(blk_expert, n_valid, xs, e_gate, e_up, e_down, l)
        ln2 = _row_pad(jnp.stack([ln2_g[l], ln2_b[l]]), SUBLANES)
        x2d = _combine(x1, yb, dest, route, ln2, min(256, m))
    return x2d.reshape(batch, seq, D_MODEL)
```

```python
import functools
import math

import jax
import jax.numpy as jnp
import numpy as np
from jax import lax
from jax.experimental import pallas as pl
from jax.experimental.pallas import tpu as pltpu

F32 = jnp.float32
BF16 = jnp.bfloat16
I32 = jnp.int32
NEG_INF = float("-inf")
INT_MIN = -(2 ** 31)

LANES = 128
SUBLANES = 8

D_MODEL = 1024
DEPTH = 2
HEAD_DIM = 64
Q_BLOCK = 128

A_HEADS = 4
A_PATTERNS = ((128, 1), (512, 4), (2048, 16))
A_GROUPS = len(A_PATTERNS)
A_WIDTH = A_HEADS * HEAD_DIM
A_MAXWIN = max(w for w, _ in A_PATTERNS)

B_HEADS = 4
B_WIDTH = B_HEADS * HEAD_DIM
IDX_HEADS = 4
IDX_DIM = 64
TOPK_MAX = 256

C_HEADS = 4
C_WIDTH = C_HEADS * HEAD_DIM
C_LORA_W = 32
C_LORA_A = 32
C_LORA_G = 64
C_LORA_V = 16
C_IN = 3 * C_WIDTH + C_LORA_W + C_LORA_A + C_LORA_G
C_GN_EPS = 64e-5
C_CHUNK = 64

D_HEADS = 4
D_INNER = D_HEADS * HEAD_DIM
D_GROUPS = 2
D_STATE = 64
D_CONV = 4
D_CHUNK = 128
D_XBC = D_INNER + 2 * D_GROUPS * D_STATE
D_NORM_EPS = 1e-5

N_BRANCH = 4
N_EXPERT_GROUPS = 4
EXPERTS_PER_GROUP = 8
N_EXPERTS = N_EXPERT_GROUPS * EXPERTS_PER_GROUP
TOP_K_EXPERTS = 2
D_EXPERT = 512
MOE_ROWS = 512

LN_EPS = 1e-5
DEEPNORM_ALPHA = (2 * DEPTH) ** 0.25

PB_AKV = 0
PB_AQ = PB_AKV + A_GROUPS * 2 * A_WIDTH
PB_BQ = PB_AQ + A_GROUPS * A_WIDTH
PB_BK = PB_BQ + B_WIDTH
PB_BV = PB_BK + B_WIDTH
PB_IQ = PB_BV + B_WIDTH
PB_IK = PB_IQ + IDX_HEADS * IDX_DIM
PB_WIDTH = PB_IK + 512
PF_GATES = 0
PF_CIN = PF_GATES + N_BRANCH * D_MODEL
PF_MISC = PF_CIN + C_IN
PF_XBC = PF_CIN + 1024
PF_Z = PF_XBC + D_XBC
PROJ_TN = 768
PF_WIDTH = -(-(PF_Z + D_INNER) // PROJ_TN) * PROJ_TN
MISC_DT = 4


def _dot(a, b, precision=None):
    return jnp.dot(a, b, preferred_element_type=F32, precision=precision)


def _dot_nt(a, b, precision=None):
    return lax.dot_general(a, b, (((1,), (1,)), ((), ())), preferred_element_type=F32, precision=precision)


def _split2(a):
    hi = a.astype(BF16)
    return hi, (a - hi.astype(F32)).astype(BF16)


def _dot3(a, b, nt=False):
    mm = _dot_nt if nt else _dot
    ah, al = _split2(a)
    bh, bl = _split2(b)
    return mm(ah, bh) + (mm(ah, bl) + mm(al, bh))


def _dot_sel(sel, x, left=True):
    hi = x.astype(BF16)
    r1 = x - hi.astype(F32)
    mid = r1.astype(BF16)
    lo = (r1 - mid.astype(F32)).astype(BF16)
    if left:
        return _dot(sel, hi) + (_dot(sel, mid) + _dot(sel, lo))
    return _dot(hi, sel) + (_dot(mid, sel) + _dot(lo, sel))


def _sigmoid(x):
    return 1.0 / (1.0 + jnp.exp(-x))


def _softplus(x):
    return jnp.maximum(x, 0.0) + jnp.log1p(jnp.exp(-jnp.abs(x)))


def _silu(x):
    return x * _sigmoid(x)


def _layer_norm(h, g, b):
    mu = jnp.mean(h, axis=-1, keepdims=True)
    d = h - mu
    var = jnp.mean(d * d, axis=-1, keepdims=True)
    return d * lax.rsqrt(var + LN_EPS) * g + b


def _params(*sem, vmem_mb=None):
    kw = {}
    if vmem_mb is not None:
        kw["vmem_limit_bytes"] = vmem_mb * 1024 * 1024
    return pltpu.CompilerParams(dimension_semantics=sem, **kw)


def _proj_kernel(x_ref, w_ref, o_ref):
    o_ref[...] = _dot(x_ref[...].astype(BF16), w_ref[...]).astype(o_ref.dtype)


def _proj(x2d, w, out_dtype, tm, tn):
    m, k = x2d.shape
    n = w.shape[1]
    return pl.pallas_call(
        _proj_kernel,
        grid=(m // tm, n // tn),
        in_specs=[pl.BlockSpec((tm, k), lambda i, j: (i, 0)),
                  pl.BlockSpec((k, tn), lambda i, j: (0, j))],
        out_specs=pl.BlockSpec((tm, tn), lambda i, j: (i, j)),
        out_shape=jax.ShapeDtypeStruct((m, n), out_dtype),
        compiler_params=_params("parallel", "arbitrary", vmem_mb=48),
        name="in_proj",
    )(x2d, w)


def _proj_weights(w):
    k = w.shape[0]
    off = 0
    seg = {}
    for name, size in (("a_qkv", 3 * A_GROUPS * A_WIDTH), ("b_qkv", 3 * B_WIDTH), ("b_idx_q", IDX_HEADS * IDX_DIM),
                       ("b_idx_k", IDX_DIM), ("b_idx_w", IDX_HEADS), ("c_in", C_IN), ("d_z", D_INNER),
                       ("d_xbc", D_XBC), ("d_dt", D_HEADS), ("gates", N_BRANCH * D_MODEL)):
        seg[name] = w[:, off:off + size]
        off += size
    a = seg["a_qkv"].reshape(k, 3, A_GROUPS, A_HEADS, HEAD_DIM)
    a_kv = jnp.transpose(a[:, 1:3], (0, 2, 1, 3, 4)).reshape(k, -1)
    a_q = a[:, 0].reshape(k, -1)
    zeros = lambda n: jnp.zeros((k, n), w.dtype)
    wb = jnp.concatenate([a_kv, a_q, seg["b_qkv"], seg["b_idx_q"], seg["b_idx_k"],
                          zeros(PB_WIDTH - PB_IK - IDX_DIM)], axis=1)
    misc = jnp.concatenate([seg["b_idx_w"], seg["d_dt"], zeros(LANES - IDX_HEADS - D_HEADS)], axis=1)
    wf = jnp.concatenate([seg["gates"], seg["c_in"], misc, seg["d_xbc"], seg["d_z"]], axis=1)
    wf = jnp.concatenate([wf, zeros(PF_WIDTH - wf.shape[1])], axis=1)
    assert wb.shape[1] == PB_WIDTH and PB_WIDTH % PROJ_TN == 0
    return wb.astype(BF16), wf.astype(BF16)


A_ROWS2 = Q_BLOCK // 16
A_SPAN1 = 512 // 4 + Q_BLOCK // 4
A_SPAN2 = 144


def _mixa_class_bias():
    x = (np.arange(Q_BLOCK) % A_ROWS2) * 16 + np.arange(Q_BLOCK) // A_ROWS2
    c0 = np.arange(2 * Q_BLOCK)[None, :]
    b0 = np.where((c0 >= x[:, None]) & (c0 <= x[:, None] + Q_BLOCK), 0.0, -np.inf)
    u, a16 = np.arange(32) // A_ROWS2, np.arange(32) % A_ROWS2
    a4 = 4 * a16 + u
    c1 = np.arange(A_SPAN1)[None, :]
    b1 = np.where((c1 >= a4[:, None]) & (c1 <= a4[:, None] + Q_BLOCK), 0.0, -np.inf)
    a2 = np.arange(A_ROWS2)
    c2 = np.arange(A_SPAN2)[None, :]
    b2 = np.where((c2 >= a2[:, None]) & (c2 <= a2[:, None] + Q_BLOCK), 0.0, -np.inf)
    return [jnp.asarray(b.astype(np.float32)) for b in (b0, b1, b2)]


def _mixa2_kernel(q0_ref, q1_ref, q2_ref, k0p_ref, k0_ref, v0p_ref, v0_ref, k1_ref, v1_ref, k2_ref, v2_ref,
                  b0_ref, b1_ref, b2_ref, o_ref):
    pid = pl.program_id(2)
    start = pid * Q_BLOCK
    r8 = A_ROWS2
    scale = HEAD_DIM ** -0.5
    head_of_lane = lax.broadcasted_iota(I32, (1, 2 * HEAD_DIM), 1) // HEAD_DIM
    qf = [q_ref[0].astype(F32) * scale for q_ref in (q0_ref, q1_ref, q2_ref)]

    def masked_bias(b_ref, first_valid):
        col = lax.broadcasted_iota(I32, b_ref.shape, 1)
        return jnp.where(col >= first_valid, b_ref[...], NEG_INF)

    classes = [(qf[0], jnp.concatenate([k0p_ref[0], k0_ref[0]], axis=0),
                jnp.concatenate([v0p_ref[0], v0_ref[0]], axis=0), masked_bias(b0_ref, Q_BLOCK - start))]
    w1 = pl.ds(pl.multiple_of(pid * (Q_BLOCK // 4), Q_BLOCK // 4), A_SPAN1)
    bias1 = masked_bias(b1_ref, Q_BLOCK - pid * (Q_BLOCK // 4))
    for r4 in range(4):
        q1 = jnp.concatenate([qf[1][(r4 + 4 * u) * r8:(r4 + 4 * u + 1) * r8] for u in range(4)], axis=0)
        classes.append((q1, k1_ref[0, r4, w1, :], v1_ref[0, r4, w1, :], bias1))
    w2 = pl.ds(pl.multiple_of(pid * r8, r8), A_SPAN2)
    bias2 = masked_bias(b2_ref, Q_BLOCK - pid * r8)
    for r16 in range(16):
        classes.append((qf[2][r16 * r8:(r16 + 1) * r8], k2_ref[0, r16, w2, :].astype(BF16),
                        v2_ref[0, r16, w2, :].astype(BF16), bias2))

    def rows_of(vals):
        g1 = jnp.concatenate([vals[1 + r16 % 4][(r16 // 4) * r8:(r16 // 4 + 1) * r8] for r16 in range(16)], axis=0)
        return [vals[0], g1, jnp.concatenate(vals[5:21], axis=0)]

    first = head_of_lane == 0

    def stacked(q):
        return jnp.concatenate([jnp.where(first, q, 0.0), jnp.where(first, 0.0, q)], axis=0).astype(BF16)

    s = [_dot_nt(stacked(q), k) + jnp.concatenate([bias, bias], axis=0) for q, k, _, bias in classes]
    m = [jnp.max(si, axis=-1, keepdims=True) for si in s]
    p = [jnp.exp(si - mi) for si, mi in zip(s, m)]
    l = [jnp.sum(pi, axis=-1, keepdims=True) for pi in p]
    o = [_dot(pi.astype(BF16), c[2]) for pi, c in zip(p, classes)]

    res = []
    for hh in range(2):
        half = lambda vals: [v[hh * (v.shape[0] // 2):(hh + 1) * (v.shape[0] // 2)] for v in vals]
        ms, ls, os_ = rows_of(half(m)), rows_of(half(l)), rows_of(half(o))
        m_all = jnp.maximum(jnp.maximum(ms[0], ms[1]), ms[2])
        num = jnp.zeros((Q_BLOCK, 2 * HEAD_DIM), F32)
        den = jnp.zeros((Q_BLOCK, 1), F32)
        for g in range(A_GROUPS):
            wg = jnp.exp(ms[g] - m_all)
            num = num + wg * os_[g]
            den = den + wg * ls[g]
        res.append(num / den)
    o_ref[0] = jnp.where(head_of_lane == 0, res[0], res[1]).astype(o_ref.dtype)


def _mixer_a2(pb, biases, batch, seq):
    nblk = seq // Q_BLOCK
    r8 = A_ROWS2
    gw = 2 * A_WIDTH
    kv1 = jnp.pad(pb[:, :, PB_AKV + gw:PB_AKV + 2 * gw], ((0, 0), (512, 0), (0, 0)))
    kv1 = kv1.reshape(batch, -1, 4, gw).transpose(0, 2, 1, 3)
    kv2 = jnp.pad(pb[:, :, PB_AKV + 2 * gw:PB_AKV + 3 * gw].astype(F32),
                  ((0, 0), (A_MAXWIN, 16 * (A_SPAN2 - 136)), (0, 0)))
    kv2 = kv2.reshape(batch, -1, 16, gw).transpose(0, 2, 1, 3)
    q = pb[:, :, PB_AQ:PB_BQ].reshape(batch, nblk, r8, 16, -1).transpose(0, 1, 3, 2, 4).reshape(batch, seq, -1)
    pair = 2 * HEAD_DIM
    pairs_per_part = A_WIDTH // pair
    q_spec = lambda g: pl.BlockSpec((1, Q_BLOCK, pair), lambda b, hp, i: (b, i, g * pairs_per_part + hp))
    cur = lambda part: pl.BlockSpec((1, Q_BLOCK, pair), lambda b, hp, i: (b, i, PB_AKV // pair + part * pairs_per_part + hp))
    prev = lambda part: pl.BlockSpec((1, Q_BLOCK, pair),
                                     lambda b, hp, i: (b, jnp.maximum(i - 1, 0), PB_AKV // pair + part * pairs_per_part + hp))
    k4 = lambda a: pl.BlockSpec((1, a.shape[1], a.shape[2], pair), lambda b, hp, i: (b, 0, 0, hp))
    v4 = lambda a: pl.BlockSpec((1, a.shape[1], a.shape[2], pair), lambda b, hp, i: (b, 0, 0, pairs_per_part + hp))
    in_specs = [q_spec(0), q_spec(1), q_spec(2), prev(0), cur(0), prev(1), cur(1), k4(kv1), v4(kv1), k4(kv2), v4(kv2)]
    for bias in biases:
        in_specs.append(pl.BlockSpec(bias.shape, lambda b, hp, i: (0, 0)))
    o = pl.pallas_call(
        _mixa2_kernel,
        grid=(batch, A_HEADS // 2, nblk),
        in_specs=in_specs,
        out_specs=pl.BlockSpec((1, Q_BLOCK, pair), lambda b, hp, i: (b, i, hp)),
        out_shape=jax.ShapeDtypeStruct((batch, seq, A_WIDTH), BF16),
        compiler_params=_params("parallel", "parallel", "arbitrary", vmem_mb=48),
        name="mixer_a",
    )(q, q, q, pb, pb, pb, pb, kv1, kv1, kv2, kv2, *biases)
    return o.reshape(batch, nblk, 16, r8, A_WIDTH).transpose(0, 1, 3, 2, 4).reshape(batch, seq, A_WIDTH)


def _mixb_kernel(qb_ref, kb_ref, vb_ref, iq_ref, ik_ref, misc_ref, tri_ref, o_ref, keys_ref, selb_ref, s_ref, *,
                 top_k):
    _, rows, ch = keys_ref.shape
    start = pl.program_id(1) * rows
    nch = (start + rows + ch - 1) // ch
    qpos = start + lax.broadcasted_iota(I32, (rows, 1), 0)
    w = misc_ref[0][:, 0:IDX_HEADS] * (IDX_DIM ** -0.5 * IDX_HEADS ** -0.5)
    iq = iq_ref[0]
    zero_col = jnp.zeros((rows, 1), F32)

    def score_chunk(c, carry):
        off = pl.multiple_of(c * ch, ch)
        ik = ik_ref[0, pl.ds(off, ch), 0:IDX_DIM]
        acc = jnp.zeros((rows, ch), F32)
        for h in range(IDX_HEADS):
            lg = _dot_nt(iq[:, h * IDX_DIM:(h + 1) * IDX_DIM], ik)
            acc = acc + jnp.maximum(lg, 0.0) * w[:, h:h + 1]
        acc = acc + 0.0
        bits = lax.bitcast_convert_type(acc, I32)
        key = bits ^ ((bits >> 31) & 0x7FFFFFFF)
        kpos = off + lax.broadcasted_iota(I32, (rows, ch), 1)
        keys_ref[c] = jnp.where(kpos <= qpos, key, INT_MIN)
        return carry

    lax.fori_loop(0, nch, score_chunk, 0)

    def count(pred):
        def chunk(c, cnt):
            kc = keys_ref[c]
            for j in range(ch // LANES):
                cnt = cnt + jnp.where(pred(kc[:, j * LANES:(j + 1) * LANES]), 1.0, 0.0)
            return cnt
        cnt = lax.fori_loop(0, nch, chunk, jnp.zeros((rows, LANES), F32))
        return jnp.sum(cnt, axis=-1, keepdims=True)

    kf = float(top_k)
    ans = jnp.where(count(lambda kc: kc >= 0) >= kf, 0, INT_MIN).astype(I32)

    def body(it, ans):
        cand = ans + jnp.left_shift(jnp.int32(1), 30 - it)
        return jnp.where(count(lambda kc: kc >= cand) >= kf, cand, ans)

    thr = lax.fori_loop(0, 31, body, ans)
    need = kf - count(lambda kc: kc > thr)

    tw = tri_ref.shape[0]

    def select_chunk(c, run):
        kc_all = keys_ref[c]
        pieces = [kc_all[:, j * tw:(j + 1) * tw] for j in range(ch // tw)]
        eqf = [jnp.where(kc == thr, 1.0, 0.0) for kc in pieces]
        pre = [_dot(e.astype(BF16), tri_ref[...]) for e in eqf]
        for j, kc in enumerate(pieces):
            take = jnp.where(pre[j] + run <= need, eqf[j], 0.0)
            sel = jnp.where(kc > thr, 1.0, take)
            sel = jnp.where(kc != INT_MIN, sel, 0.0)
            selb_ref[c, :, j * tw:(j + 1) * tw] = jnp.where(sel > 0.5, 0.0, NEG_INF)
            run = run + jnp.sum(eqf[j], axis=-1, keepdims=True)
        return run

    def select_chunk_no_ties(c, carry):
        kc = keys_ref[c]
        selb_ref[c] = jnp.where(kc >= thr, jnp.where(kc != INT_MIN, 0.0, NEG_INF), NEG_INF)
        return carry

    excess = jnp.max(count(lambda kc: kc >= thr)) > kf
    lax.cond(excess,
             lambda: lax.fori_loop(0, nch, select_chunk, zero_col),
             lambda: lax.fori_loop(0, nch, select_chunk_no_ties, zero_col))

    heads = range(B_HEADS)
    lane = lax.broadcasted_iota(I32, (1, B_WIDTH), 1) // HEAD_DIM
    qs = (qb_ref[0].astype(F32) * HEAD_DIM ** -0.5).astype(BF16)
    qm_all = jnp.concatenate([jnp.where(lane == h, qs, jnp.zeros_like(qs)) for h in heads], axis=0)
    groups =[slice(j * LANES, (j + 1) * LANES) for j in range(ch // LANES)]

    def per_lane(cols):
        out = cols[B_HEADS - 1]
        for h in reversed(range(B_HEADS - 1)):
            out = jnp.where(lane <= h, cols[h], out)
        return out

    def score_pass(c, mx):
        off = pl.multiple_of(c * ch, ch)
        bias = selb_ref[c]
        kc = kb_ref[0, pl.ds(off, ch), :]
        s_all = _dot_nt(qm_all, kc)
        out = []
        for h in heads:
            s = s_all[h * rows:(h + 1) * rows] + bias
            s_ref[h, c] = s
            m = mx[h]
            for g in groups:
                m = jnp.maximum(m, s[:, g])
            out.append(m)
        return tuple(out)

    mx = lax.fori_loop(0, nch, score_pass, (jnp.full((rows, LANES), NEG_INF, F32),) * B_HEADS)
    m = [jnp.max(mx[h], axis=-1, keepdims=True) for h in heads]

    def value_pass(c, carry):
        off = pl.multiple_of(c * ch, ch)
        vc = vb_ref[0, pl.ds(off, ch), :]
        p = [jnp.exp(s_ref[h, c] - m[h]) for h in heads]
        ls = []
        for h in heads:
            l = carry[h]
            for g in groups:
                l = l + p[h][:, g]
            ls.append(l)
        pv_all = _dot(jnp.concatenate([p[h].astype(BF16) for h in heads], axis=0), vc)
        pv = [pv_all[h * rows:(h + 1) * rows] for h in heads]
        return tuple(ls) + (carry[B_HEADS] + per_lane(pv),)

    init = (jnp.zeros((rows, LANES), F32),) * B_HEADS + (jnp.zeros((rows, B_WIDTH), F32),)
    res = lax.fori_loop(0, nch, value_pass, init)
    l = [jnp.sum(res[h], axis=-1, keepdims=True) for h in heads]
    o_ref[0] = (res[B_HEADS] / per_lane(l)).astype(o_ref.dtype)


def _mixer_b(pb, pf, tri, batch, seq):
    rows = Q_BLOCK
    top_k = min(TOPK_MAX, seq // 4)
    ch = min(512, seq)
    kern = functools.partial(_mixb_kernel, top_k=top_k)
    return pl.pallas_call(
        kern,
        grid=(batch, seq // rows),
        in_specs=[pl.BlockSpec((1, rows, B_WIDTH), lambda b, i: (b, i, PB_BQ // B_WIDTH)),
                  pl.BlockSpec((1, seq, B_WIDTH), lambda b, i: (b, 0, PB_BK // B_WIDTH)),
                  pl.BlockSpec((1, seq, B_WIDTH), lambda b, i: (b, 0, PB_BV // B_WIDTH)),
                  pl.BlockSpec((1, rows, 256), lambda b, i: (b, i, PB_IQ // 256)),
                  pl.BlockSpec((1, seq, LANES), lambda b, i: (b, 0, PB_IK // LANES)),
                  pl.BlockSpec((1, rows, LANES), lambda b, i: (b, i, PF_MISC // LANES)),
                  pl.BlockSpec(tri.shape, lambda b, i: (0, 0))],
        out_specs=pl.BlockSpec((1, rows, B_WIDTH), lambda b, i: (b, i, 0)),
        out_shape=jax.ShapeDtypeStruct((batch, seq, B_WIDTH), BF16),
        scratch_shapes=[pltpu.VMEM((seq // ch, rows, ch), I32), pltpu.VMEM((seq // ch, rows, ch), F32),
                        pltpu.VMEM((B_HEADS, seq // ch, rows, ch), F32)],
        compiler_params=_params("parallel", "arbitrary", vmem_mb=48),
        name="mixer_b",
    )(pb, pb, pb, pb, pb, pf, tri)


def _head_consts():
    lane = np.arange(C_WIDTH)
    same = (lane[:, None] // HEAD_DIM == lane[None, :] // HEAD_DIM).astype(np.float32)
    return jnp.asarray(same)


def _rwkv1_kernel(cin_ref, prev_ref, vf_ref, mu_ref, vec_ref, lora_ref, vl_ref, bd_ref,
                  phi_ref, psi_ref, rp_ref, y0_ref, g_ref, bonus_ref, v_ref, *, rows, use_v_lora):
    i = pl.program_id(1)
    pc = cin_ref[0][:, :C_IN]
    prev = jnp.where(i > 0, prev_ref[0][SUBLANES - 1:SUBLANES, :C_IN], 0.0)
    row = lax.broadcasted_iota(I32, (rows, 1), 0)
    shifted = jnp.where(row == 0, prev, pltpu.roll(pc, 1, 0))
    pc = pc + (shifted - pc) * mu_ref[...]
    r = pc[:, 0:C_WIDTH]
    k = pc[:, C_WIDTH:2 * C_WIDTH]
    v = pc[:, 2 * C_WIDTH:3 * C_WIDTH]
    xl = pc[:, 3 * C_WIDTH:C_IN]
    lane = lax.broadcasted_iota(I32, xl.shape, 1)
    feat = jnp.where(lane < C_LORA_W, jnp.tanh(xl), jnp.where(lane < C_LORA_W + C_LORA_A, xl, _sigmoid(xl)))
    w0, a0, k_k, k_a, r_k = (vec_ref[n:n + 1, :] for n in range(5))
    bd = bd_ref[...]
    bd_sel = bd.astype(BF16)
    w_raw = -_softplus(-(w0 + _dot3(feat, lora_ref[0]))) - 0.5
    a = _sigmoid(a0 + _dot3(feat, lora_ref[1]))
    g_ref[0] = _dot3(feat, lora_ref[2])
    if use_v_lora:
        v0 = vec_ref[5:6, :]
        v = v + (vf_ref[0] - v) * _sigmoid(v0 + _dot3(_dot3(v, vl_ref[0]), vl_ref[1]))
    v_ref[0] = v
    kk = k * k_k
    kk = kk / jnp.maximum(jnp.sqrt(_dot_sel(bd_sel, kk * kk, left=False)), 1e-12)
    k = k * (1.0 + (a - 1.0) * k_a)
    logw = -jnp.exp(w_raw)
    bonus_ref[0] = _dot_sel(bd_sel, r * k * r_k, left=False) * v
    av = -kk
    bv = kk * a

    cc = C_CHUNK
    ri = lax.broadcasted_iota(I32, (cc, cc), 0)
    ci = lax.broadcasted_iota(I32, (cc, cc), 1)
    tril = jnp.where(ci <= ri, 1.0, 0.0).astype(BF16)
    strict = ci < ri
    incl = ci <= ri
    eye_c = jnp.where(ci == ri, 1.0, 0.0)
    lane_w = lax.broadcasted_iota(I32, (1, C_WIDTH), 1)
    ri2 = lax.broadcasted_iota(I32, (C_WIDTH, C_WIDTH), 0)
    ci2 = lax.broadcasted_iota(I32, (C_WIDTH, C_WIDTH), 1)
    eye_w = jnp.where(ri2 == ci2, 1.0, 0.0)
    chunks = range(rows // cc)
    heads = range(C_HEADS)
    pairs = [(c, h) for c in chunks for h in heads]
    mh = [jnp.where(lane_w // HEAD_DIM == h, 1.0, 0.0) for h in heads]
    sl = [slice(c * cc, (c + 1) * cc) for c in chunks]
    cs = [_dot_sel(tril, logw[sl[c]]) for c in chunks]
    cs_end = [cs[c][cc - 1:cc, :] for c in chunks]
    at = [av[sl[c]] * jnp.exp(cs[c] - logw[sl[c]]) for c in chunks]
    rt = [r[sl[c]] * jnp.exp(cs[c]) for c in chunks]
    inv = [jnp.exp(-cs[c]) for c in chunks]
    rhs = [jnp.concatenate([bv[sl[c]] * inv[c], k[sl[c]] * inv[c]], axis=0) for c in chunks]
    tail = [jnp.exp(cs_end[c] - cs[c]) for c in chunks]
    vc = [v[sl[c]] for c in chunks]
    vc_b = [vc[c].astype(BF16) for c in chunks]
    ath = {(c, h): at[c] * mh[h] for c, h in pairs}
    aa = {(c, h): _dot3(jnp.concatenate([ath[c, h], rt[c] * mh[h]], axis=0), rhs[c], nt=True) for c, h in pairs}
    a_ab = {p: jnp.where(strict, aa[p][:cc, :cc], 0.0) for p in pairs}
    a_ak = {p: jnp.where(strict, aa[p][:cc, cc:], 0.0).astype(BF16) for p in pairs}
    a_rb = {p: jnp.where(incl, aa[p][cc:, :cc], 0.0).astype(BF16) for p in pairs}
    a_rk = {p: jnp.where(incl, aa[p][cc:, cc:], 0.0).astype(BF16) for p in pairs}
    x = {p: eye_c + a_ab[p] for p in pairs}
    pw = a_ab
    for _ in range(int(math.log2(cc)) - 1):
        pw = {p: _dot3(pw[p], pw[p]) for p in pairs}
        x = {p: x[p] + _dot3(x[p], pw[p]) for p in pairs}
    akv = {(c, h): _dot(a_ak[c, h], vc_b[c]) for c, h in pairs}
    ap_h = {p: _dot3(x[p], ath[p]) for p in pairs}
    w2_h = {(c, h): _dot3(x[c, h], akv[c, h]) * mh[h] for c, h in pairs}
    rp_h = {p: _dot(a_rb[p], ap_h[p].astype(BF16)) for p in pairs}
    y0_h = {(c, h): (_dot(a_rb[c, h], w2_h[c, h].astype(BF16)) + _dot(a_rk[c, h], vc_b[c])) * mh[h] for c, h in pairs}
    for c in chunks:
        ap = sum(ap_h[c, h] for h in heads)
        w2 = sum(w2_h[c, h] for h in heads)
        rp_ref[0, sl[c], :] = rt[c] + sum(rp_h[c, h] for h in heads)
        y0_ref[0, sl[c], :] = sum(y0_h[c, h] for h in heads)
        bh = bv[sl[c]] * tail[c]
        kh = k[sl[c]] * tail[c]
        phi_ref[0, c] = eye_w * jnp.exp(cs_end[c]) + _dot3(bh.T, ap) * bd
        psi_ref[0, c] = _dot(jnp.concatenate([bh, kh], axis=0).T.astype(BF16),
                             jnp.concatenate([w2, vc[c]], axis=0).astype(BF16)) * bd


def _rwkv2_kernel(phi_ref, psi_ref, rp_ref, y0_ref, g_ref, bonus_ref, gn_ref, bd_ref, o_ref, s_ref):
    @pl.when(pl.program_id(1) == 0)
    def _():
        s_ref[...] = jnp.zeros_like(s_ref)

    states = [s_ref[...]]
    for c in range(phi_ref.shape[1]):
        states.append(_dot3(phi_ref[0, c], states[c]) + psi_ref[0, c])
    s_ref[...] = states[-1]
    cc = C_CHUNK
    y = jnp.concatenate([_dot3(rp_ref[0, c * cc:(c + 1) * cc, :], states[c]) for c in range(phi_ref.shape[1])],
                        axis=0) + y0_ref[0]
    bd_sel = bd_ref[...].astype(BF16)
    mu = _dot_sel(bd_sel, y, left=False) * (1.0 / HEAD_DIM)
    d = y - mu
    var = _dot_sel(bd_sel, d * d, left=False) * (1.0 / HEAD_DIM)
    yn = d * lax.rsqrt(var + C_GN_EPS) * gn_ref[0:1, :] + gn_ref[1:2, :]
    o_ref[0] = ((yn + bonus_ref[0]) * g_ref[0]).astype(o_ref.dtype)


def _mixer_c(pf, v_first, mu, vec, lora, vl, gn, bd, batch, seq, use_v_lora):
    rows = min(256, seq)
    nblk = seq // rows
    nch = seq // C_CHUNK
    cpb = rows // C_CHUNK
    kern = functools.partial(_rwkv1_kernel, rows=rows, use_v_lora=use_v_lora)
    full2 = lambda a: pl.BlockSpec(a.shape, lambda b, i: (0,) * a.ndim)
    seq_spec = pl.BlockSpec((1, rows, C_WIDTH), lambda b, i: (b, i, 0))
    mat_spec = pl.BlockSpec((1, cpb, C_WIDTH, C_WIDTH), lambda b, i: (b, i, 0, 0))
    seq_shape = jax.ShapeDtypeStruct((batch, seq, C_WIDTH), F32)
    mat_shape = jax.ShapeDtypeStruct((batch, nch, C_WIDTH, C_WIDTH), F32)
    prev_blk = rows // SUBLANES
    phi, psi, rp, y0, g, bonus, v = pl.pallas_call(
        kern,
        grid=(batch, nblk),
        in_specs=[pl.BlockSpec((1, rows, 1024), lambda b, i: (b, i, PF_CIN // 1024)),
                  pl.BlockSpec((1, SUBLANES, 1024), lambda b, i: (b, jnp.maximum(i * prev_blk - 1, 0), PF_CIN // 1024)),
                  seq_spec, full2(mu), full2(vec), full2(lora), full2(vl), full2(bd)],
        out_specs=[mat_spec, mat_spec, seq_spec, seq_spec, seq_spec, seq_spec, seq_spec],
        out_shape=[mat_shape, mat_shape, seq_shape, seq_shape, seq_shape, seq_shape, seq_shape],
        compiler_params=_params("parallel", "parallel", vmem_mb=48),
        name="rwkv_chunks",
    )(pf, pf, v_first, mu, vec, lora, vl, bd)
    scan_chunks = min(4, nch)
    cseq = pl.BlockSpec((1, scan_chunks * C_CHUNK, C_WIDTH), lambda b, c: (b, c, 0))
    cmat = pl.BlockSpec((1, scan_chunks, C_WIDTH, C_WIDTH), lambda b, c: (b, c, 0, 0))
    o = pl.pallas_call(
        _rwkv2_kernel,
        grid=(batch, nch // scan_chunks),
        in_specs=[cmat, cmat, cseq, cseq, cseq, cseq,
                  pl.BlockSpec(gn.shape, lambda b, c: (0, 0)), pl.BlockSpec(bd.shape, lambda b, c: (0, 0))],
        out_specs=cseq,
        out_shape=jax.ShapeDtypeStruct((batch, seq, C_WIDTH), BF16),
        scratch_shapes=[pltpu.VMEM((C_WIDTH, C_WIDTH), F32)],
        compiler_params=_params("parallel", "arbitrary"),
        name="rwkv_scan",
    )(phi, psi, rp, y0, g, bonus, gn, bd)
    return o, v


def _ssd_consts():
    expand = np.zeros((LANES, D_INNER), np.float32)
    for h in range(D_HEADS):
        expand[MISC_DT + h, h * HEAD_DIM:(h + 1) * HEAD_DIM] = 1.0
    return jnp.asarray(expand)


def _ssd_kernel(xbc_ref, z_ref, misc_ref, conv_ref, vec_ref, hp_ref, expand_ref, bd_ref, o_ref, st_ref, prev_ref):
    @pl.when(pl.program_id(1) == 0)
    def _():
        st_ref[...] = jnp.zeros_like(st_ref)
        prev_ref[...] = jnp.zeros_like(prev_ref)

    q = D_CHUNK
    x_raw = xbc_ref[0]
    ext = jnp.concatenate([prev_ref[...], x_raw], axis=0)
    conv = jnp.zeros((q, D_XBC), F32)
    for t in range(D_CONV):
        lo = SUBLANES - (D_CONV - 1) + t
        conv = conv + ext[lo:lo + q, :] * conv_ref[t:t + 1, :]
    prev_ref[...] = x_raw[q - SUBLANES:, :]
    xbc = _silu(conv + conv_ref[D_CONV:D_CONV + 1, :])
    xs = xbc[:, :D_INNER]
    bm = xbc[:, D_INNER:D_INNER + D_GROUPS * D_STATE]
    cm = xbc[:, D_INNER + D_GROUPS * D_STATE:]

    dt_col = _softplus(misc_ref[0] + hp_ref[0:1, :])
    a_col = dt_col * hp_ref[1:2, :]
    ri = lax.broadcasted_iota(I32, (q, q), 0)
    ci = lax.broadcasted_iota(I32, (q, q), 1)
    causal = ci <= ri
    tril = jnp.where(causal, 1.0, 0.0).astype(BF16)
    acs_col = _dot_sel(tril, a_col)
    acs_row = acs_col.T
    expand = expand_ref[...].astype(BF16)
    acs = _dot_sel(expand, acs_col, left=False)
    dt = _dot_sel(expand, dt_col, left=False)
    acs_end = acs[q - 1:q, :]
    xdt = xs * dt

    lane = lax.broadcasted_iota(I32, (1, LANES), 1)
    lane_w = lax.broadcasted_iota(I32, (1, D_INNER), 1)
    left = lane < D_STATE
    bm_sw = pltpu.roll(bm, D_STATE, 1)
    cm_sw = pltpu.roll(cm, D_STATE, 1)
    b_exp = jnp.concatenate([jnp.where(left, bm, bm_sw), jnp.where(left, bm_sw, bm)], axis=1)
    c_exp = jnp.concatenate([jnp.where(left, cm, cm_sw), jnp.where(left, cm_sw, cm)], axis=1)
    cb = [_dot3(jnp.where(left == (g == 0), cm, 0.0), bm, nt=True) for g in range(D_GROUPS)]

    scores = []
    for h in range(D_HEADS):
        col = acs_col[:, MISC_DT + h:MISC_DT + h + 1]
        rw = acs_row[MISC_DT + h:MISC_DT + h + 1, :]
        decay = jnp.exp(jnp.where(causal, col - rw, NEG_INF))
        scores.append(cb[h // (D_HEADS // D_GROUPS)] * decay)
    y_h = [_dot3(scores[h], xdt) for h in range(D_HEADS)]
    y = jnp.zeros((q, D_INNER), F32)
    for h in range(D_HEADS):
        y = y + jnp.where(lane_w // HEAD_DIM == h, y_h[h], 0.0)
    st = st_ref[...]
    y = y + _dot3(c_exp, st) * jnp.exp(acs)
    st_ref[...] = st * jnp.exp(acs_end) + _dot3(b_exp.T, xdt * jnp.exp(acs_end - acs)) * bd_ref[...]
    y = y + xs * vec_ref[0:1, :]
    y = y * _silu(z_ref[0])
    half = D_INNER // D_GROUPS
    outs = []
    for g in range(D_GROUPS):
        yg = y[:, g * half:(g + 1) * half]
        outs.append(yg * lax.rsqrt(jnp.mean(yg * yg, axis=-1, keepdims=True) + D_NORM_EPS))
    o_ref[0] = (jnp.concatenate(outs, axis=1) * vec_ref[1:2, :]).astype(o_ref.dtype)


def _mixer_d(pf, conv, vec, hp, expand, bd, batch, seq):
    q = D_CHUNK
    full = lambda a: pl.BlockSpec(a.shape, lambda b, i: (0, 0))
    return pl.pallas_call(
        _ssd_kernel,
        grid=(batch, seq // q),
        in_specs=[pl.BlockSpec((1, q, D_XBC), lambda b, i: (b, i, PF_XBC // D_XBC)),
                  pl.BlockSpec((1, q, D_INNER), lambda b, i: (b, i, PF_Z // D_INNER)),
                  pl.BlockSpec((1, q, LANES), lambda b, i: (b, i, PF_MISC // LANES)),
                  full(conv), full(vec), full(hp), full(expand), full(bd)],
        out_specs=pl.BlockSpec((1, q, D_INNER), lambda b, i: (b, i, 0)),
        out_shape=jax.ShapeDtypeStruct((batch, seq, D_INNER), BF16),
        scratch_shapes=[pltpu.VMEM((D_INNER, D_INNER), F32), pltpu.VMEM((SUBLANES, D_XBC), F32)],
        compiler_params=_params("parallel", "arbitrary"),
        name="ssd",
    )(pf, pf, pf, conv, vec, hp, expand, bd)


def _merge_kernel(x_ref, gates_ref, oa_ref, ob_ref, oc_ref, od_ref, wbr_ref, wout_ref, ln_ref, wr_ref, br_ref,
                  x1_ref, route_ref):
    acc = None
    for n, o_ref in enumerate((oa_ref, ob_ref, oc_ref, od_ref)):
        term = _sigmoid(gates_ref[:, n * D_MODEL:(n + 1) * D_MODEL]) * _dot(o_ref[...], wbr_ref[n])
        acc = term if acc is None else acc + term
    h = DEEPNORM_ALPHA * x_ref[...] + _dot(acc.astype(BF16), wout_ref[...])
    x1 = _layer_norm(h, ln_ref[0:1, :], ln_ref[1:2, :])
    x1_ref[...] = x1

    logits = _dot3(x1, wr_ref[...]) + br_ref[...]
    lane = lax.broadcasted_iota(I32, logits.shape, 1)
    big = jnp.int32(LANES)
    gl = jnp.where(lane < N_EXPERT_GROUPS, logits, NEG_INF)
    gm = jnp.max(gl, axis=-1, keepdims=True)
    pg_top = 1.0 / jnp.sum(jnp.exp(gl - gm), axis=-1, keepdims=True)
    g_sel = jnp.min(jnp.where(gl == gm, lane, big), axis=-1, keepdims=True)
    off = N_EXPERT_GROUPS + g_sel * EXPERTS_PER_GROUP
    el = jnp.where((lane >= off) & (lane < off + EXPERTS_PER_GROUP), logits, NEG_INF)
    em = jnp.max(el, axis=-1, keepdims=True)
    es = jnp.sum(jnp.exp(el - em), axis=-1, keepdims=True)
    idx1 = jnp.min(jnp.where(el == em, lane, big), axis=-1, keepdims=True)
    el2 = jnp.where(lane == idx1, NEG_INF, el)
    em2 = jnp.max(el2, axis=-1, keepdims=True)
    idx2 = jnp.min(jnp.where(el2 == em2, lane, big), axis=-1, keepdims=True)
    p1 = 1.0 / es
    p2 = jnp.exp(em2 - em) / es
    gate1 = pg_top * p1 / (p1 + p2)
    gate2 = pg_top * p2 / (p1 + p2)
    e1 = (idx1 - N_EXPERT_GROUPS).astype(F32)
    e2 = (idx2 - N_EXPERT_GROUPS).astype(F32)
    route_ref[...] = jnp.where(lane == 0, e1, jnp.where(lane == 1, e2, jnp.where(lane == 2, gate1,
                               jnp.where(lane == 3, gate2, 0.0))))


def _merge(x2d, pf, oa, ob, oc, od, wbr, wout, ln, wr, br, tm):
    m = x2d.shape[0]
    row = lambda w: pl.BlockSpec((tm, w), lambda i: (i, 0))
    full = lambda a: pl.BlockSpec(a.shape, lambda i: (0,) * a.ndim)
    return pl.pallas_call(
        _merge_kernel,
        grid=(m // tm,),
        in_specs=[row(D_MODEL), pl.BlockSpec((tm, N_BRANCH * D_MODEL), lambda i: (i, PF_GATES)),
                  row(A_WIDTH), row(B_WIDTH), row(C_WIDTH), row(D_INNER),
                  full(wbr), full(wout), full(ln), full(wr), full(br)],
        out_specs=[row(D_MODEL), row(LANES)],
        out_shape=[jax.ShapeDtypeStruct((m, D_MODEL), F32), jax.ShapeDtypeStruct((m, LANES), F32)],
        compiler_params=_params("parallel", vmem_mb=48),
        name="merge_route",
    )(x2d, pf, oa, ob, oc, od, wbr, wout, ln, wr, br)


def _ffn_kernel(be_ref, nv_ref, xs_ref, wg_ref, wu_ref, wd_ref, o_ref):
    i = pl.program_id(0)

    @pl.when(i < nv_ref[0])
    def _():
        xb = xs_ref[...]
        hg = _dot(xb, wg_ref[0, 0].astype(BF16))
        hu = _dot(xb, wu_ref[0, 0].astype(BF16))
        o_ref[...] = _dot((_silu(hg) * hu).astype(BF16), wd_ref[0, 0].astype(BF16))

    @pl.when(i >= nv_ref[0])
    def _():
        o_ref[...] = jnp.zeros_like(o_ref)


def _ffn(blk_expert, n_valid, xs, e_gate, e_up, e_down, layer):
    cap = xs.shape[0]
    wspec = lambda a: pl.BlockSpec((1, 1) + a.shape[2:], lambda i, be, nv: (layer, be[i], 0, 0))
    return pl.pallas_call(
        _ffn_kernel,
        grid_spec=pltpu.PrefetchScalarGridSpec(
            num_scalar_prefetch=2,
            grid=(cap // MOE_ROWS,),
            in_specs=[pl.BlockSpec((MOE_ROWS, D_MODEL), lambda i, be, nv: (i, 0)),
                      wspec(e_gate), wspec(e_up), wspec(e_down)],
            out_specs=pl.BlockSpec((MOE_ROWS, D_MODEL), lambda i, be, nv: (i, 0)),
        ),
        out_shape=jax.ShapeDtypeStruct((cap, D_MODEL), F32),
        compiler_params=_params("arbitrary", vmem_mb=48),
        name="expert_ffn",
    )(blk_expert, n_valid, xs, e_gate, e_up, e_down)


def _combine_kernel(dest_ref, x_ref, route_ref, ln_ref, yb_ref, o_ref, ybuf, sem):
    i = pl.program_id(0)
    tm = x_ref.shape[0]

    def gather(tile, slot):
        base = tile * (TOP_K_EXPERTS * tm)

        def body(r, carry):
            for s in range(TOP_K_EXPERTS):
                row = dest_ref[base + TOP_K_EXPERTS * r + s]
                pltpu.make_async_copy(yb_ref.at[pl.ds(row, 1)], ybuf.at[slot, pl.ds(s * tm + r, 1)],
                                      sem.at[slot]).start()
            return carry

        lax.fori_loop(0, tm, body, 0, unroll=8)

    @pl.when(i == 0)
    def _():
        gather(0, 0)

    @pl.when(i + 1 < pl.num_programs(0))
    def _():
        gather(i + 1, (i + 1) % 2)

    slot = i % 2
    pltpu.make_async_copy(yb_ref.at[pl.ds(0, TOP_K_EXPERTS * tm)], ybuf.at[slot], sem.at[slot]).wait()
    g0 = route_ref[:, 2:3]
    g1 = route_ref[:, 3:4]
    h = DEEPNORM_ALPHA * x_ref[...] + (ybuf[slot, 0:tm, :] * g0 + ybuf[slot, tm:2 * tm, :] * g1)
    o_ref[...] = _layer_norm(h, ln_ref[0:1, :], ln_ref[1:2, :])


def _combine(x1, yb, dest, route, ln, tm):
    m = x1.shape[0]
    return pl.pallas_call(
        _combine_kernel,
        grid_spec=pltpu.PrefetchScalarGridSpec(
            num_scalar_prefetch=1,
            grid=(m // tm,),
            in_specs=[pl.BlockSpec((tm, D_MODEL), lambda i, d: (i, 0)),
                      pl.BlockSpec((tm, LANES), lambda i, d: (i, 0)),
                      pl.BlockSpec(ln.shape, lambda i, d: (0, 0)),
                      pl.BlockSpec(memory_space=pl.ANY)],
            out_specs=pl.BlockSpec((tm, D_MODEL), lambda i, d: (i, 0)),
            scratch_shapes=[pltpu.VMEM((2, TOP_K_EXPERTS * tm, D_MODEL), F32), pltpu.SemaphoreType.DMA((2,))],
        ),
        out_shape=jax.ShapeDtypeStruct((m, D_MODEL), F32),
        compiler_params=_params("arbitrary"),
        name="combine",
    )(dest, x1, route, ln, yb)


def _dispatch_tables(route, m):
    flat_e = route[:, 0:TOP_K_EXPERTS].astype(I32).reshape(-1)
    n_assign = m * TOP_K_EXPERTS
    onehot = (flat_e[:, None] == jnp.arange(N_EXPERTS, dtype=I32)[None, :]).astype(I32)
    csum = jnp.cumsum(onehot, axis=0)
    rank = jnp.sum(csum * onehot, axis=1) - 1
    counts = csum[-1]
    padded = (counts + MOE_ROWS - 1) // MOE_ROWS * MOE_ROWS
    pad_end = jnp.cumsum(padded)
    pad_start = pad_end - padded
    dest = pad_start[flat_e] + rank
    cap = (n_assign + N_EXPERTS * (MOE_ROWS - 1) + MOE_ROWS - 1) // MOE_ROWS * MOE_ROWS
    n_blocks = cap // MOE_ROWS
    blk_start = jnp.arange(n_blocks, dtype=I32) * MOE_ROWS
    blk_expert = jnp.minimum(jnp.sum((pad_end[None, :] <= blk_start[:, None]).astype(I32), axis=1), N_EXPERTS - 1)
    buf_tok = (jnp.arange(cap, dtype=I32) % m).at[dest].set(jnp.arange(n_assign, dtype=I32) // TOP_K_EXPERTS)
    n_valid = (pad_end[-1] // MOE_ROWS).astype(I32).reshape(1)
    return dest, buf_tok, blk_expert, n_valid


def _row_pad(a, rows):
    return jnp.pad(a, ((0, rows - a.shape[0]), (0, 0)))


def kernel(x, w_in, c_mu, c_w0, c_w2, c_a0, c_a2, c_g2, c_kk, c_ka, c_rk, c_gn_w, c_gn_b, c_v0, c_v1, c_v2,
           d_conv_w, d_conv_b, d_dt_bias, d_a_log, d_skip, d_norm_w, w_branch, w_out, ln1_g, ln1_b,
           r_group, r_group_b, r_expert, r_expert_b, e_gate, e_up, e_down, ln2_g, ln2_b):
    batch, seq, _ = x.shape
    m = batch * seq
    biases = _mixa_class_bias()
    tw = min(256, seq)
    tri = jnp.asarray(np.triu(np.ones((tw, tw), np.float32))).astype(BF16)
    bd = _head_consts()
    expand = _ssd_consts()
    x2d = x.reshape(m, D_MODEL)
    v_first = jnp.zeros((batch, seq, C_WIDTH), F32)
    tm_proj = min(2048, m)
    tm_tok = min(512, m)
    for l in range(DEPTH):
        wb, wf = _proj_weights(w_in[l])
        pb = _proj(x2d, wb, BF16, tm_proj, PROJ_TN).reshape(batch, seq, PB_WIDTH)
        pf = _proj(x2d, wf, F32, tm_proj, PROJ_TN).reshape(batch, seq, PF_WIDTH)

        o_a = _mixer_a2(pb, biases, batch, seq)
        o_b = _mixer_b(pb, pf, tri, batch, seq)

        use_v_lora = l > 0
        vec_rows = [c_w0[l], c_a0[l], c_kk[l], c_ka[l], c_rk[l].reshape(-1)]
        vec_rows.append(c_v0[l - 1] if use_v_lora else jnp.zeros((C_WIDTH,), F32))
        vec = _row_pad(jnp.stack(vec_rows), SUBLANES)
        lora = jnp.stack([
            jnp.pad(c_w2[l], ((0, LANES - C_LORA_W), (0, 0))),
            jnp.pad(c_a2[l], ((C_LORA_W, LANES - C_LORA_W - C_LORA_A), (0, 0))),
            jnp.pad(c_g2[l], ((C_LORA_W + C_LORA_A, 0), (0, 0)))])
        if use_v_lora:
            vl = jnp.stack([jnp.pad(c_v1[l - 1], ((0, 0), (0, C_WIDTH - C_LORA_V))),
                            jnp.pad(c_v2[l - 1], ((0, C_WIDTH - C_LORA_V), (0, 0)))])
        else:
            vl = jnp.zeros((2, C_WIDTH, C_WIDTH), F32)
        gn = _row_pad(jnp.stack([c_gn_w[l], c_gn_b[l]]), SUBLANES)
        o_c, v_c = _mixer_c(pf, v_first, c_mu[l].reshape(1, C_IN), vec, lora, vl, gn, bd, batch, seq, use_v_lora)
        if l == 0:
            v_first = v_c

        conv = _row_pad(jnp.concatenate([d_conv_w[l], d_conv_b[l][None, :]], axis=0), SUBLANES)
        dvec = _row_pad(jnp.stack([jnp.repeat(d_skip[l], HEAD_DIM), d_norm_w[l]]), SUBLANES)
        place = lambda a: jnp.pad(a, (MISC_DT, LANES - MISC_DT - D_HEADS))
        hp = _row_pad(jnp.stack([place(d_dt_bias[l]), place(-jnp.exp(d_a_log[l]))]), SUBLANES)
        o_d = _mixer_d(pf, conv, dvec, hp, expand, bd, batch, seq)

        wr = jnp.pad(jnp.concatenate([r_group[l], r_expert[l]], axis=1),
                     ((0, 0), (0, LANES - N_EXPERT_GROUPS - N_EXPERTS)))
        br = jnp.pad(jnp.concatenate([r_group_b[l], r_expert_b[l]]), (0, LANES - N_EXPERT_GROUPS - N_EXPERTS))
        ln1 = _row_pad(jnp.stack([ln1_g[l], ln1_b[l]]), SUBLANES)
        x1, route = _merge(x2d, pf.reshape(m, PF_WIDTH), o_a.reshape(m, -1), o_b.reshape(m, -1),
                           o_c.reshape(m, -1), o_d.reshape(m, -1), w_branch[l].astype(BF16),
                           w_out[l].astype(BF16), ln1, wr, br.reshape(1, LANES), tm_tok)

        dest, buf_tok, blk_expert, n_valid = _dispatch_tables(route, m)
        xs = jnp.take(x1.astype(BF16), buf_tok, axis=0)
        yb = _ffn(blk_expert, n_valid, xs, e_gate, e_up, e_down, l)
        ln2 = _row_pad(jnp.stack([ln2_g[l], ln2_b[l]]), SUBLANES)
        x2d = _combine(x1, yb, dest, route, ln2, min(256, m))
    return x2d.reshape(batch, seq, D_MODEL)
```

```python
import functools
import math
import types

import jax
import jax.numpy as jnp
import numpy as np
from jax import lax
from jax.experimental import pallas as pl
from jax.experimental.pallas import tpu as pltpu

F32 = jnp.float32
BF16 = jnp.bfloat16
I32 = jnp.int32
NEG_INF = float("-inf")
INT_MIN = -(2 ** 31)

LANES = 128
SUBLANES = 8

D_MODEL = 1024
DEPTH = 2
HEAD_DIM = 64
Q_BLOCK = 128

A_HEADS = 4
A_PATTERNS = ((128, 1), (512, 4), (2048, 16))
A_GROUPS = len(A_PATTERNS)
A_WIDTH = A_HEADS * HEAD_DIM
A_MAXWIN = max(w for w, _ in A_PATTERNS)

B_HEADS = 4
B_WIDTH = B_HEADS * HEAD_DIM
IDX_HEADS = 4
IDX_DIM = 64
TOPK_MAX = 256

C_HEADS = 4
C_WIDTH = C_HEADS * HEAD_DIM
C_LORA_W = 32
C_LORA_A = 32
C_LORA_G = 64
C_LORA_V = 16
C_IN = 3 * C_WIDTH + C_LORA_W + C_LORA_A + C_LORA_G
C_GN_EPS = 64e-5
C_CHUNK = 64

D_HEADS = 4
D_INNER = D_HEADS * HEAD_DIM
D_GROUPS = 2
D_STATE = 64
D_CONV = 4
D_CHUNK = 128
D_XBC = D_INNER + 2 * D_GROUPS * D_STATE
D_NORM_EPS = 1e-5

N_BRANCH = 4
N_EXPERT_GROUPS = 4
EXPERTS_PER_GROUP = 8
N_EXPERTS = N_EXPERT_GROUPS * EXPERTS_PER_GROUP
TOP_K_EXPERTS = 2
D_EXPERT = 512
MOE_ROWS = 512

LN_EPS = 1e-5
DEEPNORM_ALPHA = (2 * DEPTH) ** 0.25

PB_AKV = 0
PB_AQ = PB_AKV + A_GROUPS * 2 * A_WIDTH
PB_BQ = PB_AQ + A_GROUPS * A_WIDTH
PB_BK = PB_BQ + B_WIDTH
PB_BV = PB_BK + B_WIDTH
PB_IQ = PB_BV + B_WIDTH
PB_IK = PB_IQ + IDX_HEADS * IDX_DIM
PB_WIDTH = PB_IK + 512
PF_GATES = 0
PF_CIN = PF_GATES + N_BRANCH * D_MODEL
PF_MISC = PF_CIN + C_IN
PF_XBC = PF_CIN + 1024
PF_Z = PF_XBC + D_XBC
PROJ_TN = 768
PF_WIDTH = -(-(PF_Z + D_INNER) // PROJ_TN) * PROJ_TN
MISC_DT = 4


def _dot(a, b, precision=None):
    return jnp.dot(a, b, preferred_element_type=F32, precision=precision)


def _dot_nt(a, b, precision=None):
    return lax.dot_general(a, b, (((1,), (1,)), ((), ())), preferred_element_type=F32, precision=precision)


def _split2(a):
    hi = a.astype(BF16)
    return hi, (a - hi.astype(F32)).astype(BF16)


def _dot3(a, b, nt=False):
    mm = _dot_nt if nt else _dot
    ah, al = _split2(a)
    bh, bl = _split2(b)
    return mm(ah, bh) + (mm(ah, bl) + mm(al, bh))


def _dot_sel(sel, x, left=True):
    hi = x.astype(BF16)
    r1 = x - hi.astype(F32)
    mid = r1.astype(BF16)
    lo = (r1 - mid.astype(F32)).astype(BF16)
    if left:
        return _dot(sel, hi) + (_dot(sel, mid) + _dot(sel, lo))
    return _dot(hi, sel) + (_dot(mid, sel) + _dot(lo, sel))


def _sigmoid(x):
    return 1.0 / (1.0 + jnp.exp(-x))


def _softplus(x):
    return jnp.maximum(x, 0.0) + jnp.log1p(jnp.exp(-jnp.abs(x)))


def _silu(x):
    return x * _sigmoid(x)


def _layer_norm(h, g, b):
    mu = jnp.mean(h, axis=-1, keepdims=True)
    d = h - mu
    var = jnp.mean(d * d, axis=-1, keepdims=True)
    return d * lax.rsqrt(var + LN_EPS) * g + b


def _params(*sem, vmem_mb=None):
    kw = {}
    if vmem_mb is not None:
        kw["vmem_limit_bytes"] = vmem_mb * 1024 * 1024
    return pltpu.CompilerParams(dimension_semantics=sem, **kw)


def _proj_kernel(x_ref, w_ref, o_ref):
    o_ref[...] = _dot(x_ref[...].astype(BF16), w_ref[...]).astype(o_ref.dtype)


def _proj(x2d, w, out_dtype, tm, tn):
    m, k = x2d.shape
    n = w.shape[1]
    return pl.pallas_call(
        _proj_kernel,
        grid=(m // tm, n // tn),
        in_specs=[pl.BlockSpec((tm, k), lambda i, j: (i, 0)),
                  pl.BlockSpec((k, tn), lambda i, j: (0, j))],
        out_specs=pl.BlockSpec((tm, tn), lambda i, j: (i, j)),
        out_shape=jax.ShapeDtypeStruct((m, n), out_dtype),
        compiler_params=_params("parallel", "arbitrary", vmem_mb=48),
        name="in_proj",
    )(x2d, w)


def _proj_weights(w):
    k = w.shape[0]
    off = 0
    seg = {}
    for name, size in (("a_qkv", 3 * A_GROUPS * A_WIDTH), ("b_qkv", 3 * B_WIDTH), ("b_idx_q", IDX_HEADS * IDX_DIM),
                       ("b_idx_k", IDX_DIM), ("b_idx_w", IDX_HEADS), ("c_in", C_IN), ("d_z", D_INNER),
                       ("d_xbc", D_XBC), ("d_dt", D_HEADS), ("gates", N_BRANCH * D_MODEL)):
        seg[name] = w[:, off:off + size]
        off += size
    a = seg["a_qkv"].reshape(k, 3, A_GROUPS, A_HEADS, HEAD_DIM)
    a_kv = jnp.transpose(a[:, 1:3], (0, 2, 1, 3, 4)).reshape(k, -1)
    a_q = a[:, 0].reshape(k, -1)
    zeros = lambda n: jnp.zeros((k, n), w.dtype)
    wb = jnp.concatenate([a_kv, a_q, seg["b_qkv"], seg["b_idx_q"], seg["b_idx_k"],
                          zeros(PB_WIDTH - PB_IK - IDX_DIM)], axis=1)
    misc = jnp.concatenate([seg["b_idx_w"], seg["d_dt"], zeros(LANES - IDX_HEADS - D_HEADS)], axis=1)
    wf = jnp.concatenate([seg["gates"], seg["c_in"], misc, seg["d_xbc"], seg["d_z"]], axis=1)
    wf = jnp.concatenate([wf, zeros(PF_WIDTH - wf.shape[1])], axis=1)
    assert wb.shape[1] == PB_WIDTH and PB_WIDTH % PROJ_TN == 0
    return wb.astype(BF16), wf.astype(BF16)


A_ROWS2 = Q_BLOCK // 16
A_SPAN1 = 512 // 4 + Q_BLOCK // 4
A_SPAN2 = 144


def _mixa_class_bias():
    x = (np.arange(Q_BLOCK) % A_ROWS2) * 16 + np.arange(Q_BLOCK) // A_ROWS2
    c0 = np.arange(2 * Q_BLOCK)[None, :]
    b0 = np.where((c0 >= x[:, None]) & (c0 <= x[:, None] + Q_BLOCK), 0.0, -np.inf)
    u, a16 = np.arange(32) // A_ROWS2, np.arange(32) % A_ROWS2
    a4 = 4 * a16 + u
    c1 = np.arange(A_SPAN1)[None, :]
    b1 = np.where((c1 >= a4[:, None]) & (c1 <= a4[:, None] + Q_BLOCK), 0.0, -np.inf)
    a2 = np.arange(A_ROWS2)
    c2 = np.arange(A_SPAN2)[None, :]
    b2 = np.where((c2 >= a2[:, None]) & (c2 <= a2[:, None] + Q_BLOCK), 0.0, -np.inf)
    return [jnp.asarray(b.astype(np.float32)) for b in (b0, b1, b2)]


def _mixa2_kernel(q0_ref, q1_ref, q2_ref, k0p_ref, k0_ref, v0p_ref, v0_ref, k1_ref, v1_ref, k2_ref, v2_ref,
                  b0_ref, b1_ref, b2_ref, o_ref):
    pid = pl.program_id(2)
    start = pid * Q_BLOCK
    r8 = A_ROWS2
    scale = HEAD_DIM ** -0.5
    head_of_lane = lax.broadcasted_iota(I32, (1, 2 * HEAD_DIM), 1) // HEAD_DIM
    qf = [q_ref[0].astype(F32) * scale for q_ref in (q0_ref, q1_ref, q2_ref)]

    def masked_bias(b_ref, first_valid):
        col = lax.broadcasted_iota(I32, b_ref.shape, 1)
        return jnp.where(col >= first_valid, b_ref[...], NEG_INF)

    classes = [(qf[0], jnp.concatenate([k0p_ref[0], k0_ref[0]], axis=0),
                jnp.concatenate([v0p_ref[0], v0_ref[0]], axis=0), masked_bias(b0_ref, Q_BLOCK - start))]
    w1 = pl.ds(pl.multiple_of(pid * (Q_BLOCK // 4), Q_BLOCK // 4), A_SPAN1)
    bias1 = masked_bias(b1_ref, Q_BLOCK - pid * (Q_BLOCK // 4))
    for r4 in range(4):
        q1 = jnp.concatenate([qf[1][(r4 + 4 * u) * r8:(r4 + 4 * u + 1) * r8] for u in range(4)], axis=0)
        classes.append((q1, k1_ref[0, r4, w1, :], v1_ref[0, r4, w1, :], bias1))
    w2 = pl.ds(pl.multiple_of(pid * r8, r8), A_SPAN2)
    bias2 = masked_bias(b2_ref, Q_BLOCK - pid * r8)
    for r16 in range(16):
        classes.append((qf[2][r16 * r8:(r16 + 1) * r8], k2_ref[0, r16, w2, :].astype(BF16),
                        v2_ref[0, r16, w2, :].astype(BF16), bias2))

    def rows_of(vals):
        g1 = jnp.concatenate([vals[1 + r16 % 4][(r16 // 4) * r8:(r16 // 4 + 1) * r8] for r16 in range(16)], axis=0)
        return [vals[0], g1, jnp.concatenate(vals[5:21], axis=0)]

    first = head_of_lane == 0

    def stacked(q):
        return jnp.concatenate([jnp.where(first, q, 0.0), jnp.where(first, 0.0, q)], axis=0).astype(BF16)

    s = [_dot_nt(stacked(q), k) + jnp.concatenate([bias, bias], axis=0) for q, k, _, bias in classes]
    m = [jnp.max(si, axis=-1, keepdims=True) for si in s]
    p = [jnp.exp(si - mi) for si, mi in zip(s, m)]
    l = [jnp.sum(pi, axis=-1, keepdims=True) for pi in p]
    o = [_dot(pi.astype(BF16), c[2]) for pi, c in zip(p, classes)]

    res = []
    for hh in range(2):
        half = lambda vals: [v[hh * (v.shape[0] // 2):(hh + 1) * (v.shape[0] // 2)] for v in vals]
        ms, ls, os_ = rows_of(half(m)), rows_of(half(l)), rows_of(half(o))
        m_all = jnp.maximum(jnp.maximum(ms[0], ms[1]), ms[2])
        num = jnp.zeros((Q_BLOCK, 2 * HEAD_DIM), F32)
        den = jnp.zeros((Q_BLOCK, 1), F32)
        for g in range(A_GROUPS):
            wg = jnp.exp(ms[g] - m_all)
            num = num + wg * os_[g]
            den = den + wg * ls[g]
        res.append(num / den)
    o_ref[0] = jnp.where(head_of_lane == 0, res[0], res[1]).astype(o_ref.dtype)


def _mixer_a2(pb, biases, batch, seq):
    nblk = seq // Q_BLOCK
    r8 = A_ROWS2
    gw = 2 * A_WIDTH
    kv1 = jnp.pad(pb[:, :, PB_AKV + gw:PB_AKV + 2 * gw], ((0, 0), (512, 0), (0, 0)))
    kv1 = kv1.reshape(batch, -1, 4, gw).transpose(0, 2, 1, 3)
    kv2 = jnp.pad(pb[:, :, PB_AKV + 2 * gw:PB_AKV + 3 * gw].astype(F32),
                  ((0, 0), (A_MAXWIN, 16 * (A_SPAN2 - 136)), (0, 0)))
    kv2 = kv2.reshape(batch, -1, 16, gw).transpose(0, 2, 1, 3)
    q = pb[:, :, PB_AQ:PB_BQ].reshape(batch, nblk, r8, 16, -1).transpose(0, 1, 3, 2, 4).reshape(batch, seq, -1)
    pair = 2 * HEAD_DIM
    pairs_per_part = A_WIDTH // pair
    q_spec = lambda g: pl.BlockSpec((1, Q_BLOCK, pair), lambda b, hp, i: (b, i, g * pairs_per_part + hp))
    cur = lambda part: pl.BlockSpec((1, Q_BLOCK, pair), lambda b, hp, i: (b, i, PB_AKV // pair + part * pairs_per_part + hp))
    prev = lambda part: pl.BlockSpec((1, Q_BLOCK, pair),
                                     lambda b, hp, i: (b, jnp.maximum(i - 1, 0), PB_AKV // pair + part * pairs_per_part + hp))
    k4 = lambda a: pl.BlockSpec((1, a.shape[1], a.shape[2], pair), lambda b, hp, i: (b, 0, 0, hp))
    v4 = lambda a: pl.BlockSpec((1, a.shape[1], a.shape[2], pair), lambda b, hp, i: (b, 0, 0, pairs_per_part + hp))
    in_specs = [q_spec(0), q_spec(1), q_spec(2), prev(0), cur(0), prev(1), cur(1), k4(kv1), v4(kv1), k4(kv2), v4(kv2)]
    for bias in biases:
        in_specs.append(pl.BlockSpec(bias.shape, lambda b, hp, i: (0, 0)))
    o = pl.pallas_call(
        _mixa2_kernel,
        grid=(batch, A_HEADS // 2, nblk),
        in_specs=in_specs,
        out_specs=pl.BlockSpec((1, Q_BLOCK, pair), lambda b, hp, i: (b, i, hp)),
        out_shape=jax.ShapeDtypeStruct((batch, seq, A_WIDTH), BF16),
        compiler_params=_params("parallel", "parallel", "arbitrary", vmem_mb=48),
        name="mixer_a",
    )(q, q, q, pb, pb, pb, pb, kv1, kv1, kv2, kv2, *biases)
    return o.reshape(batch, nblk, 16, r8, A_WIDTH).transpose(0, 1, 3, 2, 4).reshape(batch, seq, A_WIDTH)


def _mixb_kernel(qb_ref, kb_ref, vb_ref, iq_ref, ik_ref, misc_ref, tri_ref, o_ref, keys_ref, selb_ref, s_ref, *,
                 top_k):
    rows = Q_BLOCK
    nparts = keys_ref.shape[1] // rows
    kf = float(top_k)
    parts = [_mixb_part(pt, rows, kf, qb_ref, kb_ref, vb_ref, iq_ref, ik_ref, misc_ref, tri_ref, o_ref,
                        keys_ref, selb_ref, s_ref) for pt in range(nparts)]
    for part in parts:
        part.score()
    ans = tuple(jnp.where(part.total(part.count(lambda kc: kc >= 0)) >= kf, 0, INT_MIN).astype(I32) for part in parts)

    def body(it, ans):
        bit = jnp.left_shift(jnp.int32(1), 30 - it)
        cands = [a + bit for a in ans]
        raw = [part.count(lambda kc, cand=cand: kc >= cand) for part, cand in zip(parts, cands)]
        return tuple(jnp.where(part.total(r) >= kf, cand, a) for part, r, cand, a in zip(parts, raw, cands, ans))

    thr = lax.fori_loop(0, 31, body, ans)
    for part, t in zip(parts, thr):
        part.attend(t)


def _mixb_part(pt, rows, kf, qb_ref, kb_ref, vb_ref, iq_ref, ik_ref, misc_ref, tri_ref, o_ref,
               keys_ref, selb_ref, s_ref):
    ch = keys_ref.shape[2]
    rs = pl.ds(pt * rows, rows)
    start = pl.program_id(1) * keys_ref.shape[1] + pt * rows
    nch = (start + rows + ch - 1) // ch
    qpos = start + lax.broadcasted_iota(I32, (rows, 1), 0)
    zero_col = jnp.zeros((rows, 1), F32)
    part = types.SimpleNamespace()

    def score():
        w = misc_ref[0, rs, 0:IDX_HEADS] * (IDX_DIM ** -0.5 * IDX_HEADS ** -0.5)
        iq = iq_ref[0, rs, :]
        lax.fori_loop(0, nch, functools.partial(score_chunk, w, iq), 0)

    def score_chunk(w, iq, c, carry):
        off = pl.multiple_of(c * ch, ch)
        ik = ik_ref[0, pl.ds(off, ch), 0:IDX_DIM]
        acc = jnp.zeros((rows, ch), F32)
        for h in range(IDX_HEADS):
            lg = _dot_nt(iq[:, h * IDX_DIM:(h + 1) * IDX_DIM], ik)
            acc = acc + jnp.maximum(lg, 0.0) * w[:, h:h + 1]
        acc = acc + 0.0
        bits = lax.bitcast_convert_type(acc, I32)
        key = bits ^ ((bits >> 31) & 0x7FFFFFFF)
        kpos = off + lax.broadcasted_iota(I32, (rows, ch), 1)
        keys_ref[c, rs, :] = jnp.where(kpos <= qpos, key, INT_MIN)
        return carry

    def count(pred):
        def chunk(c, cnt):
            kc = keys_ref[c, rs, :]
            for j in range(ch // LANES):
                cnt = cnt + jnp.where(pred(kc[:, j * LANES:(j + 1) * LANES]), 1.0, 0.0)
            return cnt
        return lax.fori_loop(0, nch, chunk, jnp.zeros((rows, LANES), F32))

    def total(cnt):
        return jnp.sum(cnt, axis=-1, keepdims=True)

    part.score, part.count, part.total = score, count, total
    part.attend = functools.partial(_mixb_attend, part, pt, rows, kf, nch, zero_col, qb_ref, kb_ref, vb_ref,
                                    tri_ref, o_ref, keys_ref, selb_ref, s_ref)
    return part


def _mixb_attend(part, pt, rows, kf, nch, zero_col, qb_ref, kb_ref, vb_ref, tri_ref, o_ref, keys_ref, selb_ref, s_ref,
                 thr):
    ch = keys_ref.shape[2]
    rs = pl.ds(pt * rows, rows)
    count = lambda pred: part.total(part.count(pred))
    need = kf - count(lambda kc: kc > thr)

    tw = tri_ref.shape[0]

    def select_chunk(c, run):
        kc_all = keys_ref[c, rs, :]
        pieces = [kc_all[:, j * tw:(j + 1) * tw] for j in range(ch // tw)]
        eqf = [jnp.where(kc == thr, 1.0, 0.0) for kc in pieces]
        pre = [_dot(e.astype(BF16), tri_ref[...]) for e in eqf]
        for j, kc in enumerate(pieces):
            take = jnp.where(pre[j] + run <= need, eqf[j], 0.0)
            sel = jnp.where(kc > thr, 1.0, take)
            sel = jnp.where(kc != INT_MIN, sel, 0.0)
            selb_ref[c, rs, j * tw:(j + 1) * tw] = jnp.where(sel > 0.5, 0.0, NEG_INF)
            run = run + jnp.sum(eqf[j], axis=-1, keepdims=True)
        return run

    def select_chunk_no_ties(c, carry):
        kc = keys_ref[c, rs, :]
        selb_ref[c, rs, :] = jnp.where(kc >= thr, jnp.where(kc != INT_MIN, 0.0, NEG_INF), NEG_INF)
        return carry

    excess = jnp.max(count(lambda kc: kc >= thr)) > kf
    lax.cond(excess,
             lambda: lax.fori_loop(0, nch, select_chunk, zero_col),
             lambda: lax.fori_loop(0, nch, select_chunk_no_ties, zero_col))

    heads = range(B_HEADS)
    lane = lax.broadcasted_iota(I32, (1, B_WIDTH), 1) // HEAD_DIM
    qs = (qb_ref[0, rs, :].astype(F32) * HEAD_DIM ** -0.5).astype(BF16)
    qm_all = jnp.concatenate([jnp.where(lane == h, qs, jnp.zeros_like(qs)) for h in heads], axis=0)
    groups =[slice(j * LANES, (j + 1) * LANES) for j in range(ch // LANES)]

    def per_lane(cols):
        out = cols[B_HEADS - 1]
        for h in reversed(range(B_HEADS - 1)):
            out = jnp.where(lane <= h, cols[h], out)
        return out

    def score_pass(c, mx):
        off = pl.multiple_of(c * ch, ch)
        bias = selb_ref[c, rs, :]
        kc = kb_ref[0, pl.ds(off, ch), :]
        s_all = _dot_nt(qm_all, kc)
        out = []
        for h in heads:
            s = s_all[h * rows:(h + 1) * rows] + bias
            s_ref[h, c, rs, :] = s
            m = mx[h]
            for g in groups:
                m = jnp.maximum(m, s[:, g])
            out.append(m)
        return tuple(out)

    mx = lax.fori_loop(0, nch, score_pass, (jnp.full((rows, LANES), NEG_INF, F32),) * B_HEADS)
    m = [jnp.max(mx[h], axis=-1, keepdims=True) for h in heads]

    def value_pass(c, carry):
        off = pl.multiple_of(c * ch, ch)
        vc = vb_ref[0, pl.ds(off, ch), :]
        p = [jnp.exp(s_ref[h, c, rs, :] - m[h]) for h in heads]
        ls = []
        for h in heads:
            l = carry[h]
            for g in groups:
                l = l + p[h][:, g]
            ls.append(l)
        pv_all = _dot(jnp.concatenate([p[h].astype(BF16) for h in heads], axis=0), vc)
        pv = [pv_all[h * rows:(h + 1) * rows] for h in heads]
        return tuple(ls) + (carry[B_HEADS] + per_lane(pv),)

    init = (jnp.zeros((rows, LANES), F32),) * B_HEADS + (jnp.zeros((rows, B_WIDTH), F32),)
    res = lax.fori_loop(0, nch, value_pass, init)
    l = [jnp.sum(res[h], axis=-1, keepdims=True) for h in heads]
    o_ref[0, rs, :] = (res[B_HEADS] / per_lane(l)).astype(o_ref.dtype)


def _mixer_b(pb, pf, tri, batch, seq):
    rows = min(2 * Q_BLOCK, seq)
    top_k = min(TOPK_MAX, seq // 4)
    ch = min(512, seq)
    kern = functools.partial(_mixb_kernel, top_k=top_k)
    return pl.pallas_call(
        kern,
        grid=(batch, seq // rows),
        in_specs=[pl.BlockSpec((1, rows, B_WIDTH), lambda b, i: (b, i, PB_BQ // B_WIDTH)),
                  pl.BlockSpec((1, seq, B_WIDTH), lambda b, i: (b, 0, PB_BK // B_WIDTH)),
                  pl.BlockSpec((1, seq, B_WIDTH), lambda b, i: (b, 0, PB_BV // B_WIDTH)),
                  pl.BlockSpec((1, rows, 256), lambda b, i: (b, i, PB_IQ // 256)),
                  pl.BlockSpec((1, seq, LANES), lambda b, i: (b, 0, PB_IK // LANES)),
                  pl.BlockSpec((1, rows, LANES), lambda b, i: (b, i, PF_MISC // LANES)),
                  pl.BlockSpec(tri.shape, lambda b, i: (0, 0))],
        out_specs=pl.BlockSpec((1, rows, B_WIDTH), lambda b, i: (b, i, 0)),
        out_shape=jax.ShapeDtypeStruct((batch, seq, B_WIDTH), BF16),
        scratch_shapes=[pltpu.VMEM((seq // ch, rows, ch), I32), pltpu.VMEM((seq // ch, rows, ch), F32),
                        pltpu.VMEM((B_HEADS, seq // ch, rows, ch), F32)],
        compiler_params=_params("parallel", "arbitrary", vmem_mb=48),
        name="mixer_b",
    )(pb, pb, pb, pb, pb, pf, tri)


def _head_consts():
    lane = np.arange(C_WIDTH)
    same = (lane[:, None] // HEAD_DIM == lane[None, :] // HEAD_DIM).astype(np.float32)
    return jnp.asarray(same)


def _rwkv1_kernel(cin_ref, prev_ref, vf_ref, mu_ref, vec_ref, lora_ref, vl_ref, bd_ref,
                  phi_ref, psi_ref, rp_ref, y0_ref, g_ref, bonus_ref, v_ref, *, rows, use_v_lora):
    i = pl.program_id(1)
    pc = cin_ref[0][:, :C_IN]
    prev = jnp.where(i > 0, prev_ref[0][SUBLANES - 1:SUBLANES, :C_IN], 0.0)
    row = lax.broadcasted_iota(I32, (rows, 1), 0)
    shifted = jnp.where(row == 0, prev, pltpu.roll(pc, 1, 0))
    pc = pc + (shifted - pc) * mu_ref[...]
    r = pc[:, 0:C_WIDTH]
    k = pc[:, C_WIDTH:2 * C_WIDTH]
    v = pc[:, 2 * C_WIDTH:3 * C_WIDTH]
    xl = pc[:, 3 * C_WIDTH:C_IN]
    lane = lax.broadcasted_iota(I32, xl.shape, 1)
    feat = jnp.where(lane < C_LORA_W, jnp.tanh(xl), jnp.where(lane < C_LORA_W + C_LORA_A, xl, _sigmoid(xl)))
    w0, a0, k_k, k_a, r_k = (vec_ref[n:n + 1, :] for n in range(5))
    bd = bd_ref[...]
    bd_sel = bd.astype(BF16)
    w_raw = -_softplus(-(w0 + _dot3(feat, lora_ref[0]))) - 0.5
    a = _sigmoid(a0 + _dot3(feat, lora_ref[1]))
    g_ref[0] = _dot3(feat, lora_ref[2])
    if use_v_lora:
        v0 = vec_ref[5:6, :]
        v = v + (vf_ref[0] - v) * _sigmoid(v0 + _dot3(_dot3(v, vl_ref[0]), vl_ref[1]))
    v_ref[0] = v
    kk = k * k_k
    kk = kk / jnp.maximum(jnp.sqrt(_dot_sel(bd_sel, kk * kk, left=False)), 1e-12)
    k = k * (1.0 + (a - 1.0) * k_a)
    logw = -jnp.exp(w_raw)
    bonus_ref[0] = _dot_sel(bd_sel, r * k * r_k, left=False) * v
    av = -kk
    bv = kk * a

    cc = C_CHUNK
    ri = lax.broadcasted_iota(I32, (cc, cc), 0)
    ci = lax.broadcasted_iota(I32, (cc, cc), 1)
    tril = jnp.where(ci <= ri, 1.0, 0.0).astype(BF16)
    strict = ci < ri
    incl = ci <= ri
    eye_c = jnp.where(ci == ri, 1.0, 0.0)
    lane_w = lax.broadcasted_iota(I32, (1, C_WIDTH), 1)
    ri2 = lax.broadcasted_iota(I32, (C_WIDTH, C_WIDTH), 0)
    ci2 = lax.broadcasted_iota(I32, (C_WIDTH, C_WIDTH), 1)
    eye_w = jnp.where(ri2 == ci2, 1.0, 0.0)
    chunks = range(rows // cc)
    heads = range(C_HEADS)
    pairs = [(c, h) for c in chunks for h in heads]
    mh = [jnp.where(lane_w // HEAD_DIM == h, 1.0, 0.0) for h in heads]
    sl = [slice(c * cc, (c + 1) * cc) for c in chunks]
    cs = [_dot_sel(tril, logw[sl[c]]) for c in chunks]
    cs_end = [cs[c][cc - 1:cc, :] for c in chunks]
    at = [av[sl[c]] * jnp.exp(cs[c] - logw[sl[c]]) for c in chunks]
    rt = [r[sl[c]] * jnp.exp(cs[c]) for c in chunks]
    inv = [jnp.exp(-cs[c]) for c in chunks]
    rhs = [jnp.concatenate([bv[sl[c]] * inv[c], k[sl[c]] * inv[c]], axis=0) for c in chunks]
    tail = [jnp.exp(cs_end[c] - cs[c]) for c in chunks]
    vc = [v[sl[c]] for c in chunks]
    vc_b = [vc[c].astype(BF16) for c in chunks]
    ath = {(c, h): at[c] * mh[h] for c, h in pairs}
    aa = {(c, h): _dot3(jnp.concatenate([ath[c, h], rt[c] * mh[h]], axis=0), rhs[c], nt=True) for c, h in pairs}
    a_ab = {p: jnp.where(strict, aa[p][:cc, :cc], 0.0) for p in pairs}
    a_ak = {p: jnp.where(strict, aa[p][:cc, cc:], 0.0).astype(BF16) for p in pairs}
    a_rb = {p: jnp.where(incl, aa[p][cc:, :cc], 0.0).astype(BF16) for p in pairs}
    a_rk = {p: jnp.where(incl, aa[p][cc:, cc:], 0.0).astype(BF16) for p in pairs}
    x = {p: eye_c + a_ab[p] for p in pairs}
    pw = a_ab
    for _ in range(int(math.log2(cc)) - 1):
        pw = {p: _dot3(pw[p], pw[p]) for p in pairs}
        x = {p: x[p] + _dot3(x[p], pw[p]) for p in pairs}
    akv = {(c, h): _dot(a_ak[c, h], vc_b[c]) for c, h in pairs}
    ap_h = {p: _dot3(x[p], ath[p]) for p in pairs}
    w2_h = {(c, h): _dot3(x[c, h], akv[c, h]) * mh[h] for c, h in pairs}
    rp_h = {p: _dot(a_rb[p], ap_h[p].astype(BF16)) for p in pairs}
    y0_h = {(c, h): (_dot(a_rb[c, h], w2_h[c, h].astype(BF16)) + _dot(a_rk[c, h], vc_b[c])) * mh[h] for c, h in pairs}
    for c in chunks:
        ap = sum(ap_h[c, h] for h in heads)
        w2 = sum(w2_h[c, h] for h in heads)
        rp_ref[0, sl[c], :] = rt[c] + sum(rp_h[c, h] for h in heads)
        y0_ref[0, sl[c], :] = sum(y0_h[c, h] for h in heads)
        bh = bv[sl[c]] * tail[c]
        kh = k[sl[c]] * tail[c]
        phi_ref[0, c] = eye_w * jnp.exp(cs_end[c]) + _dot3(bh.T, ap) * bd
        psi_ref[0, c] = _dot(jnp.concatenate([bh, kh], axis=0).T.astype(BF16),
                             jnp.concatenate([w2, vc[c]], axis=0).astype(BF16)) * bd


def _rwkv2_kernel(phi_ref, psi_ref, rp_ref, y0_ref, g_ref, bonus_ref, gn_ref, bd_ref, o_ref, s_ref):
    @pl.when(pl.program_id(1) == 0)
    def _():
        s_ref[...] = jnp.zeros_like(s_ref)

    states = [s_ref[...]]
    for c in range(phi_ref.shape[1]):
        states.append(_dot3(phi_ref[0, c], states[c]) + psi_ref[0, c])
    s_ref[...] = states[-1]
    cc = C_CHUNK
    y = jnp.concatenate([_dot3(rp_ref[0, c * cc:(c + 1) * cc, :], states[c]) for c in range(phi_ref.shape[1])],
                        axis=0) + y0_ref[0]
    bd_sel = bd_ref[...].astype(BF16)
    mu = _dot_sel(bd_sel, y, left=False) * (1.0 / HEAD_DIM)
    d = y - mu
    var = _dot_sel(bd_sel, d * d, left=False) * (1.0 / HEAD_DIM)
    yn = d * lax.rsqrt(var + C_GN_EPS) * gn_ref[0:1, :] + gn_ref[1:2, :]
    o_ref[0] = ((yn + bonus_ref[0]) * g_ref[0]).astype(o_ref.dtype)


def _mixer_c(pf, v_first, mu, vec, lora, vl, gn, bd, batch, seq, use_v_lora):
    rows = min(256, seq)
    nblk = seq // rows
    nch = seq // C_CHUNK
    cpb = rows // C_CHUNK
    kern = functools.partial(_rwkv1_kernel, rows=rows, use_v_lora=use_v_lora)
    full2 = lambda a: pl.BlockSpec(a.shape, lambda b, i: (0,) * a.ndim)
    seq_spec = pl.BlockSpec((1, rows, C_WIDTH), lambda b, i: (b, i, 0))
    mat_spec = pl.BlockSpec((1, cpb, C_WIDTH, C_WIDTH), lambda b, i: (b, i, 0, 0))
    seq_shape = jax.ShapeDtypeStruct((batch, seq, C_WIDTH), F32)
    mat_shape = jax.ShapeDtypeStruct((batch, nch, C_WIDTH, C_WIDTH), F32)
    prev_blk = rows // SUBLANES
    phi, psi, rp, y0, g, bonus, v = pl.pallas_call(
        kern,
        grid=(batch, nblk),
        in_specs=[pl.BlockSpec((1, rows, 1024), lambda b, i: (b, i, PF_CIN // 1024)),
                  pl.BlockSpec((1, SUBLANES, 1024), lambda b, i: (b, jnp.maximum(i * prev_blk - 1, 0), PF_CIN // 1024)),
                  seq_spec, full2(mu), full2(vec), full2(lora), full2(vl), full2(bd)],
        out_specs=[mat_spec, mat_spec, seq_spec, seq_spec, seq_spec, seq_spec, seq_spec],
        out_shape=[mat_shape, mat_shape, seq_shape, seq_shape, seq_shape, seq_shape, seq_shape],
        compiler_params=_params("parallel", "parallel", vmem_mb=48),
        name="rwkv_chunks",
    )(pf, pf, v_first, mu, vec, lora, vl, bd)
    scan_chunks = min(4, nch)
    cseq = pl.BlockSpec((1, scan_chunks * C_CHUNK, C_WIDTH), lambda b, c: (b, c, 0))
    cmat = pl.BlockSpec((1, scan_chunks, C_WIDTH, C_WIDTH), lambda b, c: (b, c, 0, 0))
    o = pl.pallas_call(
        _rwkv2_kernel,
        grid=(batch, nch // scan_chunks),
        in_specs=[cmat, cmat, cseq, cseq, cseq, cseq,
                  pl.BlockSpec(gn.shape, lambda b, c: (0, 0)), pl.BlockSpec(bd.shape, lambda b, c: (0, 0))],
        out_specs=cseq,
        out_shape=jax.ShapeDtypeStruct((batch, seq, C_WIDTH), BF16),
        scratch_shapes=[pltpu.VMEM((C_WIDTH, C_WIDTH), F32)],
        compiler_params=_params("parallel", "arbitrary"),
        name="rwkv_scan",
    )(phi, psi, rp, y0, g, bonus, gn, bd)
    return o, v


def _ssd_consts():
    expand = np.zeros((LANES, D_INNER), np.float32)
    for h in range(D_HEADS):
        expand[MISC_DT + h, h * HEAD_DIM:(h + 1) * HEAD_DIM] = 1.0
    return jnp.asarray(expand)


def _ssd_kernel(xbc_ref, z_ref, misc_ref, conv_ref, vec_ref, hp_ref, expand_ref, bd_ref, o_ref, st_ref, prev_ref):
    @pl.when(pl.program_id(1) == 0)
    def _():
        st_ref[...] = jnp.zeros_like(st_ref)
        prev_ref[...] = jnp.zeros_like(prev_ref)

    q = D_CHUNK
    x_raw = xbc_ref[0]
    ext = jnp.concatenate([prev_ref[...], x_raw], axis=0)
    conv = jnp.zeros((q, D_XBC), F32)
    for t in range(D_CONV):
        lo = SUBLANES - (D_CONV - 1) + t
        conv = conv + ext[lo:lo + q, :] * conv_ref[t:t + 1, :]
    prev_ref[...] = x_raw[q - SUBLANES:, :]
    xbc = _silu(conv + conv_ref[D_CONV:D_CONV + 1, :])
    xs = xbc[:, :D_INNER]
    bm = xbc[:, D_INNER:D_INNER + D_GROUPS * D_STATE]
    cm = xbc[:, D_INNER + D_GROUPS * D_STATE:]

    dt_col = _softplus(misc_ref[0] + hp_ref[0:1, :])
    a_col = dt_col * hp_ref[1:2, :]
    ri = lax.broadcasted_iota(I32, (q, q), 0)
    ci = lax.broadcasted_iota(I32, (q, q), 1)
    causal = ci <= ri
    tril = jnp.where(causal, 1.0, 0.0).astype(BF16)
    acs_col = _dot_sel(tril, a_col)
    acs_row = acs_col.T
    expand = expand_ref[...].astype(BF16)
    acs = _dot_sel(expand, acs_col, left=False)
    dt = _dot_sel(expand, dt_col, left=False)
    acs_end = acs[q - 1:q, :]
    xdt = xs * dt

    lane = lax.broadcasted_iota(I32, (1, LANES), 1)
    lane_w = lax.broadcasted_iota(I32, (1, D_INNER), 1)
    left = lane < D_STATE
    bm_sw = pltpu.roll(bm, D_STATE, 1)
    cm_sw = pltpu.roll(cm, D_STATE, 1)
    b_exp = jnp.concatenate([jnp.where(left, bm, bm_sw), jnp.where(left, bm_sw, bm)], axis=1)
    c_exp = jnp.concatenate([jnp.where(left, cm, cm_sw), jnp.where(left, cm_sw, cm)], axis=1)
    cb = [_dot3(jnp.where(left == (g == 0), cm, 0.0), bm, nt=True) for g in range(D_GROUPS)]

    scores = []
    for h in range(D_HEADS):
        col = acs_col[:, MISC_DT + h:MISC_DT + h + 1]
        rw = acs_row[MISC_DT + h:MISC_DT + h + 1, :]
        decay = jnp.exp(jnp.where(causal, col - rw, NEG_INF))
        scores.append(cb[h // (D_HEADS // D_GROUPS)] * decay)
    y_h = [_dot3(scores[h], xdt) for h in range(D_HEADS)]
    y = jnp.zeros((q, D_INNER), F32)
    for h in range(D_HEADS):
        y = y + jnp.where(lane_w // HEAD_DIM == h, y_h[h], 0.0)
    st = st_ref[...]
    y = y + _dot3(c_exp, st) * jnp.exp(acs)
    st_ref[...] = st * jnp.exp(acs_end) + _dot3(b_exp.T, xdt * jnp.exp(acs_end - acs)) * bd_ref[...]
    y = y + xs * vec_ref[0:1, :]
    y = y * _silu(z_ref[0])
    half = D_INNER // D_GROUPS
    outs = []
    for g in range(D_GROUPS):
        yg = y[:, g * half:(g + 1) * half]
        outs.append(yg * lax.rsqrt(jnp.mean(yg * yg, axis=-1, keepdims=True) + D_NORM_EPS))
    o_ref[0] = (jnp.concatenate(outs, axis=1) * vec_ref[1:2, :]).astype(o_ref.dtype)


def _mixer_d(pf, conv, vec, hp, expand, bd, batch, seq):
    q = D_CHUNK
    full = lambda a: pl.BlockSpec(a.shape, lambda b, i: (0, 0))
    return pl.pallas_call(
        _ssd_kernel,
        grid=(batch, seq // q),
        in_specs=[pl.BlockSpec((1, q, D_XBC), lambda b, i: (b, i, PF_XBC // D_XBC)),
                  pl.BlockSpec((1, q, D_INNER), lambda b, i: (b, i, PF_Z // D_INNER)),
                  pl.BlockSpec((1, q, LANES), lambda b, i: (b, i, PF_MISC // LANES)),
                  full(conv), full(vec), full(hp), full(expand), full(bd)],
        out_specs=pl.BlockSpec((1, q, D_INNER), lambda b, i: (b, i, 0)),
        out_shape=jax.ShapeDtypeStruct((batch, seq, D_INNER), BF16),
        scratch_shapes=[pltpu.VMEM((D_INNER, D_INNER), F32), pltpu.VMEM((SUBLANES, D_XBC), F32)],
        compiler_params=_params("parallel", "arbitrary"),
        name="ssd",
    )(pf, pf, pf, conv, vec, hp, expand, bd)


def _merge_kernel(x_ref, gates_ref, oa_ref, ob_ref, oc_ref, od_ref, wbr_ref, wout_ref, ln_ref, wr_ref, br_ref,
                  x1_ref, route_ref):
    acc = None
    for n, o_ref in enumerate((oa_ref, ob_ref, oc_ref, od_ref)):
        term = _sigmoid(gates_ref[:, n * D_MODEL:(n + 1) * D_MODEL]) * _dot(o_ref[...], wbr_ref[n])
        acc = term if acc is None else acc + term
    h = DEEPNORM_ALPHA * x_ref[...] + _dot(acc.astype(BF16), wout_ref[...])
    x1 = _layer_norm(h, ln_ref[0:1, :], ln_ref[1:2, :])
    x1_ref[...] = x1

    logits = _dot3(x1, wr_ref[...]) + br_ref[...]
    lane = lax.broadcasted_iota(I32, logits.shape, 1)
    big = jnp.int32(LANES)
    gl = jnp.where(lane < N_EXPERT_GROUPS, logits, NEG_INF)
    gm = jnp.max(gl, axis=-1, keepdims=True)
    pg_top = 1.0 / jnp.sum(jnp.exp(gl - gm), axis=-1, keepdims=True)
    g_sel = jnp.min(jnp.where(gl == gm, lane, big), axis=-1, keepdims=True)
    off = N_EXPERT_GROUPS + g_sel * EXPERTS_PER_GROUP
    el = jnp.where((lane >= off) & (lane < off + EXPERTS_PER_GROUP), logits, NEG_INF)
    em = jnp.max(el, axis=-1, keepdims=True)
    es = jnp.sum(jnp.exp(el - em), axis=-1, keepdims=True)
    idx1 = jnp.min(jnp.where(el == em, lane, big), axis=-1, keepdims=True)
    el2 = jnp.where(lane == idx1, NEG_INF, el)
    em2 = jnp.max(el2, axis=-1, keepdims=True)
    idx2 = jnp.min(jnp.where(el2 == em2, lane, big), axis=-1, keepdims=True)
    p1 = 1.0 / es
    p2 = jnp.exp(em2 - em) / es
    gate1 = pg_top * p1 / (p1 + p2)
    gate2 = pg_top * p2 / (p1 + p2)
    e1 = (idx1 - N_EXPERT_GROUPS).astype(F32)
    e2 = (idx2 - N_EXPERT_GROUPS).astype(F32)
    route_ref[...] = jnp.where(lane == 0, e1, jnp.where(lane == 1, e2, jnp.where(lane == 2, gate1,
                               jnp.where(lane == 3, gate2, 0.0))))


def _merge(x2d, pf, oa, ob, oc, od, wbr, wout, ln, wr, br, tm):
    m = x2d.shape[0]
    row = lambda w: pl.BlockSpec((tm, w), lambda i: (i, 0))
    full = lambda a: pl.BlockSpec(a.shape, lambda i: (0,) * a.ndim)
    return pl.pallas_call(
        _merge_kernel,
        grid=(m // tm,),
        in_specs=[row(D_MODEL), pl.BlockSpec((tm, N_BRANCH * D_MODEL), lambda i: (i, PF_GATES)),
                  row(A_WIDTH), row(B_WIDTH), row(C_WIDTH), row(D_INNER),
                  full(wbr), full(wout), full(ln), full(wr), full(br)],
        out_specs=[row(D_MODEL), row(LANES)],
        out_shape=[jax.ShapeDtypeStruct((m, D_MODEL), F32), jax.ShapeDtypeStruct((m, LANES), F32)],
        compiler_params=_params("parallel", vmem_mb=48),
        name="merge_route",
    )(x2d, pf, oa, ob, oc, od, wbr, wout, ln, wr, br)


def _ffn_kernel(be_ref, nv_ref, xs_ref, wg_ref, wu_ref, wd_ref, o_ref):
    i = pl.program_id(0)

    @pl.when(i < nv_ref[0])
    def _():
        xb = xs_ref[...]
        hg = _dot(xb, wg_ref[0, 0].astype(BF16))
        hu = _dot(xb, wu_ref[0, 0].astype(BF16))
        o_ref[...] = _dot((_silu(hg) * hu).astype(BF16), wd_ref[0, 0].astype(BF16))

    @pl.when(i >= nv_ref[0])
    def _():
        o_ref[...] = jnp.zeros_like(o_ref)


def _ffn(blk_expert, n_valid, xs, e_gate, e_up, e_down, layer):
    cap = xs.shape[0]
    wspec = lambda a: pl.BlockSpec((1, 1) + a.shape[2:], lambda i, be, nv: (layer, be[i], 0, 0))
    return pl.pallas_call(
        _ffn_kernel,
        grid_spec=pltpu.PrefetchScalarGridSpec(
            num_scalar_prefetch=2,
            grid=(cap // MOE_ROWS,),
            in_specs=[pl.BlockSpec((MOE_ROWS, D_MODEL), lambda i, be, nv: (i, 0)),
                      wspec(e_gate), wspec(e_up), wspec(e_down)],
            out_specs=pl.BlockSpec((MOE_ROWS, D_MODEL), lambda i, be, nv: (i, 0)),
        ),
        out_shape=jax.ShapeDtypeStruct((cap, D_MODEL), F32),
        compiler_params=_params("arbitrary", vmem_mb=48),
        name="expert_ffn",
    )(blk_expert, n_valid, xs, e_gate, e_up, e_down)


def _combine_kernel(dest_ref, x_ref, route_ref, ln_ref, yb_ref, o_ref, ybuf, sem):
    i = pl.program_id(0)
    tm = x_ref.shape[0]

    def gather(tile, slot):
        base = tile * (TOP_K_EXPERTS * tm)

        def body(r, carry):
            for s in range(TOP_K_EXPERTS):
                row = dest_ref[base + TOP_K_EXPERTS * r + s]
                pltpu.make_async_copy(yb_ref.at[pl.ds(row, 1)], ybuf.at[slot, pl.ds(s * tm + r, 1)],
                                      sem.at[slot]).start()
            return carry

        lax.fori_loop(0, tm, body, 0, unroll=8)

    @pl.when(i == 0)
    def _():
        gather(0, 0)

    @pl.when(i + 1 < pl.num_programs(0))
    def _():
        gather(i + 1, (i + 1) % 2)

    slot = i % 2
    pltpu.make_async_copy(yb_ref.at[pl.ds(0, TOP_K_EXPERTS * tm)], ybuf.at[slot], sem.at[slot]).wait()
    g0 = route_ref[:, 2:3]
    g1 = route_ref[:, 3:4]
    h = DEEPNORM_ALPHA * x_ref[...] + (ybuf[slot, 0:tm, :] * g0 + ybuf[slot, tm:2 * tm, :] * g1)
    o_ref[...] = _layer_norm(h, ln_ref[0:1, :], ln_ref[1:2, :])


def _combine(x1, yb, dest, route, ln, tm):
    m = x1.shape[0]
    return pl.pallas_call(
        _combine_kernel,
        grid_spec=pltpu.PrefetchScalarGridSpec(
            num_scalar_prefetch=1,
            grid=(m // tm,),
            in_specs=[pl.BlockSpec((tm, D_MODEL), lambda i, d: (i, 0)),
                      pl.BlockSpec((tm, LANES), lambda i, d: (i, 0)),
                      pl.BlockSpec(ln.shape, lambda i, d: (0, 0)),
                      pl.BlockSpec(memory_space=pl.ANY)],
            out_specs=pl.BlockSpec((tm, D_MODEL), lambda i, d: (i, 0)),
            scratch_shapes=[pltpu.VMEM((2, TOP_K_EXPERTS * tm, D_MODEL), F32), pltpu.SemaphoreType.DMA((2,))],
        ),
        out_shape=jax.ShapeDtypeStruct((m, D_MODEL), F32),
        compiler_params=_params("arbitrary"),
        name="combine",
    )(dest, x1, route, ln, yb)


def _dispatch_tables(route, m):
    flat_e = route[:, 0:TOP_K_EXPERTS].astype(I32).reshape(-1)
    n_assign = m * TOP_K_EXPERTS
    onehot = (flat_e[:, None] == jnp.arange(N_EXPERTS, dtype=I32)[None, :]).astype(I32)
    csum = jnp.cumsum(onehot, axis=0)
    rank = jnp.sum(csum * onehot, axis=1) - 1
    counts = csum[-1]
    padded = (counts + MOE_ROWS - 1) // MOE_ROWS * MOE_ROWS
    pad_end = jnp.cumsum(padded)
    pad_start = pad_end - padded
    dest = pad_start[flat_e] + rank
    cap = (n_assign + N_EXPERTS * (MOE_ROWS - 1) + MOE_ROWS - 1) // MOE_ROWS * MOE_ROWS
    n_blocks = cap // MOE_ROWS
    blk_start = jnp.arange(n_blocks, dtype=I32) * MOE_ROWS
    blk_expert = jnp.minimum(jnp.sum((pad_end[None, :] <= blk_start[:, None]).astype(I32), axis=1), N_EXPERTS - 1)
    buf_tok = (jnp.arange(cap, dtype=I32) % m).at[dest].set(jnp.arange(n_assign, dtype=I32) // TOP_K_EXPERTS)
    n_valid = (pad_end[-1] // MOE_ROWS).astype(I32).reshape(1)
    return dest, buf_tok, blk_expert, n_valid


def _row_pad(a, rows):
    return jnp.pad(a, ((0, rows - a.shape[0]), (0, 0)))


def kernel(x, w_in, c_mu, c_w0, c_w2, c_a0, c_a2, c_g2, c_kk, c_ka, c_rk, c_gn_w, c_gn_b, c_v0, c_v1, c_v2,
           d_conv_w, d_conv_b, d_dt_bias, d_a_log, d_skip, d_norm_w, w_branch, w_out, ln1_g, ln1_b,
           r_group, r_group_b, r_expert, r_expert_b, e_gate, e_up, e_down, ln2_g, ln2_b):
    batch, seq, _ = x.shape
    m = batch * seq
    biases = _mixa_class_bias()
    tw = min(256, seq)
    tri = jnp.asarray(np.triu(np.ones((tw, tw), np.float32))).astype(BF16)
    bd = _head_consts()
    expand = _ssd_consts()
    x2d = x.reshape(m, D_MODEL)
    v_first = jnp.zeros((batch, seq, C_WIDTH), F32)
    tm_proj = min(2048, m)
    tm_tok = min(512, m)
    for l in range(DEPTH):
        wb, wf = _proj_weights(w_in[l])
        pb = _proj(x2d, wb, BF16, tm_proj, PROJ_TN).reshape(batch, seq, PB_WIDTH)
        pf = _proj(x2d, wf, F32, tm_proj, PROJ_TN).reshape(batch, seq, PF_WIDTH)

        o_a = _mixer_a2(pb, biases, batch, seq)
        o_b = _mixer_b(pb, pf, tri, batch, seq)

        use_v_lora = l > 0
        vec_rows = [c_w0[l], c_a0[l], c_kk[l], c_ka[l], c_rk[l].reshape(-1)]
        vec_rows.append(c_v0[l - 1] if use_v_lora else jnp.zeros((C_WIDTH,), F32))
        vec = _row_pad(jnp.stack(vec_rows), SUBLANES)
        lora = jnp.stack([
            jnp.pad(c_w2[l], ((0, LANES - C_LORA_W), (0, 0))),
            jnp.pad(c_a2[l], ((C_LORA_W, LANES - C_LORA_W - C_LORA_A), (0, 0))),
            jnp.pad(c_g2[l], ((C_LORA_W + C_LORA_A, 0), (0, 0)))])
        if use_v_lora:
            vl = jnp.stack([jnp.pad(c_v1[l - 1], ((0, 0), (0, C_WIDTH - C_LORA_V))),
                            jnp.pad(c_v2[l - 1], ((0, C_WIDTH - C_LORA_V), (0, 0)))])
        else:
            vl = jnp.zeros((2, C_WIDTH, C_WIDTH), F32)
        gn = _row_pad(jnp.stack([c_gn_w[l], c_gn_b[l]]), SUBLANES)
        o_c, v_c = _mixer_c(pf, v_first, c_mu[l].reshape(1, C_IN), vec, lora, vl, gn, bd, batch, seq, use_v_lora)
        if l == 0:
            v_first = v_c

        conv = _row_pad(jnp.concatenate([d_conv_w[l], d_conv_b[l][None, :]], axis=0), SUBLANES)
        dvec = _row_pad(jnp.stack([jnp.repeat(d_skip[l], HEAD_DIM), d_norm_w[l]]), SUBLANES)
        place = lambda a: jnp.pad(a, (MISC_DT, LANES - MISC_DT - D_HEADS))
        hp = _row_pad(jnp.stack([place(d_dt_bias[l]), place(-jnp.exp(d_a_log[l]))]), SUBLANES)
        o_d = _mixer_d(pf, conv, dvec, hp, expand, bd, batch, seq)

        wr = jnp.pad(jnp.concatenate([r_group[l], r_expert[l]], axis=1),
                     ((0, 0), (0, LANES - N_EXPERT_GROUPS - N_EXPERTS)))
        br = jnp.pad(jnp.concatenate([r_group_b[l], r_expert_b[l]]), (0, LANES - N_EXPERT_GROUPS - N_EXPERTS))
        ln1 = _row_pad(jnp.stack([ln1_g[l], ln1_b[l]]), SUBLANES)
        x1, route = _merge(x2d, pf.reshape(m, PF_WIDTH), o_a.reshape(m, -1), o_b.reshape(m, -1),
                           o_c.reshape(m, -1), o_d.reshape(m, -1), w_branch[l].astype(BF16),
                           w_out[l].astype(BF16), ln1, wr, br.reshape(1, LANES), tm_tok)

        dest, buf_tok, blk_expert, n_valid = _dispatch_tables(route, m)
        xs = jnp.take(x1.astype(BF16), buf_tok, axis=0)
        yb = _ffn(blk_expert, n_valid, xs, e_gate, e_up, e_down, l)
        ln2 = _row_pad(jnp.stack([ln2_g[l], ln2_b[l]]), SUBLANES)
        x2d = _combine(x1, yb, dest, route, ln2, min(256, m))
    return x2d.reshape(batch, seq, D_MODEL)
```

```python
import functools
import math
import types

import jax
import jax.numpy as jnp
import numpy as np
from jax import lax
from jax.experimental import pallas as pl
from jax.experimental.pallas import tpu as pltpu

F32 = jnp.float32
BF16 = jnp.bfloat16
I32 = jnp.int32
NEG_INF = float("-inf")
INT_MIN = -(2 ** 31)

LANES = 128
SUBLANES = 8

D_MODEL = 1024
DEPTH = 2
HEAD_DIM = 64
Q_BLOCK = 128

A_HEADS = 4
A_PATTERNS = ((128, 1), (512, 4), (2048, 16))
A_GROUPS = len(A_PATTERNS)
A_WIDTH = A_HEADS * HEAD_DIM
A_MAXWIN = max(w for w, _ in A_PATTERNS)

B_HEADS = 4
B_WIDTH = B_HEADS * HEAD_DIM
IDX_HEADS = 4
IDX_DIM = 64
TOPK_MAX = 256

C_HEADS = 4
C_WIDTH = C_HEADS * HEAD_DIM
C_LORA_W = 32
C_LORA_A = 32
C_LORA_G = 64
C_LORA_V = 16
C_IN = 3 * C_WIDTH + C_LORA_W + C_LORA_A + C_LORA_G
C_GN_EPS = 64e-5
C_CHUNK = 64

D_HEADS = 4
D_INNER = D_HEADS * HEAD_DIM
D_GROUPS = 2
D_STATE = 64
D_CONV = 4
D_CHUNK = 128
D_XBC = D_INNER + 2 * D_GROUPS * D_STATE
D_NORM_EPS = 1e-5

N_BRANCH = 4
N_EXPERT_GROUPS = 4
EXPERTS_PER_GROUP = 8
N_EXPERTS = N_EXPERT_GROUPS * EXPERTS_PER_GROUP
TOP_K_EXPERTS = 2
D_EXPERT = 512
MOE_ROWS = 512

LN_EPS = 1e-5
DEEPNORM_ALPHA = (2 * DEPTH) ** 0.25

PB_AKV = 0
PB_AQ = PB_AKV + A_GROUPS * 2 * A_WIDTH
PB_BQ = PB_AQ + A_GROUPS * A_WIDTH
PB_BK = PB_BQ + B_WIDTH
PB_BV = PB_BK + B_WIDTH
PB_IQ = PB_BV + B_WIDTH
PB_IK = PB_IQ + IDX_HEADS * IDX_DIM
PB_WIDTH = PB_IK + 512
PF_GATES = 0
PF_CIN = PF_GATES + N_BRANCH * D_MODEL
PF_MISC = PF_CIN + C_IN
PF_XBC = PF_CIN + 1024
PF_Z = PF_XBC + D_XBC
PROJ_TN = 768
PF_WIDTH = -(-(PF_Z + D_INNER) // PROJ_TN) * PROJ_TN
MISC_DT = 4


def _dot(a, b, precision=None):
    return jnp.dot(a, b, preferred_element_type=F32, precision=precision)


def _dot_nt(a, b, precision=None):
    return lax.dot_general(a, b, (((1,), (1,)), ((), ())), preferred_element_type=F32, precision=precision)


def _split2(a):
    hi = a.astype(BF16)
    return hi, (a - hi.astype(F32)).astype(BF16)


def _dot3(a, b, nt=False):
    mm = _dot_nt if nt else _dot
    ah, al = _split2(a)
    bh, bl = _split2(b)
    return mm(ah, bh) + (mm(ah, bl) + mm(al, bh))


def _dot_sel(sel, x, left=True):
    hi = x.astype(BF16)
    r1 = x - hi.astype(F32)
    mid = r1.astype(BF16)
    lo = (r1 - mid.astype(F32)).astype(BF16)
    if left:
        return _dot(sel, hi) + (_dot(sel, mid) + _dot(sel, lo))
    return _dot(hi, sel) + (_dot(mid, sel) + _dot(lo, sel))


def _sigmoid(x):
    return 1.0 / (1.0 + jnp.exp(-x))


def _softplus(x):
    return jnp.maximum(x, 0.0) + jnp.log1p(jnp.exp(-jnp.abs(x)))


def _silu(x):
    return x * _sigmoid(x)


def _layer_norm(h, g, b):
    mu = jnp.mean(h, axis=-1, keepdims=True)
    d = h - mu
    var = jnp.mean(d * d, axis=-1, keepdims=True)
    return d * lax.rsqrt(var + LN_EPS) * g + b


def _params(*sem, vmem_mb=None):
    kw = {}
    if vmem_mb is not None:
        kw["vmem_limit_bytes"] = vmem_mb * 1024 * 1024
    return pltpu.CompilerParams(dimension_semantics=sem, **kw)


def _proj_kernel(x_ref, w_ref, o_ref):
    o_ref[...] = _dot(x_ref[...].astype(BF16), w_ref[...]).astype(o_ref.dtype)


def _proj(x2d, w, out_dtype, tm, tn):
    m, k = x2d.shape
    n = w.shape[1]
    return pl.pallas_call(
        _proj_kernel,
        grid=(m // tm, n // tn),
        in_specs=[pl.BlockSpec((tm, k), lambda i, j: (i, 0)),
                  pl.BlockSpec((k, tn), lambda i, j: (0, j))],
        out_specs=pl.BlockSpec((tm, tn), lambda i, j: (i, j)),
        out_shape=jax.ShapeDtypeStruct((m, n), out_dtype),
        compiler_params=_params("parallel", "arbitrary", vmem_mb=48),
        name="in_proj",
    )(x2d, w)


def _proj_weights(w):
    k = w.shape[0]
    off = 0
    seg = {}
    for name, size in (("a_qkv", 3 * A_GROUPS * A_WIDTH), ("b_qkv", 3 * B_WIDTH), ("b_idx_q", IDX_HEADS * IDX_DIM),
                       ("b_idx_k", IDX_DIM), ("b_idx_w", IDX_HEADS), ("c_in", C_IN), ("d_z", D_INNER),
                       ("d_xbc", D_XBC), ("d_dt", D_HEADS), ("gates", N_BRANCH * D_MODEL)):
        seg[name] = w[:, off:off + size]
        off += size
    a = seg["a_qkv"].reshape(k, 3, A_GROUPS, A_HEADS, HEAD_DIM)
    a_kv = jnp.transpose(a[:, 1:3], (0, 2, 1, 3, 4)).reshape(k, -1)
    a_q = a[:, 0].reshape(k, -1)
    zeros = lambda n: jnp.zeros((k, n), w.dtype)
    wb = jnp.concatenate([a_kv, a_q, seg["b_qkv"], seg["b_idx_q"], seg["b_idx_k"],
                          zeros(PB_WIDTH - PB_IK - IDX_DIM)], axis=1)
    misc = jnp.concatenate([seg["b_idx_w"], seg["d_dt"], zeros(LANES - IDX_HEADS - D_HEADS)], axis=1)
    wf = jnp.concatenate([seg["gates"], seg["c_in"], misc, seg["d_xbc"], seg["d_z"]], axis=1)
    wf = jnp.concatenate([wf, zeros(PF_WIDTH - wf.shape[1])], axis=1)
    assert wb.shape[1] == PB_WIDTH and PB_WIDTH % PROJ_TN == 0
    return wb.astype(BF16), wf.astype(BF16)


A_ROWS2 = Q_BLOCK // 16
A_SPAN1 = 512 // 4 + Q_BLOCK // 4
A_SPAN2 = 144


def _mixa_class_bias():
    x = (np.arange(Q_BLOCK) % A_ROWS2) * 16 + np.arange(Q_BLOCK) // A_ROWS2
    c0 = np.arange(2 * Q_BLOCK)[None, :]
    b0 = np.where((c0 >= x[:, None]) & (c0 <= x[:, None] + Q_BLOCK), 0.0, -np.inf)
    u, a16 = np.arange(32) // A_ROWS2, np.arange(32) % A_ROWS2
    a4 = 4 * a16 + u
    c1 = np.arange(A_SPAN1)[None, :]
    b1 = np.where((c1 >= a4[:, None]) & (c1 <= a4[:, None] + Q_BLOCK), 0.0, -np.inf)
    a2 = np.arange(A_ROWS2)
    c2 = np.arange(A_SPAN2)[None, :]
    b2 = np.where((c2 >= a2[:, None]) & (c2 <= a2[:, None] + Q_BLOCK), 0.0, -np.inf)
    return [jnp.asarray(b.astype(np.float32)) for b in (b0, b1, b2)]


def _mixa2_kernel(q0_ref, q1_ref, q2_ref, k0p_ref, k0_ref, v0p_ref, v0_ref, k1_ref, v1_ref, k2_ref, v2_ref,
                  b0_ref, b1_ref, b2_ref, o_ref):
    pid = pl.program_id(2)
    start = pid * Q_BLOCK
    r8 = A_ROWS2
    scale = HEAD_DIM ** -0.5
    head_of_lane = lax.broadcasted_iota(I32, (1, 2 * HEAD_DIM), 1) // HEAD_DIM
    qf = [q_ref[0].astype(F32) * scale for q_ref in (q0_ref, q1_ref, q2_ref)]

    def masked_bias(b_ref, first_valid):
        col = lax.broadcasted_iota(I32, b_ref.shape, 1)
        return jnp.where(col >= first_valid, b_ref[...], NEG_INF)

    classes = [(qf[0], jnp.concatenate([k0p_ref[0], k0_ref[0]], axis=0),
                jnp.concatenate([v0p_ref[0], v0_ref[0]], axis=0), masked_bias(b0_ref, Q_BLOCK - start))]
    w1 = pl.ds(pl.multiple_of(pid * (Q_BLOCK // 4), Q_BLOCK // 4), A_SPAN1)
    bias1 = masked_bias(b1_ref, Q_BLOCK - pid * (Q_BLOCK // 4))
    for r4 in range(4):
        q1 = jnp.concatenate([qf[1][(r4 + 4 * u) * r8:(r4 + 4 * u + 1) * r8] for u in range(4)], axis=0)
        classes.append((q1, k1_ref[0, r4, w1, :], v1_ref[0, r4, w1, :], bias1))
    w2 = pl.ds(pl.multiple_of(pid * r8, r8), A_SPAN2)
    bias2 = masked_bias(b2_ref, Q_BLOCK - pid * r8)
    for r16 in range(16):
        classes.append((qf[2][r16 * r8:(r16 + 1) * r8], k2_ref[0, r16, w2, :].astype(BF16),
                        v2_ref[0, r16, w2, :].astype(BF16), bias2))

    def rows_of(vals):
        g1 = jnp.concatenate([vals[1 + r16 % 4][(r16 // 4) * r8:(r16 // 4 + 1) * r8] for r16 in range(16)], axis=0)
        return [vals[0], g1, jnp.concatenate(vals[5:21], axis=0)]

    first = head_of_lane == 0

    def stacked(q):
        return jnp.concatenate([jnp.where(first, q, 0.0), jnp.where(first, 0.0, q)], axis=0).astype(BF16)

    s = [_dot_nt(stacked(q), k) + jnp.concatenate([bias, bias], axis=0) for q, k, _, bias in classes]
    m = [jnp.max(si, axis=-1, keepdims=True) for si in s]
    p = [jnp.exp(si - mi) for si, mi in zip(s, m)]
    l = [jnp.sum(pi, axis=-1, keepdims=True) for pi in p]
    o = [_dot(pi.astype(BF16), c[2]) for pi, c in zip(p, classes)]

    res = []
    for hh in range(2):
        half = lambda vals: [v[hh * (v.shape[0] // 2):(hh + 1) * (v.shape[0] // 2)] for v in vals]
        ms, ls, os_ = rows_of(half(m)), rows_of(half(l)), rows_of(half(o))
        m_all = jnp.maximum(jnp.maximum(ms[0], ms[1]), ms[2])
        num = jnp.zeros((Q_BLOCK, 2 * HEAD_DIM), F32)
        den = jnp.zeros((Q_BLOCK, 1), F32)
        for g in range(A_GROUPS):
            wg = jnp.exp(ms[g] - m_all)
            num = num + wg * os_[g]
            den = den + wg * ls[g]
        res.append(num / den)
    o_ref[0] = jnp.where(head_of_lane == 0, res[0], res[1]).astype(o_ref.dtype)


def _mixer_a2(pb, biases, batch, seq):
    nblk = seq // Q_BLOCK
    r8 = A_ROWS2
    gw = 2 * A_WIDTH
    kv1 = jnp.pad(pb[:, :, PB_AKV + gw:PB_AKV + 2 * gw], ((0, 0), (512, 0), (0, 0)))
    kv1 = kv1.reshape(batch, -1, 4, gw).transpose(0, 2, 1, 3)
    kv2 = jnp.pad(pb[:, :, PB_AKV + 2 * gw:PB_AKV + 3 * gw].astype(F32),
                  ((0, 0), (A_MAXWIN, 16 * (A_SPAN2 - 136)), (0, 0)))
    kv2 = kv2.reshape(batch, -1, 16, gw).transpose(0, 2, 1, 3)
    q = pb[:, :, PB_AQ:PB_BQ].reshape(batch, nblk, r8, 16, -1).transpose(0, 1, 3, 2, 4).reshape(batch, seq, -1)
    pair = 2 * HEAD_DIM
    pairs_per_part = A_WIDTH // pair
    q_spec = lambda g: pl.BlockSpec((1, Q_BLOCK, pair), lambda b, hp, i: (b, i, g * pairs_per_part + hp))
    cur = lambda part: pl.BlockSpec((1, Q_BLOCK, pair), lambda b, hp, i: (b, i, PB_AKV // pair + part * pairs_per_part + hp))
    prev = lambda part: pl.BlockSpec((1, Q_BLOCK, pair),
                                     lambda b, hp, i: (b, jnp.maximum(i - 1, 0), PB_AKV // pair + part * pairs_per_part + hp))
    k4 = lambda a: pl.BlockSpec((1, a.shape[1], a.shape[2], pair), lambda b, hp, i: (b, 0, 0, hp))
    v4 = lambda a: pl.BlockSpec((1, a.shape[1], a.shape[2], pair), lambda b, hp, i: (b, 0, 0, pairs_per_part + hp))
    in_specs = [q_spec(0), q_spec(1), q_spec(2), prev(0), cur(0), prev(1), cur(1), k4(kv1), v4(kv1), k4(kv2), v4(kv2)]
    for bias in biases:
        in_specs.append(pl.BlockSpec(bias.shape, lambda b, hp, i: (0, 0)))
    o = pl.pallas_call(
        _mixa2_kernel,
        grid=(batch, A_HEADS // 2, nblk),
        in_specs=in_specs,
        out_specs=pl.BlockSpec((1, Q_BLOCK, pair), lambda b, hp, i: (b, i, hp)),
        out_shape=jax.ShapeDtypeStruct((batch, seq, A_WIDTH), BF16),
        compiler_params=_params("parallel", "parallel", "arbitrary", vmem_mb=48),
        name="mixer_a",
    )(q, q, q, pb, pb, pb, pb, kv1, kv1, kv2, kv2, *biases)
    return o.reshape(batch, nblk, 16, r8, A_WIDTH).transpose(0, 1, 3, 2, 4).reshape(batch, seq, A_WIDTH)


def _mixb_kernel(qb_ref, kb_ref, vb_ref, iq_ref, ik_ref, misc_ref, tri_ref, o_ref, keys_ref, selb_ref, s_ref, *,
                 top_k):
    rows = Q_BLOCK
    nparts = keys_ref.shape[1] // rows
    kf = float(top_k)
    parts = [_mixb_part(pt, rows, kf, qb_ref, kb_ref, vb_ref, iq_ref, ik_ref, misc_ref, tri_ref, o_ref,
                        keys_ref, selb_ref, s_ref) for pt in range(nparts)]
    for part in parts:
        part.score()
    ans = tuple(jnp.where(part.total(part.count(lambda kc: kc >= 0)) >= kf, 0, INT_MIN).astype(I32) for part in parts)

    def body(it, ans):
        bit = jnp.left_shift(jnp.int32(1), 30 - it)
        cands = [a + bit for a in ans]
        raw = [part.count(lambda kc, cand=cand: kc >= cand) for part, cand in zip(parts, cands)]
        return tuple(jnp.where(part.total(r) >= kf, cand, a) for part, r, cand, a in zip(parts, raw, cands, ans))

    thr = lax.fori_loop(0, 31, body, ans)
    for part, t in zip(parts, thr):
        part.attend(t)


def _mixb_part(pt, rows, kf, qb_ref, kb_ref, vb_ref, iq_ref, ik_ref, misc_ref, tri_ref, o_ref,
               keys_ref, selb_ref, s_ref):
    ch = keys_ref.shape[2]
    rs = pl.ds(pt * rows, rows)
    start = pl.program_id(1) * keys_ref.shape[1] + pt * rows
    nch = (start + rows + ch - 1) // ch
    qpos = start + lax.broadcasted_iota(I32, (rows, 1), 0)
    zero_col = jnp.zeros((rows, 1), F32)
    part = types.SimpleNamespace()

    def score():
        w = misc_ref[0, rs, 0:IDX_HEADS] * (IDX_DIM ** -0.5 * IDX_HEADS ** -0.5)
        iq = iq_ref[0, rs, :]
        lax.fori_loop(0, nch, functools.partial(score_chunk, w, iq), 0)

    def score_chunk(w, iq, c, carry):
        off = pl.multiple_of(c * ch, ch)
        ik = ik_ref[0, pl.ds(off, ch), 0:IDX_DIM]
        acc = jnp.zeros((rows, ch), F32)
        for h in range(IDX_HEADS):
            lg = _dot_nt(iq[:, h * IDX_DIM:(h + 1) * IDX_DIM], ik)
            acc = acc + jnp.maximum(lg, 0.0) * w[:, h:h + 1]
        acc = acc + 0.0
        bits = lax.bitcast_convert_type(acc, I32)
        key = bits ^ ((bits >> 31) & 0x7FFFFFFF)
        kpos = off + lax.broadcasted_iota(I32, (rows, ch), 1)
        keys_ref[c, rs, :] = jnp.where(kpos <= qpos, key, INT_MIN)
        return carry

    def count(pred):
        def chunk(c, cnt):
            kc = keys_ref[c, rs, :]
            for j in range(ch // LANES):
                cnt = cnt + jnp.where(pred(kc[:, j * LANES:(j + 1) * LANES]), 1.0, 0.0)
            return cnt
        return lax.fori_loop(0, nch, chunk, jnp.zeros((rows, LANES), F32))

    def total(cnt):
        return jnp.sum(cnt, axis=-1, keepdims=True)

    part.score, part.count, part.total = score, count, total
    part.attend = functools.partial(_mixb_attend, part, pt, rows, kf, nch, zero_col, qb_ref, kb_ref, vb_ref,
                                    tri_ref, o_ref, keys_ref, selb_ref, s_ref)
    return part


def _mixb_attend(part, pt, rows, kf, nch, zero_col, qb_ref, kb_ref, vb_ref, tri_ref, o_ref, keys_ref, selb_ref, s_ref,
                 thr):
    ch = keys_ref.shape[2]
    rs = pl.ds(pt * rows, rows)
    count = lambda pred: part.total(part.count(pred))
    need = kf - count(lambda kc: kc > thr)

    tw = tri_ref.shape[0]

    def select_chunk(c, run):
        kc_all = keys_ref[c, rs, :]
        pieces = [kc_all[:, j * tw:(j + 1) * tw] for j in range(ch // tw)]
        eqf = [jnp.where(kc == thr, 1.0, 0.0) for kc in pieces]
        pre = [_dot(e.astype(BF16), tri_ref[...]) for e in eqf]
        for j, kc in enumerate(pieces):
            take = jnp.where(pre[j] + run <= need, eqf[j], 0.0)
            sel = jnp.where(kc > thr, 1.0, take)
            sel = jnp.where(kc != INT_MIN, sel, 0.0)
            selb_ref[c, rs, j * tw:(j + 1) * tw] = jnp.where(sel > 0.5, 0.0, NEG_INF)
            run = run + jnp.sum(eqf[j], axis=-1, keepdims=True)
        return run

    def select_chunk_no_ties(c, carry):
        kc = keys_ref[c, rs, :]
        selb_ref[c, rs, :] = jnp.where(kc >= thr, jnp.where(kc != INT_MIN, 0.0, NEG_INF), NEG_INF)
        return carry

    excess = jnp.max(count(lambda kc: kc >= thr)) > kf
    lax.cond(excess,
             lambda: lax.fori_loop(0, nch, select_chunk, zero_col),
             lambda: lax.fori_loop(0, nch, select_chunk_no_ties, zero_col))

    heads = range(B_HEADS)
    lane = lax.broadcasted_iota(I32, (1, B_WIDTH), 1) // HEAD_DIM
    qs = (qb_ref[0, rs, :].astype(F32) * HEAD_DIM ** -0.5).astype(BF16)
    qm_all = jnp.concatenate([jnp.where(lane == h, qs, jnp.zeros_like(qs)) for h in heads], axis=0)
    groups =[slice(j * LANES, (j + 1) * LANES) for j in range(ch // LANES)]

    def per_lane(cols):
        out = cols[B_HEADS - 1]
        for h in reversed(range(B_HEADS - 1)):
            out = jnp.where(lane <= h, cols[h], out)
        return out

    def score_pass(c, mx):
        off = pl.multiple_of(c * ch, ch)
        bias = selb_ref[c, rs, :]
        kc = kb_ref[0, pl.ds(off, ch), :]
        s_all = _dot_nt(qm_all, kc)
        out = []
        for h in heads:
            s = s_all[h * rows:(h + 1) * rows] + bias
            s_ref[h, c] = s
            m = mx[h]
            for g in groups:
                m = jnp.maximum(m, s[:, g])
            out.append(m)
        return tuple(out)

    mx = lax.fori_loop(0, nch, score_pass, (jnp.full((rows, LANES), NEG_INF, F32),) * B_HEADS)
    m = [jnp.max(mx[h], axis=-1, keepdims=True) for h in heads]

    def value_pass(c, carry):
        off = pl.multiple_of(c * ch, ch)
        vc = vb_ref[0, pl.ds(off, ch), :]
        p = [jnp.exp(s_ref[h, c] - m[h]) for h in heads]
        ls = []
        for h in heads:
            l = carry[h]
            for g in groups:
                l = l + p[h][:, g]
            ls.append(l)
        pv_all = _dot(jnp.concatenate([p[h].astype(BF16) for h in heads], axis=0), vc)
        pv = [pv_all[h * rows:(h + 1) * rows] for h in heads]
        return tuple(ls) + (carry[B_HEADS] + per_lane(pv),)

    init = (jnp.zeros((rows, LANES), F32),) * B_HEADS + (jnp.zeros((rows, B_WIDTH), F32),)
    res = lax.fori_loop(0, nch, value_pass, init)
    l = [jnp.sum(res[h], axis=-1, keepdims=True) for h in heads]
    o_ref[0, rs, :] = (res[B_HEADS] / per_lane(l)).astype(o_ref.dtype)


def _mixer_b(pb, pf, tri, batch, seq):
    rows = min(4 * Q_BLOCK, seq)
    top_k = min(TOPK_MAX, seq // 4)
    ch = min(512, seq)
    kern = functools.partial(_mixb_kernel, top_k=top_k)
    return pl.pallas_call(
        kern,
        grid=(batch, seq // rows),
        in_specs=[pl.BlockSpec((1, rows, B_WIDTH), lambda b, i: (b, i, PB_BQ // B_WIDTH)),
                  pl.BlockSpec((1, seq, B_WIDTH), lambda b, i: (b, 0, PB_BK // B_WIDTH)),
                  pl.BlockSpec((1, seq, B_WIDTH), lambda b, i: (b, 0, PB_BV // B_WIDTH)),
                  pl.BlockSpec((1, rows, 256), lambda b, i: (b, i, PB_IQ // 256)),
                  pl.BlockSpec((1, seq, LANES), lambda b, i: (b, 0, PB_IK // LANES)),
                  pl.BlockSpec((1, rows, LANES), lambda b, i: (b, i, PF_MISC // LANES)),
                  pl.BlockSpec(tri.shape, lambda b, i: (0, 0))],
        out_specs=pl.BlockSpec((1, rows, B_WIDTH), lambda b, i: (b, i, 0)),
        out_shape=jax.ShapeDtypeStruct((batch, seq, B_WIDTH), BF16),
        scratch_shapes=[pltpu.VMEM((seq // ch, rows, ch), I32), pltpu.VMEM((seq // ch, rows, ch), F32),
                        pltpu.VMEM((B_HEADS, seq // ch, Q_BLOCK, ch), F32)],
        compiler_params=_params("parallel", "arbitrary", vmem_mb=48),
        name="mixer_b",
    )(pb, pb, pb, pb, pb, pf, tri)


def _head_consts():
    lane = np.arange(C_WIDTH)
    same = (lane[:, None] // HEAD_DIM == lane[None, :] // HEAD_DIM).astype(np.float32)
    return jnp.asarray(same)


def _rwkv1_kernel(cin_ref, prev_ref, vf_ref, mu_ref, vec_ref, lora_ref, vl_ref, bd_ref,
                  phi_ref, psi_ref, rp_ref, y0_ref, g_ref, bonus_ref, v_ref, *, rows, use_v_lora):
    i = pl.program_id(1)
    pc = cin_ref[0][:, :C_IN]
    prev = jnp.where(i > 0, prev_ref[0][SUBLANES - 1:SUBLANES, :C_IN], 0.0)
    row = lax.broadcasted_iota(I32, (rows, 1), 0)
    shifted = jnp.where(row == 0, prev, pltpu.roll(pc, 1, 0))
    pc = pc + (shifted - pc) * mu_ref[...]
    r = pc[:, 0:C_WIDTH]
    k = pc[:, C_WIDTH:2 * C_WIDTH]
    v = pc[:, 2 * C_WIDTH:3 * C_WIDTH]
    xl = pc[:, 3 * C_WIDTH:C_IN]
    lane = lax.broadcasted_iota(I32, xl.shape, 1)
    feat = jnp.where(lane < C_LORA_W, jnp.tanh(xl), jnp.where(lane < C_LORA_W + C_LORA_A, xl, _sigmoid(xl)))
    w0, a0, k_k, k_a, r_k = (vec_ref[n:n + 1, :] for n in range(5))
    bd = bd_ref[...]
    bd_sel = bd.astype(BF16)
    w_raw = -_softplus(-(w0 + _dot3(feat, lora_ref[0]))) - 0.5
    a = _sigmoid(a0 + _dot3(feat, lora_ref[1]))
    g_ref[0] = _dot3(feat, lora_ref[2])
    if use_v_lora:
        v0 = vec_ref[5:6, :]
        v = v + (vf_ref[0] - v) * _sigmoid(v0 + _dot3(_dot3(v, vl_ref[0]), vl_ref[1]))
    v_ref[0] = v
    kk = k * k_k
    kk = kk / jnp.maximum(jnp.sqrt(_dot_sel(bd_sel, kk * kk, left=False)), 1e-12)
    k = k * (1.0 + (a - 1.0) * k_a)
    logw = -jnp.exp(w_raw)
    bonus_ref[0] = _dot_sel(bd_sel, r * k * r_k, left=False) * v
    av = -kk
    bv = kk * a

    cc = C_CHUNK
    ri = lax.broadcasted_iota(I32, (cc, cc), 0)
    ci = lax.broadcasted_iota(I32, (cc, cc), 1)
    tril = jnp.where(ci <= ri, 1.0, 0.0).astype(BF16)
    strict = ci < ri
    incl = ci <= ri
    eye_c = jnp.where(ci == ri, 1.0, 0.0)
    lane_w = lax.broadcasted_iota(I32, (1, C_WIDTH), 1)
    ri2 = lax.broadcasted_iota(I32, (C_WIDTH, C_WIDTH), 0)
    ci2 = lax.broadcasted_iota(I32, (C_WIDTH, C_WIDTH), 1)
    eye_w = jnp.where(ri2 == ci2, 1.0, 0.0)
    chunks = range(rows // cc)
    heads = range(C_HEADS)
    pairs = [(c, h) for c in chunks for h in heads]
    mh = [jnp.where(lane_w // HEAD_DIM == h, 1.0, 0.0) for h in heads]
    sl = [slice(c * cc, (c + 1) * cc) for c in chunks]
    cs = [_dot_sel(tril, logw[sl[c]]) for c in chunks]
    cs_end = [cs[c][cc - 1:cc, :] for c in chunks]
    at = [av[sl[c]] * jnp.exp(cs[c] - logw[sl[c]]) for c in chunks]
    rt = [r[sl[c]] * jnp.exp(cs[c]) for c in chunks]
    inv = [jnp.exp(-cs[c]) for c in chunks]
    rhs = [jnp.concatenate([bv[sl[c]] * inv[c], k[sl[c]] * inv[c]], axis=0) for c in chunks]
    tail = [jnp.exp(cs_end[c] - cs[c]) for c in chunks]
    vc = [v[sl[c]] for c in chunks]
    vc_b = [vc[c].astype(BF16) for c in chunks]
    ath = {(c, h): at[c] * mh[h] for c, h in pairs}
    aa = {(c, h): _dot3(jnp.concatenate([ath[c, h], rt[c] * mh[h]], axis=0), rhs[c], nt=True) for c, h in pairs}
    a_ab = {p: jnp.where(strict, aa[p][:cc, :cc], 0.0) for p in pairs}
    a_ak = {p: jnp.where(strict, aa[p][:cc, cc:], 0.0).astype(BF16) for p in pairs}
    a_rb = {p: jnp.where(incl, aa[p][cc:, :cc], 0.0).astype(BF16) for p in pairs}
    a_rk = {p: jnp.where(incl, aa[p][cc:, cc:], 0.0).astype(BF16) for p in pairs}
    x = {p: eye_c + a_ab[p] for p in pairs}
    pw = a_ab
    for _ in range(int(math.log2(cc)) - 1):
        pw = {p: _dot3(pw[p], pw[p]) for p in pairs}
        x = {p: x[p] + _dot3(x[p], pw[p]) for p in pairs}
    akv = {(c, h): _dot(a_ak[c, h], vc_b[c]) for c, h in pairs}
    ap_h = {p: _dot3(x[p], ath[p]) for p in pairs}
    w2_h = {(c, h): _dot3(x[c, h], akv[c, h]) * mh[h] for c, h in pairs}
    rp_h = {p: _dot(a_rb[p], ap_h[p].astype(BF16)) for p in pairs}
    y0_h = {(c, h): (_dot(a_rb[c, h], w2_h[c, h].astype(BF16)) + _dot(a_rk[c, h], vc_b[c])) * mh[h] for c, h in pairs}
    for c in chunks:
        ap = sum(ap_h[c, h] for h in heads)
        w2 = sum(w2_h[c, h] for h in heads)
        rp_ref[0, sl[c], :] = rt[c] + sum(rp_h[c, h] for h in heads)
        y0_ref[0, sl[c], :] = sum(y0_h[c, h] for h in heads)
        bh = bv[sl[c]] * tail[c]
        kh = k[sl[c]] * tail[c]
        phi_ref[0, c] = eye_w * jnp.exp(cs_end[c]) + _dot3(bh.T, ap) * bd
        psi_ref[0, c] = _dot(jnp.concatenate([bh, kh], axis=0).T.astype(BF16),
                             jnp.concatenate([w2, vc[c]], axis=0).astype(BF16)) * bd


def _rwkv2_kernel(phi_ref, psi_ref, rp_ref, y0_ref, g_ref, bonus_ref, gn_ref, bd_ref, o_ref, s_ref):
    @pl.when(pl.program_id(1) == 0)
    def _():
        s_ref[...] = jnp.zeros_like(s_ref)

    states = [s_ref[...]]
    for c in range(phi_ref.shape[1]):
        states.append(_dot3(phi_ref[0, c], states[c]) + psi_ref[0, c])
    s_ref[...] = states[-1]
    cc = C_CHUNK
    y = jnp.concatenate([_dot3(rp_ref[0, c * cc:(c + 1) * cc, :], states[c]) for c in range(phi_ref.shape[1])],
                        axis=0) + y0_ref[0]
    bd_sel = bd_ref[...].astype(BF16)
    mu = _dot_sel(bd_sel, y, left=False) * (1.0 / HEAD_DIM)
    d = y - mu
    var = _dot_sel(bd_sel, d * d, left=False) * (1.0 / HEAD_DIM)
    yn = d * lax.rsqrt(var + C_GN_EPS) * gn_ref[0:1, :] + gn_ref[1:2, :]
    o_ref[0] = ((yn + bonus_ref[0]) * g_ref[0]).astype(o_ref.dtype)


def _mixer_c(pf, v_first, mu, vec, lora, vl, gn, bd, batch, seq, use_v_lora):
    rows = min(256, seq)
    nblk = seq // rows
    nch = seq // C_CHUNK
    cpb = rows // C_CHUNK
    kern = functools.partial(_rwkv1_kernel, rows=rows, use_v_lora=use_v_lora)
    full2 = lambda a: pl.BlockSpec(a.shape, lambda b, i: (0,) * a.ndim)
    seq_spec = pl.BlockSpec((1, rows, C_WIDTH), lambda b, i: (b, i, 0))
    mat_spec = pl.BlockSpec((1, cpb, C_WIDTH, C_WIDTH), lambda b, i: (b, i, 0, 0))
    seq_shape = jax.ShapeDtypeStruct((batch, seq, C_WIDTH), F32)
    mat_shape = jax.ShapeDtypeStruct((batch, nch, C_WIDTH, C_WIDTH), F32)
    prev_blk = rows // SUBLANES
    phi, psi, rp, y0, g, bonus, v = pl.pallas_call(
        kern,
        grid=(batch, nblk),
        in_specs=[pl.BlockSpec((1, rows, 1024), lambda b, i: (b, i, PF_CIN // 1024)),
                  pl.BlockSpec((1, SUBLANES, 1024), lambda b, i: (b, jnp.maximum(i * prev_blk - 1, 0), PF_CIN // 1024)),
                  seq_spec, full2(mu), full2(vec), full2(lora), full2(vl), full2(bd)],
        out_specs=[mat_spec, mat_spec, seq_spec, seq_spec, seq_spec, seq_spec, seq_spec],
        out_shape=[mat_shape, mat_shape, seq_shape, seq_shape, seq_shape, seq_shape, seq_shape],
        compiler_params=_params("parallel", "parallel", vmem_mb=48),
        name="rwkv_chunks",
    )(pf, pf, v_first, mu, vec, lora, vl, bd)
    scan_chunks = min(4, nch)
    cseq = pl.BlockSpec((1, scan_chunks * C_CHUNK, C_WIDTH), lambda b, c: (b, c, 0))
    cmat = pl.BlockSpec((1, scan_chunks, C_WIDTH, C_WIDTH), lambda b, c: (b, c, 0, 0))
    o = pl.pallas_call(
        _rwkv2_kernel,
        grid=(batch, nch // scan_chunks),
        in_specs=[cmat, cmat, cseq, cseq, cseq, cseq,
                  pl.BlockSpec(gn.shape, lambda b, c: (0, 0)), pl.BlockSpec(bd.shape, lambda b, c: (0, 0))],
        out_specs=cseq,
        out_shape=jax.ShapeDtypeStruct((batch, seq, C_WIDTH), BF16),
        scratch_shapes=[pltpu.VMEM((C_WIDTH, C_WIDTH), F32)],
        compiler_params=_params("parallel", "arbitrary"),
        name="rwkv_scan",
    )(phi, psi, rp, y0, g, bonus, gn, bd)
    return o, v


def _ssd_consts():
    expand = np.zeros((LANES, D_INNER), np.float32)
    for h in range(D_HEADS):
        expand[MISC_DT + h, h * HEAD_DIM:(h + 1) * HEAD_DIM] = 1.0
    return jnp.asarray(expand)


def _ssd_kernel(xbc_ref, z_ref, misc_ref, conv_ref, vec_ref, hp_ref, expand_ref, bd_ref, o_ref, st_ref, prev_ref):
    @pl.when(pl.program_id(1) == 0)
    def _():
        st_ref[...] = jnp.zeros_like(st_ref)
        prev_ref[...] = jnp.zeros_like(prev_ref)

    q = D_CHUNK
    x_raw = xbc_ref[0]
    ext = jnp.concatenate([prev_ref[...], x_raw], axis=0)
    conv = jnp.zeros((q, D_XBC), F32)
    for t in range(D_CONV):
        lo = SUBLANES - (D_CONV - 1) + t
        conv = conv + ext[lo:lo + q, :] * conv_ref[t:t + 1, :]
    prev_ref[...] = x_raw[q - SUBLANES:, :]
    xbc = _silu(conv + conv_ref[D_CONV:D_CONV + 1, :])
    xs = xbc[:, :D_INNER]
    bm = xbc[:, D_INNER:D_INNER + D_GROUPS * D_STATE]
    cm = xbc[:, D_INNER + D_GROUPS * D_STATE:]

    dt_col = _softplus(misc_ref[0] + hp_ref[0:1, :])
    a_col = dt_col * hp_ref[1:2, :]
    ri = lax.broadcasted_iota(I32, (q, q), 0)
    ci = lax.broadcasted_iota(I32, (q, q), 1)
    causal = ci <= ri
    tril = jnp.where(causal, 1.0, 0.0).astype(BF16)
    acs_col = _dot_sel(tril, a_col)
    acs_row = acs_col.T
    expand = expand_ref[...].astype(BF16)
    acs = _dot_sel(expand, acs_col, left=False)
    dt = _dot_sel(expand, dt_col, left=False)
    acs_end = acs[q - 1:q, :]
    xdt = xs * dt

    lane = lax.broadcasted_iota(I32, (1, LANES), 1)
    lane_w = lax.broadcasted_iota(I32, (1, D_INNER), 1)
    left = lane < D_STATE
    bm_sw = pltpu.roll(bm, D_STATE, 1)
    cm_sw = pltpu.roll(cm, D_STATE, 1)
    b_exp = jnp.concatenate([jnp.where(left, bm, bm_sw), jnp.where(left, bm_sw, bm)], axis=1)
    c_exp = jnp.concatenate([jnp.where(left, cm, cm_sw), jnp.where(left, cm_sw, cm)], axis=1)
    cb = [_dot3(jnp.where(left == (g == 0), cm, 0.0), bm, nt=True) for g in range(D_GROUPS)]

    scores = []
    for h in range(D_HEADS):
        col = acs_col[:, MISC_DT + h:MISC_DT + h + 1]
        rw = acs_row[MISC_DT + h:MISC_DT + h + 1, :]
        decay = jnp.exp(jnp.where(causal, col - rw, NEG_INF))
        scores.append(cb[h // (D_HEADS // D_GROUPS)] * decay)
    y_h = [_dot3(scores[h], xdt) for h in range(D_HEADS)]
    y = jnp.zeros((q, D_INNER), F32)
    for h in range(D_HEADS):
        y = y + jnp.where(lane_w // HEAD_DIM == h, y_h[h], 0.0)
    st = st_ref[...]
    y = y + _dot3(c_exp, st) * jnp.exp(acs)
    st_ref[...] = st * jnp.exp(acs_end) + _dot3(b_exp.T, xdt * jnp.exp(acs_end - acs)) * bd_ref[...]
    y = y + xs * vec_ref[0:1, :]
    y = y * _silu(z_ref[0])
    half = D_INNER // D_GROUPS
    outs = []
    for g in range(D_GROUPS):
        yg = y[:, g * half:(g + 1) * half]
        outs.append(yg * lax.rsqrt(jnp.mean(yg * yg, axis=-1, keepdims=True) + D_NORM_EPS))
    o_ref[0] = (jnp.concatenate(outs, axis=1) * vec_ref[1:2, :]).astype(o_ref.dtype)


def _mixer_d(pf, conv, vec, hp, expand, bd, batch, seq):
    q = D_CHUNK
    full = lambda a: pl.BlockSpec(a.shape, lambda b, i: (0, 0))
    return pl.pallas_call(
        _ssd_kernel,
        grid=(batch, seq // q),
        in_specs=[pl.BlockSpec((1, q, D_XBC), lambda b, i: (b, i, PF_XBC // D_XBC)),
                  pl.BlockSpec((1, q, D_INNER), lambda b, i: (b, i, PF_Z // D_INNER)),
                  pl.BlockSpec((1, q, LANES), lambda b, i: (b, i, PF_MISC // LANES)),
                  full(conv), full(vec), full(hp), full(expand), full(bd)],
        out_specs=pl.BlockSpec((1, q, D_INNER), lambda b, i: (b, i, 0)),
        out_shape=jax.ShapeDtypeStruct((batch, seq, D_INNER), BF16),
        scratch_shapes=[pltpu.VMEM((D_INNER, D_INNER), F32), pltpu.VMEM((SUBLANES, D_XBC), F32)],
        compiler_params=_params("parallel", "arbitrary"),
        name="ssd",
    )(pf, pf, pf, conv, vec, hp, expand, bd)


def _merge_kernel(x_ref, gates_ref, oa_ref, ob_ref, oc_ref, od_ref, wbr_ref, wout_ref, ln_ref, wr_ref, br_ref,
                  x1_ref, route_ref):
    acc = None
    for n, o_ref in enumerate((oa_ref, ob_ref, oc_ref, od_ref)):
        term = _sigmoid(gates_ref[:, n * D_MODEL:(n + 1) * D_MODEL]) * _dot(o_ref[...], wbr_ref[n])
        acc = term if acc is None else acc + term
    h = DEEPNORM_ALPHA * x_ref[...] + _dot(acc.astype(BF16), wout_ref[...])
    x1 = _layer_norm(h, ln_ref[0:1, :], ln_ref[1:2, :])
    x1_ref[...] = x1

    logits = _dot3(x1, wr_ref[...]) + br_ref[...]
    lane = lax.broadcasted_iota(I32, logits.shape, 1)
    big = jnp.int32(LANES)
    gl = jnp.where(lane < N_EXPERT_GROUPS, logits, NEG_INF)
    gm = jnp.max(gl, axis=-1, keepdims=True)
    pg_top = 1.0 / jnp.sum(jnp.exp(gl - gm), axis=-1, keepdims=True)
    g_sel = jnp.min(jnp.where(gl == gm, lane, big), axis=-1, keepdims=True)
    off = N_EXPERT_GROUPS + g_sel * EXPERTS_PER_GROUP
    el = jnp.where((lane >= off) & (lane < off + EXPERTS_PER_GROUP), logits, NEG_INF)
    em = jnp.max(el, axis=-1, keepdims=True)
    es = jnp.sum(jnp.exp(el - em), axis=-1, keepdims=True)
    idx1 = jnp.min(jnp.where(el == em, lane, big), axis=-1, keepdims=True)
    el2 = jnp.where(lane == idx1, NEG_INF, el)
    em2 = jnp.max(el2, axis=-1, keepdims=True)
    idx2 = jnp.min(jnp.where(el2 == em2, lane, big), axis=-1, keepdims=True)
    p1 = 1.0 / es
    p2 = jnp.exp(em2 - em) / es
    gate1 = pg_top * p1 / (p1 + p2)
    gate2 = pg_top * p2 / (p1 + p2)
    e1 = (idx1 - N_EXPERT_GROUPS).astype(F32)
    e2 = (idx2 - N_EXPERT_GROUPS).astype(F32)
    route_ref[...] = jnp.where(lane == 0, e1, jnp.where(lane == 1, e2, jnp.where(lane == 2, gate1,
                               jnp.where(lane == 3, gate2, 0.0))))


def _merge(x2d, pf, oa, ob, oc, od, wbr, wout, ln, wr, br, tm):
    m = x2d.shape[0]
    row = lambda w: pl.BlockSpec((tm, w), lambda i: (i, 0))
    full = lambda a: pl.BlockSpec(a.shape, lambda i: (0,) * a.ndim)
    return pl.pallas_call(
        _merge_kernel,
        grid=(m // tm,),
        in_specs=[row(D_MODEL), pl.BlockSpec((tm, N_BRANCH * D_MODEL), lambda i: (i, PF_GATES)),
                  row(A_WIDTH), row(B_WIDTH), row(C_WIDTH), row(D_INNER),
                  full(wbr), full(wout), full(ln), full(wr), full(br)],
        out_specs=[row(D_MODEL), row(LANES)],
        out_shape=[jax.ShapeDtypeStruct((m, D_MODEL), F32), jax.ShapeDtypeStruct((m, LANES), F32)],
        compiler_params=_params("parallel", vmem_mb=48),
        name="merge_route",
    )(x2d, pf, oa, ob, oc, od, wbr, wout, ln, wr, br)


def _ffn_kernel(be_ref, nv_ref, xs_ref, wg_ref, wu_ref, wd_ref, o_ref):
    i = pl.program_id(0)

    @pl.when(i < nv_ref[0])
    def _():
        xb = xs_ref[...]
        hg = _dot(xb, wg_ref[0, 0].astype(BF16))
        hu = _dot(xb, wu_ref[0, 0].astype(BF16))
        o_ref[...] = _dot((_silu(hg) * hu).astype(BF16), wd_ref[0, 0].astype(BF16))

    @pl.when(i >= nv_ref[0])
    def _():
        o_ref[...] = jnp.zeros_like(o_ref)


def _ffn(blk_expert, n_valid, xs, e_gate, e_up, e_down, layer):
    cap = xs.shape[0]
    wspec = lambda a: pl.BlockSpec((1, 1) + a.shape[2:], lambda i, be, nv: (layer, be[i], 0, 0))
    return pl.pallas_call(
        _ffn_kernel,
        grid_spec=pltpu.PrefetchScalarGridSpec(
            num_scalar_prefetch=2,
            grid=(cap // MOE_ROWS,),
            in_specs=[pl.BlockSpec((MOE_ROWS, D_MODEL), lambda i, be, nv: (i, 0)),
                      wspec(e_gate), wspec(e_up), wspec(e_down)],
            out_specs=pl.BlockSpec((MOE_ROWS, D_MODEL), lambda i, be, nv: (i, 0)),
        ),
        out_shape=jax.ShapeDtypeStruct((cap, D_MODEL), F32),
        compiler_params=_params("arbitrary", vmem_mb=48),
        name="expert_ffn",
    )(blk_expert, n_valid, xs, e_gate, e_up, e_down)


def _combine_kernel(dest_ref, x_ref, route_ref, ln_ref, yb_ref, o_ref, ybuf, sem):
    i = pl.program_id(0)
    tm = x_ref.shape[0]

    def gather(tile, slot):
        base = tile * (TOP_K_EXPERTS * tm)

        def body(r, carry):
            for s in range(TOP_K_EXPERTS):
                row = dest_ref[base + TOP_K_EXPERTS * r + s]
                pltpu.make_async_copy(yb_ref.at[pl.ds(row, 1)], ybuf.at[slot, pl.ds(s * tm + r, 1)],
                                      sem.at[slot]).start()
            return carry

        lax.fori_loop(0, tm, body, 0, unroll=8)

    @pl.when(i == 0)
    def _():
        gather(0, 0)

    @pl.when(i + 1 < pl.num_programs(0))
    def _():
        gather(i + 1, (i + 1) % 2)

    slot = i % 2
    pltpu.make_async_copy(yb_ref.at[pl.ds(0, TOP_K_EXPERTS * tm)], ybuf.at[slot], sem.at[slot]).wait()
    g0 = route_ref[:, 2:3]
    g1 = route_ref[:, 3:4]
    h = DEEPNORM_ALPHA * x_ref[...] + (ybuf[slot, 0:tm, :] * g0 + ybuf[slot, tm:2 * tm, :] * g1)
    o_ref[...] = _layer_norm(h, ln_ref[0:1, :], ln_ref[1:2, :])


def _combine(x1, yb, dest, route, ln, tm):
    m = x1.shape[0]
    return pl.pallas_call(
        _combine_kernel,
        grid_spec=pltpu.PrefetchScalarGridSpec(
            num_scalar_prefetch=1,
            grid=(m // tm,),
            in_specs=[pl.BlockSpec((tm, D_MODEL), lambda i, d: (i, 0)),
                      pl.BlockSpec((tm, LANES), lambda i, d: (i, 0)),
                      pl.BlockSpec(ln.shape, lambda i, d: (0, 0)),
                      pl.BlockSpec(memory_space=pl.ANY)],
            out_specs=pl.BlockSpec((tm, D_MODEL), lambda i, d: (i, 0)),
            scratch_shapes=[pltpu.VMEM((2, TOP_K_EXPERTS * tm, D_MODEL), F32), pltpu.SemaphoreType.DMA((2,))],
        ),
        out_shape=jax.ShapeDtypeStruct((m, D_MODEL), F32),
        compiler_params=_params("arbitrary"),
        name="combine",
    )(dest, x1, route, ln, yb)


def _dispatch_tables(route, m):
    flat_e = route[:, 0:TOP_K_EXPERTS].astype(I32).reshape(-1)
    n_assign = m * TOP_K_EXPERTS
    onehot = (flat_e[:, None] == jnp.arange(N_EXPERTS, dtype=I32)[None, :]).astype(I32)
    csum = jnp.cumsum(onehot, axis=0)
    rank = jnp.sum(csum * onehot, axis=1) - 1
    counts = csum[-1]
    padded = (counts + MOE_ROWS - 1) // MOE_ROWS * MOE_ROWS
    pad_end = jnp.cumsum(padded)
    pad_start = pad_end - padded
    dest = pad_start[flat_e] + rank
    cap = (n_assign + N_EXPERTS * (MOE_ROWS - 1) + MOE_ROWS - 1) // MOE_ROWS * MOE_ROWS
    n_blocks = cap // MOE_ROWS
    blk_start = jnp.arange(n_blocks, dtype=I32) * MOE_ROWS
    blk_expert = jnp.minimum(jnp.sum((pad_end[None, :] <= blk_start[:, None]).astype(I32), axis=1), N_EXPERTS - 1)
    buf_tok = (jnp.arange(cap, dtype=I32) % m).at[dest].set(jnp.arange(n_assign, dtype=I32) // TOP_K_EXPERTS)
    n_valid = (pad_end[-1] // MOE_ROWS).astype(I32).reshape(1)
    return dest, buf_tok, blk_expert, n_valid


def _row_pad(a, rows):
    return jnp.pad(a, ((0, rows - a.shape[0]), (0, 0)))


def kernel(x, w_in, c_mu, c_w0, c_w2, c_a0, c_a2, c_g2, c_kk, c_ka, c_rk, c_gn_w, c_gn_b, c_v0, c_v1, c_v2,
           d_conv_w, d_conv_b, d_dt_bias, d_a_log, d_skip, d_norm_w, w_branch, w_out, ln1_g, ln1_b,
           r_group, r_group_b, r_expert, r_expert_b, e_gate, e_up, e_down, ln2_g, ln2_b):
    batch, seq, _ = x.shape
    m = batch * seq
    biases = _mixa_class_bias()
    tw = min(256, seq)
    tri = jnp.asarray(np.triu(np.ones((tw, tw), np.float32))).astype(BF16)
    bd = _head_consts()
    expand = _ssd_consts()
    x2d = x.reshape(m, D_MODEL)
    v_first = jnp.zeros((batch, seq, C_WIDTH), F32)
    tm_proj = min(2048, m)
    tm_tok = min(512, m)
    for l in range(DEPTH):
        wb, wf = _proj_weights(w_in[l])
        pb = _proj(x2d, wb, BF16, tm_proj, PROJ_TN).reshape(batch, seq, PB_WIDTH)
        pf = _proj(x2d, wf, F32, tm_proj, PROJ_TN).reshape(batch, seq, PF_WIDTH)

        o_a = _mixer_a2(pb, biases, batch, seq)
        o_b = _mixer_b(pb, pf, tri, batch, seq)

        use_v_lora = l > 0
        vec_rows = [c_w0[l], c_a0[l], c_kk[l], c_ka[l], c_rk[l].reshape(-1)]
        vec_rows.append(c_v0[l - 1] if use_v_lora else jnp.zeros((C_WIDTH,), F32))
        vec = _row_pad(jnp.stack(vec_rows), SUBLANES)
        lora = jnp.stack([
            jnp.pad(c_w2[l], ((0, LANES - C_LORA_W), (0, 0))),
            jnp.pad(c_a2[l], ((C_LORA_W, LANES - C_LORA_W - C_LORA_A), (0, 0))),
            jnp.pad(c_g2[l], ((C_LORA_W + C_LORA_A, 0), (0, 0)))])
        if use_v_lora:
            vl = jnp.stack([jnp.pad(c_v1[l - 1], ((0, 0), (0, C_WIDTH - C_LORA_V))),
                            jnp.pad(c_v2[l - 1], ((0, C_WIDTH - C_LORA_V), (0, 0)))])
        else:
            vl = jnp.zeros((2, C_WIDTH, C_WIDTH), F32)
        gn = _row_pad(jnp.stack([c_gn_w[l], c_gn_b[l]]), SUBLANES)
        o_c, v_c = _mixer_c(pf, v_first, c_mu[l].reshape(1, C_IN), vec, lora, vl, gn, bd, batch, seq, use_v_lora)
        if l == 0:
            v_first = v_c

        conv = _row_pad(jnp.concatenate([d_conv_w[l], d_conv_b[l][None, :]], axis=0), SUBLANES)
        dvec = _row_pad(jnp.stack([jnp.repeat(d_skip[l], HEAD_DIM), d_norm_w[l]]), SUBLANES)
        place = lambda a: jnp.pad(a, (MISC_DT, LANES - MISC_DT - D_HEADS))
        hp = _row_pad(jnp.stack([place(d_dt_bias[l]), place(-jnp.exp(d_a_log[l]))]), SUBLANES)
        o_d = _mixer_d(pf, conv, dvec, hp, expand, bd, batch, seq)

        wr = jnp.pad(jnp.concatenate([r_group[l], r_expert[l]], axis=1),
                     ((0, 0), (0, LANES - N_EXPERT_GROUPS - N_EXPERTS)))
        br = jnp.pad(jnp.concatenate([r_group_b[l], r_expert_b[l]]), (0, LANES - N_EXPERT_GROUPS - N_EXPERTS))
        ln1 = _row_pad(jnp.stack([ln1_g[l], ln1_b[l]]), SUBLANES)
        x1, route = _merge(x2d, pf.reshape(m, PF_WIDTH), o_a.reshape(m, -1), o_b.reshape(m, -1),
                           o_c.reshape(m, -1), o_d.reshape(m, -1), w_branch[l].astype(BF16),
                           w_out[l].astype(BF16), ln1, wr, br.reshape(1, LANES), tm_tok)

        dest, buf_tok, blk_expert, n_valid = _dispatch_tables(route, m)
        xs = jnp.take(x1.astype(BF16), buf_tok, axis=0)
        yb = _ffn(blk_expert, n_valid, xs, e_gate, e_up, e_down, l)
        ln2 = _row_pad(jnp.stack([ln2_g[l], ln2_b[l]]), SUBLANES)
        x2d = _combine(x1, yb, dest, route, ln2, min(256, m))
    return x2d.reshape(batch, seq, D_MODEL)
```

```python
import functools
import math
import types

import jax
import jax.numpy as jnp
import numpy as np
from jax import lax
from jax.experimental import pallas as pl
from jax.experimental.pallas import tpu as pltpu

F32 = jnp.float32
BF16 = jnp.bfloat16
I32 = jnp.int32
NEG_INF = float("-inf")
INT_MIN = -(2 ** 31)

LANES = 128
SUBLANES = 8

D_MODEL = 1024
DEPTH = 2
HEAD_DIM = 64
Q_BLOCK = 128

A_HEADS = 4
A_PATTERNS = ((128, 1), (512, 4), (2048, 16))
A_GROUPS = len(A_PATTERNS)
A_WIDTH = A_HEADS * HEAD_DIM
A_MAXWIN = max(w for w, _ in A_PATTERNS)

B_HEADS = 4
B_WIDTH = B_HEADS * HEAD_DIM
IDX_HEADS = 4
IDX_DIM = 64
TOPK_MAX = 256

C_HEADS = 4
C_WIDTH = C_HEADS * HEAD_DIM
C_LORA_W = 32
C_LORA_A = 32
C_LORA_G = 64
C_LORA_V = 16
C_IN = 3 * C_WIDTH + C_LORA_W + C_LORA_A + C_LORA_G
C_GN_EPS = 64e-5
C_CHUNK = 64

D_HEADS = 4
D_INNER = D_HEADS * HEAD_DIM
D_GROUPS = 2
D_STATE = 64
D_CONV = 4
D_CHUNK = 128
D_XBC = D_INNER + 2 * D_GROUPS * D_STATE
D_NORM_EPS = 1e-5

N_BRANCH = 4
N_EXPERT_GROUPS = 4
EXPERTS_PER_GROUP = 8
N_EXPERTS = N_EXPERT_GROUPS * EXPERTS_PER_GROUP
TOP_K_EXPERTS = 2
D_EXPERT = 512
MOE_ROWS = 512

LN_EPS = 1e-5
DEEPNORM_ALPHA = (2 * DEPTH) ** 0.25

PB_AKV = 0
PB_AQ = PB_AKV + A_GROUPS * 2 * A_WIDTH
PB_BQ = PB_AQ + A_GROUPS * A_WIDTH
PB_BK = PB_BQ + B_WIDTH
PB_BV = PB_BK + B_WIDTH
PB_IQ = PB_BV + B_WIDTH
PB_IK = PB_IQ + IDX_HEADS * IDX_DIM
PB_WIDTH = PB_IK + 512
PF_GATES = 0
PF_CIN = PF_GATES + N_BRANCH * D_MODEL
PF_MISC = PF_CIN + C_IN
PF_XBC = PF_CIN + 1024
PF_Z = PF_XBC + D_XBC
PROJ_TN = 768
PF_WIDTH = -(-(PF_Z + D_INNER) // PROJ_TN) * PROJ_TN
MISC_DT = 4


def _dot(a, b, precision=None):
    return jnp.dot(a, b, preferred_element_type=F32, precision=precision)


def _dot_nt(a, b, precision=None):
    return lax.dot_general(a, b, (((1,), (1,)), ((), ())), preferred_element_type=F32, precision=precision)


def _split2(a):
    hi = a.astype(BF16)
    return hi, (a - hi.astype(F32)).astype(BF16)


def _dot3(a, b, nt=False):
    mm = _dot_nt if nt else _dot
    ah, al = _split2(a)
    bh, bl = _split2(b)
    return mm(ah, bh) + (mm(ah, bl) + mm(al, bh))


def _dot_sel(sel, x, left=True):
    hi = x.astype(BF16)
    r1 = x - hi.astype(F32)
    mid = r1.astype(BF16)
    lo = (r1 - mid.astype(F32)).astype(BF16)
    if left:
        return _dot(sel, hi) + (_dot(sel, mid) + _dot(sel, lo))
    return _dot(hi, sel) + (_dot(mid, sel) + _dot(lo, sel))


def _sigmoid(x):
    return 1.0 / (1.0 + jnp.exp(-x))


def _softplus(x):
    return jnp.maximum(x, 0.0) + jnp.log1p(jnp.exp(-jnp.abs(x)))


def _silu(x):
    return x * _sigmoid(x)


def _layer_norm(h, g, b):
    mu = jnp.mean(h, axis=-1, keepdims=True)
    d = h - mu
    var = jnp.mean(d * d, axis=-1, keepdims=True)
    return d * lax.rsqrt(var + LN_EPS) * g + b


def _params(*sem, vmem_mb=None):
    kw = {}
    if vmem_mb is not None:
        kw["vmem_limit_bytes"] = vmem_mb * 1024 * 1024
    return pltpu.CompilerParams(dimension_semantics=sem, **kw)


def _proj_kernel(x_ref, w_ref, o_ref):
    o_ref[...] = _dot(x_ref[...].astype(BF16), w_ref[...]).astype(o_ref.dtype)


def _proj(x2d, w, out_dtype, tm, tn):
    m, k = x2d.shape
    n = w.shape[1]
    return pl.pallas_call(
        _proj_kernel,
        grid=(m // tm, n // tn),
        in_specs=[pl.BlockSpec((tm, k), lambda i, j: (i, 0)),
                  pl.BlockSpec((k, tn), lambda i, j: (0, j))],
        out_specs=pl.BlockSpec((tm, tn), lambda i, j: (i, j)),
        out_shape=jax.ShapeDtypeStruct((m, n), out_dtype),
        compiler_params=_params("parallel", "arbitrary", vmem_mb=48),
        name="in_proj",
    )(x2d, w)


def _proj_weights(w):
    k = w.shape[0]
    off = 0
    seg = {}
    for name, size in (("a_qkv", 3 * A_GROUPS * A_WIDTH), ("b_qkv", 3 * B_WIDTH), ("b_idx_q", IDX_HEADS * IDX_DIM),
                       ("b_idx_k", IDX_DIM), ("b_idx_w", IDX_HEADS), ("c_in", C_IN), ("d_z", D_INNER),
                       ("d_xbc", D_XBC), ("d_dt", D_HEADS), ("gates", N_BRANCH * D_MODEL)):
        seg[name] = w[:, off:off + size]
        off += size
    a = seg["a_qkv"].reshape(k, 3, A_GROUPS, A_HEADS, HEAD_DIM)
    a_kv = jnp.transpose(a[:, 1:3], (0, 2, 1, 3, 4)).reshape(k, -1)
    a_q = a[:, 0].reshape(k, -1)
    zeros = lambda n: jnp.zeros((k, n), w.dtype)
    wb = jnp.concatenate([a_kv, a_q, seg["b_qkv"], seg["b_idx_q"], seg["b_idx_k"],
                          zeros(PB_WIDTH - PB_IK - IDX_DIM)], axis=1)
    misc = jnp.concatenate([seg["b_idx_w"], seg["d_dt"], zeros(LANES - IDX_HEADS - D_HEADS)], axis=1)
    wf = jnp.concatenate([seg["gates"], seg["c_in"], misc, seg["d_xbc"], seg["d_z"]], axis=1)
    wf = jnp.concatenate([wf, zeros(PF_WIDTH - wf.shape[1])], axis=1)
    assert wb.shape[1] == PB_WIDTH and PB_WIDTH % PROJ_TN == 0
    return wb.astype(BF16), wf.astype(BF16)


A_ROWS2 = Q_BLOCK // 16
A_SPAN1 = 512 // 4 + Q_BLOCK // 4
A_SPAN2 = 144
A_STEP_HEADS = 4


def _mixa_class_bias():
    x = (np.arange(Q_BLOCK) % A_ROWS2) * 16 + np.arange(Q_BLOCK) // A_ROWS2
    c0 = np.arange(2 * Q_BLOCK)[None, :]
    b0 = np.where((c0 >= x[:, None]) & (c0 <= x[:, None] + Q_BLOCK), 0.0, -np.inf)
    u, a16 = np.arange(32) // A_ROWS2, np.arange(32) % A_ROWS2
    a4 = 4 * a16 + u
    c1 = np.arange(A_SPAN1)[None, :]
    b1 = np.where((c1 >= a4[:, None]) & (c1 <= a4[:, None] + Q_BLOCK), 0.0, -np.inf)
    a2 = np.arange(A_ROWS2)
    c2 = np.arange(A_SPAN2)[None, :]
    b2 = np.where((c2 >= a2[:, None]) & (c2 <= a2[:, None] + Q_BLOCK), 0.0, -np.inf)
    return [jnp.asarray(b.astype(np.float32)) for b in (b0, b1, b2)]


def _mixa2_kernel(q0_ref, q1_ref, q2_ref, k0p_ref, k0_ref, v0p_ref, v0_ref, k1_ref, v1_ref, k2_ref, v2_ref,
                  b0_ref, b1_ref, b2_ref, o_ref):
    pid = pl.program_id(2)
    start = pid * Q_BLOCK
    r8 = A_ROWS2
    scale = HEAD_DIM ** -0.5
    nh = q0_ref.shape[2] // HEAD_DIM
    head_of_lane = lax.broadcasted_iota(I32, (1, nh * HEAD_DIM), 1) // HEAD_DIM
    qf = [q_ref[0].astype(F32) * scale for q_ref in (q0_ref, q1_ref, q2_ref)]

    def masked_bias(b_ref, first_valid):
        col = lax.broadcasted_iota(I32, b_ref.shape, 1)
        return jnp.where(col >= first_valid, b_ref[...], NEG_INF)

    classes = [(qf[0], jnp.concatenate([k0p_ref[0], k0_ref[0]], axis=0),
                jnp.concatenate([v0p_ref[0], v0_ref[0]], axis=0), masked_bias(b0_ref, Q_BLOCK - start))]
    w1 = pl.ds(pl.multiple_of(pid * (Q_BLOCK // 4), Q_BLOCK // 4), A_SPAN1)
    bias1 = masked_bias(b1_ref, Q_BLOCK - pid * (Q_BLOCK // 4))
    for r4 in range(4):
        q1 = jnp.concatenate([qf[1][(r4 + 4 * u) * r8:(r4 + 4 * u + 1) * r8] for u in range(4)], axis=0)
        classes.append((q1, k1_ref[0, r4, w1, :], v1_ref[0, r4, w1, :], bias1))
    w2 = pl.ds(pl.multiple_of(pid * r8, r8), A_SPAN2)
    bias2 = masked_bias(b2_ref, Q_BLOCK - pid * r8)
    for r16 in range(16):
        classes.append((qf[2][r16 * r8:(r16 + 1) * r8], k2_ref[0, r16, w2, :].astype(BF16),
                        v2_ref[0, r16, w2, :].astype(BF16), bias2))

    def rows_of(vals):
        g1 = jnp.concatenate([vals[1 + r16 % 4][(r16 // 4) * r8:(r16 // 4 + 1) * r8] for r16 in range(16)], axis=0)
        return [vals[0], g1, jnp.concatenate(vals[5:21], axis=0)]

    def stacked(q):
        return jnp.concatenate([jnp.where(head_of_lane == h, q, 0.0) for h in range(nh)], axis=0).astype(BF16)

    s = [_dot_nt(stacked(q), k) + jnp.concatenate([bias] * nh, axis=0) for q, k, _, bias in classes]
    m = [jnp.max(si, axis=-1, keepdims=True) for si in s]
    p = [jnp.exp(si - mi) for si, mi in zip(s, m)]
    l = [jnp.sum(pi, axis=-1, keepdims=True) for pi in p]
    o = [_dot(pi.astype(BF16), c[2]) for pi, c in zip(p, classes)]

    out = None
    for hh in reversed(range(nh)):
        mine = lambda vals: [v[hh * (v.shape[0] // nh):(hh + 1) * (v.shape[0] // nh)] for v in vals]
        ms, ls, os_ = rows_of(mine(m)), rows_of(mine(l)), rows_of(mine(o))
        m_all = jnp.maximum(jnp.maximum(ms[0], ms[1]), ms[2])
        num = jnp.zeros((Q_BLOCK, nh * HEAD_DIM), F32)
        den = jnp.zeros((Q_BLOCK, 1), F32)
        for g in range(A_GROUPS):
            wg = jnp.exp(ms[g] - m_all)
            num = num + wg * os_[g]
            den = den + wg * ls[g]
        res = num / den
        out = res if out is None else jnp.where(head_of_lane == hh, res, out)
    o_ref[0] = out.astype(o_ref.dtype)


def _mixer_a2(pb, biases, batch, seq):
    nblk = seq // Q_BLOCK
    r8 = A_ROWS2
    gw = 2 * A_WIDTH
    kv1 = jnp.pad(pb[:, :, PB_AKV + gw:PB_AKV + 2 * gw], ((0, 0), (512, 0), (0, 0)))
    kv1 = kv1.reshape(batch, -1, 4, gw).transpose(0, 2, 1, 3)
    kv2 = jnp.pad(pb[:, :, PB_AKV + 2 * gw:PB_AKV + 3 * gw].astype(F32),
                  ((0, 0), (A_MAXWIN, 16 * (A_SPAN2 - 136)), (0, 0)))
    kv2 = kv2.reshape(batch, -1, 16, gw).transpose(0, 2, 1, 3)
    q = pb[:, :, PB_AQ:PB_BQ].reshape(batch, nblk, r8, 16, -1).transpose(0, 1, 3, 2, 4).reshape(batch, seq, -1)
    pair = A_STEP_HEADS * HEAD_DIM
    pairs_per_part = A_WIDTH // pair
    q_spec = lambda g: pl.BlockSpec((1, Q_BLOCK, pair), lambda b, hp, i: (b, i, g * pairs_per_part + hp))
    cur = lambda part: pl.BlockSpec((1, Q_BLOCK, pair), lambda b, hp, i: (b, i, PB_AKV // pair + part * pairs_per_part + hp))
    prev = lambda part: pl.BlockSpec((1, Q_BLOCK, pair),
                                     lambda b, hp, i: (b, jnp.maximum(i - 1, 0), PB_AKV // pair + part * pairs_per_part + hp))
    k4 = lambda a: pl.BlockSpec((1, a.shape[1], a.shape[2], pair), lambda b, hp, i: (b, 0, 0, hp))
    v4 = lambda a: pl.BlockSpec((1, a.shape[1], a.shape[2], pair), lambda b, hp, i: (b, 0, 0, pairs_per_part + hp))
    in_specs = [q_spec(0), q_spec(1), q_spec(2), prev(0), cur(0), prev(1), cur(1), k4(kv1), v4(kv1), k4(kv2), v4(kv2)]
    for bias in biases:
        in_specs.append(pl.BlockSpec(bias.shape, lambda b, hp, i: (0, 0)))
    o = pl.pallas_call(
        _mixa2_kernel,
        grid=(batch, A_HEADS // A_STEP_HEADS, nblk),
        in_specs=in_specs,
        out_specs=pl.BlockSpec((1, Q_BLOCK, pair), lambda b, hp, i: (b, i, hp)),
        out_shape=jax.ShapeDtypeStruct((batch, seq, A_WIDTH), BF16),
        compiler_params=_params("parallel", "parallel", "arbitrary", vmem_mb=52),
        name="mixer_a",
    )(q, q, q, pb, pb, pb, pb, kv1, kv1, kv2, kv2, *biases)
    return o.reshape(batch, nblk, 16, r8, A_WIDTH).transpose(0, 1, 3, 2, 4).reshape(batch, seq, A_WIDTH)


def _mixb_kernel(qb_ref, kb_ref, vb_ref, iq_ref, ik_ref, misc_ref, tri_ref, o_ref, keys_ref, selb_ref, s_ref, *,
                 top_k):
    rows = Q_BLOCK
    nparts = keys_ref.shape[1] // rows
    kf = float(top_k)
    parts = [_mixb_part(pt, rows, kf, qb_ref, kb_ref, vb_ref, iq_ref, ik_ref, misc_ref, tri_ref, o_ref,
                        keys_ref, selb_ref, s_ref) for pt in range(nparts)]
    for part in parts:
        part.score()
    def pending(cnts):
        open_rows = [jnp.where((c == kf) | part.short, 0.0, 1.0) for part, c in zip(parts, cnts)]
        return sum(jnp.sum(o) for o in open_rows)

    tot0 = [part.total(part.count(lambda kc: kc >= 0)) for part in parts]
    ans0 = tuple(jnp.where(t >= kf, 0, INT_MIN).astype(I32) for t in tot0)
    cnt0 = tuple(jnp.where(t >= kf, t, -1.0) for t in tot0)

    def cond(state):
        it, _, _, left = state
        return (it < 31) & (left > 0.0)

    def body(state):
        it, ans, cnt, _ = state
        bit = jnp.left_shift(jnp.int32(1), 30 - it)
        cands = [a + bit for a in ans]
        raw = [part.count(lambda kc, cand=cand: kc >= cand) for part, cand in zip(parts, cands)]
        tot = [part.total(r) for part, r in zip(parts, raw)]
        ans = tuple(jnp.where(t >= kf, cand, a) for t, cand, a in zip(tot, cands, ans))
        cnt = tuple(jnp.where(t >= kf, t, c) for t, c in zip(tot, cnt))
        return it + 1, ans, cnt, pending(cnt)

    _, thr, _, _ = lax.while_loop(cond, body, (jnp.int32(0), ans0, cnt0, pending(cnt0)))
    for part, t in zip(parts, thr):
        part.attend(t)


def _mixb_part(pt, rows, kf, qb_ref, kb_ref, vb_ref, iq_ref, ik_ref, misc_ref, tri_ref, o_ref,
               keys_ref, selb_ref, s_ref):
    ch = keys_ref.shape[2]
    rs = pl.ds(pt * rows, rows)
    start = pl.program_id(1) * keys_ref.shape[1] + pt * rows
    nch = (start + rows + ch - 1) // ch
    qpos = start + lax.broadcasted_iota(I32, (rows, 1), 0)
    zero_col = jnp.zeros((rows, 1), F32)
    part = types.SimpleNamespace()

    def score():
        w = misc_ref[0, rs, 0:IDX_HEADS] * (IDX_DIM ** -0.5 * IDX_HEADS ** -0.5)
        iq = iq_ref[0, rs, :]
        lax.fori_loop(0, nch, functools.partial(score_chunk, w, iq), 0)

    def score_chunk(w, iq, c, carry):
        off = pl.multiple_of(c * ch, ch)
        ik = ik_ref[0, pl.ds(off, ch), 0:IDX_DIM]
        acc = jnp.zeros((rows, ch), F32)
        for h in range(IDX_HEADS):
            lg = _dot_nt(iq[:, h * IDX_DIM:(h + 1) * IDX_DIM], ik)
            acc = acc + jnp.maximum(lg, 0.0) * w[:, h:h + 1]
        acc = acc + 0.0
        bits = lax.bitcast_convert_type(acc, I32)
        key = bits ^ ((bits >> 31) & 0x7FFFFFFF)
        kpos = off + lax.broadcasted_iota(I32, (rows, ch), 1)
        keys_ref[c, rs, :] = jnp.where(kpos <= qpos, key, INT_MIN)
        return carry

    def count(pred):
        def chunk(c, cnt):
            kc = keys_ref[c, rs, :]
            for j in range(ch // LANES):
                cnt = cnt + jnp.where(pred(kc[:, j * LANES:(j + 1) * LANES]), 1.0, 0.0)
            return cnt
        return lax.fori_loop(0, nch, chunk, jnp.zeros((rows, LANES), F32))

    def total(cnt):
        return jnp.sum(cnt, axis=-1, keepdims=True)

    part.score, part.count, part.total = score, count, total
    part.short = qpos < int(kf)
    part.attend = functools.partial(_mixb_attend, part, pt, rows, kf, nch, zero_col, qb_ref, kb_ref, vb_ref,
                                    tri_ref, o_ref, keys_ref, selb_ref, s_ref)
    return part


def _mixb_attend(part, pt, rows, kf, nch, zero_col, qb_ref, kb_ref, vb_ref, tri_ref, o_ref, keys_ref, selb_ref, s_ref,
                 thr):
    ch = keys_ref.shape[2]
    rs = pl.ds(pt * rows, rows)
    count = lambda pred: part.total(part.count(pred))
    need = kf - count(lambda kc: kc > thr)

    tw = tri_ref.shape[0]

    def select_chunk(c, run):
        kc_all = keys_ref[c, rs, :]
        pieces = [kc_all[:, j * tw:(j + 1) * tw] for j in range(ch // tw)]
        eqf = [jnp.where(kc == thr, 1.0, 0.0) for kc in pieces]
        pre = [_dot(e.astype(BF16), tri_ref[...]) for e in eqf]
        for j, kc in enumerate(pieces):
            take = jnp.where(pre[j] + run <= need, eqf[j], 0.0)
            sel = jnp.where(kc > thr, 1.0, take)
            sel = jnp.where(kc != INT_MIN, sel, 0.0)
            selb_ref[c, rs, j * tw:(j + 1) * tw] = jnp.where(sel > 0.5, 0.0, NEG_INF)
            run = run + jnp.sum(eqf[j], axis=-1, keepdims=True)
        return run

    def select_chunk_no_ties(c, carry):
        kc = keys_ref[c, rs, :]
        selb_ref[c, rs, :] = jnp.where(kc >= thr, jnp.where(kc != INT_MIN, 0.0, NEG_INF), NEG_INF)
        return carry

    excess = jnp.max(count(lambda kc: kc >= thr)) > kf
    lax.cond(excess,
             lambda: lax.fori_loop(0, nch, select_chunk, zero_col),
             lambda: lax.fori_loop(0, nch, select_chunk_no_ties, zero_col))

    heads = range(B_HEADS)
    lane = lax.broadcasted_iota(I32, (1, B_WIDTH), 1) // HEAD_DIM
    qs = (qb_ref[0, rs, :].astype(F32) * HEAD_DIM ** -0.5).astype(BF16)
    qm_all = jnp.concatenate([jnp.where(lane == h, qs, jnp.zeros_like(qs)) for h in heads], axis=0)
    groups = [slice(j * LANES, (j + 1) * LANES) for j in range(ch // LANES)]

    def per_lane(cols):
        out = cols[B_HEADS - 1]
        for h in reversed(range(B_HEADS - 1)):
            out = jnp.where(lane <= h, cols[h], out)
        return out

    def score_pass(c, mx):
        off = pl.multiple_of(c * ch, ch)
        bias = selb_ref[c, rs, :]
        kc = kb_ref[0, pl.ds(off, ch), :]
        s_all = _dot_nt(qm_all, kc)
        out = []
        for h in heads:
            s = s_all[h * rows:(h + 1) * rows] + bias
            s_ref[h, c] = s
            m = mx[h]
            for g in groups:
                m = jnp.maximum(m, s[:, g])
            out.append(m)
        return tuple(out)

    mx = lax.fori_loop(0, nch, score_pass, (jnp.full((rows, LANES), NEG_INF, F32),) * B_HEADS)
    m = [jnp.max(mx[h], axis=-1, keepdims=True) for h in heads]

    def value_pass(c, carry):
        off = pl.multiple_of(c * ch, ch)
        vc = vb_ref[0, pl.ds(off, ch), :]
        p = [jnp.exp(s_ref[h, c] - m[h]) for h in heads]
        ls = []
        for h in heads:
            l = carry[h]
            for g in groups:
                l = l + p[h][:, g]
            ls.append(l)
        pv_all = _dot(jnp.concatenate([p[h].astype(BF16) for h in heads], axis=0), vc)
        pv = [pv_all[h * rows:(h + 1) * rows] for h in heads]
        return tuple(ls) + (carry[B_HEADS] + per_lane(pv),)

    init = (jnp.zeros((rows, LANES), F32),) * B_HEADS + (jnp.zeros((rows, B_WIDTH), F32),)
    res = lax.fori_loop(0, nch, value_pass, init)
    l = [jnp.sum(res[h], axis=-1, keepdims=True) for h in heads]
    o_ref[0, rs, :] = (res[B_HEADS] / per_lane(l)).astype(o_ref.dtype)


def _mixer_b(pb, pf, tri, batch, seq):
    rows = min(4 * Q_BLOCK, seq)
    top_k = min(TOPK_MAX, seq // 4)
    ch = min(512, seq)
    kern = functools.partial(_mixb_kernel, top_k=top_k)
    return pl.pallas_call(
        kern,
        grid=(batch, seq // rows),
        in_specs=[pl.BlockSpec((1, rows, B_WIDTH), lambda b, i: (b, i, PB_BQ // B_WIDTH)),
                  pl.BlockSpec((1, seq, B_WIDTH), lambda b, i: (b, 0, PB_BK // B_WIDTH)),
                  pl.BlockSpec((1, seq, B_WIDTH), lambda b, i: (b, 0, PB_BV // B_WIDTH)),
                  pl.BlockSpec((1, rows, 256), lambda b, i: (b, i, PB_IQ // 256)),
                  pl.BlockSpec((1, seq, LANES), lambda b, i: (b, 0, PB_IK // LANES)),
                  pl.BlockSpec((1, rows, LANES), lambda b, i: (b, i, PF_MISC // LANES)),
                  pl.BlockSpec(tri.shape, lambda b, i: (0, 0))],
        out_specs=pl.BlockSpec((1, rows, B_WIDTH), lambda b, i: (b, i, 0)),
        out_shape=jax.ShapeDtypeStruct((batch, seq, B_WIDTH), BF16),
        scratch_shapes=[pltpu.VMEM((seq // ch, rows, ch), I32), pltpu.VMEM((seq // ch, rows, ch), F32),
                        pltpu.VMEM((B_HEADS, seq // ch, Q_BLOCK, ch), F32)],
        compiler_params=_params("parallel", "arbitrary", vmem_mb=48),
        name="mixer_b",
    )(pb, pb, pb, pb, pb, pf, tri)


def _head_consts():
    lane = np.arange(C_WIDTH)
    same = (lane[:, None] // HEAD_DIM == lane[None, :] // HEAD_DIM).astype(np.float32)
    return jnp.asarray(same)


def _rwkv1_kernel(cin_ref, prev_ref, vf_ref, mu_ref, vec_ref, lora_ref, vl_ref, bd_ref,
                  phi_ref, psi_ref, rp_ref, y0_ref, g_ref, bonus_ref, v_ref, *, rows, use_v_lora):
    i = pl.program_id(1)
    pc = cin_ref[0][:, :C_IN]
    prev = jnp.where(i > 0, prev_ref[0][SUBLANES - 1:SUBLANES, :C_IN], 0.0)
    row = lax.broadcasted_iota(I32, (rows, 1), 0)
    shifted = jnp.where(row == 0, prev, pltpu.roll(pc, 1, 0))
    pc = pc + (shifted - pc) * mu_ref[...]
    r = pc[:, 0:C_WIDTH]
    k = pc[:, C_WIDTH:2 * C_WIDTH]
    v = pc[:, 2 * C_WIDTH:3 * C_WIDTH]
    xl = pc[:, 3 * C_WIDTH:C_IN]
    lane = lax.broadcasted_iota(I32, xl.shape, 1)
    feat = jnp.where(lane < C_LORA_W, jnp.tanh(xl), jnp.where(lane < C_LORA_W + C_LORA_A, xl, _sigmoid(xl)))
    w0, a0, k_k, k_a, r_k = (vec_ref[n:n + 1, :] for n in range(5))
    bd = bd_ref[...]
    bd_sel = bd.astype(BF16)
    w_raw = -_softplus(-(w0 + _dot3(feat, lora_ref[0]))) - 0.5
    a = _sigmoid(a0 + _dot3(feat, lora_ref[1]))
    g_ref[0] = _dot3(feat, lora_ref[2])
    if use_v_lora:
        v0 = vec_ref[5:6, :]
        v = v + (vf_ref[0] - v) * _sigmoid(v0 + _dot3(_dot3(v, vl_ref[0]), vl_ref[1]))
    v_ref[0] = v
    kk = k * k_k
    kk = kk / jnp.maximum(jnp.sqrt(_dot_sel(bd_sel, kk * kk, left=False)), 1e-12)
    k = k * (1.0 + (a - 1.0) * k_a)
    logw = -jnp.exp(w_raw)
    bonus_ref[0] = _dot_sel(bd_sel, r * k * r_k, left=False) * v
    av = -kk
    bv = kk * a

    cc = C_CHUNK
    ri = lax.broadcasted_iota(I32, (cc, cc), 0)
    ci = lax.broadcasted_iota(I32, (cc, cc), 1)
    tril = jnp.where(ci <= ri, 1.0, 0.0).astype(BF16)
    strict = ci < ri
    incl = ci <= ri
    eye_c = jnp.where(ci == ri, 1.0, 0.0)
    lane_w = lax.broadcasted_iota(I32, (1, C_WIDTH), 1)
    ri2 = lax.broadcasted_iota(I32, (C_WIDTH, C_WIDTH), 0)
    ci2 = lax.broadcasted_iota(I32, (C_WIDTH, C_WIDTH), 1)
    eye_w = jnp.where(ri2 == ci2, 1.0, 0.0)
    chunks = range(rows // cc)
    heads = range(C_HEADS)
    pairs = [(c, h) for c in chunks for h in heads]
    mh = [jnp.where(lane_w // HEAD_DIM == h, 1.0, 0.0) for h in heads]
    sl = [slice(c * cc, (c + 1) * cc) for c in chunks]
    cs = [_dot_sel(tril, logw[sl[c]]) for c in chunks]
    cs_end = [cs[c][cc - 1:cc, :] for c in chunks]
    at = [av[sl[c]] * jnp.exp(cs[c] - logw[sl[c]]) for c in chunks]
    rt = [r[sl[c]] * jnp.exp(cs[c]) for c in chunks]
    inv = [jnp.exp(-cs[c]) for c in chunks]
    rhs = [jnp.concatenate([bv[sl[c]] * inv[c], k[sl[c]] * inv[c]], axis=0) for c in chunks]
    tail = [jnp.exp(cs_end[c] - cs[c]) for c in chunks]
    vc = [v[sl[c]] for c in chunks]
    vc_b = [vc[c].astype(BF16) for c in chunks]
    ath = {(c, h): at[c] * mh[h] for c, h in pairs}
    aa = {(c, h): _dot3(jnp.concatenate([ath[c, h], rt[c] * mh[h]], axis=0), rhs[c], nt=True) for c, h in pairs}
    a_ab = {p: jnp.where(strict, aa[p][:cc, :cc], 0.0) for p in pairs}
    a_ak = {p: jnp.where(strict, aa[p][:cc, cc:], 0.0).astype(BF16) for p in pairs}
    a_rb = {p: jnp.where(incl, aa[p][cc:, :cc], 0.0).astype(BF16) for p in pairs}
    a_rk = {p: jnp.where(incl, aa[p][cc:, cc:], 0.0).astype(BF16) for p in pairs}
    x = {p: eye_c + a_ab[p] for p in pairs}
    pw = a_ab
    for _ in range(int(math.log2(cc)) - 1):
        pw = {p: _dot3(pw[p], pw[p]) for p in pairs}
        x = {p: x[p] + _dot3(x[p], pw[p]) for p in pairs}
    akv = {(c, h): _dot(a_ak[c, h], vc_b[c]) for c, h in pairs}
    ap_h = {p: _dot3(x[p], ath[p]) for p in pairs}
    w2_h = {(c, h): _dot3(x[c, h], akv[c, h]) * mh[h] for c, h in pairs}
    rp_h = {p: _dot(a_rb[p], ap_h[p].astype(BF16)) for p in pairs}
    y0_h = {(c, h): (_dot(a_rb[c, h], w2_h[c, h].astype(BF16)) + _dot(a_rk[c, h], vc_b[c])) * mh[h] for c, h in pairs}
    for c in chunks:
        ap = sum(ap_h[c, h] for h in heads)
        w2 = sum(w2_h[c, h] for h in heads)
        rp_ref[0, sl[c], :] = rt[c] + sum(rp_h[c, h] for h in heads)
        y0_ref[0, sl[c], :] = sum(y0_h[c, h] for h in heads)
        bh = bv[sl[c]] * tail[c]
        kh = k[sl[c]] * tail[c]
        phi_ref[0, c] = eye_w * jnp.exp(cs_end[c]) + _dot3(bh.T, ap) * bd
        psi_ref[0, c] = _dot(jnp.concatenate([bh, kh], axis=0).T.astype(BF16),
                             jnp.concatenate([w2, vc[c]], axis=0).astype(BF16)) * bd


def _rwkv2_kernel(phi_ref, psi_ref, rp_ref, y0_ref, g_ref, bonus_ref, gn_ref, bd_ref, o_ref, s_ref):
    @pl.when(pl.program_id(1) == 0)
    def _():
        s_ref[...] = jnp.zeros_like(s_ref)

    states = [s_ref[...]]
    for c in range(phi_ref.shape[1]):
        states.append(_dot3(phi_ref[0, c], states[c]) + psi_ref[0, c])
    s_ref[...] = states[-1]
    cc = C_CHUNK
    y = jnp.concatenate([_dot3(rp_ref[0, c * cc:(c + 1) * cc, :], states[c]) for c in range(phi_ref.shape[1])],
                        axis=0) + y0_ref[0]
    bd_sel = bd_ref[...].astype(BF16)
    mu = _dot_sel(bd_sel, y, left=False) * (1.0 / HEAD_DIM)
    d = y - mu
    var = _dot_sel(bd_sel, d * d, left=False) * (1.0 / HEAD_DIM)
    yn = d * lax.rsqrt(var + C_GN_EPS) * gn_ref[0:1, :] + gn_ref[1:2, :]
    o_ref[0] = ((yn + bonus_ref[0]) * g_ref[0]).astype(o_ref.dtype)


def _mixer_c(pf, v_first, mu, vec, lora, vl, gn, bd, batch, seq, use_v_lora):
    rows = min(256, seq)
    nblk = seq // rows
    nch = seq // C_CHUNK
    cpb = rows // C_CHUNK
    kern = functools.partial(_rwkv1_kernel, rows=rows, use_v_lora=use_v_lora)
    full2 = lambda a: pl.BlockSpec(a.shape, lambda b, i: (0,) * a.ndim)
    seq_spec = pl.BlockSpec((1, rows, C_WIDTH), lambda b, i: (b, i, 0))
    mat_spec = pl.BlockSpec((1, cpb, C_WIDTH, C_WIDTH), lambda b, i: (b, i, 0, 0))
    seq_shape = jax.ShapeDtypeStruct((batch, seq, C_WIDTH), F32)
    mat_shape = jax.ShapeDtypeStruct((batch, nch, C_WIDTH, C_WIDTH), F32)
    prev_blk = rows // SUBLANES
    phi, psi, rp, y0, g, bonus, v = pl.pallas_call(
        kern,
        grid=(batch, nblk),
        in_specs=[pl.BlockSpec((1, rows, 1024), lambda b, i: (b, i, PF_CIN // 1024)),
                  pl.BlockSpec((1, SUBLANES, 1024), lambda b, i: (b, jnp.maximum(i * prev_blk - 1, 0), PF_CIN // 1024)),
                  seq_spec, full2(mu), full2(vec), full2(lora), full2(vl), full2(bd)],
        out_specs=[mat_spec, mat_spec, seq_spec, seq_spec, seq_spec, seq_spec, seq_spec],
        out_shape=[mat_shape, mat_shape, seq_shape, seq_shape, seq_shape, seq_shape, seq_shape],
        compiler_params=_params("parallel", "parallel", vmem_mb=48),
        name="rwkv_chunks",
    )(pf, pf, v_first, mu, vec, lora, vl, bd)
    scan_chunks = min(4, nch)
    cseq = pl.BlockSpec((1, scan_chunks * C_CHUNK, C_WIDTH), lambda b, c: (b, c, 0))
    cmat = pl.BlockSpec((1, scan_chunks, C_WIDTH, C_WIDTH), lambda b, c: (b, c, 0, 0))
    o = pl.pallas_call(
        _rwkv2_kernel,
        grid=(batch, nch // scan_chunks),
        in_specs=[cmat, cmat, cseq, cseq, cseq, cseq,
                  pl.BlockSpec(gn.shape, lambda b, c: (0, 0)), pl.BlockSpec(bd.shape, lambda b, c: (0, 0))],
        out_specs=cseq,
        out_shape=jax.ShapeDtypeStruct((batch, seq, C_WIDTH), BF16),
        scratch_shapes=[pltpu.VMEM((C_WIDTH, C_WIDTH), F32)],
        compiler_params=_params("parallel", "arbitrary"),
        name="rwkv_scan",
    )(phi, psi, rp, y0, g, bonus, gn, bd)
    return o, v


def _ssd_consts():
    expand = np.zeros((LANES, D_INNER), np.float32)
    for h in range(D_HEADS):
        expand[MISC_DT + h, h * HEAD_DIM:(h + 1) * HEAD_DIM] = 1.0
    return jnp.asarray(expand)


def _ssd_kernel(xbc_ref, z_ref, misc_ref, conv_ref, vec_ref, hp_ref, expand_ref, bd_ref, o_ref, st_ref, prev_ref):
    @pl.when(pl.program_id(1) == 0)
    def _():
        st_ref[...] = jnp.zeros_like(st_ref)
        prev_ref[...] = jnp.zeros_like(prev_ref)

    q = D_CHUNK
    x_raw = xbc_ref[0]
    ext = jnp.concatenate([prev_ref[...], x_raw], axis=0)
    conv = jnp.zeros((q, D_XBC), F32)
    for t in range(D_CONV):
        lo = SUBLANES - (D_CONV - 1) + t
        conv = conv + ext[lo:lo + q, :] * conv_ref[t:t + 1, :]
    prev_ref[...] = x_raw[q - SUBLANES:, :]
    xbc = _silu(conv + conv_ref[D_CONV:D_CONV + 1, :])
    xs = xbc[:, :D_INNER]
    bm = xbc[:, D_INNER:D_INNER + D_GROUPS * D_STATE]
    cm = xbc[:, D_INNER + D_GROUPS * D_STATE:]

    dt_col = _softplus(misc_ref[0] + hp_ref[0:1, :])
    a_col = dt_col * hp_ref[1:2, :]
    ri = lax.broadcasted_iota(I32, (q, q), 0)
    ci = lax.broadcasted_iota(I32, (q, q), 1)
    causal = ci <= ri
    tril = jnp.where(causal, 1.0, 0.0).astype(BF16)
    acs_col = _dot_sel(tril, a_col)
    acs_row = acs_col.T
    expand = expand_ref[...].astype(BF16)
    acs = _dot_sel(expand, acs_col, left=False)
    dt = _dot_sel(expand, dt_col, left=False)
    acs_end = acs[q - 1:q, :]
    xdt = xs * dt

    lane = lax.broadcasted_iota(I32, (1, LANES), 1)
    lane_w = lax.broadcasted_iota(I32, (1, D_INNER), 1)
    left = lane < D_STATE
    bm_sw = pltpu.roll(bm, D_STATE, 1)
    cm_sw = pltpu.roll(cm, D_STATE, 1)
    b_exp = jnp.concatenate([jnp.where(left, bm, bm_sw), jnp.where(left, bm_sw, bm)], axis=1)
    c_exp = jnp.concatenate([jnp.where(left, cm, cm_sw), jnp.where(left, cm_sw, cm)], axis=1)
    cb = [_dot3(jnp.where(left == (g == 0), cm, 0.0), bm, nt=True) for g in range(D_GROUPS)]

    scores = []
    for h in range(D_HEADS):
        col = acs_col[:, MISC_DT + h:MISC_DT + h + 1]
        rw = acs_row[MISC_DT + h:MISC_DT + h + 1, :]
        decay = jnp.exp(jnp.where(causal, col - rw, NEG_INF))
        scores.append(cb[h // (D_HEADS // D_GROUPS)] * decay)
    y_h = [_dot3(scores[h], xdt) for h in range(D_HEADS)]
    y = jnp.zeros((q, D_INNER), F32)
    for h in range(D_HEADS):
        y = y + jnp.where(lane_w // HEAD_DIM == h, y_h[h], 0.0)
    st = st_ref[...]
    y = y + _dot3(c_exp, st) * jnp.exp(acs)
    st_ref[...] = st * jnp.exp(acs_end) + _dot3(b_exp.T, xdt * jnp.exp(acs_end - acs)) * bd_ref[...]
    y = y + xs * vec_ref[0:1, :]
    y = y * _silu(z_ref[0])
    half = D_INNER // D_GROUPS
    outs = []
    for g in range(D_GROUPS):
        yg = y[:, g * half:(g + 1) * half]
        outs.append(yg * lax.rsqrt(jnp.mean(yg * yg, axis=-1, keepdims=True) + D_NORM_EPS))
    o_ref[0] = (jnp.concatenate(outs, axis=1) * vec_ref[1:2, :]).astype(o_ref.dtype)


def _mixer_d(pf, conv, vec, hp, expand, bd, batch, seq):
    q = D_CHUNK
    full = lambda a: pl.BlockSpec(a.shape, lambda b, i: (0, 0))
    return pl.pallas_call(
        _ssd_kernel,
        grid=(batch, seq // q),
        in_specs=[pl.BlockSpec((1, q, D_XBC), lambda b, i: (b, i, PF_XBC // D_XBC)),
                  pl.BlockSpec((1, q, D_INNER), lambda b, i: (b, i, PF_Z // D_INNER)),
                  pl.BlockSpec((1, q, LANES), lambda b, i: (b, i, PF_MISC // LANES)),
                  full(conv), full(vec), full(hp), full(expand), full(bd)],
        out_specs=pl.BlockSpec((1, q, D_INNER), lambda b, i: (b, i, 0)),
        out_shape=jax.ShapeDtypeStruct((batch, seq, D_INNER), BF16),
        scratch_shapes=[pltpu.VMEM((D_INNER, D_INNER), F32), pltpu.VMEM((SUBLANES, D_XBC), F32)],
        compiler_params=_params("parallel", "arbitrary"),
        name="ssd",
    )(pf, pf, pf, conv, vec, hp, expand, bd)


def _merge_kernel(x_ref, gates_ref, oa_ref, ob_ref, oc_ref, od_ref, wbr_ref, wout_ref, ln_ref, wr_ref, br_ref,
                  x1_ref, route_ref):
    acc = None
    for n, o_ref in enumerate((oa_ref, ob_ref, oc_ref, od_ref)):
        term = _sigmoid(gates_ref[:, n * D_MODEL:(n + 1) * D_MODEL]) * _dot(o_ref[...], wbr_ref[n])
        acc = term if acc is None else acc + term
    h = DEEPNORM_ALPHA * x_ref[...] + _dot(acc.astype(BF16), wout_ref[...])
    x1 = _layer_norm(h, ln_ref[0:1, :], ln_ref[1:2, :])
    x1_ref[...] = x1

    logits = _dot3(x1, wr_ref[...]) + br_ref[...]
    lane = lax.broadcasted_iota(I32, logits.shape, 1)
    big = jnp.int32(LANES)
    gl = jnp.where(lane < N_EXPERT_GROUPS, logits, NEG_INF)
    gm = jnp.max(gl, axis=-1, keepdims=True)
    pg_top = 1.0 / jnp.sum(jnp.exp(gl - gm), axis=-1, keepdims=True)
    g_sel = jnp.min(jnp.where(gl == gm, lane, big), axis=-1, keepdims=True)
    off = N_EXPERT_GROUPS + g_sel * EXPERTS_PER_GROUP
    el = jnp.where((lane >= off) & (lane < off + EXPERTS_PER_GROUP), logits, NEG_INF)
    em = jnp.max(el, axis=-1, keepdims=True)
    es = jnp.sum(jnp.exp(el - em), axis=-1, keepdims=True)
    idx1 = jnp.min(jnp.where(el == em, lane, big), axis=-1, keepdims=True)
    el2 = jnp.where(lane == idx1, NEG_INF, el)
    em2 = jnp.max(el2, axis=-1, keepdims=True)
    idx2 = jnp.min(jnp.where(el2 == em2, lane, big), axis=-1, keepdims=True)
    p1 = 1.0 / es
    p2 = jnp.exp(em2 - em) / es
    gate1 = pg_top * p1 / (p1 + p2)
    gate2 = pg_top * p2 / (p1 + p2)
    e1 = (idx1 - N_EXPERT_GROUPS).astype(F32)
    e2 = (idx2 - N_EXPERT_GROUPS).astype(F32)
    route_ref[...] = jnp.where(lane == 0, e1, jnp.where(lane == 1, e2, jnp.where(lane == 2, gate1,
                               jnp.where(lane == 3, gate2, 0.0))))


def _merge(x2d, pf, oa, ob, oc, od, wbr, wout, ln, wr, br, tm):
    m = x2d.shape[0]
    row = lambda w: pl.BlockSpec((tm, w), lambda i: (i, 0))
    full = lambda a: pl.BlockSpec(a.shape, lambda i: (0,) * a.ndim)
    return pl.pallas_call(
        _merge_kernel,
        grid=(m // tm,),
        in_specs=[row(D_MODEL), pl.BlockSpec((tm, N_BRANCH * D_MODEL), lambda i: (i, PF_GATES)),
                  row(A_WIDTH), row(B_WIDTH), row(C_WIDTH), row(D_INNER),
                  full(wbr), full(wout), full(ln), full(wr), full(br)],
        out_specs=[row(D_MODEL), row(LANES)],
        out_shape=[jax.ShapeDtypeStruct((m, D_MODEL), F32), jax.ShapeDtypeStruct((m, LANES), F32)],
        compiler_params=_params("parallel", vmem_mb=48),
        name="merge_route",
    )(x2d, pf, oa, ob, oc, od, wbr, wout, ln, wr, br)


def _ffn_kernel(be_ref, nv_ref, xs_ref, wg_ref, wu_ref, wd_ref, o_ref):
    i = pl.program_id(0)

    @pl.when(i < nv_ref[0])
    def _():
        xb = xs_ref[...]
        hg = _dot(xb, wg_ref[0, 0].astype(BF16))
        hu = _dot(xb, wu_ref[0, 0].astype(BF16))
        o_ref[...] = _dot((_silu(hg) * hu).astype(BF16), wd_ref[0, 0].astype(BF16))

    @pl.when(i >= nv_ref[0])
    def _():
        o_ref[...] = jnp.zeros_like(o_ref)


def _ffn(blk_expert, n_valid, xs, e_gate, e_up, e_down, layer):
    cap = xs.shape[0]
    wspec = lambda a: pl.BlockSpec((1, 1) + a.shape[2:], lambda i, be, nv: (layer, be[i], 0, 0))
    return pl.pallas_call(
        _ffn_kernel,
        grid_spec=pltpu.PrefetchScalarGridSpec(
            num_scalar_prefetch=2,
            grid=(cap // MOE_ROWS,),
            in_specs=[pl.BlockSpec((MOE_ROWS, D_MODEL), lambda i, be, nv: (i, 0)),
                      wspec(e_gate), wspec(e_up), wspec(e_down)],
            out_specs=pl.BlockSpec((MOE_ROWS, D_MODEL), lambda i, be, nv: (i, 0)),
        ),
        out_shape=jax.ShapeDtypeStruct((cap, D_MODEL), F32),
        compiler_params=_params("arbitrary", vmem_mb=48),
        name="expert_ffn",
    )(blk_expert, n_valid, xs, e_gate, e_up, e_down)


def _combine_kernel(dest_ref, x_ref, route_ref, ln_ref, yb_ref, o_ref, ybuf, sem):
    i = pl.program_id(0)
    tm = x_ref.shape[0]

    def gather(tile, slot):
        base = tile * (TOP_K_EXPERTS * tm)

        def body(r, carry):
            for s in range(TOP_K_EXPERTS):
                row = dest_ref[base + TOP_K_EXPERTS * r + s]
                pltpu.make_async_copy(yb_ref.at[pl.ds(row, 1)], ybuf.at[slot, pl.ds(s * tm + r, 1)],
                                      sem.at[slot]).start()
            return carry

        lax.fori_loop(0, tm, body, 0, unroll=8)

    @pl.when(i == 0)
    def _():
        gather(0, 0)

    @pl.when(i + 1 < pl.num_programs(0))
    def _():
        gather(i + 1, (i + 1) % 2)

    slot = i % 2
    pltpu.make_async_copy(yb_ref.at[pl.ds(0, TOP_K_EXPERTS * tm)], ybuf.at[slot], sem.at[slot]).wait()
    g0 = route_ref[:, 2:3]
    g1 = route_ref[:, 3:4]
    h = DEEPNORM_ALPHA * x_ref[...] + (ybuf[slot, 0:tm, :] * g0 + ybuf[slot, tm:2 * tm, :] * g1)
    o_ref[...] = _layer_norm(h, ln_ref[0:1, :], ln_ref[1:2, :])


def _combine(x1, yb, dest, route, ln, tm):
    m = x1.shape[0]
    return pl.pallas_call(
        _combine_kernel,
        grid_spec=pltpu.PrefetchScalarGridSpec(
            num_scalar_prefetch=1,
            grid=(m // tm,),
            in_specs=[pl.BlockSpec((tm, D_MODEL), lambda i, d: (i, 0)),
                      pl.BlockSpec((tm, LANES), lambda i, d: (i, 0)),
                      pl.BlockSpec(ln.shape, lambda i, d: (0, 0)),
                      pl.BlockSpec(memory_space=pl.ANY)],
            out_specs=pl.BlockSpec((tm, D_MODEL), lambda i, d: (i, 0)),
            scratch_shapes=[pltpu.VMEM((2, TOP_K_EXPERTS * tm, D_MODEL), F32), pltpu.SemaphoreType.DMA((2,))],
        ),
        out_shape=jax.ShapeDtypeStruct((m, D_MODEL), F32),
        compiler_params=_params("arbitrary"),
        name="combine",
    )(dest, x1, route, ln, yb)


def _dispatch_tables(route, m):
    flat_e = route[:, 0:TOP_K_EXPERTS].astype(I32).reshape(-1)
    n_assign = m * TOP_K_EXPERTS
    onehot = (flat_e[:, None] == jnp.arange(N_EXPERTS, dtype=I32)[None, :]).astype(I32)
    csum = jnp.cumsum(onehot, axis=0)
    rank = jnp.sum(csum * onehot, axis=1) - 1
    counts = csum[-1]
    padded = (counts + MOE_ROWS - 1) // MOE_ROWS * MOE_ROWS
    pad_end = jnp.cumsum(padded)
    pad_start = pad_end - padded
    dest = pad_start[flat_e] + rank
    cap = (n_assign + N_EXPERTS * (MOE_ROWS - 1) + MOE_ROWS - 1) // MOE_ROWS * MOE_ROWS
    n_blocks = cap // MOE_ROWS
    blk_start = jnp.arange(n_blocks, dtype=I32) * MOE_ROWS
    blk_expert = jnp.minimum(jnp.sum((pad_end[None, :] <= blk_start[:, None]).astype(I32), axis=1), N_EXPERTS - 1)
    buf_tok = (jnp.arange(cap, dtype=I32) % m).at[dest].set(jnp.arange(n_assign, dtype=I32) // TOP_K_EXPERTS)
    n_valid = (pad_end[-1] // MOE_ROWS).astype(I32).reshape(1)
    return dest, buf_tok, blk_expert, n_valid


def _row_pad(a, rows):
    return jnp.pad(a, ((0, rows - a.shape[0]), (0, 0)))


def kernel(x, w_in, c_mu, c_w0, c_w2, c_a0, c_a2, c_g2, c_kk, c_ka, c_rk, c_gn_w, c_gn_b, c_v0, c_v1, c_v2,
           d_conv_w, d_conv_b, d_dt_bias, d_a_log, d_skip, d_norm_w, w_branch, w_out, ln1_g, ln1_b,
           r_group, r_group_b, r_expert, r_expert_b, e_gate, e_up, e_down, ln2_g, ln2_b):
    batch, seq, _ = x.shape
    m = batch * seq
    biases = _mixa_class_bias()
    tw = min(256, seq)
    tri = jnp.asarray(np.triu(np.ones((tw, tw), np.float32))).astype(BF16)
    bd = _head_consts()
    expand = _ssd_consts()
    x2d = x.reshape(m, D_MODEL)
    v_first = jnp.zeros((batch, seq, C_WIDTH), F32)
    tm_proj = min(2048, m)
    tm_tok = min(512, m)
    for l in range(DEPTH):
        wb, wf = _proj_weights(w_in[l])
        pb = _proj(x2d, wb, BF16, tm_proj, PROJ_TN).reshape(batch, seq, PB_WIDTH)
        pf = _proj(x2d, wf, F32, tm_proj, PROJ_TN).reshape(batch, seq, PF_WIDTH)

        o_a = _mixer_a2(pb, biases, batch, seq)
        o_b = _mixer_b(pb, pf, tri, batch, seq)

        use_v_lora = l > 0
        vec_rows = [c_w0[l], c_a0[l], c_kk[l], c_ka[l], c_rk[l].reshape(-1)]
        vec_rows.append(c_v0[l - 1] if use_v_lora else jnp.zeros((C_WIDTH,), F32))
        vec = _row_pad(jnp.stack(vec_rows), SUBLANES)
        lora = jnp.stack([
            jnp.pad(c_w2[l], ((0, LANES - C_LORA_W), (0, 0))),
            jnp.pad(c_a2[l], ((C_LORA_W, LANES - C_LORA_W - C_LORA_A), (0, 0))),
            jnp.pad(c_g2[l], ((C_LORA_W + C_LORA_A, 0), (0, 0)))])
        if use_v_lora:
            vl = jnp.stack([jnp.pad(c_v1[l - 1], ((0, 0), (0, C_WIDTH - C_LORA_V))),
                            jnp.pad(c_v2[l - 1], ((0, C_WIDTH - C_LORA_V), (0, 0)))])
        else:
            vl = jnp.zeros((2, C_WIDTH, C_WIDTH), F32)
        gn = _row_pad(jnp.stack([c_gn_w[l], c_gn_b[l]]), SUBLANES)
        o_c, v_c = _mixer_c(pf, v_first, c_mu[l].reshape(1, C_IN), vec, lora, vl, gn, bd, batch, seq, use_v_lora)
        if l == 0:
            v_first = v_c

        conv = _row_pad(jnp.concatenate([d_conv_w[l], d_conv_b[l][None, :]], axis=0), SUBLANES)
        dvec = _row_pad(jnp.stack([jnp.repeat(d_skip[l], HEAD_DIM), d_norm_w[l]]), SUBLANES)
        place = lambda a: jnp.pad(a, (MISC_DT, LANES - MISC_DT - D_HEADS))
        hp = _row_pad(jnp.stack([place(d_dt_bias[l]), place(-jnp.exp(d_a_log[l]))]), SUBLANES)
        o_d = _mixer_d(pf, conv, dvec, hp, expand, bd, batch, seq)

        wr = jnp.pad(jnp.concatenate([r_group[l], r_expert[l]], axis=1),
                     ((0, 0), (0, LANES - N_EXPERT_GROUPS - N_EXPERTS)))
        br = jnp.pad(jnp.concatenate([r_group_b[l], r_expert_b[l]]), (0, LANES - N_EXPERT_GROUPS - N_EXPERTS))
        ln1 = _row_pad(jnp.stack([ln1_g[l], ln1_b[l]]), SUBLANES)
        x1, route = _merge(x2d, pf.reshape(m, PF_WIDTH), o_a.reshape(m, -1), o_b.reshape(m, -1),
                           o_c.reshape(m, -1), o_d.reshape(m, -1), w_branch[l].astype(BF16),
                           w_out[l].astype(BF16), ln1, wr, br.reshape(1, LANES), tm_tok)

        dest, buf_tok, blk_expert, n_valid = _dispatch_tables(route, m)
        xs = jnp.take(x1.astype(BF16), buf_tok, axis=0)
        yb = _ffn(blk_expert, n_valid, xs, e_gate, e_up, e_down, l)
        ln2 = _row_pad(jnp.stack([ln2_g[l], ln2_b[l]]), SUBLANES)
        x2d = _combine(x1, yb, dest, route, ln2, min(256, m))
    return x2d.reshape(batch, seq, D_MODEL)
```

```python
import functools
import math
import types

import jax
import jax.numpy as jnp
import numpy as np
from jax import lax
from jax.experimental import pallas as pl
from jax.experimental.pallas import tpu as pltpu

F32 = jnp.float32
BF16 = jnp.bfloat16
I32 = jnp.int32
NEG_INF = float("-inf")
INT_MIN = -(2 ** 31)

LANES = 128
SUBLANES = 8

D_MODEL = 1024
DEPTH = 2
HEAD_DIM = 64
Q_BLOCK = 128

A_HEADS = 4
A_PATTERNS = ((128, 1), (512, 4), (2048, 16))
A_GROUPS = len(A_PATTERNS)
A_WIDTH = A_HEADS * HEAD_DIM
A_MAXWIN = max(w for w, _ in A_PATTERNS)

B_HEADS = 4
B_WIDTH = B_HEADS * HEAD_DIM
IDX_HEADS = 4
IDX_DIM = 64
TOPK_MAX = 256

C_HEADS = 4
C_WIDTH = C_HEADS * HEAD_DIM
C_LORA_W = 32
C_LORA_A = 32
C_LORA_G = 64
C_LORA_V = 16
C_IN = 3 * C_WIDTH + C_LORA_W + C_LORA_A + C_LORA_G
C_GN_EPS = 64e-5
C_CHUNK = 64

D_HEADS = 4
D_INNER = D_HEADS * HEAD_DIM
D_GROUPS = 2
D_STATE = 64
D_CONV = 4
D_CHUNK = 128
D_XBC = D_INNER + 2 * D_GROUPS * D_STATE
D_NORM_EPS = 1e-5

N_BRANCH = 4
N_EXPERT_GROUPS = 4
EXPERTS_PER_GROUP = 8
N_EXPERTS = N_EXPERT_GROUPS * EXPERTS_PER_GROUP
TOP_K_EXPERTS = 2
D_EXPERT = 512
MOE_ROWS = 512

LN_EPS = 1e-5
DEEPNORM_ALPHA = (2 * DEPTH) ** 0.25

PB_AKV = 0
PB_AQ = PB_AKV + A_GROUPS * 2 * A_WIDTH
PB_BQ = PB_AQ + A_GROUPS * A_WIDTH
PB_BK = PB_BQ + B_WIDTH
PB_BV = PB_BK + B_WIDTH
PB_IQ = PB_BV + B_WIDTH
PB_IK = PB_IQ + IDX_HEADS * IDX_DIM
PB_WIDTH = PB_IK + 512
PF_GATES = 0
PF_CIN = PF_GATES + N_BRANCH * D_MODEL
PF_MISC = PF_CIN + C_IN
PF_XBC = PF_CIN + 1024
PF_Z = PF_XBC + D_XBC
PROJ_TN = 768
PF_WIDTH = -(-(PF_Z + D_INNER) // PROJ_TN) * PROJ_TN
MISC_DT = 4


def _dot(a, b, precision=None):
    return jnp.dot(a, b, preferred_element_type=F32, precision=precision)


def _dot_nt(a, b, precision=None):
    return lax.dot_general(a, b, (((1,), (1,)), ((), ())), preferred_element_type=F32, precision=precision)


def _split2(a):
    hi = a.astype(BF16)
    return hi, (a - hi.astype(F32)).astype(BF16)


def _dot3(a, b, nt=False):
    mm = _dot_nt if nt else _dot
    ah, al = _split2(a)
    bh, bl = _split2(b)
    return mm(ah, bh) + (mm(ah, bl) + mm(al, bh))


def _dot_sel(sel, x, left=True):
    hi = x.astype(BF16)
    r1 = x - hi.astype(F32)
    mid = r1.astype(BF16)
    lo = (r1 - mid.astype(F32)).astype(BF16)
    if left:
        return _dot(sel, hi) + (_dot(sel, mid) + _dot(sel, lo))
    return _dot(hi, sel) + (_dot(mid, sel) + _dot(lo, sel))


def _sigmoid(x):
    return 1.0 / (1.0 + jnp.exp(-x))


def _softplus(x):
    return jnp.maximum(x, 0.0) + jnp.log1p(jnp.exp(-jnp.abs(x)))


def _silu(x):
    return x * _sigmoid(x)


def _layer_norm(h, g, b):
    mu = jnp.mean(h, axis=-1, keepdims=True)
    d = h - mu
    var = jnp.mean(d * d, axis=-1, keepdims=True)
    return d * lax.rsqrt(var + LN_EPS) * g + b


def _params(*sem, vmem_mb=None):
    kw = {}
    if vmem_mb is not None:
        kw["vmem_limit_bytes"] = vmem_mb * 1024 * 1024
    return pltpu.CompilerParams(dimension_semantics=sem, **kw)


def _proj_kernel(x_ref, w_ref, o_ref):
    o_ref[...] = _dot(x_ref[...].astype(BF16), w_ref[...]).astype(o_ref.dtype)


def _proj(x2d, w, out_dtype, tm, tn):
    m, k = x2d.shape
    n = w.shape[1]
    return pl.pallas_call(
        _proj_kernel,
        grid=(m // tm, n // tn),
        in_specs=[pl.BlockSpec((tm, k), lambda i, j: (i, 0)),
                  pl.BlockSpec((k, tn), lambda i, j: (0, j))],
        out_specs=pl.BlockSpec((tm, tn), lambda i, j: (i, j)),
        out_shape=jax.ShapeDtypeStruct((m, n), out_dtype),
        compiler_params=_params("parallel", "arbitrary", vmem_mb=48),
        name="in_proj",
    )(x2d, w)


def _proj_weights(w):
    k = w.shape[0]
    off = 0
    seg = {}
    for name, size in (("a_qkv", 3 * A_GROUPS * A_WIDTH), ("b_qkv", 3 * B_WIDTH), ("b_idx_q", IDX_HEADS * IDX_DIM),
                       ("b_idx_k", IDX_DIM), ("b_idx_w", IDX_HEADS), ("c_in", C_IN), ("d_z", D_INNER),
                       ("d_xbc", D_XBC), ("d_dt", D_HEADS), ("gates", N_BRANCH * D_MODEL)):
        seg[name] = w[:, off:off + size]
        off += size
    a = seg["a_qkv"].reshape(k, 3, A_GROUPS, A_HEADS, HEAD_DIM)
    a_kv = jnp.transpose(a[:, 1:3], (0, 2, 1, 3, 4)).reshape(k, -1)
    a_q = a[:, 0].reshape(k, -1)
    zeros = lambda n: jnp.zeros((k, n), w.dtype)
    wb = jnp.concatenate([a_kv, a_q, seg["b_qkv"], seg["b_idx_q"], seg["b_idx_k"],
                          zeros(PB_WIDTH - PB_IK - IDX_DIM)], axis=1)
    misc = jnp.concatenate([seg["b_idx_w"], seg["d_dt"], zeros(LANES - IDX_HEADS - D_HEADS)], axis=1)
    wf = jnp.concatenate([seg["gates"], seg["c_in"], misc, seg["d_xbc"], seg["d_z"]], axis=1)
    wf = jnp.concatenate([wf, zeros(PF_WIDTH - wf.shape[1])], axis=1)
    assert wb.shape[1] == PB_WIDTH and PB_WIDTH % PROJ_TN == 0
    return wb.astype(BF16), wf.astype(BF16)


A_ROWS2 = Q_BLOCK // 16
A_SPAN1 = 512 // 4 + Q_BLOCK // 4
A_SPAN2 = 144
A_STEP_HEADS = 4


def _mixa_class_bias():
    x = (np.arange(Q_BLOCK) % A_ROWS2) * 16 + np.arange(Q_BLOCK) // A_ROWS2
    c0 = np.arange(2 * Q_BLOCK)[None, :]
    b0 = np.where((c0 >= x[:, None]) & (c0 <= x[:, None] + Q_BLOCK), 0.0, -np.inf)
    u, a16 = np.arange(32) // A_ROWS2, np.arange(32) % A_ROWS2
    a4 = 4 * a16 + u
    c1 = np.arange(A_SPAN1)[None, :]
    b1 = np.where((c1 >= a4[:, None]) & (c1 <= a4[:, None] + Q_BLOCK), 0.0, -np.inf)
    a2 = np.arange(A_ROWS2)
    c2 = np.arange(A_SPAN2)[None, :]
    b2 = np.where((c2 >= a2[:, None]) & (c2 <= a2[:, None] + Q_BLOCK), 0.0, -np.inf)
    return [jnp.asarray(b.astype(np.float32)) for b in (b0, b1, b2)]


def _mixa2_kernel(q0_ref, q1_ref, q2_ref, k0p_ref, k0_ref, v0p_ref, v0_ref, k1_ref, v1_ref, k2_ref, v2_ref,
                  b0_ref, b1_ref, b2_ref, o_ref):
    pid = pl.program_id(2)
    start = pid * Q_BLOCK
    r8 = A_ROWS2
    scale = HEAD_DIM ** -0.5
    nh = q0_ref.shape[2] // HEAD_DIM
    head_of_lane = lax.broadcasted_iota(I32, (1, nh * HEAD_DIM), 1) // HEAD_DIM
    qf = [q_ref[0].astype(F32) * scale for q_ref in (q0_ref, q1_ref, q2_ref)]

    def masked_bias(b_ref, first_valid):
        col = lax.broadcasted_iota(I32, b_ref.shape, 1)
        return jnp.where(col >= first_valid, b_ref[...], NEG_INF)

    classes = [(qf[0], jnp.concatenate([k0p_ref[0], k0_ref[0]], axis=0),
                jnp.concatenate([v0p_ref[0], v0_ref[0]], axis=0), masked_bias(b0_ref, Q_BLOCK - start))]
    w1 = pl.ds(pl.multiple_of(pid * (Q_BLOCK // 4), Q_BLOCK // 4), A_SPAN1)
    bias1 = masked_bias(b1_ref, Q_BLOCK - pid * (Q_BLOCK // 4))
    for r4 in range(4):
        q1 = jnp.concatenate([qf[1][(r4 + 4 * u) * r8:(r4 + 4 * u + 1) * r8] for u in range(4)], axis=0)
        classes.append((q1, k1_ref[0, r4, w1, :], v1_ref[0, r4, w1, :], bias1))
    w2 = pl.ds(pl.multiple_of(pid * r8, r8), A_SPAN2)
    bias2 = masked_bias(b2_ref, Q_BLOCK - pid * r8)
    for r16 in range(16):
        classes.append((qf[2][r16 * r8:(r16 + 1) * r8], k2_ref[0, r16, w2, :].astype(BF16),
                        v2_ref[0, r16, w2, :].astype(BF16), bias2))

    def rows_of(vals):
        g1 = jnp.concatenate([vals[1 + r16 % 4][(r16 // 4) * r8:(r16 // 4 + 1) * r8] for r16 in range(16)], axis=0)
        return [vals[0], g1, jnp.concatenate(vals[5:21], axis=0)]

    def stacked(q):
        return jnp.concatenate([jnp.where(head_of_lane == h, q, 0.0) for h in range(nh)], axis=0).astype(BF16)

    s = [_dot_nt(stacked(q), k) + jnp.concatenate([bias] * nh, axis=0) for q, k, _, bias in classes]
    m = [jnp.max(si, axis=-1, keepdims=True) for si in s]
    p = [jnp.exp(si - mi) for si, mi in zip(s, m)]
    l = [jnp.sum(pi, axis=-1, keepdims=True) for pi in p]
    o = [_dot(pi.astype(BF16), c[2]) for pi, c in zip(p, classes)]

    out = None
    for hh in reversed(range(nh)):
        mine = lambda vals: [v[hh * (v.shape[0] // nh):(hh + 1) * (v.shape[0] // nh)] for v in vals]
        ms, ls, os_ = rows_of(mine(m)), rows_of(mine(l)), rows_of(mine(o))
        m_all = jnp.maximum(jnp.maximum(ms[0], ms[1]), ms[2])
        num = jnp.zeros((Q_BLOCK, nh * HEAD_DIM), F32)
        den = jnp.zeros((Q_BLOCK, 1), F32)
        for g in range(A_GROUPS):
            wg = jnp.exp(ms[g] - m_all)
            num = num + wg * os_[g]
            den = den + wg * ls[g]
        res = num / den
        out = res if out is None else jnp.where(head_of_lane == hh, res, out)
    o_ref[0] = out.astype(o_ref.dtype)


def _mixer_a2(pb, biases, batch, seq):
    nblk = seq // Q_BLOCK
    r8 = A_ROWS2
    gw = 2 * A_WIDTH
    kv1 = jnp.pad(pb[:, :, PB_AKV + gw:PB_AKV + 2 * gw], ((0, 0), (512, 0), (0, 0)))
    kv1 = kv1.reshape(batch, -1, 4, gw).transpose(0, 2, 1, 3)
    kv2 = jnp.pad(pb[:, :, PB_AKV + 2 * gw:PB_AKV + 3 * gw].astype(F32),
                  ((0, 0), (A_MAXWIN, 16 * (A_SPAN2 - 136)), (0, 0)))
    kv2 = kv2.reshape(batch, -1, 16, gw).transpose(0, 2, 1, 3)
    q = pb[:, :, PB_AQ:PB_BQ].reshape(batch, nblk, r8, 16, -1).transpose(0, 1, 3, 2, 4).reshape(batch, seq, -1)
    pair = A_STEP_HEADS * HEAD_DIM
    pairs_per_part = A_WIDTH // pair
    q_spec = lambda g: pl.BlockSpec((1, Q_BLOCK, pair), lambda b, hp, i: (b, i, g * pairs_per_part + hp))
    cur = lambda part: pl.BlockSpec((1, Q_BLOCK, pair), lambda b, hp, i: (b, i, PB_AKV // pair + part * pairs_per_part + hp))
    prev = lambda part: pl.BlockSpec((1, Q_BLOCK, pair),
                                     lambda b, hp, i: (b, jnp.maximum(i - 1, 0), PB_AKV // pair + part * pairs_per_part + hp))
    k4 = lambda a: pl.BlockSpec((1, a.shape[1], a.shape[2], pair), lambda b, hp, i: (b, 0, 0, hp))
    v4 = lambda a: pl.BlockSpec((1, a.shape[1], a.shape[2], pair), lambda b, hp, i: (b, 0, 0, pairs_per_part + hp))
    in_specs = [q_spec(0), q_spec(1), q_spec(2), prev(0), cur(0), prev(1), cur(1), k4(kv1), v4(kv1), k4(kv2), v4(kv2)]
    for bias in biases:
        in_specs.append(pl.BlockSpec(bias.shape, lambda b, hp, i: (0, 0)))
    o = pl.pallas_call(
        _mixa2_kernel,
        grid=(batch, A_HEADS // A_STEP_HEADS, nblk),
        in_specs=in_specs,
        out_specs=pl.BlockSpec((1, Q_BLOCK, pair), lambda b, hp, i: (b, i, hp)),
        out_shape=jax.ShapeDtypeStruct((batch, seq, A_WIDTH), BF16),
        compiler_params=_params("parallel", "parallel", "arbitrary", vmem_mb=52),
        name="mixer_a",
    )(q, q, q, pb, pb, pb, pb, kv1, kv1, kv2, kv2, *biases)
    return o.reshape(batch, nblk, 16, r8, A_WIDTH).transpose(0, 1, 3, 2, 4).reshape(batch, seq, A_WIDTH)


def _mixb_kernel(qb_ref, kb_ref, vb_ref, iq_ref, ik_ref, misc_ref, tri_ref, o_ref, keys_ref, selb_ref, s_ref, *,
                 top_k):
    rows = Q_BLOCK
    nparts = keys_ref.shape[1] // rows
    kf = float(top_k)
    parts = [_mixb_part(pt, rows, kf, qb_ref, kb_ref, vb_ref, iq_ref, ik_ref, misc_ref, tri_ref, o_ref,
                        keys_ref, selb_ref, s_ref) for pt in range(nparts)]
    for part in parts:
        part.score()
    ans = tuple(jnp.where(part.total(part.count(lambda sc: sc >= 0.0)) >= kf, 0, INT_MIN).astype(I32) for part in parts)

    def body(it, ans):
        bit = jnp.left_shift(jnp.int32(1), 30 - it)
        cands = [a + bit for a in ans]
        raw = [part.count(lambda sc, cf=_key_to_f32(cand): sc >= cf) for part, cand in zip(parts, cands)]
        return tuple(jnp.where(part.total(r) >= kf, cand, a) for part, r, cand, a in zip(parts, raw, cands, ans))

    thr = lax.fori_loop(0, 31, body, ans)
    for part, t in zip(parts, thr):
        part.attend(jnp.where(t == INT_MIN, NEG_INF, _key_to_f32(t)))


def _key_to_f32(key):
    return lax.bitcast_convert_type(key ^ ((key >> 31) & 0x7FFFFFFF), F32)


def _mixb_part(pt, rows, kf, qb_ref, kb_ref, vb_ref, iq_ref, ik_ref, misc_ref, tri_ref, o_ref,
               keys_ref, selb_ref, s_ref):
    ch = keys_ref.shape[2]
    rs = pl.ds(pt * rows, rows)
    start = pl.program_id(1) * keys_ref.shape[1] + pt * rows
    nch = (start + rows + ch - 1) // ch
    qpos = start + lax.broadcasted_iota(I32, (rows, 1), 0)
    zero_col = jnp.zeros((rows, 1), F32)
    part = types.SimpleNamespace()

    def score():
        w = misc_ref[0, rs, 0:IDX_HEADS] * (IDX_DIM ** -0.5 * IDX_HEADS ** -0.5)
        iq = iq_ref[0, rs, :]
        lax.fori_loop(0, nch, functools.partial(score_chunk, w, iq), 0)

    def score_chunk(w, iq, c, carry):
        off = pl.multiple_of(c * ch, ch)
        ik = ik_ref[0, pl.ds(off, ch), 0:IDX_DIM]
        acc = jnp.zeros((rows, ch), F32)
        for h in range(IDX_HEADS):
            lg = _dot_nt(iq[:, h * IDX_DIM:(h + 1) * IDX_DIM], ik)
            acc = acc + jnp.maximum(lg, 0.0) * w[:, h:h + 1]
        kpos = off + lax.broadcasted_iota(I32, (rows, ch), 1)
        keys_ref[c, rs, :] = jnp.where(kpos <= qpos, acc, NEG_INF)
        return carry

    def count(pred):
        def chunk(c, cnt):
            kc = keys_ref[c, rs, :]
            for j in range(ch // LANES):
                cnt = cnt + jnp.where(pred(kc[:, j * LANES:(j + 1) * LANES]), 1.0, 0.0)
            return cnt
        return lax.fori_loop(0, nch, chunk, jnp.zeros((rows, LANES), F32))

    def total(cnt):
        return jnp.sum(cnt, axis=-1, keepdims=True)

    part.score, part.count, part.total = score, count, total
    part.attend = functools.partial(_mixb_attend, part, pt, rows, kf, nch, zero_col, qb_ref, kb_ref, vb_ref,
                                    tri_ref, o_ref, keys_ref, selb_ref, s_ref)
    return part


def _mixb_attend(part, pt, rows, kf, nch, zero_col, qb_ref, kb_ref, vb_ref, tri_ref, o_ref, keys_ref, selb_ref, s_ref,
                 thr):
    ch = keys_ref.shape[2]
    rs = pl.ds(pt * rows, rows)
    count = lambda pred: part.total(part.count(pred))
    need = kf - count(lambda kc: kc > thr)

    tw = tri_ref.shape[0]

    def select_chunk(c, run):
        kc_all = keys_ref[c, rs, :]
        pieces = [kc_all[:, j * tw:(j + 1) * tw] for j in range(ch // tw)]
        eqf = [jnp.where(kc == thr, 1.0, 0.0) for kc in pieces]
        pre = [_dot(e.astype(BF16), tri_ref[...]) for e in eqf]
        for j, kc in enumerate(pieces):
            take = jnp.where(pre[j] + run <= need, eqf[j], 0.0)
            sel = jnp.where(kc > thr, 1.0, take)
            sel = jnp.where(kc > NEG_INF, sel, 0.0)
            selb_ref[c, rs, j * tw:(j + 1) * tw] = jnp.where(sel > 0.5, 0.0, NEG_INF)
            run = run + jnp.sum(eqf[j], axis=-1, keepdims=True)
        return run

    def select_chunk_no_ties(c, carry):
        kc = keys_ref[c, rs, :]
        selb_ref[c, rs, :] = jnp.where(kc >= thr, jnp.where(kc > NEG_INF, 0.0, NEG_INF), NEG_INF)
        return carry

    excess = jnp.max(count(lambda kc: kc >= thr)) > kf
    lax.cond(excess,
             lambda: lax.fori_loop(0, nch, select_chunk, zero_col),
             lambda: lax.fori_loop(0, nch, select_chunk_no_ties, zero_col))

    heads = range(B_HEADS)
    lane = lax.broadcasted_iota(I32, (1, B_WIDTH), 1) // HEAD_DIM
    qs = (qb_ref[0, rs, :].astype(F32) * HEAD_DIM ** -0.5).astype(BF16)
    qm_all = jnp.concatenate([jnp.where(lane == h, qs, jnp.zeros_like(qs)) for h in heads], axis=0)
    groups = [slice(j * LANES, (j + 1) * LANES) for j in range(ch // LANES)]

    def per_lane(cols):
        out = cols[B_HEADS - 1]
        for h in reversed(range(B_HEADS - 1)):
            out = jnp.where(lane <= h, cols[h], out)
        return out

    def score_pass(c, mx):
        off = pl.multiple_of(c * ch, ch)
        bias = selb_ref[c, rs, :]
        kc = kb_ref[0, pl.ds(off, ch), :]
        s_all = _dot_nt(qm_all, kc)
        out = []
        for h in heads:
            s = s_all[h * rows:(h + 1) * rows] + bias
            s_ref[h, c] = s
            m = mx[h]
            for g in groups:
                m = jnp.maximum(m, s[:, g])
            out.append(m)
        return tuple(out)

    mx = lax.fori_loop(0, nch, score_pass, (jnp.full((rows, LANES), NEG_INF, F32),) * B_HEADS)
    m = [jnp.max(mx[h], axis=-1, keepdims=True) for h in heads]

    def value_pass(c, carry):
        off = pl.multiple_of(c * ch, ch)
        vc = vb_ref[0, pl.ds(off, ch), :]
        p = [jnp.exp(s_ref[h, c] - m[h]) for h in heads]
        ls = []
        for h in heads:
            l = carry[h]
            for g in groups:
                l = l + p[h][:, g]
            ls.append(l)
        pv_all = _dot(jnp.concatenate([p[h].astype(BF16) for h in heads], axis=0), vc)
        pv = [pv_all[h * rows:(h + 1) * rows] for h in heads]
        return tuple(ls) + (carry[B_HEADS] + per_lane(pv),)

    init = (jnp.zeros((rows, LANES), F32),) * B_HEADS + (jnp.zeros((rows, B_WIDTH), F32),)
    res = lax.fori_loop(0, nch, value_pass, init)
    l = [jnp.sum(res[h], axis=-1, keepdims=True) for h in heads]
    o_ref[0, rs, :] = (res[B_HEADS] / per_lane(l)).astype(o_ref.dtype)


def _mixer_b(pb, pf, tri, batch, seq):
    rows = min(4 * Q_BLOCK, seq)
    top_k = min(TOPK_MAX, seq // 4)
    ch = min(512, seq)
    kern = functools.partial(_mixb_kernel, top_k=top_k)
    return pl.pallas_call(
        kern,
        grid=(batch, seq // rows),
        in_specs=[pl.BlockSpec((1, rows, B_WIDTH), lambda b, i: (b, i, PB_BQ // B_WIDTH)),
                  pl.BlockSpec((1, seq, B_WIDTH), lambda b, i: (b, 0, PB_BK // B_WIDTH)),
                  pl.BlockSpec((1, seq, B_WIDTH), lambda b, i: (b, 0, PB_BV // B_WIDTH)),
                  pl.BlockSpec((1, rows, 256), lambda b, i: (b, i, PB_IQ // 256)),
                  pl.BlockSpec((1, seq, LANES), lambda b, i: (b, 0, PB_IK // LANES)),
                  pl.BlockSpec((1, rows, LANES), lambda b, i: (b, i, PF_MISC // LANES)),
                  pl.BlockSpec(tri.shape, lambda b, i: (0, 0))],
        out_specs=pl.BlockSpec((1, rows, B_WIDTH), lambda b, i: (b, i, 0)),
        out_shape=jax.ShapeDtypeStruct((batch, seq, B_WIDTH), BF16),
        scratch_shapes=[pltpu.VMEM((seq // ch, rows, ch), F32), pltpu.VMEM((seq // ch, rows, ch), F32),
                        pltpu.VMEM((B_HEADS, seq // ch, Q_BLOCK, ch), F32)],
        compiler_params=_params("parallel", "arbitrary", vmem_mb=48),
        name="mixer_b",
    )(pb, pb, pb, pb, pb, pf, tri)


def _head_consts():
    lane = np.arange(C_WIDTH)
    same = (lane[:, None] // HEAD_DIM == lane[None, :] // HEAD_DIM).astype(np.float32)
    return jnp.asarray(same)


def _rwkv1_kernel(cin_ref, prev_ref, vf_ref, mu_ref, vec_ref, lora_ref, vl_ref, bd_ref,
                  phi_ref, psi_ref, rp_ref, y0_ref, g_ref, bonus_ref, v_ref, *, rows, use_v_lora):
    i = pl.program_id(1)
    pc = cin_ref[0][:, :C_IN]
    prev = jnp.where(i > 0, prev_ref[0][SUBLANES - 1:SUBLANES, :C_IN], 0.0)
    row = lax.broadcasted_iota(I32, (rows, 1), 0)
    shifted = jnp.where(row == 0, prev, pltpu.roll(pc, 1, 0))
    pc = pc + (shifted - pc) * mu_ref[...]
    r = pc[:, 0:C_WIDTH]
    k = pc[:, C_WIDTH:2 * C_WIDTH]
    v = pc[:, 2 * C_WIDTH:3 * C_WIDTH]
    xl = pc[:, 3 * C_WIDTH:C_IN]
    lane = lax.broadcasted_iota(I32, xl.shape, 1)
    feat = jnp.where(lane < C_LORA_W, jnp.tanh(xl), jnp.where(lane < C_LORA_W + C_LORA_A, xl, _sigmoid(xl)))
    w0, a0, k_k, k_a, r_k = (vec_ref[n:n + 1, :] for n in range(5))
    bd = bd_ref[...]
    bd_sel = bd.astype(BF16)
    w_raw = -_softplus(-(w0 + _dot3(feat, lora_ref[0]))) - 0.5
    a = _sigmoid(a0 + _dot3(feat, lora_ref[1]))
    g_ref[0] = _dot3(feat, lora_ref[2])
    if use_v_lora:
        v0 = vec_ref[5:6, :]
        v = v + (vf_ref[0] - v) * _sigmoid(v0 + _dot3(_dot3(v, vl_ref[0]), vl_ref[1]))
    v_ref[0] = v
    kk = k * k_k
    kk = kk / jnp.maximum(jnp.sqrt(_dot_sel(bd_sel, kk * kk, left=False)), 1e-12)
    k = k * (1.0 + (a - 1.0) * k_a)
    logw = -jnp.exp(w_raw)
    bonus_ref[0] = _dot_sel(bd_sel, r * k * r_k, left=False) * v
    av = -kk
    bv = kk * a

    cc = C_CHUNK
    ri = lax.broadcasted_iota(I32, (cc, cc), 0)
    ci = lax.broadcasted_iota(I32, (cc, cc), 1)
    tril = jnp.where(ci <= ri, 1.0, 0.0).astype(BF16)
    strict = ci < ri
    incl = ci <= ri
    eye_c = jnp.where(ci == ri, 1.0, 0.0)
    lane_w = lax.broadcasted_iota(I32, (1, C_WIDTH), 1)
    ri2 = lax.broadcasted_iota(I32, (C_WIDTH, C_WIDTH), 0)
    ci2 = lax.broadcasted_iota(I32, (C_WIDTH, C_WIDTH), 1)
    eye_w = jnp.where(ri2 == ci2, 1.0, 0.0)
    chunks = range(rows // cc)
    heads = range(C_HEADS)
    pairs = [(c, h) for c in chunks for h in heads]
    mh = [jnp.where(lane_w // HEAD_DIM == h, 1.0, 0.0) for h in heads]
    sl = [slice(c * cc, (c + 1) * cc) for c in chunks]
    cs = [_dot_sel(tril, logw[sl[c]]) for c in chunks]
    cs_end = [cs[c][cc - 1:cc, :] for c in chunks]
    at = [av[sl[c]] * jnp.exp(cs[c] - logw[sl[c]]) for c in chunks]
    rt = [r[sl[c]] * jnp.exp(cs[c]) for c in chunks]
    inv = [jnp.exp(-cs[c]) for c in chunks]
    rhs = [jnp.concatenate([bv[sl[c]] * inv[c], k[sl[c]] * inv[c]], axis=0) for c in chunks]
    tail = [jnp.exp(cs_end[c] - cs[c]) for c in chunks]
    vc = [v[sl[c]] for c in chunks]
    vc_b = [vc[c].astype(BF16) for c in chunks]
    ath = {(c, h): at[c] * mh[h] for c, h in pairs}
    aa = {(c, h): _dot3(jnp.concatenate([ath[c, h], rt[c] * mh[h]], axis=0), rhs[c], nt=True) for c, h in pairs}
    a_ab = {p: jnp.where(strict, aa[p][:cc, :cc], 0.0) for p in pairs}
    a_ak = {p: jnp.where(strict, aa[p][:cc, cc:], 0.0).astype(BF16) for p in pairs}
    a_rb = {p: jnp.where(incl, aa[p][cc:, :cc], 0.0).astype(BF16) for p in pairs}
    a_rk = {p: jnp.where(incl, aa[p][cc:, cc:], 0.0).astype(BF16) for p in pairs}
    x = {p: eye_c + a_ab[p] for p in pairs}
    pw = a_ab
    for _ in range(int(math.log2(cc)) - 1):
        pw = {p: _dot3(pw[p], pw[p]) for p in pairs}
        x = {p: x[p] + _dot3(x[p], pw[p]) for p in pairs}
    akv = {(c, h): _dot(a_ak[c, h], vc_b[c]) for c, h in pairs}
    ap_h = {p: _dot3(x[p], ath[p]) for p in pairs}
    w2_h = {(c, h): _dot3(x[c, h], akv[c, h]) * mh[h] for c, h in pairs}
    rp_h = {p: _dot(a_rb[p], ap_h[p].astype(BF16)) for p in pairs}
    y0_h = {(c, h): (_dot(a_rb[c, h], w2_h[c, h].astype(BF16)) + _dot(a_rk[c, h], vc_b[c])) * mh[h] for c, h in pairs}
    for c in chunks:
        ap = sum(ap_h[c, h] for h in heads)
        w2 = sum(w2_h[c, h] for h in heads)
        rp_ref[0, sl[c], :] = rt[c] + sum(rp_h[c, h] for h in heads)
        y0_ref[0, sl[c], :] = sum(y0_h[c, h] for h in heads)
        bh = bv[sl[c]] * tail[c]
        kh = k[sl[c]] * tail[c]
        phi_ref[0, c] = eye_w * jnp.exp(cs_end[c]) + _dot3(bh.T, ap) * bd
        psi_ref[0, c] = _dot(jnp.concatenate([bh, kh], axis=0).T.astype(BF16),
                             jnp.concatenate([w2, vc[c]], axis=0).astype(BF16)) * bd


def _rwkv2_kernel(phi_ref, psi_ref, rp_ref, y0_ref, g_ref, bonus_ref, gn_ref, bd_ref, o_ref, s_ref):
    @pl.when(pl.program_id(1) == 0)
    def _():
        s_ref[...] = jnp.zeros_like(s_ref)

    states = [s_ref[...]]
    for c in range(phi_ref.shape[1]):
        states.append(_dot3(phi_ref[0, c], states[c]) + psi_ref[0, c])
    s_ref[...] = states[-1]
    cc = C_CHUNK
    y = jnp.concatenate([_dot3(rp_ref[0, c * cc:(c + 1) * cc, :], states[c]) for c in range(phi_ref.shape[1])],
                        axis=0) + y0_ref[0]
    bd_sel = bd_ref[...].astype(BF16)
    mu = _dot_sel(bd_sel, y, left=False) * (1.0 / HEAD_DIM)
    d = y - mu
    var = _dot_sel(bd_sel, d * d, left=False) * (1.0 / HEAD_DIM)
    yn = d * lax.rsqrt(var + C_GN_EPS) * gn_ref[0:1, :] + gn_ref[1:2, :]
    o_ref[0] = ((yn + bonus_ref[0]) * g_ref[0]).astype(o_ref.dtype)


def _mixer_c(pf, v_first, mu, vec, lora, vl, gn, bd, batch, seq, use_v_lora):
    rows = min(256, seq)
    nblk = seq // rows
    nch = seq // C_CHUNK
    cpb = rows // C_CHUNK
    kern = functools.partial(_rwkv1_kernel, rows=rows, use_v_lora=use_v_lora)
    full2 = lambda a: pl.BlockSpec(a.shape, lambda b, i: (0,) * a.ndim)
    seq_spec = pl.BlockSpec((1, rows, C_WIDTH), lambda b, i: (b, i, 0))
    mat_spec = pl.BlockSpec((1, cpb, C_WIDTH, C_WIDTH), lambda b, i: (b, i, 0, 0))
    seq_shape = jax.ShapeDtypeStruct((batch, seq, C_WIDTH), F32)
    mat_shape = jax.ShapeDtypeStruct((batch, nch, C_WIDTH, C_WIDTH), F32)
    prev_blk = rows // SUBLANES
    phi, psi, rp, y0, g, bonus, v = pl.pallas_call(
        kern,
        grid=(batch, nblk),
        in_specs=[pl.BlockSpec((1, rows, 1024), lambda b, i: (b, i, PF_CIN // 1024)),
                  pl.BlockSpec((1, SUBLANES, 1024), lambda b, i: (b, jnp.maximum(i * prev_blk - 1, 0), PF_CIN // 1024)),
                  seq_spec, full2(mu), full2(vec), full2(lora), full2(vl), full2(bd)],
        out_specs=[mat_spec, mat_spec, seq_spec, seq_spec, seq_spec, seq_spec, seq_spec],
        out_shape=[mat_shape, mat_shape, seq_shape, seq_shape, seq_shape, seq_shape, seq_shape],
        compiler_params=_params("parallel", "parallel", vmem_mb=48),
        name="rwkv_chunks",
    )(pf, pf, v_first, mu, vec, lora, vl, bd)
    scan_chunks = min(4, nch)
    cseq = pl.BlockSpec((1, scan_chunks * C_CHUNK, C_WIDTH), lambda b, c: (b, c, 0))
    cmat = pl.BlockSpec((1, scan_chunks, C_WIDTH, C_WIDTH), lambda b, c: (b, c, 0, 0))
    o = pl.pallas_call(
        _rwkv2_kernel,
        grid=(batch, nch // scan_chunks),
        in_specs=[cmat, cmat, cseq, cseq, cseq, cseq,
                  pl.BlockSpec(gn.shape, lambda b, c: (0, 0)), pl.BlockSpec(bd.shape, lambda b, c: (0, 0))],
        out_specs=cseq,
        out_shape=jax.ShapeDtypeStruct((batch, seq, C_WIDTH), BF16),
        scratch_shapes=[pltpu.VMEM((C_WIDTH, C_WIDTH), F32)],
        compiler_params=_params("parallel", "arbitrary"),
        name="rwkv_scan",
    )(phi, psi, rp, y0, g, bonus, gn, bd)
    return o, v


def _ssd_consts():
    expand = np.zeros((LANES, D_INNER), np.float32)
    for h in range(D_HEADS):
        expand[MISC_DT + h, h * HEAD_DIM:(h + 1) * HEAD_DIM] = 1.0
    return jnp.asarray(expand)


def _ssd_kernel(xbc_ref, z_ref, misc_ref, conv_ref, vec_ref, hp_ref, expand_ref, bd_ref, o_ref, st_ref, prev_ref):
    @pl.when(pl.program_id(1) == 0)
    def _():
        st_ref[...] = jnp.zeros_like(st_ref)
        prev_ref[...] = jnp.zeros_like(prev_ref)

    q = D_CHUNK
    x_raw = xbc_ref[0]
    ext = jnp.concatenate([prev_ref[...], x_raw], axis=0)
    conv = jnp.zeros((q, D_XBC), F32)
    for t in range(D_CONV):
        lo = SUBLANES - (D_CONV - 1) + t
        conv = conv + ext[lo:lo + q, :] * conv_ref[t:t + 1, :]
    prev_ref[...] = x_raw[q - SUBLANES:, :]
    xbc = _silu(conv + conv_ref[D_CONV:D_CONV + 1, :])
    xs = xbc[:, :D_INNER]
    bm = xbc[:, D_INNER:D_INNER + D_GROUPS * D_STATE]
    cm = xbc[:, D_INNER + D_GROUPS * D_STATE:]

    dt_col = _softplus(misc_ref[0] + hp_ref[0:1, :])
    a_col = dt_col * hp_ref[1:2, :]
    ri = lax.broadcasted_iota(I32, (q, q), 0)
    ci = lax.broadcasted_iota(I32, (q, q), 1)
    causal = ci <= ri
    tril = jnp.where(causal, 1.0, 0.0).astype(BF16)
    acs_col = _dot_sel(tril, a_col)
    acs_row = acs_col.T
    expand = expand_ref[...].astype(BF16)
    acs = _dot_sel(expand, acs_col, left=False)
    dt = _dot_sel(expand, dt_col, left=False)
    acs_end = acs[q - 1:q, :]
    xdt = xs * dt

    lane = lax.broadcasted_iota(I32, (1, LANES), 1)
    lane_w = lax.broadcasted_iota(I32, (1, D_INNER), 1)
    left = lane < D_STATE
    bm_sw = pltpu.roll(bm, D_STATE, 1)
    cm_sw = pltpu.roll(cm, D_STATE, 1)
    b_exp = jnp.concatenate([jnp.where(left, bm, bm_sw), jnp.where(left, bm_sw, bm)], axis=1)
    c_exp = jnp.concatenate([jnp.where(left, cm, cm_sw), jnp.where(left, cm_sw, cm)], axis=1)
    cb = [_dot3(jnp.where(left == (g == 0), cm, 0.0), bm, nt=True) for g in range(D_GROUPS)]

    scores = []
    for h in range(D_HEADS):
        col = acs_col[:, MISC_DT + h:MISC_DT + h + 1]
        rw = acs_row[MISC_DT + h:MISC_DT + h + 1, :]
        decay = jnp.exp(jnp.where(causal, col - rw, NEG_INF))
        scores.append(cb[h // (D_HEADS // D_GROUPS)] * decay)
    y_h = [_dot3(scores[h], xdt) for h in range(D_HEADS)]
    y = jnp.zeros((q, D_INNER), F32)
    for h in range(D_HEADS):
        y = y + jnp.where(lane_w // HEAD_DIM == h, y_h[h], 0.0)
    st = st_ref[...]
    y = y + _dot3(c_exp, st) * jnp.exp(acs)
    st_ref[...] = st * jnp.exp(acs_end) + _dot3(b_exp.T, xdt * jnp.exp(acs_end - acs)) * bd_ref[...]
    y = y + xs * vec_ref[0:1, :]
    y = y * _silu(z_ref[0])
    half = D_INNER // D_GROUPS
    outs = []
    for g in range(D_GROUPS):
        yg = y[:, g * half:(g + 1) * half]
        outs.append(yg * lax.rsqrt(jnp.mean(yg * yg, axis=-1, keepdims=True) + D_NORM_EPS))
    o_ref[0] = (jnp.concatenate(outs, axis=1) * vec_ref[1:2, :]).astype(o_ref.dtype)


def _mixer_d(pf, conv, vec, hp, expand, bd, batch, seq):
    q = D_CHUNK
    full = lambda a: pl.BlockSpec(a.shape, lambda b, i: (0, 0))
    return pl.pallas_call(
        _ssd_kernel,
        grid=(batch, seq // q),
        in_specs=[pl.BlockSpec((1, q, D_XBC), lambda b, i: (b, i, PF_XBC // D_XBC)),
                  pl.BlockSpec((1, q, D_INNER), lambda b, i: (b, i, PF_Z // D_INNER)),
                  pl.BlockSpec((1, q, LANES), lambda b, i: (b, i, PF_MISC // LANES)),
                  full(conv), full(vec), full(hp), full(expand), full(bd)],
        out_specs=pl.BlockSpec((1, q, D_INNER), lambda b, i: (b, i, 0)),
        out_shape=jax.ShapeDtypeStruct((batch, seq, D_INNER), BF16),
        scratch_shapes=[pltpu.VMEM((D_INNER, D_INNER), F32), pltpu.VMEM((SUBLANES, D_XBC), F32)],
        compiler_params=_params("parallel", "arbitrary"),
        name="ssd",
    )(pf, pf, pf, conv, vec, hp, expand, bd)


def _merge_kernel(x_ref, gates_ref, oa_ref, ob_ref, oc_ref, od_ref, wbr_ref, wout_ref, ln_ref, wr_ref, br_ref,
                  x1_ref, route_ref):
    acc = None
    for n, o_ref in enumerate((oa_ref, ob_ref, oc_ref, od_ref)):
        term = _sigmoid(gates_ref[:, n * D_MODEL:(n + 1) * D_MODEL]) * _dot(o_ref[...], wbr_ref[n])
        acc = term if acc is None else acc + term
    h = DEEPNORM_ALPHA * x_ref[...] + _dot(acc.astype(BF16), wout_ref[...])
    x1 = _layer_norm(h, ln_ref[0:1, :], ln_ref[1:2, :])
    x1_ref[...] = x1

    logits = _dot3(x1, wr_ref[...]) + br_ref[...]
    lane = lax.broadcasted_iota(I32, logits.shape, 1)
    big = jnp.int32(LANES)
    gl = jnp.where(lane < N_EXPERT_GROUPS, logits, NEG_INF)
    gm = jnp.max(gl, axis=-1, keepdims=True)
    pg_top = 1.0 / jnp.sum(jnp.exp(gl - gm), axis=-1, keepdims=True)
    g_sel = jnp.min(jnp.where(gl == gm, lane, big), axis=-1, keepdims=True)
    off = N_EXPERT_GROUPS + g_sel * EXPERTS_PER_GROUP
    el = jnp.where((lane >= off) & (lane < off + EXPERTS_PER_GROUP), logits, NEG_INF)
    em = jnp.max(el, axis=-1, keepdims=True)
    es = jnp.sum(jnp.exp(el - em), axis=-1, keepdims=True)
    idx1 = jnp.min(jnp.where(el == em, lane, big), axis=-1, keepdims=True)
    el2 = jnp.where(lane == idx1, NEG_INF, el)
    em2 = jnp.max(el2, axis=-1, keepdims=True)
    idx2 = jnp.min(jnp.where(el2 == em2, lane, big), axis=-1, keepdims=True)
    p1 = 1.0 / es
    p2 = jnp.exp(em2 - em) / es
    gate1 = pg_top * p1 / (p1 + p2)
    gate2 = pg_top * p2 / (p1 + p2)
    e1 = (idx1 - N_EXPERT_GROUPS).astype(F32)
    e2 = (idx2 - N_EXPERT_GROUPS).astype(F32)
    route_ref[...] = jnp.where(lane == 0, e1, jnp.where(lane == 1, e2, jnp.where(lane == 2, gate1,
                               jnp.where(lane == 3, gate2, 0.0))))


def _merge(x2d, pf, oa, ob, oc, od, wbr, wout, ln, wr, br, tm):
    m = x2d.shape[0]
    row = lambda w: pl.BlockSpec((tm, w), lambda i: (i, 0))
    full = lambda a: pl.BlockSpec(a.shape, lambda i: (0,) * a.ndim)
    return pl.pallas_call(
        _merge_kernel,
        grid=(m // tm,),
        in_specs=[row(D_MODEL), pl.BlockSpec((tm, N_BRANCH * D_MODEL), lambda i: (i, PF_GATES)),
                  row(A_WIDTH), row(B_WIDTH), row(C_WIDTH), row(D_INNER),
                  full(wbr), full(wout), full(ln), full(wr), full(br)],
        out_specs=[row(D_MODEL), row(LANES)],
        out_shape=[jax.ShapeDtypeStruct((m, D_MODEL), F32), jax.ShapeDtypeStruct((m, LANES), F32)],
        compiler_params=_params("parallel", vmem_mb=48),
        name="merge_route",
    )(x2d, pf, oa, ob, oc, od, wbr, wout, ln, wr, br)


def _ffn_kernel(be_ref, nv_ref, xs_ref, wg_ref, wu_ref, wd_ref, o_ref):
    i = pl.program_id(0)

    @pl.when(i < nv_ref[0])
    def _():
        xb = xs_ref[...]
        hg = _dot(xb, wg_ref[0, 0].astype(BF16))
        hu = _dot(xb, wu_ref[0, 0].astype(BF16))
        o_ref[...] = _dot((_silu(hg) * hu).astype(BF16), wd_ref[0, 0].astype(BF16))

    @pl.when(i >= nv_ref[0])
    def _():
        o_ref[...] = jnp.zeros_like(o_ref)


def _ffn(blk_expert, n_valid, xs, e_gate, e_up, e_down, layer):
    cap = xs.shape[0]
    wspec = lambda a: pl.BlockSpec((1, 1) + a.shape[2:], lambda i, be, nv: (layer, be[i], 0, 0))
    return pl.pallas_call(
        _ffn_kernel,
        grid_spec=pltpu.PrefetchScalarGridSpec(
            num_scalar_prefetch=2,
            grid=(cap // MOE_ROWS,),
            in_specs=[pl.BlockSpec((MOE_ROWS, D_MODEL), lambda i, be, nv: (i, 0)),
                      wspec(e_gate), wspec(e_up), wspec(e_down)],
            out_specs=pl.BlockSpec((MOE_ROWS, D_MODEL), lambda i, be, nv: (i, 0)),
        ),
        out_shape=jax.ShapeDtypeStruct((cap, D_MODEL), F32),
        compiler_params=_params("arbitrary", vmem_mb=48),
        name="expert_ffn",
    )(blk_expert, n_valid, xs, e_gate, e_up, e_down)


def _combine_kernel(dest_ref, x_ref, route_ref, ln_ref, yb_ref, o_ref, ybuf, sem):
    i = pl.program_id(0)
    tm = x_ref.shape[0]

    def gather(tile, slot):
        base = tile * (TOP_K_EXPERTS * tm)

        def body(r, carry):
            for s in range(TOP_K_EXPERTS):
                row = dest_ref[base + TOP_K_EXPERTS * r + s]
                pltpu.make_async_copy(yb_ref.at[pl.ds(row, 1)], ybuf.at[slot, pl.ds(s * tm + r, 1)],
                                      sem.at[slot]).start()
            return carry

        lax.fori_loop(0, tm, body, 0, unroll=8)

    @pl.when(i == 0)
    def _():
        gather(0, 0)

    @pl.when(i + 1 < pl.num_programs(0))
    def _():
        gather(i + 1, (i + 1) % 2)

    slot = i % 2
    pltpu.make_async_copy(yb_ref.at[pl.ds(0, TOP_K_EXPERTS * tm)], ybuf.at[slot], sem.at[slot]).wait()
    g0 = route_ref[:, 2:3]
    g1 = route_ref[:, 3:4]
    h = DEEPNORM_ALPHA * x_ref[...] + (ybuf[slot, 0:tm, :] * g0 + ybuf[slot, tm:2 * tm, :] * g1)
    o_ref[...] = _layer_norm(h, ln_ref[0:1, :], ln_ref[1:2, :])


def _combine(x1, yb, dest, route, ln, tm):
    m = x1.shape[0]
    return pl.pallas_call(
        _combine_kernel,
        grid_spec=pltpu.PrefetchScalarGridSpec(
            num_scalar_prefetch=1,
            grid=(m // tm,),
            in_specs=[pl.BlockSpec((tm, D_MODEL), lambda i, d: (i, 0)),
                      pl.BlockSpec((tm, LANES), lambda i, d: (i, 0)),
                      pl.BlockSpec(ln.shape, lambda i, d: (0, 0)),
                      pl.BlockSpec(memory_space=pl.ANY)],
            out_specs=pl.BlockSpec((tm, D_MODEL), lambda i, d: (i, 0)),
            scratch_shapes=[pltpu.VMEM((2, TOP_K_EXPERTS * tm, D_MODEL), F32), pltpu.SemaphoreType.DMA((2,))],
        ),
        out_shape=jax.ShapeDtypeStruct((m, D_MODEL), F32),
        compiler_params=_params("arbitrary"),
        name="combine",
    )(dest, x1, route, ln, yb)


def _dispatch_tables(route, m):
    flat_e = route[:, 0:TOP_K_EXPERTS].astype(I32).reshape(-1)
    n_assign = m * TOP_K_EXPERTS
    onehot = (flat_e[:, None] == jnp.arange(N_EXPERTS, dtype=I32)[None, :]).astype(I32)
    csum = jnp.cumsum(onehot, axis=0)
    rank = jnp.sum(csum * onehot, axis=1) - 1
    counts = csum[-1]
    padded = (counts + MOE_ROWS - 1) // MOE_ROWS * MOE_ROWS
    pad_end = jnp.cumsum(padded)
    pad_start = pad_end - padded
    dest = pad_start[flat_e] + rank
    cap = (n_assign + N_EXPERTS * (MOE_ROWS - 1) + MOE_ROWS - 1) // MOE_ROWS * MOE_ROWS
    n_blocks = cap // MOE_ROWS
    blk_start = jnp.arange(n_blocks, dtype=I32) * MOE_ROWS
    blk_expert = jnp.minimum(jnp.sum((pad_end[None, :] <= blk_start[:, None]).astype(I32), axis=1), N_EXPERTS - 1)
    buf_tok = (jnp.arange(cap, dtype=I32) % m).at[dest].set(jnp.arange(n_assign, dtype=I32) // TOP_K_EXPERTS)
    n_valid = (pad_end[-1] // MOE_ROWS).astype(I32).reshape(1)
    return dest, buf_tok, blk_expert, n_valid


def _row_pad(a, rows):
    return jnp.pad(a, ((0, rows - a.shape[0]), (0, 0)))


def kernel(x, w_in, c_mu, c_w0, c_w2, c_a0, c_a2, c_g2, c_kk, c_ka, c_rk, c_gn_w, c_gn_b, c_v0, c_v1, c_v2,
           d_conv_w, d_conv_b, d_dt_bias, d_a_log, d_skip, d_norm_w, w_branch, w_out, ln1_g, ln1_b,
           r_group, r_group_b, r_expert, r_expert_b, e_gate, e_up, e_down, ln2_g, ln2_b):
    batch, seq, _ = x.shape
    m = batch * seq
    biases = _mixa_class_bias()
    tw = min(256, seq)
    tri = jnp.asarray(np.triu(np.ones((tw, tw), np.float32))).astype(BF16)
    bd = _head_consts()
    expand = _ssd_consts()
    x2d = x.reshape(m, D_MODEL)
    v_first = jnp.zeros((batch, seq, C_WIDTH), F32)
    tm_proj = min(2048, m)
    tm_tok = min(512, m)
    for l in range(DEPTH):
        wb, wf = _proj_weights(w_in[l])
        pb = _proj(x2d, wb, BF16, tm_proj, PROJ_TN).reshape(batch, seq, PB_WIDTH)
        pf = _proj(x2d, wf, F32, tm_proj, PROJ_TN).reshape(batch, seq, PF_WIDTH)

        o_a = _mixer_a2(pb, biases, batch, seq)
        o_b = _mixer_b(pb, pf, tri, batch, seq)

        use_v_lora = l > 0
        vec_rows = [c_w0[l], c_a0[l], c_kk[l], c_ka[l], c_rk[l].reshape(-1)]
        vec_rows.append(c_v0[l - 1] if use_v_lora else jnp.zeros((C_WIDTH,), F32))
        vec = _row_pad(jnp.stack(vec_rows), SUBLANES)
        lora = jnp.stack([
            jnp.pad(c_w2[l], ((0, LANES - C_LORA_W), (0, 0))),
            jnp.pad(c_a2[l], ((C_LORA_W, LANES - C_LORA_W - C_LORA_A), (0, 0))),
            jnp.pad(c_g2[l], ((C_LORA_W + C_LORA_A, 0), (0, 0)))])
        if use_v_lora:
            vl = jnp.stack([jnp.pad(c_v1[l - 1], ((0, 0), (0, C_WIDTH - C_LORA_V))),
                            jnp.pad(c_v2[l - 1], ((0, C_WIDTH - C_LORA_V), (0, 0)))])
        else:
            vl = jnp.zeros((2, C_WIDTH, C_WIDTH), F32)
        gn = _row_pad(jnp.stack([c_gn_w[l], c_gn_b[l]]), SUBLANES)
        o_c, v_c = _mixer_c(pf, v_first, c_mu[l].reshape(1, C_IN), vec, lora, vl, gn, bd, batch, seq, use_v_lora)
        if l == 0:
            v_first = v_c

        conv = _row_pad(jnp.concatenate([d_conv_w[l], d_conv_b[l][None, :]], axis=0), SUBLANES)
        dvec = _row_pad(jnp.stack([jnp.repeat(d_skip[l], HEAD_DIM), d_norm_w[l]]), SUBLANES)
        place = lambda a: jnp.pad(a, (MISC_DT, LANES - MISC_DT - D_HEADS))
        hp = _row_pad(jnp.stack([place(d_dt_bias[l]), place(-jnp.exp(d_a_log[l]))]), SUBLANES)
        o_d = _mixer_d(pf, conv, dvec, hp, expand, bd, batch, seq)

        wr = jnp.pad(jnp.concatenate([r_group[l], r_expert[l]], axis=1),
                     ((0, 0), (0, LANES - N_EXPERT_GROUPS - N_EXPERTS)))
        br = jnp.pad(jnp.concatenate([r_group_b[l], r_expert_b[l]]), (0, LANES - N_EXPERT_GROUPS - N_EXPERTS))
        ln1 = _row_pad(jnp.stack([ln1_g[l], ln1_b[l]]), SUBLANES)
        x1, route = _merge(x2d, pf.reshape(m, PF_WIDTH), o_a.reshape(m, -1), o_b.reshape(m, -1),
                           o_c.reshape(m, -1), o_d.reshape(m, -1), w_branch[l].astype(BF16),
                           w_out[l].astype(BF16), ln1, wr, br.reshape(1, LANES), tm_tok)

        dest, buf_tok, blk_expert, n_valid = _dispatch_tables(route, m)
        xs = jnp.take(x1.astype(BF16), buf_tok, axis=0)
        yb = _ffn(blk_expert, n_valid, xs, e_gate, e_up, e_down, l)
        ln2 = _row_pad(jnp.stack([ln2_g[l], ln2_b[l]]), SUBLANES)
        x2d = _combine(x1, yb, dest, route, ln2, min(256, m))
    return x2d.reshape(batch, seq, D_MODEL)
```

```python
import functools
import math
import types

import jax
import jax.numpy as jnp
import numpy as np
from jax import lax
from jax.experimental import pallas as pl
from jax.experimental.pallas import tpu as pltpu

F32 = jnp.float32
BF16 = jnp.bfloat16
I32 = jnp.int32
NEG_INF = float("-inf")
INT_MIN = -(2 ** 31)

LANES = 128
SUBLANES = 8

D_MODEL = 1024
DEPTH = 2
HEAD_DIM = 64
Q_BLOCK = 128

A_HEADS = 4
A_PATTERNS = ((128, 1), (512, 4), (2048, 16))
A_GROUPS = len(A_PATTERNS)
A_WIDTH = A_HEADS * HEAD_DIM
A_MAXWIN = max(w for w, _ in A_PATTERNS)

B_HEADS = 4
B_WIDTH = B_HEADS * HEAD_DIM
IDX_HEADS = 4
IDX_DIM = 64
TOPK_MAX = 256

C_HEADS = 4
C_WIDTH = C_HEADS * HEAD_DIM
C_LORA_W = 32
C_LORA_A = 32
C_LORA_G = 64
C_LORA_V = 16
C_IN = 3 * C_WIDTH + C_LORA_W + C_LORA_A + C_LORA_G
C_GN_EPS = 64e-5
C_CHUNK = 64

D_HEADS = 4
D_INNER = D_HEADS * HEAD_DIM
D_GROUPS = 2
D_STATE = 64
D_CONV = 4
D_CHUNK = 128
D_XBC = D_INNER + 2 * D_GROUPS * D_STATE
D_NORM_EPS = 1e-5

N_BRANCH = 4
N_EXPERT_GROUPS = 4
EXPERTS_PER_GROUP = 8
N_EXPERTS = N_EXPERT_GROUPS * EXPERTS_PER_GROUP
TOP_K_EXPERTS = 2
D_EXPERT = 512
MOE_ROWS = 512

LN_EPS = 1e-5
DEEPNORM_ALPHA = (2 * DEPTH) ** 0.25

PB_AKV = 0
PB_AQ = PB_AKV + A_GROUPS * 2 * A_WIDTH
PB_BQ = PB_AQ + A_GROUPS * A_WIDTH
PB_BK = PB_BQ + B_WIDTH
PB_BV = PB_BK + B_WIDTH
PB_IQ = PB_BV + B_WIDTH
PB_IK = PB_IQ + IDX_HEADS * IDX_DIM
PB_WIDTH = PB_IK + 512
PF_GATES = 0
PF_CIN = PF_GATES + N_BRANCH * D_MODEL
PF_MISC = PF_CIN + C_IN
PF_XBC = PF_CIN + 1024
PF_Z = PF_XBC + D_XBC
PROJ_TN = 768
PF_WIDTH = -(-(PF_Z + D_INNER) // PROJ_TN) * PROJ_TN
MISC_DT = 4


def _dot(a, b, precision=None):
    return jnp.dot(a, b, preferred_element_type=F32, precision=precision)


def _dot_nt(a, b, precision=None):
    return lax.dot_general(a, b, (((1,), (1,)), ((), ())), preferred_element_type=F32, precision=precision)


def _split2(a):
    hi = a.astype(BF16)
    return hi, (a - hi.astype(F32)).astype(BF16)


def _dot3(a, b, nt=False):
    mm = _dot_nt if nt else _dot
    ah, al = _split2(a)
    bh, bl = _split2(b)
    return mm(ah, bh) + (mm(ah, bl) + mm(al, bh))


def _dot_sel(sel, x, left=True):
    hi = x.astype(BF16)
    r1 = x - hi.astype(F32)
    mid = r1.astype(BF16)
    lo = (r1 - mid.astype(F32)).astype(BF16)
    if left:
        return _dot(sel, hi) + (_dot(sel, mid) + _dot(sel, lo))
    return _dot(hi, sel) + (_dot(mid, sel) + _dot(lo, sel))


def _sigmoid(x):
    return 1.0 / (1.0 + jnp.exp(-x))


def _softplus(x):
    return jnp.maximum(x, 0.0) + jnp.log1p(jnp.exp(-jnp.abs(x)))


def _silu(x):
    return x * _sigmoid(x)


def _layer_norm(h, g, b):
    mu = jnp.mean(h, axis=-1, keepdims=True)
    d = h - mu
    var = jnp.mean(d * d, axis=-1, keepdims=True)
    return d * lax.rsqrt(var + LN_EPS) * g + b


def _params(*sem, vmem_mb=None):
    kw = {}
    if vmem_mb is not None:
        kw["vmem_limit_bytes"] = vmem_mb * 1024 * 1024
    return pltpu.CompilerParams(dimension_semantics=sem, **kw)


def _proj_kernel(x_ref, w_ref, o_ref):
    o_ref[...] = _dot(x_ref[...].astype(BF16), w_ref[...]).astype(o_ref.dtype)


def _proj(x2d, w, out_dtype, tm, tn):
    m, k = x2d.shape
    n = w.shape[1]
    return pl.pallas_call(
        _proj_kernel,
        grid=(m // tm, n // tn),
        in_specs=[pl.BlockSpec((tm, k), lambda i, j: (i, 0)),
                  pl.BlockSpec((k, tn), lambda i, j: (0, j))],
        out_specs=pl.BlockSpec((tm, tn), lambda i, j: (i, j)),
        out_shape=jax.ShapeDtypeStruct((m, n), out_dtype),
        compiler_params=_params("parallel", "arbitrary", vmem_mb=48),
        name="in_proj",
    )(x2d, w)


def _proj_weights(w):
    k = w.shape[0]
    off = 0
    seg = {}
    for name, size in (("a_qkv", 3 * A_GROUPS * A_WIDTH), ("b_qkv", 3 * B_WIDTH), ("b_idx_q", IDX_HEADS * IDX_DIM),
                       ("b_idx_k", IDX_DIM), ("b_idx_w", IDX_HEADS), ("c_in", C_IN), ("d_z", D_INNER),
                       ("d_xbc", D_XBC), ("d_dt", D_HEADS), ("gates", N_BRANCH * D_MODEL)):
        seg[name] = w[:, off:off + size]
        off += size
    a = seg["a_qkv"].reshape(k, 3, A_GROUPS, A_HEADS, HEAD_DIM)
    a_kv = jnp.transpose(a[:, 1:3], (0, 2, 1, 3, 4)).reshape(k, -1)
    a_q = a[:, 0].reshape(k, -1)
    zeros = lambda n: jnp.zeros((k, n), w.dtype)
    wb = jnp.concatenate([a_kv, a_q, seg["b_qkv"], seg["b_idx_q"], seg["b_idx_k"],
                          zeros(PB_WIDTH - PB_IK - IDX_DIM)], axis=1)
    misc = jnp.concatenate([seg["b_idx_w"], seg["d_dt"], zeros(LANES - IDX_HEADS - D_HEADS)], axis=1)
    wf = jnp.concatenate([seg["gates"], seg["c_in"], misc, seg["d_xbc"], seg["d_z"]], axis=1)
    wf = jnp.concatenate([wf, zeros(PF_WIDTH - wf.shape[1])], axis=1)
    assert wb.shape[1] == PB_WIDTH and PB_WIDTH % PROJ_TN == 0
    return wb.astype(BF16), wf.astype(BF16)


A_ROWS2 = Q_BLOCK // 16
A_SPAN1 = 512 // 4 + Q_BLOCK // 4
A_SPAN2 = 144
A_STEP_HEADS = 4


def _mixa_class_bias():
    x = (np.arange(Q_BLOCK) % A_ROWS2) * 16 + np.arange(Q_BLOCK) // A_ROWS2
    c0 = np.arange(2 * Q_BLOCK)[None, :]
    b0 = np.where((c0 >= x[:, None]) & (c0 <= x[:, None] + Q_BLOCK), 0.0, -np.inf)
    u, a16 = np.arange(32) // A_ROWS2, np.arange(32) % A_ROWS2
    a4 = 4 * a16 + u
    c1 = np.arange(A_SPAN1)[None, :]
    b1 = np.where((c1 >= a4[:, None]) & (c1 <= a4[:, None] + Q_BLOCK), 0.0, -np.inf)
    a2 = np.arange(A_ROWS2)
    c2 = np.arange(A_SPAN2)[None, :]
    b2 = np.where((c2 >= a2[:, None]) & (c2 <= a2[:, None] + Q_BLOCK), 0.0, -np.inf)
    return [jnp.asarray(b.astype(np.float32)) for b in (b0, b1, b2)]


def _mixa2_kernel(q0_ref, q1_ref, q2_ref, k0p_ref, k0_ref, v0p_ref, v0_ref, k1_ref, v1_ref, k2_ref, v2_ref,
                  b0_ref, b1_ref, b2_ref, o_ref):
    pid = pl.program_id(2)
    start = pid * Q_BLOCK
    r8 = A_ROWS2
    scale = HEAD_DIM ** -0.5
    nh = q0_ref.shape[2] // HEAD_DIM
    head_of_lane = lax.broadcasted_iota(I32, (1, nh * HEAD_DIM), 1) // HEAD_DIM
    qf = [q_ref[0].astype(F32) * scale for q_ref in (q0_ref, q1_ref, q2_ref)]

    def masked_bias(b_ref, first_valid):
        col = lax.broadcasted_iota(I32, b_ref.shape, 1)
        return jnp.where(col >= first_valid, b_ref[...], NEG_INF)

    classes = [(qf[0], jnp.concatenate([k0p_ref[0], k0_ref[0]], axis=0),
                jnp.concatenate([v0p_ref[0], v0_ref[0]], axis=0), masked_bias(b0_ref, Q_BLOCK - start))]
    w1 = pl.ds(pl.multiple_of(pid * (Q_BLOCK // 4), Q_BLOCK // 4), A_SPAN1)
    bias1 = masked_bias(b1_ref, Q_BLOCK - pid * (Q_BLOCK // 4))
    for r4 in range(4):
        q1 = jnp.concatenate([qf[1][(r4 + 4 * u) * r8:(r4 + 4 * u + 1) * r8] for u in range(4)], axis=0)
        classes.append((q1, k1_ref[0, r4, w1, :], v1_ref[0, r4, w1, :], bias1))
    w2 = pl.ds(pl.multiple_of(pid * r8, r8), A_SPAN2)
    bias2 = masked_bias(b2_ref, Q_BLOCK - pid * r8)
    for r16 in range(16):
        classes.append((qf[2][r16 * r8:(r16 + 1) * r8], k2_ref[0, r16, w2, :].astype(BF16),
                        v2_ref[0, r16, w2, :].astype(BF16), bias2))

    def rows_of(vals):
        g1 = jnp.concatenate([vals[1 + r16 % 4][(r16 // 4) * r8:(r16 // 4 + 1) * r8] for r16 in range(16)], axis=0)
        return [vals[0], g1, jnp.concatenate(vals[5:21], axis=0)]

    def stacked(q):
        return jnp.concatenate([jnp.where(head_of_lane == h, q, 0.0) for h in range(nh)], axis=0).astype(BF16)

    s = [_dot_nt(stacked(q), k) + jnp.concatenate([bias] * nh, axis=0) for q, k, _, bias in classes]
    m = [jnp.max(si, axis=-1, keepdims=True) for si in s]
    p = [jnp.exp(si - mi) for si, mi in zip(s, m)]
    l = [jnp.sum(pi, axis=-1, keepdims=True) for pi in p]
    o = [_dot(pi.astype(BF16), c[2]) for pi, c in zip(p, classes)]

    out = None
    for hh in reversed(range(nh)):
        mine = lambda vals: [v[hh * (v.shape[0] // nh):(hh + 1) * (v.shape[0] // nh)] for v in vals]
        ms, ls, os_ = rows_of(mine(m)), rows_of(mine(l)), rows_of(mine(o))
        m_all = jnp.maximum(jnp.maximum(ms[0], ms[1]), ms[2])
        num = jnp.zeros((Q_BLOCK, nh * HEAD_DIM), F32)
        den = jnp.zeros((Q_BLOCK, 1), F32)
        for g in range(A_GROUPS):
            wg = jnp.exp(ms[g] - m_all)
            num = num + wg * os_[g]
            den = den + wg * ls[g]
        res = num / den
        out = res if out is None else jnp.where(head_of_lane == hh, res, out)
    o_ref[0] = out.astype(o_ref.dtype)


def _mixer_a2(pb, biases, batch, seq):
    nblk = seq // Q_BLOCK
    r8 = A_ROWS2
    gw = 2 * A_WIDTH
    kv1 = jnp.pad(pb[:, :, PB_AKV + gw:PB_AKV + 2 * gw], ((0, 0), (512, 0), (0, 0)))
    kv1 = kv1.reshape(batch, -1, 4, gw).transpose(0, 2, 1, 3)
    kv2 = jnp.pad(pb[:, :, PB_AKV + 2 * gw:PB_AKV + 3 * gw].astype(F32),
                  ((0, 0), (A_MAXWIN, 16 * (A_SPAN2 - 136)), (0, 0)))
    kv2 = kv2.reshape(batch, -1, 16, gw).transpose(0, 2, 1, 3)
    q = pb[:, :, PB_AQ:PB_BQ].reshape(batch, nblk, r8, 16, -1).transpose(0, 1, 3, 2, 4).reshape(batch, seq, -1)
    pair = A_STEP_HEADS * HEAD_DIM
    pairs_per_part = A_WIDTH // pair
    q_spec = lambda g: pl.BlockSpec((1, Q_BLOCK, pair), lambda b, hp, i: (b, i, g * pairs_per_part + hp))
    cur = lambda part: pl.BlockSpec((1, Q_BLOCK, pair), lambda b, hp, i: (b, i, PB_AKV // pair + part * pairs_per_part + hp))
    prev = lambda part: pl.BlockSpec((1, Q_BLOCK, pair),
                                     lambda b, hp, i: (b, jnp.maximum(i - 1, 0), PB_AKV // pair + part * pairs_per_part + hp))
    k4 = lambda a: pl.BlockSpec((1, a.shape[1], a.shape[2], pair), lambda b, hp, i: (b, 0, 0, hp))
    v4 = lambda a: pl.BlockSpec((1, a.shape[1], a.shape[2], pair), lambda b, hp, i: (b, 0, 0, pairs_per_part + hp))
    in_specs = [q_spec(0), q_spec(1), q_spec(2), prev(0), cur(0), prev(1), cur(1), k4(kv1), v4(kv1), k4(kv2), v4(kv2)]
    for bias in biases:
        in_specs.append(pl.BlockSpec(bias.shape, lambda b, hp, i: (0, 0)))
    o = pl.pallas_call(
        _mixa2_kernel,
        grid=(batch, A_HEADS // A_STEP_HEADS, nblk),
        in_specs=in_specs,
        out_specs=pl.BlockSpec((1, Q_BLOCK, pair), lambda b, hp, i: (b, i, hp)),
        out_shape=jax.ShapeDtypeStruct((batch, seq, A_WIDTH), BF16),
        compiler_params=_params("parallel", "parallel", "arbitrary", vmem_mb=52),
        name="mixer_a",
    )(q, q, q, pb, pb, pb, pb, kv1, kv1, kv2, kv2, *biases)
    return o.reshape(batch, nblk, 16, r8, A_WIDTH).transpose(0, 1, 3, 2, 4).reshape(batch, seq, A_WIDTH)


def _mixb_kernel(qb_ref, kb_ref, vb_ref, iq_ref, ik_ref, misc_ref, tri_ref, o_ref, keys_ref, selb_ref, s_ref, *,
                 top_k):
    rows = Q_BLOCK
    nparts = keys_ref.shape[1] // rows
    kf = float(top_k)
    parts = [_mixb_part(pt, rows, kf, qb_ref, kb_ref, vb_ref, iq_ref, ik_ref, misc_ref, tri_ref, o_ref,
                        keys_ref, selb_ref, s_ref) for pt in range(nparts)]
    for part in parts:
        part.score()
    ans = tuple(jnp.where(part.total(part.count(lambda sc: sc >= 0.0)) >= kf, 0, INT_MIN).astype(I32) for part in parts)

    def body(it, ans):
        bit = jnp.left_shift(jnp.int32(1), 30 - it)
        cands = [a + bit for a in ans]
        raw = [part.count(lambda sc, cf=_key_to_f32(cand): sc >= cf) for part, cand in zip(parts, cands)]
        return tuple(jnp.where(part.total(r) >= kf, cand, a) for part, r, cand, a in zip(parts, raw, cands, ans))

    thr = lax.fori_loop(0, 31, body, ans)
    for part, t in zip(parts, thr):
        part.attend(jnp.where(t == INT_MIN, NEG_INF, _key_to_f32(t)))


def _key_to_f32(key):
    return lax.bitcast_convert_type(key ^ ((key >> 31) & 0x7FFFFFFF), F32)


def _mixb_part(pt, rows, kf, qb_ref, kb_ref, vb_ref, iq_ref, ik_ref, misc_ref, tri_ref, o_ref,
               keys_ref, selb_ref, s_ref):
    ch = keys_ref.shape[2]
    rs = pl.ds(pt * rows, rows)
    start = pl.program_id(1) * keys_ref.shape[1] + pt * rows
    nch = (start + rows + ch - 1) // ch
    qpos = start + lax.broadcasted_iota(I32, (rows, 1), 0)
    zero_col = jnp.zeros((rows, 1), F32)
    part = types.SimpleNamespace()

    def score():
        w = misc_ref[0, rs, 0:IDX_HEADS] * (IDX_DIM ** -0.5 * IDX_HEADS ** -0.5)
        iq = iq_ref[0, rs, :]
        lax.fori_loop(0, nch, functools.partial(score_chunk, w, iq), 0)

    def score_chunk(w, iq, c, carry):
        off = pl.multiple_of(c * ch, ch)
        ik = ik_ref[0, pl.ds(off, ch), 0:IDX_DIM]
        acc = jnp.zeros((rows, ch), F32)
        for h in range(IDX_HEADS):
            lg = _dot_nt(iq[:, h * IDX_DIM:(h + 1) * IDX_DIM], ik)
            acc = acc + jnp.maximum(lg, 0.0) * w[:, h:h + 1]
        kpos = off + lax.broadcasted_iota(I32, (rows, ch), 1)
        keys_ref[c, rs, :] = jnp.where(kpos <= qpos, acc, NEG_INF)
        return carry

    def count(pred):
        def chunk(c, cnt):
            kc = keys_ref[c, rs, :]
            for j in range(ch // LANES):
                cnt = cnt + jnp.where(pred(kc[:, j * LANES:(j + 1) * LANES]), 1.0, 0.0)
            return cnt
        return lax.fori_loop(0, nch, chunk, jnp.zeros((rows, LANES), F32))

    def total(cnt):
        return jnp.sum(cnt, axis=-1, keepdims=True)

    part.score, part.count, part.total = score, count, total
    part.attend = functools.partial(_mixb_attend, part, pt, rows, kf, nch, zero_col, qb_ref, kb_ref, vb_ref,
                                    tri_ref, o_ref, keys_ref, selb_ref, s_ref)
    return part


def _mixb_attend(part, pt, rows, kf, nch, zero_col, qb_ref, kb_ref, vb_ref, tri_ref, o_ref, keys_ref, selb_ref, s_ref,
                 thr):
    ch = keys_ref.shape[2]
    rs = pl.ds(pt * rows, rows)
    count = lambda pred: part.total(part.count(pred))
    need = kf - count(lambda kc: kc > thr)

    tw = tri_ref.shape[0]

    def select_chunk(c, run):
        kc_all = keys_ref[c, rs, :]
        pieces = [kc_all[:, j * tw:(j + 1) * tw] for j in range(ch // tw)]
        eqf = [jnp.where(kc == thr, 1.0, 0.0) for kc in pieces]
        pre = [_dot(e.astype(BF16), tri_ref[...]) for e in eqf]
        for j, kc in enumerate(pieces):
            take = jnp.where(pre[j] + run <= need, eqf[j], 0.0)
            sel = jnp.where(kc > thr, 1.0, take)
            sel = jnp.where(kc > NEG_INF, sel, 0.0)
            selb_ref[c, rs, j * tw:(j + 1) * tw] = jnp.where(sel > 0.5, 0.0, NEG_INF)
            run = run + jnp.sum(eqf[j], axis=-1, keepdims=True)
        return run

    def select_chunk_no_ties(c, carry):
        kc = keys_ref[c, rs, :]
        selb_ref[c, rs, :] = jnp.where(kc >= thr, jnp.where(kc > NEG_INF, 0.0, NEG_INF), NEG_INF)
        return carry

    excess = jnp.max(count(lambda kc: kc >= thr)) > kf
    lax.cond(excess,
             lambda: lax.fori_loop(0, nch, select_chunk, zero_col),
             lambda: lax.fori_loop(0, nch, select_chunk_no_ties, zero_col))

    heads = range(B_HEADS)
    lane = lax.broadcasted_iota(I32, (1, B_WIDTH), 1) // HEAD_DIM
    qs = (qb_ref[0, rs, :].astype(F32) * HEAD_DIM ** -0.5).astype(BF16)
    qm_all = jnp.concatenate([jnp.where(lane == h, qs, jnp.zeros_like(qs)) for h in heads], axis=0)
    groups = [slice(j * LANES, (j + 1) * LANES) for j in range(ch // LANES)]

    def per_lane(cols):
        out = cols[B_HEADS - 1]
        for h in reversed(range(B_HEADS - 1)):
            out = jnp.where(lane <= h, cols[h], out)
        return out

    def score_pass(c, mx):
        off = pl.multiple_of(c * ch, ch)
        bias = selb_ref[c, rs, :]
        kc = kb_ref[0, pl.ds(off, ch), :]
        s_all = _dot_nt(qm_all, kc)
        out = []
        for h in heads:
            s = s_all[h * rows:(h + 1) * rows] + bias
            s_ref[h, c] = s
            m = mx[h]
            for g in groups:
                m = jnp.maximum(m, s[:, g])
            out.append(m)
        return tuple(out)

    mx = lax.fori_loop(0, nch, score_pass, (jnp.full((rows, LANES), NEG_INF, F32),) * B_HEADS)
    m = [jnp.max(mx[h], axis=-1, keepdims=True) for h in heads]

    def value_pass(c, carry):
        off = pl.multiple_of(c * ch, ch)
        vc = vb_ref[0, pl.ds(off, ch), :]
        p = [jnp.exp(s_ref[h, c] - m[h]) for h in heads]
        ls = []
        for h in heads:
            l = carry[h]
            for g in groups:
                l = l + p[h][:, g]
            ls.append(l)
        pv_all = _dot(jnp.concatenate([p[h].astype(BF16) for h in heads], axis=0), vc)
        pv = [pv_all[h * rows:(h + 1) * rows] for h in heads]
        return tuple(ls) + (carry[B_HEADS] + per_lane(pv),)

    init = (jnp.zeros((rows, LANES), F32),) * B_HEADS + (jnp.zeros((rows, B_WIDTH), F32),)
    res = lax.fori_loop(0, nch, value_pass, init)
    l = [jnp.sum(res[h], axis=-1, keepdims=True) for h in heads]
    o_ref[0, rs, :] = (res[B_HEADS] / per_lane(l)).astype(o_ref.dtype)


def _mixer_b(pb, pf, tri, batch, seq):
    rows = min(4 * Q_BLOCK, seq)
    top_k = min(TOPK_MAX, seq // 4)
    ch = min(512, seq)
    kern = functools.partial(_mixb_kernel, top_k=top_k)
    return pl.pallas_call(
        kern,
        grid=(batch, seq // rows),
        in_specs=[pl.BlockSpec((1, rows, B_WIDTH), lambda b, i: (b, i, PB_BQ // B_WIDTH)),
                  pl.BlockSpec((1, seq, B_WIDTH), lambda b, i: (b, 0, PB_BK // B_WIDTH)),
                  pl.BlockSpec((1, seq, B_WIDTH), lambda b, i: (b, 0, PB_BV // B_WIDTH)),
                  pl.BlockSpec((1, rows, 256), lambda b, i: (b, i, PB_IQ // 256)),
                  pl.BlockSpec((1, seq, LANES), lambda b, i: (b, 0, PB_IK // LANES)),
                  pl.BlockSpec((1, rows, LANES), lambda b, i: (b, i, PF_MISC // LANES)),
                  pl.BlockSpec(tri.shape, lambda b, i: (0, 0))],
        out_specs=pl.BlockSpec((1, rows, B_WIDTH), lambda b, i: (b, i, 0)),
        out_shape=jax.ShapeDtypeStruct((batch, seq, B_WIDTH), BF16),
        scratch_shapes=[pltpu.VMEM((seq // ch, rows, ch), F32), pltpu.VMEM((seq // ch, rows, ch), F32),
                        pltpu.VMEM((B_HEADS, seq // ch, Q_BLOCK, ch), F32)],
        compiler_params=_params("parallel", "arbitrary", vmem_mb=48),
        name="mixer_b",
    )(pb, pb, pb, pb, pb, pf, tri)


def _head_consts():
    lane = np.arange(C_WIDTH)
    same = (lane[:, None] // HEAD_DIM == lane[None, :] // HEAD_DIM).astype(np.float32)
    return jnp.asarray(same)


def _rwkv1_kernel(cin_ref, prev_ref, vf_ref, mu_ref, vec_ref, lora_ref, vl_ref, bd_ref,
                  phi_ref, psi_ref, rp_ref, y0_ref, g_ref, bonus_ref, v_ref, *, rows, use_v_lora):
    i = pl.program_id(1)
    pc = cin_ref[0][:, :C_IN]
    prev = jnp.where(i > 0, prev_ref[0][SUBLANES - 1:SUBLANES, :C_IN], 0.0)
    row = lax.broadcasted_iota(I32, (rows, 1), 0)
    shifted = jnp.where(row == 0, prev, pltpu.roll(pc, 1, 0))
    pc = pc + (shifted - pc) * mu_ref[...]
    r = pc[:, 0:C_WIDTH]
    k = pc[:, C_WIDTH:2 * C_WIDTH]
    v = pc[:, 2 * C_WIDTH:3 * C_WIDTH]
    xl = pc[:, 3 * C_WIDTH:C_IN]
    lane = lax.broadcasted_iota(I32, xl.shape, 1)
    feat = jnp.where(lane < C_LORA_W, jnp.tanh(xl), jnp.where(lane < C_LORA_W + C_LORA_A, xl, _sigmoid(xl)))
    w0, a0, k_k, k_a, r_k = (vec_ref[n:n + 1, :] for n in range(5))
    bd = bd_ref[...]
    bd_sel = bd.astype(BF16)
    w_raw = -_softplus(-(w0 + _dot3(feat, lora_ref[0]))) - 0.5
    a = _sigmoid(a0 + _dot3(feat, lora_ref[1]))
    g_ref[0] = _dot3(feat, lora_ref[2])
    if use_v_lora:
        v0 = vec_ref[5:6, :]
        v = v + (vf_ref[0] - v) * _sigmoid(v0 + _dot3(_dot3(v, vl_ref[0]), vl_ref[1]))
    v_ref[0] = v
    kk = k * k_k
    kk = kk / jnp.maximum(jnp.sqrt(_dot_sel(bd_sel, kk * kk, left=False)), 1e-12)
    k = k * (1.0 + (a - 1.0) * k_a)
    logw = -jnp.exp(w_raw)
    bonus_ref[0] = _dot_sel(bd_sel, r * k * r_k, left=False) * v
    av = -kk
    bv = kk * a

    cc = C_CHUNK
    ri = lax.broadcasted_iota(I32, (cc, cc), 0)
    ci = lax.broadcasted_iota(I32, (cc, cc), 1)
    tril = jnp.where(ci <= ri, 1.0, 0.0).astype(BF16)
    strict = ci < ri
    incl = ci <= ri
    eye_c = jnp.where(ci == ri, 1.0, 0.0)
    lane_w = lax.broadcasted_iota(I32, (1, C_WIDTH), 1)
    ri2 = lax.broadcasted_iota(I32, (C_WIDTH, C_WIDTH), 0)
    ci2 = lax.broadcasted_iota(I32, (C_WIDTH, C_WIDTH), 1)
    eye_w = jnp.where(ri2 == ci2, 1.0, 0.0)
    chunks = range(rows // cc)
    heads = range(C_HEADS)
    pairs = [(c, h) for c in chunks for h in heads]
    mh = [jnp.where(lane_w // HEAD_DIM == h, 1.0, 0.0) for h in heads]
    sl = [slice(c * cc, (c + 1) * cc) for c in chunks]
    cs = [_dot_sel(tril, logw[sl[c]]) for c in chunks]
    cs_end = [cs[c][cc - 1:cc, :] for c in chunks]
    at = [av[sl[c]] * jnp.exp(cs[c] - logw[sl[c]]) for c in chunks]
    rt = [r[sl[c]] * jnp.exp(cs[c]) for c in chunks]
    inv = [jnp.exp(-cs[c]) for c in chunks]
    rhs = [jnp.concatenate([bv[sl[c]] * inv[c], k[sl[c]] * inv[c]], axis=0) for c in chunks]
    tail = [jnp.exp(cs_end[c] - cs[c]) for c in chunks]
    vc = [v[sl[c]] for c in chunks]
    vc_b = [vc[c].astype(BF16) for c in chunks]
    ath = {(c, h): at[c] * mh[h] for c, h in pairs}
    aa = {(c, h): _dot3(jnp.concatenate([ath[c, h], rt[c] * mh[h]], axis=0), rhs[c], nt=True) for c, h in pairs}
    a_ab = {p: jnp.where(strict, aa[p][:cc, :cc], 0.0) for p in pairs}
    a_ak = {p: jnp.where(strict, aa[p][:cc, cc:], 0.0).astype(BF16) for p in pairs}
    a_rb = {p: jnp.where(incl, aa[p][cc:, :cc], 0.0).astype(BF16) for p in pairs}
    a_rk = {p: jnp.where(incl, aa[p][cc:, cc:], 0.0).astype(BF16) for p in pairs}
    x = {p: eye_c + a_ab[p] for p in pairs}
    pw = a_ab
    for _ in range(int(math.log2(cc)) - 1):
        pw = {p: _dot3(pw[p], pw[p]) for p in pairs}
        x = {p: x[p] + _dot3(x[p], pw[p]) for p in pairs}
    akv = {(c, h): _dot(a_ak[c, h], vc_b[c]) for c, h in pairs}
    ap_h = {p: _dot3(x[p], ath[p]) for p in pairs}
    w2_h = {(c, h): _dot3(x[c, h], akv[c, h]) * mh[h] for c, h in pairs}
    rp_h = {p: _dot(a_rb[p], ap_h[p].astype(BF16)) for p in pairs}
    y0_h = {(c, h): (_dot(a_rb[c, h], w2_h[c, h].astype(BF16)) + _dot(a_rk[c, h], vc_b[c])) * mh[h] for c, h in pairs}
    for c in chunks:
        ap = sum(ap_h[c, h] for h in heads)
        w2 = sum(w2_h[c, h] for h in heads)
        rp_ref[0, sl[c], :] = rt[c] + sum(rp_h[c, h] for h in heads)
        y0_ref[0, sl[c], :] = sum(y0_h[c, h] for h in heads)
        bh = bv[sl[c]] * tail[c]
        kh = k[sl[c]] * tail[c]
        phi_ref[0, c] = eye_w * jnp.exp(cs_end[c]) + _dot3(bh.T, ap) * bd
        psi_ref[0, c] = _dot(jnp.concatenate([bh, kh], axis=0).T.astype(BF16),
                             jnp.concatenate([w2, vc[c]], axis=0).astype(BF16)) * bd


def _rwkv2_kernel(phi_ref, psi_ref, rp_ref, y0_ref, g_ref, bonus_ref, gn_ref, bd_ref, o_ref, s_ref):
    @pl.when(pl.program_id(1) == 0)
    def _():
        s_ref[...] = jnp.zeros_like(s_ref)

    states = [s_ref[...]]
    for c in range(phi_ref.shape[1]):
        states.append(_dot3(phi_ref[0, c], states[c]) + psi_ref[0, c])
    s_ref[...] = states[-1]
    cc = C_CHUNK
    y = jnp.concatenate([_dot3(rp_ref[0, c * cc:(c + 1) * cc, :], states[c]) for c in range(phi_ref.shape[1])],
                        axis=0) + y0_ref[0]
    bd_sel = bd_ref[...].astype(BF16)
    mu = _dot_sel(bd_sel, y, left=False) * (1.0 / HEAD_DIM)
    d = y - mu
    var = _dot_sel(bd_sel, d * d, left=False) * (1.0 / HEAD_DIM)
    yn = d * lax.rsqrt(var + C_GN_EPS) * gn_ref[0:1, :] + gn_ref[1:2, :]
    o_ref[0] = ((yn + bonus_ref[0]) * g_ref[0]).astype(o_ref.dtype)


def _mixer_c(pf, v_first, mu, vec, lora, vl, gn, bd, batch, seq, use_v_lora):
    rows = min(256, seq)
    nblk = seq // rows
    nch = seq // C_CHUNK
    cpb = rows // C_CHUNK
    kern = functools.partial(_rwkv1_kernel, rows=rows, use_v_lora=use_v_lora)
    full2 = lambda a: pl.BlockSpec(a.shape, lambda b, i: (0,) * a.ndim)
    seq_spec = pl.BlockSpec((1, rows, C_WIDTH), lambda b, i: (b, i, 0))
    mat_spec = pl.BlockSpec((1, cpb, C_WIDTH, C_WIDTH), lambda b, i: (b, i, 0, 0))
    seq_shape = jax.ShapeDtypeStruct((batch, seq, C_WIDTH), F32)
    mat_shape = jax.ShapeDtypeStruct((batch, nch, C_WIDTH, C_WIDTH), F32)
    prev_blk = rows // SUBLANES
    phi, psi, rp, y0, g, bonus, v = pl.pallas_call(
        kern,
        grid=(batch, nblk),
        in_specs=[pl.BlockSpec((1, rows, 1024), lambda b, i: (b, i, PF_CIN // 1024)),
                  pl.BlockSpec((1, SUBLANES, 1024), lambda b, i: (b, jnp.maximum(i * prev_blk - 1, 0), PF_CIN // 1024)),
                  seq_spec, full2(mu), full2(vec), full2(lora), full2(vl), full2(bd)],
        out_specs=[mat_spec, mat_spec, seq_spec, seq_spec, seq_spec, seq_spec, seq_spec],
        out_shape=[mat_shape, mat_shape, seq_shape, seq_shape, seq_shape, seq_shape, seq_shape],
        compiler_params=_params("parallel", "parallel", vmem_mb=48),
        name="rwkv_chunks",
    )(pf, pf, v_first, mu, vec, lora, vl, bd)
    scan_chunks = min(4, nch)
    cseq = pl.BlockSpec((1, scan_chunks * C_CHUNK, C_WIDTH), lambda b, c: (b, c, 0))
    cmat = pl.BlockSpec((1, scan_chunks, C_WIDTH, C_WIDTH), lambda b, c: (b, c, 0, 0))
    o = pl.pallas_call(
        _rwkv2_kernel,
        grid=(batch, nch // scan_chunks),
        in_specs=[cmat, cmat, cseq, cseq, cseq, cseq,
                  pl.BlockSpec(gn.shape, lambda b, c: (0, 0)), pl.BlockSpec(bd.shape, lambda b, c: (0, 0))],
        out_specs=cseq,
        out_shape=jax.ShapeDtypeStruct((batch, seq, C_WIDTH), BF16),
        scratch_shapes=[pltpu.VMEM((C_WIDTH, C_WIDTH), F32)],
        compiler_params=_params("parallel", "arbitrary"),
        name="rwkv_scan",
    )(phi, psi, rp, y0, g, bonus, gn, bd)
    return o, v


def _ssd_consts():
    expand = np.zeros((LANES, D_INNER), np.float32)
    for h in range(D_HEADS):
        expand[MISC_DT + h, h * HEAD_DIM:(h + 1) * HEAD_DIM] = 1.0
    return jnp.asarray(expand)


def _ssd_kernel(xbc_ref, z_ref, misc_ref, conv_ref, vec_ref, hp_ref, expand_ref, bd_ref, o_ref, st_ref, prev_ref):
    @pl.when(pl.program_id(1) == 0)
    def _():
        st_ref[...] = jnp.zeros_like(st_ref)
        prev_ref[...] = jnp.zeros_like(prev_ref)

    q = D_CHUNK
    x_raw = xbc_ref[0]
    ext = jnp.concatenate([prev_ref[...], x_raw], axis=0)
    conv = jnp.zeros((q, D_XBC), F32)
    for t in range(D_CONV):
        lo = SUBLANES - (D_CONV - 1) + t
        conv = conv + ext[lo:lo + q, :] * conv_ref[t:t + 1, :]
    prev_ref[...] = x_raw[q - SUBLANES:, :]
    xbc = _silu(conv + conv_ref[D_CONV:D_CONV + 1, :])
    xs = xbc[:, :D_INNER]
    bm = xbc[:, D_INNER:D_INNER + D_GROUPS * D_STATE]
    cm = xbc[:, D_INNER + D_GROUPS * D_STATE:]

    dt_col = _softplus(misc_ref[0] + hp_ref[0:1, :])
    a_col = dt_col * hp_ref[1:2, :]
    ri = lax.broadcasted_iota(I32, (q, q), 0)
    ci = lax.broadcasted_iota(I32, (q, q), 1)
    causal = ci <= ri
    tril = jnp.where(causal, 1.0, 0.0).astype(BF16)
    acs_col = _dot_sel(tril, a_col)
    acs_row = acs_col.T
    expand = expand_ref[...].astype(BF16)
    acs = _dot_sel(expand, acs_col, left=False)
    dt = _dot_sel(expand, dt_col, left=False)
    acs_end = acs[q - 1:q, :]
    xdt = xs * dt

    lane = lax.broadcasted_iota(I32, (1, LANES), 1)
    lane_w = lax.broadcasted_iota(I32, (1, D_INNER), 1)
    left = lane < D_STATE
    bm_sw = pltpu.roll(bm, D_STATE, 1)
    cm_sw = pltpu.roll(cm, D_STATE, 1)
    b_exp = jnp.concatenate([jnp.where(left, bm, bm_sw), jnp.where(left, bm_sw, bm)], axis=1)
    c_exp = jnp.concatenate([jnp.where(left, cm, cm_sw), jnp.where(left, cm_sw, cm)], axis=1)
    cb = [_dot3(jnp.where(left == (g == 0), cm, 0.0), bm, nt=True) for g in range(D_GROUPS)]

    scores = []
    for h in range(D_HEADS):
        col = acs_col[:, MISC_DT + h:MISC_DT + h + 1]
        rw = acs_row[MISC_DT + h:MISC_DT + h + 1, :]
        decay = jnp.exp(jnp.where(causal, col - rw, NEG_INF))
        scores.append(cb[h // (D_HEADS // D_GROUPS)] * decay)
    y_h = [_dot3(scores[h], xdt) for h in range(D_HEADS)]
    y = jnp.zeros((q, D_INNER), F32)
    for h in range(D_HEADS):
        y = y + jnp.where(lane_w // HEAD_DIM == h, y_h[h], 0.0)
    st = st_ref[...]
    y = y + _dot3(c_exp, st) * jnp.exp(acs)
    st_ref[...] = st * jnp.exp(acs_end) + _dot3(b_exp.T, xdt * jnp.exp(acs_end - acs)) * bd_ref[...]
    y = y + xs * vec_ref[0:1, :]
    y = y * _silu(z_ref[0])
    half = D_INNER // D_GROUPS
    outs = []
    for g in range(D_GROUPS):
        yg = y[:, g * half:(g + 1) * half]
        outs.append(yg * lax.rsqrt(jnp.mean(yg * yg, axis=-1, keepdims=True) + D_NORM_EPS))
    o_ref[0] = (jnp.concatenate(outs, axis=1) * vec_ref[1:2, :]).astype(o_ref.dtype)


def _mixer_d(pf, conv, vec, hp, expand, bd, batch, seq):
    q = D_CHUNK
    full = lambda a: pl.BlockSpec(a.shape, lambda b, i: (0, 0))
    return pl.pallas_call(
        _ssd_kernel,
        grid=(batch, seq // q),
        in_specs=[pl.BlockSpec((1, q, D_XBC), lambda b, i: (b, i, PF_XBC // D_XBC)),
                  pl.BlockSpec((1, q, D_INNER), lambda b, i: (b, i, PF_Z // D_INNER)),
                  pl.BlockSpec((1, q, LANES), lambda b, i: (b, i, PF_MISC // LANES)),
                  full(conv), full(vec), full(hp), full(expand), full(bd)],
        out_specs=pl.BlockSpec((1, q, D_INNER), lambda b, i: (b, i, 0)),
        out_shape=jax.ShapeDtypeStruct((batch, seq, D_INNER), BF16),
        scratch_shapes=[pltpu.VMEM((D_INNER, D_INNER), F32), pltpu.VMEM((SUBLANES, D_XBC), F32)],
        compiler_params=_params("parallel", "arbitrary"),
        name="ssd",
    )(pf, pf, pf, conv, vec, hp, expand, bd)


def _merge_kernel(x_ref, gates_ref, oa_ref, ob_ref, oc_ref, od_ref, wbr_ref, wout_ref, ln_ref, wr_ref, br_ref,
                  x1_ref, route_ref):
    acc = None
    for n, o_ref in enumerate((oa_ref, ob_ref, oc_ref, od_ref)):
        term = _sigmoid(gates_ref[:, n * D_MODEL:(n + 1) * D_MODEL]) * _dot(o_ref[...], wbr_ref[n])
        acc = term if acc is None else acc + term
    h = DEEPNORM_ALPHA * x_ref[...] + _dot(acc.astype(BF16), wout_ref[...])
    x1 = _layer_norm(h, ln_ref[0:1, :], ln_ref[1:2, :])
    x1_ref[...] = x1

    logits = _dot3(x1, wr_ref[...]) + br_ref[...]
    lane = lax.broadcasted_iota(I32, logits.shape, 1)
    big = jnp.int32(LANES)
    gl = jnp.where(lane < N_EXPERT_GROUPS, logits, NEG_INF)
    gm = jnp.max(gl, axis=-1, keepdims=True)
    pg_top = 1.0 / jnp.sum(jnp.exp(gl - gm), axis=-1, keepdims=True)
    g_sel = jnp.min(jnp.where(gl == gm, lane, big), axis=-1, keepdims=True)
    off = N_EXPERT_GROUPS + g_sel * EXPERTS_PER_GROUP
    el = jnp.where((lane >= off) & (lane < off + EXPERTS_PER_GROUP), logits, NEG_INF)
    em = jnp.max(el, axis=-1, keepdims=True)
    es = jnp.sum(jnp.exp(el - em), axis=-1, keepdims=True)
    idx1 = jnp.min(jnp.where(el == em, lane, big), axis=-1, keepdims=True)
    el2 = jnp.where(lane == idx1, NEG_INF, el)
    em2 = jnp.max(el2, axis=-1, keepdims=True)
    idx2 = jnp.min(jnp.where(el2 == em2, lane, big), axis=-1, keepdims=True)
    p1 = 1.0 / es
    p2 = jnp.exp(em2 - em) / es
    gate1 = pg_top * p1 / (p1 + p2)
    gate2 = pg_top * p2 / (p1 + p2)
    e1 = (idx1 - N_EXPERT_GROUPS).astype(F32)
    e2 = (idx2 - N_EXPERT_GROUPS).astype(F32)
    route_ref[...] = jnp.where(lane == 0, e1, jnp.where(lane == 1, e2, jnp.where(lane == 2, gate1,
                               jnp.where(lane == 3, gate2, 0.0))))


def _merge(x2d, pf, oa, ob, oc, od, wbr, wout, ln, wr, br, tm):
    m = x2d.shape[0]
    row = lambda w: pl.BlockSpec((tm, w), lambda i: (i, 0))
    full = lambda a: pl.BlockSpec(a.shape, lambda i: (0,) * a.ndim)
    return pl.pallas_call(
        _merge_kernel,
        grid=(m // tm,),
        in_specs=[row(D_MODEL), pl.BlockSpec((tm, N_BRANCH * D_MODEL), lambda i: (i, PF_GATES)),
                  row(A_WIDTH), row(B_WIDTH), row(C_WIDTH), row(D_INNER),
                  full(wbr), full(wout), full(ln), full(wr), full(br)],
        out_specs=[row(D_MODEL), row(LANES)],
        out_shape=[jax.ShapeDtypeStruct((m, D_MODEL), F32), jax.ShapeDtypeStruct((m, LANES), F32)],
        compiler_params=_params("parallel", vmem_mb=48),
        name="merge_route",
    )(x2d, pf, oa, ob, oc, od, wbr, wout, ln, wr, br)


def _ffn_kernel(be_ref, nv_ref, xs_ref, wg_ref, wu_ref, wd_ref, o_ref):
    i = pl.program_id(0)

    @pl.when(i < nv_ref[0])
    def _():
        xb = xs_ref[...]
        hg = _dot(xb, wg_ref[0, 0].astype(BF16))
        hu = _dot(xb, wu_ref[0, 0].astype(BF16))
        o_ref[...] = _dot((_silu(hg) * hu).astype(BF16), wd_ref[0, 0].astype(BF16))

    @pl.when(i >= nv_ref[0])
    def _():
        o_ref[...] = jnp.zeros_like(o_ref)


def _ffn(blk_expert, n_valid, xs, e_gate, e_up, e_down, layer):
    cap = xs.shape[0]
    wspec = lambda a: pl.BlockSpec((1, 1) + a.shape[2:], lambda i, be, nv: (layer, be[i], 0, 0))
    return pl.pallas_call(
        _ffn_kernel,
        grid_spec=pltpu.PrefetchScalarGridSpec(
            num_scalar_prefetch=2,
            grid=(cap // MOE_ROWS,),
            in_specs=[pl.BlockSpec((MOE_ROWS, D_MODEL), lambda i, be, nv: (i, 0)),
                      wspec(e_gate), wspec(e_up), wspec(e_down)],
            out_specs=pl.BlockSpec((MOE_ROWS, D_MODEL), lambda i, be, nv: (i, 0)),
        ),
        out_shape=jax.ShapeDtypeStruct((cap, D_MODEL), F32),
        compiler_params=_params("arbitrary", vmem_mb=48),
        name="expert_ffn",
    )(blk_expert, n_valid, xs, e_gate, e_up, e_down)


def _combine_kernel(dest_ref, x_ref, route_ref, ln_ref, yb_ref, o_ref, ybuf, sem):
    i = pl.program_id(0)
    tm = x_ref.shape[0]

    def gather(tile, slot):
        base = tile * (TOP_K_EXPERTS * tm)

        def body(r, carry):
            for s in range(TOP_K_EXPERTS):
                row = dest_ref[base + TOP_K_EXPERTS * r + s]
                pltpu.make_async_copy(yb_ref.at[pl.ds(row, 1)], ybuf.at[slot, pl.ds(s * tm + r, 1)],
                                      sem.at[slot]).start(priority=s)
            return carry

        lax.fori_loop(0, tm, body, 0, unroll=8)

    @pl.when(i == 0)
    def _():
        gather(0, 0)

    @pl.when(i + 1 < pl.num_programs(0))
    def _():
        gather(i + 1, (i + 1) % 2)

    slot = i % 2
    pltpu.make_async_copy(yb_ref.at[pl.ds(0, TOP_K_EXPERTS * tm)], ybuf.at[slot], sem.at[slot]).wait()
    g0 = route_ref[:, 2:3]
    g1 = route_ref[:, 3:4]
    h = DEEPNORM_ALPHA * x_ref[...] + (ybuf[slot, 0:tm, :] * g0 + ybuf[slot, tm:2 * tm, :] * g1)
    o_ref[...] = _layer_norm(h, ln_ref[0:1, :], ln_ref[1:2, :])


def _combine(x1, yb, dest, route, ln, tm):
    m = x1.shape[0]
    return pl.pallas_call(
        _combine_kernel,
        grid_spec=pltpu.PrefetchScalarGridSpec(
            num_scalar_prefetch=1,
            grid=(m // tm,),
            in_specs=[pl.BlockSpec((tm, D_MODEL), lambda i, d: (i, 0)),
                      pl.BlockSpec((tm, LANES), lambda i, d: (i, 0)),
                      pl.BlockSpec(ln.shape, lambda i, d: (0, 0)),
                      pl.BlockSpec(memory_space=pl.ANY)],
            out_specs=pl.BlockSpec((tm, D_MODEL), lambda i, d: (i, 0)),
            scratch_shapes=[pltpu.VMEM((2, TOP_K_EXPERTS * tm, D_MODEL), F32), pltpu.SemaphoreType.DMA((2,))],
        ),
        out_shape=jax.ShapeDtypeStruct((m, D_MODEL), F32),
        compiler_params=_params("arbitrary"),
        name="combine",
    )(dest, x1, route, ln, yb)


def _dispatch_tables(route, m):
    flat_e = route[:, 0:TOP_K_EXPERTS].astype(I32).reshape(-1)
    n_assign = m * TOP_K_EXPERTS
    onehot = (flat_e[:, None] == jnp.arange(N_EXPERTS, dtype=I32)[None, :]).astype(I32)
    csum = jnp.cumsum(onehot, axis=0)
    rank = jnp.sum(csum * onehot, axis=1) - 1
    counts = csum[-1]
    padded = (counts + MOE_ROWS - 1) // MOE_ROWS * MOE_ROWS
    pad_end = jnp.cumsum(padded)
    pad_start = pad_end - padded
    dest = pad_start[flat_e] + rank
    cap = (n_assign + N_EXPERTS * (MOE_ROWS - 1) + MOE_ROWS - 1) // MOE_ROWS * MOE_ROWS
    n_blocks = cap // MOE_ROWS
    blk_start = jnp.arange(n_blocks, dtype=I32) * MOE_ROWS
    blk_expert = jnp.minimum(jnp.sum((pad_end[None, :] <= blk_start[:, None]).astype(I32), axis=1), N_EXPERTS - 1)
    buf_tok = (jnp.arange(cap, dtype=I32) % m).at[dest].set(jnp.arange(n_assign, dtype=I32) // TOP_K_EXPERTS)
    n_valid = (pad_end[-1] // MOE_ROWS).astype(I32).reshape(1)
    return dest, buf_tok, blk_expert, n_valid


def _row_pad(a, rows):
    return jnp.pad(a, ((0, rows - a.shape[0]), (0, 0)))


def kernel(x, w_in, c_mu, c_w0, c_w2, c_a0, c_a2, c_g2, c_kk, c_ka, c_rk, c_gn_w, c_gn_b, c_v0, c_v1, c_v2,
           d_conv_w, d_conv_b, d_dt_bias, d_a_log, d_skip, d_norm_w, w_branch, w_out, ln1_g, ln1_b,
           r_group, r_group_b, r_expert, r_expert_b, e_gate, e_up, e_down, ln2_g, ln2_b):
    batch, seq, _ = x.shape
    m = batch * seq
    biases = _mixa_class_bias()
    tw = min(256, seq)
    tri = jnp.asarray(np.triu(np.ones((tw, tw), np.float32))).astype(BF16)
    bd = _head_consts()
    expand = _ssd_consts()
    x2d = x.reshape(m, D_MODEL)
    v_first = jnp.zeros((batch, seq, C_WIDTH), F32)
    tm_proj = min(2048, m)
    tm_tok = min(512, m)
    for l in range(DEPTH):
        wb, wf = _proj_weights(w_in[l])
        pb = _proj(x2d, wb, BF16, tm_proj, PROJ_TN).reshape(batch, seq, PB_WIDTH)
        pf = _proj(x2d, wf, F32, tm_proj, PROJ_TN).reshape(batch, seq, PF_WIDTH)

        o_a = _mixer_a2(pb, biases, batch, seq)
        o_b = _mixer_b(pb, pf, tri, batch, seq)

        use_v_lora = l > 0
        vec_rows = [c_w0[l], c_a0[l], c_kk[l], c_ka[l], c_rk[l].reshape(-1)]
        vec_rows.append(c_v0[l - 1] if use_v_lora else jnp.zeros((C_WIDTH,), F32))
        vec = _row_pad(jnp.stack(vec_rows), SUBLANES)
        lora = jnp.stack([
            jnp.pad(c_w2[l], ((0, LANES - C_LORA_W), (0, 0))),
            jnp.pad(c_a2[l], ((C_LORA_W, LANES - C_LORA_W - C_LORA_A), (0, 0))),
            jnp.pad(c_g2[l], ((C_LORA_W + C_LORA_A, 0), (0, 0)))])
        if use_v_lora:
            vl = jnp.stack([jnp.pad(c_v1[l - 1], ((0, 0), (0, C_WIDTH - C_LORA_V))),
                            jnp.pad(c_v2[l - 1], ((0, C_WIDTH - C_LORA_V), (0, 0)))])
        else:
            vl = jnp.zeros((2, C_WIDTH, C_WIDTH), F32)
        gn = _row_pad(jnp.stack([c_gn_w[l], c_gn_b[l]]), SUBLANES)
        o_c, v_c = _mixer_c(pf, v_first, c_mu[l].reshape(1, C_IN), vec, lora, vl, gn, bd, batch, seq, use_v_lora)
        if l == 0:
            v_first = v_c

        conv = _row_pad(jnp.concatenate([d_conv_w[l], d_conv_b[l][None, :]], axis=0), SUBLANES)
        dvec = _row_pad(jnp.stack([jnp.repeat(d_skip[l], HEAD_DIM), d_norm_w[l]]), SUBLANES)
        place = lambda a: jnp.pad(a, (MISC_DT, LANES - MISC_DT - D_HEADS))
        hp = _row_pad(jnp.stack([place(d_dt_bias[l]), place(-jnp.exp(d_a_log[l]))]), SUBLANES)
        o_d = _mixer_d(pf, conv, dvec, hp, expand, bd, batch, seq)

        wr = jnp.pad(jnp.concatenate([r_group[l], r_expert[l]], axis=1),
                     ((0, 0), (0, LANES - N_EXPERT_GROUPS - N_EXPERTS)))
        br = jnp.pad(jnp.concatenate([r_group_b[l], r_expert_b[l]]), (0, LANES - N_EXPERT_GROUPS - N_EXPERTS))
        ln1 = _row_pad(jnp.stack([ln1_g[l], ln1_b[l]]), SUBLANES)
        x1, route = _merge(x2d, pf.reshape(m, PF_WIDTH), o_a.reshape(m, -1), o_b.reshape(m, -1),
                           o_c.reshape(m, -1), o_d.reshape(m, -1), w_branch[l].astype(BF16),
                           w_out[l].astype(BF16), ln1, wr, br.reshape(1, LANES), tm_tok)

        dest, buf_tok, blk_expert, n_valid = _dispatch_tables(route, m)
        xs = jnp.take(x1.astype(BF16), buf_tok, axis=0)
        yb = _ffn(blk_expert, n_valid, xs, e_gate, e_up, e_down, l)
        ln2 = _row_pad(jnp.stack([ln2_g[l], ln2_b[l]]), SUBLANES)
        x2d = _combine(x1, yb, dest, route, ln2, min(256, m))
    return x2d.reshape(batch, seq, D_MODEL)
```
